```python
import math
import jax, jax.numpy as jnp
from jax import lax
import numpy as np

D_MODEL = 1024
BATCH = 32
SEQ = 256
DEPTH = 1
DEC_BATCH = 4
DEC_SEQ = 2048
PAST_LEN = 512

GRID_W = 64
N_HEADS = 8
HEAD_DIM = 64
D_ATT = N_HEADS * HEAD_DIM
N_POOL_GROUPS = 4
POOL_WINDOWS = (2, 4, 8, 16)
D_POOL = 512
POOL_GROUP_DIM = D_POOL // N_POOL_GROUPS
N_BRANCHES = 2
D_IN = D_POOL + 3 * D_ATT + N_BRANCHES * D_MODEL
D_FF = 2816
NA_KH = 8
NA_KW = 16
Q_BLOCK = 128
N_MOD = 9
EPS = 1e-6
NEG = -1e30

kernel_name = "hybrid_pool_natten_macaron_step"


def rms_norm(x, g):
    xf = x.astype(jnp.float32)
    y = xf * lax.rsqrt(jnp.mean(xf * xf, axis=-1, keepdims=True) + EPS)
    return (y * g.astype(jnp.float32)).astype(x.dtype)


def adaln_params(cond, w_ada, b_ada):
    m = jax.nn.silu(cond) @ w_ada + b_ada
    m = m.reshape(-1, 1, N_MOD, D_MODEL)
    return [m[:, :, i] for i in range(N_MOD)]


def modulate(h, shift, scale):
    return h * (1.0 + scale) + shift


def swiglu(h, w_gu, w_dn):
    a, u = jnp.split(h @ w_gu, 2, axis=-1)
    return (jax.nn.silu(a) * u) @ w_dn


def pool_mixer(p, w_pool, pool_scale):
    B, L, _ = p.shape
    pg = p.reshape(B, L, N_POOL_GROUPS, POOL_GROUP_DIM)
    cs = jnp.cumsum(pg.astype(jnp.float32), axis=1)
    cs = jnp.concatenate([jnp.zeros_like(cs[:, :1]), cs], axis=1)
    t = jnp.arange(L)
    means = []
    for g, w in enumerate(POOL_WINDOWS):
        lo = jnp.clip(t - w // 2, 0, L)
        hi = jnp.clip(t + w // 2, 0, L)
        s = cs[:, hi, g] - cs[:, lo, g]
        means.append(s / (hi - lo).astype(jnp.float32)[None, :, None])
    mean = jnp.stack(means, axis=2)
    d = (mean - pg.astype(jnp.float32)).astype(p.dtype)
    y = jnp.einsum('blgc,gcd->blgd', d, w_pool)
    return y.reshape(B, L, D_POOL) * pool_scale


def context_attention(q, k, v):
    B, P, H, d = q.shape
    scale = 1.0 / math.sqrt(HEAD_DIM)
    qb = q.reshape(B, P // Q_BLOCK, Q_BLOCK, H, d).transpose(1, 0, 2, 3, 4)

    def one_block(q_blk):
        s = jnp.einsum('bqhd,bkhd->bhqk', q_blk, k).astype(jnp.float32) * scale
        pr = jax.nn.softmax(s, axis=-1).astype(v.dtype)
        return jnp.einsum('bhqk,bkhd->bqhd', pr, v)

    o = lax.map(one_block, qb)
    return o.transpose(1, 0, 2, 3, 4).reshape(B, P, H * d)


def latent_attention(q, k, v, k_ctx, v_ctx, rpb):
    B, L, H, d = q.shape
    rows = L // GRID_W
    kh = min(NA_KH, rows)
    scale = 1.0 / math.sqrt(HEAD_DIM)
    qg = q.reshape(B, rows, GRID_W, H, d)
    kg = k.reshape(B, rows, GRID_W, H, d)
    vg = v.reshape(B, rows, GRID_W, H, d)
    r = jnp.arange(rows)
    r0 = jnp.clip(r - kh // 2, 0, rows - kh)
    col = jnp.arange(GRID_W)
    c0 = jnp.clip(col - NA_KW // 2, 0, GRID_W - NA_KW)
    col_ok = (col[None, :] >= c0[:, None]) & (col[None, :] < c0[:, None] + NA_KW)
    dc_idx = jnp.clip(col[None, :] - col[:, None] + NA_KW - 1, 0, 2 * NA_KW - 2)
    col_bias = rpb.astype(jnp.float32)[:, :, dc_idx]
    col_bias = jnp.where(col_ok[None, None], col_bias, NEG)

    def one_row(args):
        q_row, r_q, r_start = args
        k_win = lax.dynamic_slice_in_dim(kg, r_start, kh, axis=1)
        v_win = lax.dynamic_slice_in_dim(vg, r_start, kh, axis=1)
        dr_idx = r_start + jnp.arange(kh) - r_q + NA_KH - 1
        bias = col_bias[:, dr_idx].transpose(0, 2, 1, 3)
        s_loc = jnp.einsum('bqhd,bikhd->bhqik', q_row, k_win).astype(jnp.float32) * scale
        s_loc = (s_loc + bias[None]).reshape(B, H, GRID_W, kh * GRID_W)
        s_ctx = jnp.einsum('bqhd,bphd->bhqp', q_row, k_ctx).astype(jnp.float32) * scale
        pr = jax.nn.softmax(jnp.concatenate([s_loc, s_ctx], axis=-1), axis=-1).astype(v.dtype)
        p_loc = pr[..., :kh * GRID_W].reshape(B, H, GRID_W, kh, GRID_W)
        p_ctx = pr[..., kh * GRID_W:]
        return (jnp.einsum('bhqik,bikhd->bqhd', p_loc, v_win)
                + jnp.einsum('bhqp,bphd->bqhd', p_ctx, v_ctx))

    o = lax.map(one_row, (qg.transpose(1, 0, 2, 3, 4), r, r0))
    return o.transpose(1, 0, 2, 3, 4).reshape(B, L, H * d)


def trunk_layer(x, mod, lp, k_ctx=None, v_ctx=None):
    sh1, sc1, g1, sh2, sc2, g2, sh3, sc3, g3 = mod
    B, L, _ = x.shape
    h = modulate(rms_norm(x, lp['g_ff1']), sh1, sc1)
    x = x + 0.5 * g1 * swiglu(h, lp['w_ff1_in'], lp['w_ff1_out'])

    h = modulate(rms_norm(x, lp['g_mix']), sh2, sc2)
    proj = h @ lp['w_in']
    o1 = D_POOL
    o2 = o1 + D_ATT
    o3 = o2 + D_ATT
    o4 = o3 + D_ATT
    p = proj[..., :o1]
    q = rms_norm(proj[..., o1:o2].reshape(B, L, N_HEADS, HEAD_DIM), lp['q_gain'])
    k = rms_norm(proj[..., o2:o3].reshape(B, L, N_HEADS, HEAD_DIM), lp['k_gain'])
    v = proj[..., o3:o4].reshape(B, L, N_HEADS, HEAD_DIM)
    gates = jax.nn.sigmoid(proj[..., o4:].astype(jnp.float32)).astype(x.dtype)
    gates = gates.reshape(B, L, N_BRANCHES, D_MODEL)

    a = pool_mixer(p, lp['w_pool'], lp['pool_scale']) @ lp['w_br_pool']
    if k_ctx is None:
        att = context_attention(q, k, v)
    else:
        att = latent_attention(q, k, v, k_ctx, v_ctx, lp['rpb'])
    b = att @ lp['w_br_att']
    merged = gates[:, :, 0] * a + gates[:, :, 1] * b
    x = x + g2 * (merged @ lp['w_out'])

    h = modulate(rms_norm(x, lp['g_ff2']), sh3, sc3)
    x = x + 0.5 * g3 * swiglu(h, lp['w_ff2_in'], lp['w_ff2_out'])
    return x, k, v


def setup_inputs(seed: int = 0) -> dict:
    key = jax.random.key(seed)
    ks = jax.random.split(key, 32)
    f32 = jnp.float32

    def nrm(k, shape, scale=1.0):
        return jax.random.normal(k, shape, f32) * scale

    def gain(k, shape):
        return 1.0 + 0.1 * jax.random.normal(k, shape, f32)

    D = D_MODEL
    return {
        "x_prompt": nrm(ks[0], (BATCH, SEQ, D)),
        "x_sample": nrm(ks[1], (DEC_BATCH, DEC_SEQ, D)),
        "cache_k": nrm(ks[2], (DEC_BATCH, DEPTH, PAST_LEN, N_HEADS, HEAD_DIM)),
        "cache_v": nrm(ks[3], (DEC_BATCH, DEPTH, PAST_LEN, N_HEADS, HEAD_DIM)),
        "c": nrm(ks[4], (DEC_BATCH, D)),
        "c_ctx": nrm(ks[5], (D,)),
        "w_ada": nrm(ks[6], (DEPTH, D, N_MOD * D), D ** -0.5),
        "b_ada": nrm(ks[7], (DEPTH, N_MOD * D), 0.02),
        "g_ff1": gain(ks[8], (DEPTH, D)),
        "w_ff1_in": nrm(ks[9], (DEPTH, D, 2 * D_FF), D ** -0.5),
        "w_ff1_out": nrm(ks[10], (DEPTH, D_FF, D), D_FF ** -0.5),
        "g_mix": gain(ks[11], (DEPTH, D)),
        "w_in": nrm(ks[12], (DEPTH, D, D_IN), D ** -0.5),
        "q_gain": gain(ks[13], (DEPTH, HEAD_DIM)),
        "k_gain": gain(ks[14], (DEPTH, HEAD_DIM)),
        "w_pool": nrm(ks[15], (DEPTH, N_POOL_GROUPS, POOL_GROUP_DIM, POOL_GROUP_DIM), POOL_GROUP_DIM ** -0.5),
        "pool_scale": gain(ks[16], (DEPTH, D_POOL)),
        "rpb": nrm(ks[17], (DEPTH, N_HEADS, 2 * NA_KH - 1, 2 * NA_KW - 1), 0.1),
        "w_br_pool": nrm(ks[18], (DEPTH, D_POOL, D), D_POOL ** -0.5),
        "w_br_att": nrm(ks[19], (DEPTH, D_ATT, D), D_ATT ** -0.5),
        "w_out": nrm(ks[20], (DEPTH, D, D), D ** -0.5),
        "g_ff2": gain(ks[21], (DEPTH, D)),
        "w_ff2_in": nrm(ks[22], (DEPTH, D, 2 * D_FF), D ** -0.5),
        "w_ff2_out": nrm(ks[23], (DEPTH, D_FF, D), D_FF ** -0.5),
    }


def reference(x_prompt, x_sample, cache_k, cache_v, c, c_ctx, w_ada, b_ada, g_ff1, w_ff1_in,
              w_ff1_out, g_mix, w_in, q_gain, k_gain, w_pool, pool_scale, rpb, w_br_pool,
              w_br_att, w_out, g_ff2, w_ff2_in, w_ff2_out):
    xp = x_prompt
    xs = x_sample
    new_ks = []
    new_vs = []
    for l in range(DEPTH):
        lp = {
            'g_ff1': g_ff1[l], 'w_ff1_in': w_ff1_in[l], 'w_ff1_out': w_ff1_out[l],
            'g_mix': g_mix[l], 'w_in': w_in[l], 'q_gain': q_gain[l], 'k_gain': k_gain[l],
            'w_pool': w_pool[l], 'pool_scale': pool_scale[l], 'rpb': rpb[l],
            'w_br_pool': w_br_pool[l], 'w_br_att': w_br_att[l], 'w_out': w_out[l],
            'g_ff2': g_ff2[l], 'w_ff2_in': w_ff2_in[l], 'w_ff2_out': w_ff2_out[l],
        }
        mod_ctx = adaln_params(c_ctx, w_ada[l], b_ada[l])
        xp, k_p, v_p = trunk_layer(xp, mod_ctx, lp)
        new_ks.append(k_p)
        new_vs.append(v_p)
        mod_lat = adaln_params(c, w_ada[l], b_ada[l])
        xs, _, _ = trunk_layer(xs, mod_lat, lp, cache_k[:, l], cache_v[:, l])
    new_k = jnp.stack(new_ks, axis=1)
    new_v = jnp.stack(new_vs, axis=1)
    return (xp, xs, new_k, new_v)
```

```python
import functools
import math

import jax
import jax.numpy as jnp
from jax import lax
from jax.experimental import pallas as pl
from jax.experimental.pallas import tpu as pltpu

F32 = jnp.float32
BF16 = jnp.bfloat16

D_MODEL = 1024
N_HEADS = 8
HEAD_DIM = 64
D_ATT = N_HEADS * HEAD_DIM
D_POOL = 512
POOL_WINDOWS = (2, 4, 8, 16)
POOL_HALO = 8
D_FF = 2816
N_MOD = 9
GRID_W = 64
NA_KH = 8
NA_KW = 16
EPS = 1e-6
NEG = -1e30

LANES = 128
N_PAIRS = D_ATT // LANES
FF_CHUNKS = ((0, 768), (768, 768), (1536, 768), (2304, 512))
FFN_TM = 512
CTX_TB = 2
LAT_ROWS = 4
LAT_KEY_ROWS = 12
KV_TM = 512
VMEM_LIMIT = 56 * 1024 * 1024


def _dot(a, b):
    return jnp.dot(a, b, preferred_element_type=F32)


def _dot_nt(a, b):
    return lax.dot_general(a, b, (((1,), (1,)), ((), ())), preferred_element_type=F32)


def _rms_mod(x, g, shift, scale):
    ms = jnp.mean(x * x, axis=-1, keepdims=True)
    y = x * lax.rsqrt(ms + EPS) * g
    return y * (1.0 + scale) + shift


def _low_half():
    return lax.broadcasted_iota(jnp.int32, (1, LANES), 1) < HEAD_DIM


def _head_norm(t, gain2):
    lo = _low_half()
    sq = t * t
    s0 = jnp.sum(jnp.where(lo, sq, 0.0), axis=-1, keepdims=True)
    s1 = jnp.sum(jnp.where(lo, 0.0, sq), axis=-1, keepdims=True)
    r0 = lax.rsqrt(s0 * (1.0 / HEAD_DIM) + EPS)
    r1 = lax.rsqrt(s1 * (1.0 / HEAD_DIM) + EPS)
    return t * jnp.where(lo, r0, r1) * gain2


def _softmax_parts(parts):
    m = parts[0].max(axis=-1, keepdims=True)
    for s in parts[1:]:
        m = jnp.maximum(m, s.max(axis=-1, keepdims=True))
    es = [jnp.exp(s - m) for s in parts]
    l = es[0].sum(axis=-1, keepdims=True)
    for e in es[1:]:
        l = l + e.sum(axis=-1, keepdims=True)
    inv = 1.0 / l
    return [(e * inv).astype(BF16) for e in es]


def _pool_delta(p_ext, pos0, seq_len):
    te = p_ext.shape[0]
    tm = te - 2 * POOL_HALO
    t = pos0 + lax.broadcasted_iota(jnp.int32, (tm, 1), 0)
    outs = []
    for g, w in enumerate(POOL_WINDOWS):
        a = p_ext[:, g * LANES:(g + 1) * LANES]
        s = a
        step = 1
        while step < w:
            s = s + pltpu.roll(s, step, 0)
            step *= 2
        ahead = w // 2 - 1
        if ahead:
            s = pltpu.roll(s, te - ahead, 0)
        win = s[POOL_HALO:POOL_HALO + tm]
        cnt = (jnp.minimum(t + w // 2, seq_len) - jnp.maximum(t - w // 2, 0)).astype(F32)
        outs.append(win / cnt - a[POOL_HALO:POOL_HALO + tm])
    return jnp.concatenate(outs, axis=1)


def _pool_branch(d, wpool_ref, pscale_ref, wbrp_ref):
    y = _dot(d.astype(BF16), wpool_ref[...]) * pscale_ref[...]
    return _dot(y.astype(BF16), wbrp_ref[...])


def _adaln_kernel(cond_ref, w_ref, b_ref, o_ref):
    c = cond_ref[...]
    s = c * jax.nn.sigmoid(c)
    o_ref[...] = _dot(s.astype(BF16), w_ref[...].astype(BF16)) + b_ref[...]


def _adaln(cond8, w_ada, b_ada):
    n = w_ada.shape[1]
    tn = 1152
    return pl.pallas_call(
        _adaln_kernel,
        grid=(n // tn,),
        in_specs=[pl.BlockSpec((8, D_MODEL), lambda i: (0, 0)),
                  pl.BlockSpec((D_MODEL, tn), lambda i: (0, i)),
                  pl.BlockSpec((1, tn), lambda i: (0, i))],
        out_specs=pl.BlockSpec((8, tn), lambda i: (0, i)),
        out_shape=jax.ShapeDtypeStruct((8, n), F32),
        compiler_params=pltpu.CompilerParams(dimension_semantics=("arbitrary",),
                                             vmem_limit_bytes=VMEM_LIMIT),
        name="adaln",
    )(cond8, w_ada, b_ada.reshape(1, n))


def _ffn_kernel(x_ref, mod_ref, g_ref, wgu_ref, wdn_ref, o_ref, *, mod_off):
    x = x_ref[...]
    shift = mod_ref[0, mod_off:mod_off + 1, :]
    scale = mod_ref[0, mod_off + 1:mod_off + 2, :]
    gate = mod_ref[0, mod_off + 2:mod_off + 3, :]
    h = _rms_mod(x, g_ref[...], shift, scale).astype(BF16)
    acc = None
    for c0, cw in FF_CHUNKS:
        a = _dot(h, wgu_ref[:, c0:c0 + cw])
        u = _dot(h, wgu_ref[:, D_FF + c0:D_FF + c0 + cw])
        t = (a * jax.nn.sigmoid(a) * u).astype(BF16)
        part = _dot(t, wdn_ref[c0:c0 + cw, :])
        acc = part if acc is None else acc + part
    o_ref[...] = x + (0.5 * gate) * acc


def _resident(shape):
    nd = len(shape)
    return pl.BlockSpec(shape, lambda *_: (0,) * nd, pipeline_mode=pl.Buffered(1))


def _ffn(x2d, mod3, mod_row, mod_off, g, wgu, wdn, name):
    t = x2d.shape[0]
    tm = FFN_TM
    return pl.pallas_call(
        functools.partial(_ffn_kernel, mod_off=mod_off),
        grid=(t // tm,),
        in_specs=[pl.BlockSpec((tm, D_MODEL), lambda i: (i, 0)),
                  pl.BlockSpec((1, N_MOD, D_MODEL), lambda i: (mod_row(i * tm), 0, 0)),
                  _resident((1, D_MODEL)),
                  _resident((D_MODEL, 2 * D_FF)),
                  _resident((D_FF, D_MODEL))],
        out_specs=pl.BlockSpec((tm, D_MODEL), lambda i: (i, 0)),
        out_shape=jax.ShapeDtypeStruct((t, D_MODEL), F32),
        compiler_params=pltpu.CompilerParams(dimension_semantics=("arbitrary",),
                                             vmem_limit_bytes=VMEM_LIMIT),
        name=name,
    )(x2d, mod3, g, wgu, wdn)


def _merge_out(x, gate2, gl, a, b, wout_ref):
    gates = jax.nn.sigmoid(gl)
    merged = gates[:, :D_MODEL] * a + gates[:, D_MODEL:] * b
    return x + gate2 * _dot(merged.astype(BF16), wout_ref[...])


def _ctx_mixer_kernel(x_ref, mod_ref, g_ref, win_ref, qg_ref, kg_ref, wpool_ref, pscale_ref,
                      wbrp_ref, wbra_ref, wout_ref, o_ref, ko_ref, vo_ref, *, seq):
    x = x_ref[...]
    m_rows = x.shape[0]
    h = _rms_mod(x, g_ref[...], mod_ref[0, 3:4, :], mod_ref[0, 4:5, :]).astype(BF16)
    p = _dot(h, win_ref[:, 0:D_POOL])
    q = _dot(h, win_ref[:, D_POOL:D_POOL + D_ATT])
    k = _dot(h, win_ref[:, D_POOL + D_ATT:D_POOL + 2 * D_ATT])
    v = _dot(h, win_ref[:, D_POOL + 2 * D_ATT:D_POOL + 3 * D_ATT])
    gl = _dot(h, win_ref[:, D_POOL + 3 * D_ATT:])
    vo_ref[...] = v

    lo = _low_half()
    scale = 1.0 / math.sqrt(HEAD_DIM)
    zeros_halo = jnp.zeros((POOL_HALO, D_POOL), F32)
    d_rows = []
    att_rows = []
    for e in range(m_rows // seq):
        rows = slice(e * seq, (e + 1) * seq)
        p_ext = jnp.concatenate([zeros_halo, p[rows], zeros_halo], axis=0)
        d_rows.append(_pool_delta(p_ext, 0, seq))
        att_pairs = []
        for pr in range(N_PAIRS):
            lanes = slice(pr * LANES, (pr + 1) * LANES)
            kn = _head_norm(k[rows, lanes], kg_ref[...])
            ko_ref[rows, lanes] = kn
            qn = _head_norm(q[rows, lanes], qg_ref[...]) * scale
            kb = kn.astype(BF16)
            vb = v[rows, lanes].astype(BF16)
            halves = []
            for half in range(2):
                qm = jnp.where(lo, qn, 0.0) if half == 0 else jnp.where(lo, 0.0, qn)
                s = _dot_nt(qm.astype(BF16), kb)
                (pm,) = _softmax_parts([s])
                halves.append(_dot(pm, vb))
            att_pairs.append(jnp.where(lo, halves[0], halves[1]))
        att_rows.append(jnp.concatenate(att_pairs, axis=1))
    d = jnp.concatenate(d_rows, axis=0)
    att = jnp.concatenate(att_rows, axis=0)
    a = _pool_branch(d, wpool_ref, pscale_ref, wbrp_ref)
    b = _dot(att.astype(BF16), wbra_ref[...])
    o_ref[...] = _merge_out(x, mod_ref[0, 5:6, :], gl, a, b, wout_ref)


def _ctx_mixer(x2d, mod3, g, w_in, qg2, kg2, wpool, pscale, wbrp, wbra, wout, seq):
    t = x2d.shape[0]
    tm = CTX_TB * seq
    row = lambda i: (i, 0)
    return pl.pallas_call(
        functools.partial(_ctx_mixer_kernel, seq=seq),
        grid=(t // tm,),
        in_specs=[pl.BlockSpec((tm, D_MODEL), row),
                  pl.BlockSpec((1, N_MOD, D_MODEL), lambda i: (0, 0, 0)),
                  _resident((1, D_MODEL)),
                  _resident(w_in.shape),
                  _resident((1, LANES)), _resident((1, LANES)),
                  _resident(wpool.shape), _resident((1, D_POOL)),
                  _resident(wbrp.shape), _resident(wbra.shape), _resident(wout.shape)],
        out_specs=[pl.BlockSpec((tm, D_MODEL), row),
                   pl.BlockSpec((tm, D_ATT), row),
                   pl.BlockSpec((tm, D_ATT), row)],
        out_shape=[jax.ShapeDtypeStruct((t, D_MODEL), F32),
                   jax.ShapeDtypeStruct((t, D_ATT), F32),
                   jax.ShapeDtypeStruct((t, D_ATT), F32)],
        compiler_params=pltpu.CompilerParams(dimension_semantics=("arbitrary",),
                                             vmem_limit_bytes=VMEM_LIMIT),
        name="ctx_mixer",
    )(x2d, mod3, g, w_in, qg2, kg2, wpool, pscale, wbrp, wbra, wout)


def _lat_kv_kernel(x_ref, mod_ref, g_ref, wkv_ref, kg_ref, ko_ref, vo_ref):
    h = _rms_mod(x_ref[...], g_ref[...], mod_ref[0, 3:4, :], mod_ref[0, 4:5, :]).astype(BF16)
    k = _dot(h, wkv_ref[:, :D_ATT])
    vo_ref[...] = _dot(h, wkv_ref[:, D_ATT:]).astype(BF16)
    for pr in range(N_PAIRS):
        lanes = slice(pr * LANES, (pr + 1) * LANES)
        ko_ref[:, lanes] = _head_norm(k[:, lanes], kg_ref[...]).astype(BF16)


def _lat_kv(x2d, mod3, mod_row, g, wkv, kg2):
    t = x2d.shape[0]
    tm = KV_TM
    row = lambda i: (i, 0)
    return pl.pallas_call(
        _lat_kv_kernel,
        grid=(t // tm,),
        in_specs=[pl.BlockSpec((tm, D_MODEL), row),
                  pl.BlockSpec((1, N_MOD, D_MODEL), lambda i: (mod_row(i * tm), 0, 0)),
                  _resident((1, D_MODEL)), _resident(wkv.shape), _resident((1, LANES))],
        out_specs=[pl.BlockSpec((tm, D_ATT), row), pl.BlockSpec((tm, D_ATT), row)],
        out_shape=[jax.ShapeDtypeStruct((t, D_ATT), BF16), jax.ShapeDtypeStruct((t, D_ATT), BF16)],
        compiler_params=pltpu.CompilerParams(dimension_semantics=("arbitrary",),
                                             vmem_limit_bytes=VMEM_LIMIT),
        name="lat_kv",
    )(x2d, mod3, g, wkv, kg2)


def _lat_mixer_kernel(x_ref, xp_ref, xn_ref, mod_ref, g_ref, wpq_ref, wg_ref, qg_ref, k_ref, v_ref,
                      ck_ref, cv_ref, bias_ref, wpool_ref, pscale_ref, wbrp_ref, wbra_ref, wout_ref,
                      o_ref, *, seq, n_rows):
    j = pl.program_id(1)
    tm = LAT_ROWS * GRID_W
    n_keys = LAT_KEY_ROWS * GRID_W
    x = x_ref[...]
    g = g_ref[...]
    shift = mod_ref[0, 3:4, :]
    scale_mod = mod_ref[0, 4:5, :]
    h = _rms_mod(x, g, shift, scale_mod).astype(BF16)
    x_halo = jnp.concatenate([xp_ref[...], xn_ref[...]], axis=0)
    h_halo = _rms_mod(x_halo, g, shift, scale_mod).astype(BF16)

    p_halo = _dot(h_halo, wpq_ref[:, :D_POOL])
    p_ext = jnp.concatenate([p_halo[:POOL_HALO], _dot(h, wpq_ref[:, :D_POOL]), p_halo[POOL_HALO:]], axis=0)
    pos = j * tm - POOL_HALO + lax.broadcasted_iota(jnp.int32, (tm + 2 * POOL_HALO, 1), 0)
    p_ext = jnp.where((pos >= 0) & (pos < seq), p_ext, 0.0)
    d = _pool_delta(p_ext, j * tm, seq)
    a = _pool_branch(d, wpool_ref, pscale_ref, wbrp_ref)

    q = _dot(h, wpq_ref[:, D_POOL:])
    gl = _dot(h, wg_ref[...])

    r0q = j * LAT_ROWS
    u0 = jnp.clip(r0q - NA_KH // 2, 0, n_rows - LAT_KEY_ROWS)
    key0 = pl.multiple_of(u0 * GRID_W, GRID_W)
    log2_w = GRID_W.bit_length() - 1
    q_row = r0q + jnp.right_shift(lax.broadcasted_iota(jnp.int32, (tm, 1), 0), log2_w)
    k_row = u0 + jnp.right_shift(lax.broadcasted_iota(jnp.int32, (1, n_keys), 1), log2_w)
    first = jnp.clip(q_row - NA_KH // 2, 0, n_rows - NA_KH)
    row_ok = (k_row >= first) & (k_row < first + NA_KH)

    lo = _low_half()
    scale = 1.0 / math.sqrt(HEAD_DIM)
    att_pairs = []
    for pr in range(N_PAIRS):
        lanes = slice(pr * LANES, (pr + 1) * LANES)
        qn = _head_norm(q[:, lanes], qg_ref[...]) * scale
        kw = k_ref[0, pl.ds(key0, n_keys), lanes]
        vw = v_ref[0, pl.ds(key0, n_keys), lanes]
        kc = ck_ref[0, :, lanes]
        vc = cv_ref[0, :, lanes]
        halves = []
        for half in range(2):
            head = 2 * pr + half
            qm = (jnp.where(lo, qn, 0.0) if half == 0 else jnp.where(lo, 0.0, qn)).astype(BF16)
            bias_rows = []
            for jr in range(LAT_ROWS):
                blocks = []
                for pi in range(LAT_KEY_ROWS // 2):
                    idx = jnp.clip(u0 - r0q + 2 * pi - jr + NA_KH - 1, 0, 2 * NA_KH - 3)
                    blocks.append(bias_ref[head, idx])
                bias_rows.append(jnp.concatenate(blocks, axis=1))
            bias = jnp.concatenate(bias_rows, axis=0)
            s_loc = jnp.where(row_ok, _dot_nt(qm, kw) + bias, NEG)
            s_ctx = _dot_nt(qm, kc)
            p_loc, p_ctx = _softmax_parts([s_loc, s_ctx])
            halves.append(_dot(p_loc, vw) + _dot(p_ctx, vc))
        att_pairs.append(jnp.where(lo, halves[0], halves[1]))
    att = jnp.concatenate(att_pairs, axis=1)
    b = _dot(att.astype(BF16), wbra_ref[...])
    o_ref[...] = _merge_out(x, mod_ref[0, 5:6, :], gl, a, b, wout_ref)


def _lat_mixer(x2d, mod3, g, wpq, wg, qg2, k3, v3, ck3, cv3, bias_tab, wpool, pscale, wbrp, wbra, wout,
               n_batch, seq):
    tm = LAT_ROWS * GRID_W
    n_rows = seq // GRID_W
    nb = seq // tm
    halo_per_tile = tm // POOL_HALO
    halo_per_seq = seq // POOL_HALO

    def prev_map(b, j):
        return (jnp.maximum(b * halo_per_seq + j * halo_per_tile - 1, b * halo_per_seq), 0)

    def next_map(b, j):
        return (jnp.minimum(b * halo_per_seq + (j + 1) * halo_per_tile, (b + 1) * halo_per_seq - 1), 0)

    seq_map = lambda b, j: (b, 0, 0)
    return pl.pallas_call(
        functools.partial(_lat_mixer_kernel, seq=seq, n_rows=n_rows),
        grid=(n_batch, nb),
        in_specs=[pl.BlockSpec((tm, D_MODEL), lambda b, j: (b * nb + j, 0)),
                  pl.BlockSpec((POOL_HALO, D_MODEL), prev_map),
                  pl.BlockSpec((POOL_HALO, D_MODEL), next_map),
                  pl.BlockSpec((1, N_MOD, D_MODEL), lambda b, j: (1 + b, 0, 0)),
                  _resident((1, D_MODEL)), _resident(wpq.shape), _resident(wg.shape),
                  _resident((1, LANES)),
                  pl.BlockSpec((1, seq, D_ATT), seq_map), pl.BlockSpec((1, seq, D_ATT), seq_map),
                  pl.BlockSpec((1,) + ck3.shape[1:], seq_map), pl.BlockSpec((1,) + cv3.shape[1:], seq_map),
                  _resident(bias_tab.shape),
                  _resident(wpool.shape), _resident((1, D_POOL)),
                  _resident(wbrp.shape), _resident(wbra.shape), _resident(wout.shape)],
        out_specs=pl.BlockSpec((tm, D_MODEL), lambda b, j: (b * nb + j, 0)),
        out_shape=jax.ShapeDtypeStruct(x2d.shape, F32),
        compiler_params=pltpu.CompilerParams(dimension_semantics=("arbitrary", "arbitrary"),
                                             vmem_limit_bytes=VMEM_LIMIT),
        name="lat_mixer",
    )(x2d, x2d, x2d, mod3, g, wpq, wg, qg2, k3, v3, ck3, cv3, bias_tab, wpool, pscale, wbrp, wbra, wout)


def _bias_table(rpb_l):
    col = jnp.arange(GRID_W)
    c0 = jnp.clip(col - NA_KW // 2, 0, GRID_W - NA_KW)
    col_ok = (col[None, :] >= c0[:, None]) & (col[None, :] < c0[:, None] + NA_KW)
    dc = jnp.clip(col[None, :] - col[:, None] + NA_KW - 1, 0, 2 * NA_KW - 2)
    cb = jnp.where(col_ok[None, None], rpb_l.astype(F32)[:, :, dc], NEG)
    return jnp.concatenate([cb[:, :-1], cb[:, 1:]], axis=-1)


def kernel(x_prompt, x_sample, cache_k, cache_v, c, c_ctx, w_ada, b_ada, g_ff1, w_ff1_in, w_ff1_out,
           g_mix, w_in, q_gain, k_gain, w_pool, pool_scale, rpb, w_br_pool, w_br_att, w_out, g_ff2,
           w_ff2_in, w_ff2_out):
    n_ctx, seq_ctx, _ = x_prompt.shape
    n_lat, seq_lat, _ = x_sample.shape
    depth = w_ada.shape[0]
    assert depth == 1 and n_lat + 1 <= 8

    xp = x_prompt.reshape(n_ctx * seq_ctx, D_MODEL)
    xs = x_sample.reshape(n_lat * seq_lat, D_MODEL)
    cond8 = jnp.zeros((8, D_MODEL), F32).at[0].set(c_ctx).at[1:1 + n_lat].set(c)

    l = 0
    mod3 = _adaln(cond8, w_ada[l], b_ada[l]).reshape(8, N_MOD, D_MODEL)
    ctx_row = lambda tok: 0
    lat_row = lambda tok: 1 + tok // seq_lat

    w1gu, w1dn = w_ff1_in[l].astype(BF16), w_ff1_out[l].astype(BF16)
    w2gu, w2dn = w_ff2_in[l].astype(BF16), w_ff2_out[l].astype(BF16)
    win = w_in[l].astype(BF16)
    o_q, o_k, o_v, o_g = D_POOL, D_POOL + D_ATT, D_POOL + 2 * D_ATT, D_POOL + 3 * D_ATT
    wpq = win[:, :o_k]
    wkv = win[:, o_k:o_g]
    wgate = win[:, o_g:]
    wpool_bd = jax.scipy.linalg.block_diag(*[w_pool[l, g] for g in range(len(POOL_WINDOWS))]).astype(BF16)
    pscale = pool_scale[l].reshape(1, D_POOL)
    wbrp, wbra, wout = w_br_pool[l].astype(BF16), w_br_att[l].astype(BF16), w_out[l].astype(BF16)
    qg2 = jnp.tile(q_gain[l], 2).reshape(1, LANES)
    kg2 = jnp.tile(k_gain[l], 2).reshape(1, LANES)
    g1, gm, g2 = g_ff1[l].reshape(1, -1), g_mix[l].reshape(1, -1), g_ff2[l].reshape(1, -1)

    xp = _ffn(xp, mod3, ctx_row, 0, g1, w1gu, w1dn, "ffn1_ctx")
    xp, k_p, v_p = _ctx_mixer(xp, mod3, gm, win, qg2, kg2, wpool_bd, pscale, wbrp, wbra, wout, seq_ctx)
    xp = _ffn(xp, mod3, ctx_row, 6, g2, w2gu, w2dn, "ffn2_ctx")

    ck3 = cache_k[:, l].reshape(n_lat, -1, D_ATT).astype(BF16)
    cv3 = cache_v[:, l].reshape(n_lat, -1, D_ATT).astype(BF16)
    xs = _ffn(xs, mod3, lat_row, 0, g1, w1gu, w1dn, "ffn1_lat")
    k_s, v_s = _lat_kv(xs, mod3, lat_row, gm, wkv, kg2)
    xs = _lat_mixer(xs, mod3, gm, wpq, wgate, qg2, k_s.reshape(n_lat, seq_lat, D_ATT),
                    v_s.reshape(n_lat, seq_lat, D_ATT), ck3, cv3, _bias_table(rpb[l]), wpool_bd, pscale,
                    wbrp, wbra, wout, n_lat, seq_lat)
    xs = _ffn(xs, mod3, lat_row, 6, g2, w2gu, w2dn, "ffn2_lat")

    new_k = k_p.reshape(n_ctx, 1, seq_ctx, N_HEADS, HEAD_DIM)
    new_v = v_p.reshape(n_ctx, 1, seq_ctx, N_HEADS, HEAD_DIM)
    return (xp.reshape(x_prompt.shape), xs.reshape(x_sample.shape), new_k, new_v)
```

```python
import functools
import math

import jax
import jax.numpy as jnp
from jax import lax
from jax.experimental import pallas as pl
from jax.experimental.pallas import tpu as pltpu

F32 = jnp.float32
BF16 = jnp.bfloat16

D_MODEL = 1024
N_HEADS = 8
HEAD_DIM = 64
D_ATT = N_HEADS * HEAD_DIM
D_POOL = 512
POOL_WINDOWS = (2, 4, 8, 16)
POOL_HALO = 8
D_FF = 2816
N_MOD = 9
GRID_W = 64
NA_KH = 8
NA_KW = 16
EPS = 1e-6
NEG = -1e30

LANES = 128
N_PAIRS = D_ATT // LANES
FF_CHUNKS = ((0, 768), (768, 768), (1536, 768), (2304, 512))
FFN_TM = 512
CTX_TB = 2
LAT_ROWS = 4
LAT_KEY_ROWS = 12
N_CONV = 8
VMEM_LIMIT = 56 * 1024 * 1024

O_Q = D_POOL
O_K = D_POOL + D_ATT
O_V = D_POOL + 2 * D_ATT
O_G = D_POOL + 3 * D_ATT


def _dot(a, b):
    return jnp.dot(a, b, preferred_element_type=F32)


def _dot_nt(a, b):
    return lax.dot_general(a, b, (((1,), (1,)), ((), ())), preferred_element_type=F32)


def _rms_mod(x, g, shift, scale):
    ms = jnp.mean(x * x, axis=-1, keepdims=True)
    y = x * lax.rsqrt(ms + EPS) * g
    return y * (1.0 + scale) + shift


def _low_half():
    return lax.broadcasted_iota(jnp.int32, (1, LANES), 1) < HEAD_DIM


def _head_norm(t, gain2):
    lo = _low_half()
    sq = t * t
    s0 = jnp.sum(jnp.where(lo, sq, 0.0), axis=-1, keepdims=True)
    s1 = jnp.sum(jnp.where(lo, 0.0, sq), axis=-1, keepdims=True)
    r0 = lax.rsqrt(s0 * (1.0 / HEAD_DIM) + EPS)
    r1 = lax.rsqrt(s1 * (1.0 / HEAD_DIM) + EPS)
    return t * jnp.where(lo, r0, r1) * gain2


def _softmax_parts(parts):
    m = parts[0].max(axis=-1, keepdims=True)
    for s in parts[1:]:
        m = jnp.maximum(m, s.max(axis=-1, keepdims=True))
    es = [jnp.exp(s - m) for s in parts]
    l = es[0].sum(axis=-1, keepdims=True)
    for e in es[1:]:
        l = l + e.sum(axis=-1, keepdims=True)
    inv = 1.0 / l
    return [(e * inv).astype(BF16) for e in es]


def _pool_delta(p_ext, pos0, seq_len):
    te = p_ext.shape[0]
    tm = te - 2 * POOL_HALO
    t = pos0 + lax.broadcasted_iota(jnp.int32, (tm, 1), 0)
    outs = []
    for g, w in enumerate(POOL_WINDOWS):
        a = p_ext[:, g * LANES:(g + 1) * LANES]
        s = a
        step = 1
        while step < w:
            s = s + pltpu.roll(s, step, 0)
            step *= 2
        ahead = w // 2 - 1
        if ahead:
            s = pltpu.roll(s, te - ahead, 0)
        win = s[POOL_HALO:POOL_HALO + tm]
        cnt = (jnp.minimum(t + w // 2, seq_len) - jnp.maximum(t - w // 2, 0)).astype(F32)
        outs.append(win / cnt - a[POOL_HALO:POOL_HALO + tm])
    return jnp.concatenate(outs, axis=1)


def _pool_branch(d, wpool_ref, pscale_ref, wbrp_ref):
    y = _dot(d.astype(BF16), wpool_ref[...]) * pscale_ref[...]
    return _dot(y.astype(BF16), wbrp_ref[...])


def _merge_out(x, gate2, gl, a, b, wout_ref):
    gates = jax.nn.sigmoid(gl)
    merged = gates[:, :D_MODEL] * a + gates[:, D_MODEL:] * b
    return x + gate2 * _dot(merged.astype(BF16), wout_ref[...])


def _chunk_spec(w, col_block=None):
    rows = w.shape[0] // N_CONV
    assert rows * N_CONV == w.shape[0] and rows % 16 == 0
    width, cidx = col_block if col_block else (w.shape[1], 0)
    return pl.BlockSpec((rows, width), lambda s, *_: (jnp.minimum(s, N_CONV - 1), cidx))


def _convert_chunk(step, src_ref, dst_ref):
    rows = src_ref.shape[0]
    r0 = pl.multiple_of(step * rows, rows)
    dst_ref[pl.ds(r0, rows), :] = src_ref[...].astype(BF16)


def _const_spec(shape):
    nd = len(shape)
    return pl.BlockSpec(shape, lambda *_: (0,) * nd)


def _tile_index(step):
    return jnp.maximum(step - N_CONV, 0)


def _adaln_kernel(cond_ref, w_ref, b_ref, o_ref):
    c = cond_ref[...]
    s = c * jax.nn.sigmoid(c)
    o_ref[...] = _dot(s.astype(BF16), w_ref[...].astype(BF16)) + b_ref[...]


def _adaln(cond8, w_ada, b_ada):
    n = w_ada.shape[1]
    tn = 1152
    return pl.pallas_call(
        _adaln_kernel,
        grid=(n // tn,),
        in_specs=[pl.BlockSpec((8, D_MODEL), lambda i: (0, 0)),
                  pl.BlockSpec((D_MODEL, tn), lambda i: (0, i)),
                  pl.BlockSpec((1, tn), lambda i: (0, i))],
        out_specs=pl.BlockSpec((8, tn), lambda i: (0, i)),
        out_shape=jax.ShapeDtypeStruct((8, n), F32),
        compiler_params=pltpu.CompilerParams(dimension_semantics=("arbitrary",),
                                             vmem_limit_bytes=VMEM_LIMIT),
        name="adaln",
    )(cond8, w_ada, b_ada.reshape(1, n))


def _ffn_kernel(*refs, mod_off, n_a, emit_kv, split_out):
    refs = list(refs)
    xa_ref, xb_ref, mod_ref, g_ref, wgu32_ref, wdn32_ref = refs[:6]
    del refs[:6]
    if emit_kv:
        gm_ref, wkv32_ref, kg_ref = refs[:3]
        del refs[:3]
    out_refs = refs[:2 if split_out else 1]
    del refs[:len(out_refs)]
    if emit_kv:
        ko_ref, vo_ref = refs[:2]
        del refs[:2]
    wgu_ref, wdn_ref = refs[:2]
    wkv_ref = refs[2] if emit_kv else None

    step = pl.program_id(0)

    @pl.when(step < N_CONV)
    def _convert():
        _convert_chunk(step, wgu32_ref, wgu_ref)
        _convert_chunk(step, wdn32_ref, wdn_ref)
        if emit_kv:
            _convert_chunk(step, wkv32_ref, wkv_ref)

    @pl.when(step >= N_CONV)
    def _compute():
        is_a = step - N_CONV < n_a
        x = jnp.where(is_a, xa_ref[...], xb_ref[...])
        shift = mod_ref[0, mod_off:mod_off + 1, :]
        scale = mod_ref[0, mod_off + 1:mod_off + 2, :]
        gate = mod_ref[0, mod_off + 2:mod_off + 3, :]
        h = _rms_mod(x, g_ref[...], shift, scale).astype(BF16)
        acc = None
        for c0, cw in FF_CHUNKS:
            a = _dot(h, wgu_ref[:, c0:c0 + cw])
            u = _dot(h, wgu_ref[:, D_FF + c0:D_FF + c0 + cw])
            t = (a * jax.nn.sigmoid(a) * u).astype(BF16)
            part = _dot(t, wdn_ref[c0:c0 + cw, :])
            acc = part if acc is None else acc + part
        y = x + (0.5 * gate) * acc
        if split_out:
            @pl.when(is_a)
            def _store_a():
                out_refs[0][...] = y

            @pl.when(jnp.logical_not(is_a))
            def _store_b():
                out_refs[1][...] = y
        else:
            out_refs[0][...] = y

        if emit_kv:
            @pl.when(jnp.logical_not(is_a))
            def _kv():
                h2 = _rms_mod(y, gm_ref[...], mod_ref[0, 3:4, :], mod_ref[0, 4:5, :]).astype(BF16)
                k = _dot(h2, wkv_ref[:, :D_ATT])
                vo_ref[...] = _dot(h2, wkv_ref[:, D_ATT:]).astype(BF16)
                for pr in range(N_PAIRS):
                    lanes = slice(pr * LANES, (pr + 1) * LANES)
                    ko_ref[:, lanes] = _head_norm(k[:, lanes], kg_ref[...]).astype(BF16)


def _ffn(xa, xb, mod3, seq_b, mod_off, g, wgu, wdn, name, kv=None, split_out=False):
    tm = FFN_TM
    n_a, n_b = xa.shape[0] // tm, xb.shape[0] // tm
    emit_kv = kv is not None

    def tile(s):
        return _tile_index(s)

    a_map = lambda s: (jnp.minimum(tile(s), n_a - 1), 0)
    b_map = lambda s: (jnp.maximum(tile(s) - n_a, 0), 0)
    mod_map = lambda s: (jnp.where(tile(s) < n_a, 0, 1 + (jnp.maximum(tile(s) - n_a, 0) * tm) // seq_b), 0, 0)

    in_specs = [pl.BlockSpec((tm, D_MODEL), a_map), pl.BlockSpec((tm, D_MODEL), b_map),
                pl.BlockSpec((1, N_MOD, D_MODEL), mod_map), _const_spec((1, D_MODEL)),
                _chunk_spec(wgu), _chunk_spec(wdn)]
    args = [xa, xb, mod3, g, wgu, wdn]
    scratch = [pltpu.VMEM(wgu.shape, BF16), pltpu.VMEM(wdn.shape, BF16)]
    if emit_kv:
        g_mix, w_in, kg2 = kv
        in_specs += [_const_spec((1, D_MODEL)), _chunk_spec(w_in, (2 * D_ATT, O_K // (2 * D_ATT))),
                     _const_spec((1, LANES))]
        args += [g_mix, w_in, kg2]
        scratch.append(pltpu.VMEM((D_MODEL, 2 * D_ATT), BF16))
    if split_out:
        out_specs = [pl.BlockSpec((tm, D_MODEL), a_map), pl.BlockSpec((tm, D_MODEL), b_map)]
        out_shape = [jax.ShapeDtypeStruct(xa.shape, F32), jax.ShapeDtypeStruct(xb.shape, F32)]
    else:
        out_specs = [pl.BlockSpec((tm, D_MODEL), lambda s: (tile(s), 0))]
        out_shape = [jax.ShapeDtypeStruct((xa.shape[0] + xb.shape[0], D_MODEL), F32)]
    if emit_kv:
        out_specs += [pl.BlockSpec((tm, D_ATT), b_map)] * 2
        out_shape += [jax.ShapeDtypeStruct((xb.shape[0], D_ATT), BF16)] * 2
    return pl.pallas_call(
        functools.partial(_ffn_kernel, mod_off=mod_off, n_a=n_a, emit_kv=emit_kv, split_out=split_out),
        grid=(N_CONV + n_a + n_b,),
        in_specs=in_specs, out_specs=out_specs, out_shape=out_shape, scratch_shapes=scratch,
        compiler_params=pltpu.CompilerParams(dimension_semantics=("arbitrary",),
                                             vmem_limit_bytes=VMEM_LIMIT),
        name=name,
    )(*args)


def _convert_mixer_weights(step, wpool32_ref, wbrp32_ref, wbra32_ref, wout32_ref,
                           wpool_ref, wbrp_ref, wbra_ref, wout_ref):
    _convert_chunk(step, wbrp32_ref, wbrp_ref)
    _convert_chunk(step, wbra32_ref, wbra_ref)
    _convert_chunk(step, wout32_ref, wout_ref)

    @pl.when(step == 0)
    def _pool_weight():
        wpool_ref[...] = jnp.zeros(wpool_ref.shape, BF16)
        for g in range(len(POOL_WINDOWS)):
            blk = slice(g * LANES, (g + 1) * LANES)
            wpool_ref[blk, blk] = wpool32_ref[g].astype(BF16)


def _mixer_weight_scratch():
    return [pltpu.VMEM((D_POOL, D_POOL), BF16), pltpu.VMEM((D_POOL, D_MODEL), BF16),
            pltpu.VMEM((D_ATT, D_MODEL), BF16), pltpu.VMEM((D_MODEL, D_MODEL), BF16)]


def _ctx_mixer_kernel(x_ref, mod_ref, g_ref, win32_ref, qg_ref, kg_ref, wpool32_ref, pscale_ref,
                      wbrp32_ref, wbra32_ref, wout32_ref, o_ref, ko_ref, vo_ref,
                      win_ref, wpool_ref, wbrp_ref, wbra_ref, wout_ref, *, seq):
    step = pl.program_id(0)

    @pl.when(step < N_CONV)
    def _convert():
        _convert_chunk(step, win32_ref, win_ref)
        _convert_mixer_weights(step, wpool32_ref, wbrp32_ref, wbra32_ref, wout32_ref,
                               wpool_ref, wbrp_ref, wbra_ref, wout_ref)

    @pl.when(step >= N_CONV)
    def _compute():
        x = x_ref[...]
        m_rows = x.shape[0]
        h = _rms_mod(x, g_ref[...], mod_ref[0, 3:4, :], mod_ref[0, 4:5, :]).astype(BF16)
        p = _dot(h, win_ref[:, 0:O_Q])
        q = _dot(h, win_ref[:, O_Q:O_K])
        k = _dot(h, win_ref[:, O_K:O_V])
        v = _dot(h, win_ref[:, O_V:O_G])
        gl = _dot(h, win_ref[:, O_G:])
        vo_ref[...] = v

        lo = _low_half()
        scale = 1.0 / math.sqrt(HEAD_DIM)
        zeros_halo = jnp.zeros((POOL_HALO, D_POOL), F32)
        d_rows = []
        att_rows = []
        for e in range(m_rows // seq):
            rows = slice(e * seq, (e + 1) * seq)
            p_ext = jnp.concatenate([zeros_halo, p[rows], zeros_halo], axis=0)
            d_rows.append(_pool_delta(p_ext, 0, seq))
            att_pairs = []
            for pr in range(N_PAIRS):
                lanes = slice(pr * LANES, (pr + 1) * LANES)
                kn = _head_norm(k[rows, lanes], kg_ref[...])
                ko_ref[rows, lanes] = kn
                qn = _head_norm(q[rows, lanes], qg_ref[...]) * scale
                kb = kn.astype(BF16)
                vb = v[rows, lanes].astype(BF16)
                halves = []
                for half in range(2):
                    qm = jnp.where(lo, qn, 0.0) if half == 0 else jnp.where(lo, 0.0, qn)
                    s = _dot_nt(qm.astype(BF16), kb)
                    (pm,) = _softmax_parts([s])
                    halves.append(_dot(pm, vb))
                att_pairs.append(jnp.where(lo, halves[0], halves[1]))
            att_rows.append(jnp.concatenate(att_pairs, axis=1))
        d = jnp.concatenate(d_rows, axis=0)
        att = jnp.concatenate(att_rows, axis=0)
        a = _pool_branch(d, wpool_ref, pscale_ref, wbrp_ref)
        b = _dot(att.astype(BF16), wbra_ref[...])
        o_ref[...] = _merge_out(x, mod_ref[0, 5:6, :], gl, a, b, wout_ref)


def _ctx_mixer(x_all, n_tok, mod3, g, w_in, qg2, kg2, w_pool, pscale, wbrp, wbra, wout, seq):
    tm = CTX_TB * seq
    row = lambda s: (_tile_index(s), 0)
    return pl.pallas_call(
        functools.partial(_ctx_mixer_kernel, seq=seq),
        grid=(N_CONV + n_tok // tm,),
        in_specs=[pl.BlockSpec((tm, D_MODEL), row),
                  _const_spec((1, N_MOD, D_MODEL)), _const_spec((1, D_MODEL)),
                  _chunk_spec(w_in), _const_spec((1, LANES)), _const_spec((1, LANES)),
                  _const_spec(w_pool.shape), _const_spec((1, D_POOL)),
                  _chunk_spec(wbrp), _chunk_spec(wbra), _chunk_spec(wout)],
        out_specs=[pl.BlockSpec((tm, D_MODEL), row),
                   pl.BlockSpec((tm, D_ATT), row),
                   pl.BlockSpec((tm, D_ATT), row)],
        out_shape=[jax.ShapeDtypeStruct((n_tok, D_MODEL), F32),
                   jax.ShapeDtypeStruct((n_tok, D_ATT), F32),
                   jax.ShapeDtypeStruct((n_tok, D_ATT), F32)],
        scratch_shapes=[pltpu.VMEM(w_in.shape, BF16)] + _mixer_weight_scratch(),
        compiler_params=pltpu.CompilerParams(dimension_semantics=("arbitrary",),
                                             vmem_limit_bytes=VMEM_LIMIT),
        name="ctx_mixer",
    )(x_all, mod3, g, w_in, qg2, kg2, w_pool, pscale, wbrp, wbra, wout)


def _build_bias_table(rpb_ref, bias_ref):
    lane = lax.broadcasted_iota(jnp.int32, (GRID_W, LANES), 1)
    qc = lax.broadcasted_iota(jnp.int32, (GRID_W, LANES), 0)
    kc = lane & (GRID_W - 1)
    c0 = jnp.clip(qc - NA_KW // 2, 0, GRID_W - NA_KW)
    col_ok = (kc >= c0) & (kc < c0 + NA_KW)
    first_half = lane < GRID_W
    shift_a = LANES - (NA_KW - 1)
    shift_b = (GRID_W - (NA_KW - 1)) % LANES
    for h in range(N_HEADS):
        for i in range(2 * NA_KH - 2):
            va = jnp.broadcast_to(rpb_ref[h, i:i + 1, :], (GRID_W, LANES))
            vb = jnp.broadcast_to(rpb_ref[h, i + 1:i + 2, :], (GRID_W, LANES))
            ra = pltpu.roll(va, shift_a, 1, stride=1, stride_axis=0)
            rb = pltpu.roll(vb, shift_b, 1, stride=1, stride_axis=0)
            bias_ref[h, i] = jnp.where(col_ok, jnp.where(first_half, ra, rb), NEG)


def _lat_mixer_kernel(x_ref, xp_ref, xn_ref, mod_ref, g_ref, wpq32_ref, wg32_ref, qg_ref, k_ref, v_ref,
                      ck_ref, cv_ref, rpb_ref, wpool32_ref, pscale_ref, wbrp32_ref, wbra32_ref, wout32_ref,
                      o_ref, wpq_ref, wg_ref, bias_ref, wpool_ref, wbrp_ref, wbra_ref, wout_ref,
                      *, seq, n_rows):
    step = pl.program_id(0)

    @pl.when(step < N_CONV)
    def _convert():
        _convert_chunk(step, wpq32_ref, wpq_ref)
        _convert_chunk(step, wg32_ref, wg_ref)
        _convert_mixer_weights(step, wpool32_ref, wbrp32_ref, wbra32_ref, wout32_ref,
                               wpool_ref, wbrp_ref, wbra_ref, wout_ref)

        @pl.when(step == 0)
        def _bias():
            _build_bias_table(rpb_ref, bias_ref)

    @pl.when(step >= N_CONV)
    def _compute():
        tm = LAT_ROWS * GRID_W
        n_keys = LAT_KEY_ROWS * GRID_W
        j = lax.rem(step - N_CONV, seq // tm)
        x = x_ref[...]
        g = g_ref[...]
        shift = mod_ref[0, 3:4, :]
        scale_mod = mod_ref[0, 4:5, :]
        h = _rms_mod(x, g, shift, scale_mod).astype(BF16)
        x_halo = jnp.concatenate([xp_ref[...], xn_ref[...]], axis=0)
        h_halo = _rms_mod(x_halo, g, shift, scale_mod).astype(BF16)

        p_halo = _dot(h_halo, wpq_ref[:, :D_POOL])
        p_ext = jnp.concatenate([p_halo[:POOL_HALO], _dot(h, wpq_ref[:, :D_POOL]), p_halo[POOL_HALO:]], axis=0)
        pos = j * tm - POOL_HALO + lax.broadcasted_iota(jnp.int32, (tm + 2 * POOL_HALO, 1), 0)
        p_ext = jnp.where((pos >= 0) & (pos < seq), p_ext, 0.0)
        d = _pool_delta(p_ext, j * tm, seq)
        a = _pool_branch(d, wpool_ref, pscale_ref, wbrp_ref)

        q = _dot(h, wpq_ref[:, D_POOL:])
        gl = _dot(h, wg_ref[...])

        r0q = j * LAT_ROWS
        u0 = jnp.clip(r0q - NA_KH // 2, 0, n_rows - LAT_KEY_ROWS)
        key0 = pl.multiple_of(u0 * GRID_W, GRID_W)
        log2_w = GRID_W.bit_length() - 1
        q_row = r0q + jnp.right_shift(lax.broadcasted_iota(jnp.int32, (tm, 1), 0), log2_w)
        k_row = u0 + jnp.right_shift(lax.broadcasted_iota(jnp.int32, (1, n_keys), 1), log2_w)
        first = jnp.clip(q_row - NA_KH // 2, 0, n_rows - NA_KH)
        row_ok = (k_row >= first) & (k_row < first + NA_KH)

        lo = _low_half()
        scale = 1.0 / math.sqrt(HEAD_DIM)
        att_pairs = []
        for pr in range(N_PAIRS):
            lanes = slice(pr * LANES, (pr + 1) * LANES)
            qn = _head_norm(q[:, lanes], qg_ref[...]) * scale
            kw = k_ref[0, pl.ds(key0, n_keys), lanes]
            vw = v_ref[0, pl.ds(key0, n_keys), lanes]
            kc = ck_ref[0, :, lanes]
            vc = cv_ref[0, :, lanes]
            halves = []
            for half in range(2):
                head = 2 * pr + half
                qm = (jnp.where(lo, qn, 0.0) if half == 0 else jnp.where(lo, 0.0, qn)).astype(BF16)
                bias_rows = []
                for jr in range(LAT_ROWS):
                    blocks = []
                    for pi in range(LAT_KEY_ROWS // 2):
                        idx = jnp.clip(u0 - r0q + 2 * pi - jr + NA_KH - 1, 0, 2 * NA_KH - 3)
                        blocks.append(bias_ref[head, idx])
                    bias_rows.append(jnp.concatenate(blocks, axis=1))
                bias = jnp.concatenate(bias_rows, axis=0)
                s_loc = jnp.where(row_ok, _dot_nt(qm, kw) + bias, NEG)
                s_ctx = _dot_nt(qm, kc)
                p_loc, p_ctx = _softmax_parts([s_loc, s_ctx])
                halves.append(_dot(p_loc, vw) + _dot(p_ctx, vc))
            att_pairs.append(jnp.where(lo, halves[0], halves[1]))
        att = jnp.concatenate(att_pairs, axis=1)
        b = _dot(att.astype(BF16), wbra_ref[...])
        o_ref[...] = _merge_out(x, mod_ref[0, 5:6, :], gl, a, b, wout_ref)


def _lat_mixer(x_all, tok0, n_batch, seq, mod3, g, w_in, qg2, k3, v3, ck3, cv3, rpb_pad, w_pool, pscale,
               wbrp, wbra, wout):
    tm = LAT_ROWS * GRID_W
    n_rows = seq // GRID_W
    nb = seq // tm
    tile0 = tok0 // tm
    halo0 = tok0 // POOL_HALO
    halo_per_tile = tm // POOL_HALO
    halo_per_seq = seq // POOL_HALO
    tile = _tile_index
    batch = lambda s: tile(s) // nb

    def prev_map(s):
        return (halo0 + jnp.maximum(tile(s) * halo_per_tile - 1, batch(s) * halo_per_seq), 0)

    def next_map(s):
        return (halo0 + jnp.minimum((tile(s) + 1) * halo_per_tile, (batch(s) + 1) * halo_per_seq - 1), 0)

    seq_map = lambda s: (batch(s), 0, 0)
    return pl.pallas_call(
        functools.partial(_lat_mixer_kernel, seq=seq, n_rows=n_rows),
        grid=(N_CONV + n_batch * nb,),
        in_specs=[pl.BlockSpec((tm, D_MODEL), lambda s: (tile0 + tile(s), 0)),
                  pl.BlockSpec((POOL_HALO, D_MODEL), prev_map),
                  pl.BlockSpec((POOL_HALO, D_MODEL), next_map),
                  pl.BlockSpec((1, N_MOD, D_MODEL), lambda s: (1 + batch(s), 0, 0)),
                  _const_spec((1, D_MODEL)),
                  _chunk_spec(w_in, (O_K, 0)), _chunk_spec(w_in, (2 * D_MODEL, O_G // (2 * D_MODEL))),
                  _const_spec((1, LANES)),
                  pl.BlockSpec((1, seq, D_ATT), seq_map), pl.BlockSpec((1, seq, D_ATT), seq_map),
                  pl.BlockSpec((1,) + ck3.shape[1:], seq_map), pl.BlockSpec((1,) + cv3.shape[1:], seq_map),
                  _const_spec(rpb_pad.shape),
                  _const_spec(w_pool.shape), _const_spec((1, D_POOL)),
                  _chunk_spec(wbrp), _chunk_spec(wbra), _chunk_spec(wout)],
        out_specs=pl.BlockSpec((tm, D_MODEL), lambda s: (tile(s), 0)),
        out_shape=jax.ShapeDtypeStruct((n_batch * seq, D_MODEL), F32),
        scratch_shapes=[pltpu.VMEM((D_MODEL, O_K), BF16), pltpu.VMEM((D_MODEL, 2 * D_MODEL), BF16),
                        pltpu.VMEM((N_HEADS, 2 * NA_KH - 2, GRID_W, LANES), F32)] + _mixer_weight_scratch(),
        compiler_params=pltpu.CompilerParams(dimension_semantics=("arbitrary",),
                                             vmem_limit_bytes=VMEM_LIMIT),
        name="lat_mixer",
    )(x_all, x_all, x_all, mod3, g, w_in, w_in, qg2, k3, v3, ck3, cv3, rpb_pad, w_pool, pscale,
      wbrp, wbra, wout)


def kernel(x_prompt, x_sample, cache_k, cache_v, c, c_ctx, w_ada, b_ada, g_ff1, w_ff1_in, w_ff1_out,
           g_mix, w_in, q_gain, k_gain, w_pool, pool_scale, rpb, w_br_pool, w_br_att, w_out, g_ff2,
           w_ff2_in, w_ff2_out):
    n_ctx, seq_ctx, _ = x_prompt.shape
    n_lat, seq_lat, _ = x_sample.shape
    depth = w_ada.shape[0]
    assert depth == 1 and n_lat + 1 <= 8
    l = 0

    xp = x_prompt.reshape(n_ctx * seq_ctx, D_MODEL)
    xs = x_sample.reshape(n_lat * seq_lat, D_MODEL)
    t_ctx = xp.shape[0]
    cond8 = jnp.zeros((8, D_MODEL), F32).at[0].set(c_ctx).at[1:1 + n_lat].set(c)
    mod3 = _adaln(cond8, w_ada[l], b_ada[l]).reshape(8, N_MOD, D_MODEL)

    pscale = pool_scale[l].reshape(1, D_POOL)
    qg2 = jnp.tile(q_gain[l], 2).reshape(1, LANES)
    kg2 = jnp.tile(k_gain[l], 2).reshape(1, LANES)
    g1, gm, g2 = g_ff1[l].reshape(1, -1), g_mix[l].reshape(1, -1), g_ff2[l].reshape(1, -1)
    rpb_pad = jnp.pad(rpb[l], ((0, 0), (0, 0), (0, LANES - rpb.shape[-1])))
    ck3 = cache_k[:, l].reshape(n_lat, -1, D_ATT).astype(BF16)
    cv3 = cache_v[:, l].reshape(n_lat, -1, D_ATT).astype(BF16)

    x1, k_s, v_s = _ffn(xp, xs, mod3, seq_lat, 0, g1, w_ff1_in[l], w_ff1_out[l], "ffn1",
                        kv=(gm, w_in[l], kg2))
    x2p, k_p, v_p = _ctx_mixer(x1, t_ctx, mod3, gm, w_in[l], qg2, kg2, w_pool[l], pscale,
                               w_br_pool[l], w_br_att[l], w_out[l], seq_ctx)
    x2s = _lat_mixer(x1, t_ctx, n_lat, seq_lat, mod3, gm, w_in[l], qg2,
                     k_s.reshape(n_lat, seq_lat, D_ATT), v_s.reshape(n_lat, seq_lat, D_ATT), ck3, cv3,
                     rpb_pad, w_pool[l], pscale, w_br_pool[l], w_br_att[l], w_out[l])
    yp, ys = _ffn(x2p, x2s, mod3, seq_lat, 6, g2, w_ff2_in[l], w_ff2_out[l], "ffn2", split_out=True)

    new_k = k_p.reshape(n_ctx, 1, seq_ctx, N_HEADS, HEAD_DIM)
    new_v = v_p.reshape(n_ctx, 1, seq_ctx, N_HEADS, HEAD_DIM)
    return (yp.reshape(x_prompt.shape), ys.reshape(x_sample.shape), new_k, new_v)
```

```python
import functools
import math

import jax
import jax.numpy as jnp
from jax import lax
from jax.experimental import pallas as pl
from jax.experimental.pallas import tpu as pltpu

F32 = jnp.float32
BF16 = jnp.bfloat16

D_MODEL = 1024
N_HEADS = 8
HEAD_DIM = 64
D_ATT = N_HEADS * HEAD_DIM
D_POOL = 512
POOL_WINDOWS = (2, 4, 8, 16)
POOL_HALO = 8
D_FF = 2816
N_MOD = 9
GRID_W = 64
NA_KH = 8
NA_KW = 16
EPS = 1e-6
NEG = -1e30

LANES = 128
N_PAIRS = D_ATT // LANES
MXU_COLS = 256
FF_CHUNKS = ((0, 768), (768, 768), (1536, 768), (2304, 512))
FFN_TM = 512
CTX_TB = 2
LAT_ROWS = 4
LAT_KEY_ROWS = 12
N_CONV = 8
VMEM_LIMIT = 56 * 1024 * 1024

O_Q = D_POOL
O_K = D_POOL + D_ATT
O_V = D_POOL + 2 * D_ATT
O_G = D_POOL + 3 * D_ATT


def _dot(a, b):
    return jnp.dot(a, b, preferred_element_type=F32)


def _dot_nt(a, b):
    return lax.dot_general(a, b, (((1,), (1,)), ((), ())), preferred_element_type=F32)


def _rms_mod(x, g, shift, scale):
    ms = jnp.mean(x * x, axis=-1, keepdims=True)
    y = x * lax.rsqrt(ms + EPS) * g
    return y * (1.0 + scale) + shift


def _low_half():
    return lax.broadcasted_iota(jnp.int32, (1, LANES), 1) < HEAD_DIM


def _head_norm(t, gain2):
    lo = _low_half()
    sq = t * t
    s0 = jnp.sum(jnp.where(lo, sq, 0.0), axis=-1, keepdims=True)
    s1 = jnp.sum(jnp.where(lo, 0.0, sq), axis=-1, keepdims=True)
    r0 = lax.rsqrt(s0 * (1.0 / HEAD_DIM) + EPS)
    r1 = lax.rsqrt(s1 * (1.0 / HEAD_DIM) + EPS)
    return t * jnp.where(lo, r0, r1) * gain2


def _softmax_parts(parts):
    m = parts[0].max(axis=-1, keepdims=True)
    for s in parts[1:]:
        m = jnp.maximum(m, s.max(axis=-1, keepdims=True))
    es = [jnp.exp(s - m) for s in parts]
    l = es[0].sum(axis=-1, keepdims=True)
    for e in es[1:]:
        l = l + e.sum(axis=-1, keepdims=True)
    inv = 1.0 / l
    return [(e * inv).astype(BF16) for e in es]


def _pool_delta(p_ext, pos0, seq_len):
    te = p_ext.shape[0]
    tm = te - 2 * POOL_HALO
    t = pos0 + lax.broadcasted_iota(jnp.int32, (tm, 1), 0)
    outs = []
    for g, w in enumerate(POOL_WINDOWS):
        a = p_ext[:, g * LANES:(g + 1) * LANES]
        s = a
        step = 1
        while step < w:
            s = s + pltpu.roll(s, step, 0)
            step *= 2
        ahead = w // 2 - 1
        if ahead:
            s = pltpu.roll(s, te - ahead, 0)
        win = s[POOL_HALO:POOL_HALO + tm]
        cnt = (jnp.minimum(t + w // 2, seq_len) - jnp.maximum(t - w // 2, 0)).astype(F32)
        outs.append(win / cnt - a[POOL_HALO:POOL_HALO + tm])
    return jnp.concatenate(outs, axis=1)


def _pool_branch(d, wpool_ref, pscale_ref, wbrp_ref):
    y = _dot(d.astype(BF16), wpool_ref[...]) * pscale_ref[...]
    return _dot(y.astype(BF16), wbrp_ref[...])


def _merge_out(x, gate2, gl, a, b, wout_ref):
    gates = jax.nn.sigmoid(gl)
    merged = gates[:, :D_MODEL] * a + gates[:, D_MODEL:] * b
    return x + gate2 * _dot(merged.astype(BF16), wout_ref[...])


def _chunk_spec(w, col_block=None):
    rows = w.shape[0] // N_CONV
    assert rows * N_CONV == w.shape[0] and rows % 16 == 0
    width, cidx = col_block if col_block else (w.shape[1], 0)
    return pl.BlockSpec((rows, width), lambda s, *_: (jnp.minimum(s, N_CONV - 1), cidx))


def _convert_chunk(step, src_ref, dst_ref):
    rows = src_ref.shape[0]
    r0 = pl.multiple_of(step * rows, rows)
    dst_ref[pl.ds(r0, rows), :] = src_ref[...].astype(BF16)


def _const_spec(shape):
    nd = len(shape)
    return pl.BlockSpec(shape, lambda *_: (0,) * nd)


def _tile_index(step):
    return jnp.maximum(step - N_CONV, 0)


def _adaln_kernel(cond_ref, w_ref, b_ref, o_ref):
    c = cond_ref[...]
    s = c * jax.nn.sigmoid(c)
    o_ref[...] = _dot(s.astype(BF16), w_ref[...].astype(BF16)) + b_ref[...]


def _adaln(cond8, w_ada, b_ada):
    n = w_ada.shape[1]
    tn = 1152
    return pl.pallas_call(
        _adaln_kernel,
        grid=(n // tn,),
        in_specs=[pl.BlockSpec((8, D_MODEL), lambda i: (0, 0)),
                  pl.BlockSpec((D_MODEL, tn), lambda i: (0, i)),
                  pl.BlockSpec((1, tn), lambda i: (0, i))],
        out_specs=pl.BlockSpec((8, tn), lambda i: (0, i)),
        out_shape=jax.ShapeDtypeStruct((8, n), F32),
        compiler_params=pltpu.CompilerParams(dimension_semantics=("arbitrary",),
                                             vmem_limit_bytes=VMEM_LIMIT),
        name="adaln",
    )(cond8, w_ada, b_ada.reshape(1, n))


def _ffn_kernel(*refs, mod_off, n_a, emit_kv, split_out):
    refs = list(refs)
    xa_ref, xb_ref, mod_ref, g_ref, wgu32_ref, wdn32_ref = refs[:6]
    del refs[:6]
    if emit_kv:
        gm_ref, wkv32_ref, kg_ref = refs[:3]
        del refs[:3]
    out_refs = refs[:2 if split_out else 1]
    del refs[:len(out_refs)]
    if emit_kv:
        ko_ref, vo_ref = refs[:2]
        del refs[:2]
    wgu_ref, wdn_ref = refs[:2]
    wkv_ref = refs[2] if emit_kv else None

    step = pl.program_id(0)

    @pl.when(step < N_CONV)
    def _convert():
        _convert_chunk(step, wgu32_ref, wgu_ref)
        _convert_chunk(step, wdn32_ref, wdn_ref)
        if emit_kv:
            _convert_chunk(step, wkv32_ref, wkv_ref)

    @pl.when(step >= N_CONV)
    def _compute():
        is_a = step - N_CONV < n_a
        x = jnp.where(is_a, xa_ref[...], xb_ref[...])
        shift = mod_ref[0, mod_off:mod_off + 1, :]
        scale = mod_ref[0, mod_off + 1:mod_off + 2, :]
        gate = mod_ref[0, mod_off + 2:mod_off + 3, :]
        h = _rms_mod(x, g_ref[...], shift, scale).astype(BF16)
        acc = None
        for c0, cw in FF_CHUNKS:
            a = _dot(h, wgu_ref[:, c0:c0 + cw])
            u = _dot(h, wgu_ref[:, D_FF + c0:D_FF + c0 + cw])
            t = (a * jax.nn.sigmoid(a) * u).astype(BF16)
            part = _dot(t, wdn_ref[c0:c0 + cw, :])
            acc = part if acc is None else acc + part
        y = x + (0.5 * gate) * acc
        if split_out:
            @pl.when(is_a)
            def _store_a():
                out_refs[0][...] = y

            @pl.when(jnp.logical_not(is_a))
            def _store_b():
                out_refs[1][...] = y
        else:
            out_refs[0][...] = y

        if emit_kv:
            @pl.when(jnp.logical_not(is_a))
            def _kv():
                h2 = _rms_mod(y, gm_ref[...], mod_ref[0, 3:4, :], mod_ref[0, 4:5, :]).astype(BF16)
                k = _dot(h2, wkv_ref[:, :D_ATT])
                vo_ref[...] = _dot(h2, wkv_ref[:, D_ATT:]).astype(BF16)
                for pr in range(N_PAIRS):
                    lanes = slice(pr * LANES, (pr + 1) * LANES)
                    ko_ref[:, lanes] = _head_norm(k[:, lanes], kg_ref[...]).astype(BF16)


def _ffn(xa, xb, mod3, seq_b, mod_off, g, wgu, wdn, name, kv=None, split_out=False):
    tm = FFN_TM
    n_a, n_b = xa.shape[0] // tm, xb.shape[0] // tm
    emit_kv = kv is not None

    def tile(s):
        return _tile_index(s)

    a_map = lambda s: (jnp.minimum(tile(s), n_a - 1), 0)
    b_map = lambda s: (jnp.maximum(tile(s) - n_a, 0), 0)
    mod_map = lambda s: (jnp.where(tile(s) < n_a, 0, 1 + (jnp.maximum(tile(s) - n_a, 0) * tm) // seq_b), 0, 0)

    in_specs = [pl.BlockSpec((tm, D_MODEL), a_map), pl.BlockSpec((tm, D_MODEL), b_map),
                pl.BlockSpec((1, N_MOD, D_MODEL), mod_map), _const_spec((1, D_MODEL)),
                _chunk_spec(wgu), _chunk_spec(wdn)]
    args = [xa, xb, mod3, g, wgu, wdn]
    scratch = [pltpu.VMEM(wgu.shape, BF16), pltpu.VMEM(wdn.shape, BF16)]
    if emit_kv:
        g_mix, w_in, kg2 = kv
        in_specs += [_const_spec((1, D_MODEL)), _chunk_spec(w_in, (2 * D_ATT, O_K // (2 * D_ATT))),
                     _const_spec((1, LANES))]
        args += [g_mix, w_in, kg2]
        scratch.append(pltpu.VMEM((D_MODEL, 2 * D_ATT), BF16))
    if split_out:
        out_specs = [pl.BlockSpec((tm, D_MODEL), a_map), pl.BlockSpec((tm, D_MODEL), b_map)]
        out_shape = [jax.ShapeDtypeStruct(xa.shape, F32), jax.ShapeDtypeStruct(xb.shape, F32)]
    else:
        out_specs = [pl.BlockSpec((tm, D_MODEL), lambda s: (tile(s), 0))]
        out_shape = [jax.ShapeDtypeStruct((xa.shape[0] + xb.shape[0], D_MODEL), F32)]
    if emit_kv:
        out_specs += [pl.BlockSpec((tm, D_ATT), b_map)] * 2
        out_shape += [jax.ShapeDtypeStruct((xb.shape[0], D_ATT), BF16)] * 2
    return pl.pallas_call(
        functools.partial(_ffn_kernel, mod_off=mod_off, n_a=n_a, emit_kv=emit_kv, split_out=split_out),
        grid=(N_CONV + n_a + n_b,),
        in_specs=in_specs, out_specs=out_specs, out_shape=out_shape, scratch_shapes=scratch,
        compiler_params=pltpu.CompilerParams(dimension_semantics=("arbitrary",),
                                             vmem_limit_bytes=VMEM_LIMIT),
        name=name,
    )(*args)


def _convert_mixer_weights(step, wpool32_ref, wbrp32_ref, wbra32_ref, wout32_ref,
                           wpool_ref, wbrp_ref, wbra_ref, wout_ref):
    _convert_chunk(step, wbrp32_ref, wbrp_ref)
    _convert_chunk(step, wbra32_ref, wbra_ref)
    _convert_chunk(step, wout32_ref, wout_ref)

    @pl.when(step == 0)
    def _pool_weight():
        wpool_ref[...] = jnp.zeros(wpool_ref.shape, BF16)
        for g in range(len(POOL_WINDOWS)):
            blk = slice(g * LANES, (g + 1) * LANES)
            wpool_ref[blk, blk] = wpool32_ref[g].astype(BF16)


def _mixer_weight_scratch():
    return [pltpu.VMEM((D_POOL, D_POOL), BF16), pltpu.VMEM((D_POOL, D_MODEL), BF16),
            pltpu.VMEM((D_ATT, D_MODEL), BF16), pltpu.VMEM((D_MODEL, D_MODEL), BF16)]


def _ctx_mixer_kernel(x_ref, mod_ref, g_ref, win32_ref, qg_ref, kg_ref, wpool32_ref, pscale_ref,
                      wbrp32_ref, wbra32_ref, wout32_ref, o_ref, ko_ref, vo_ref,
                      win_ref, wpool_ref, wbrp_ref, wbra_ref, wout_ref, *, seq):
    step = pl.program_id(0)

    @pl.when(step < N_CONV)
    def _convert():
        _convert_chunk(step, win32_ref, win_ref)
        _convert_mixer_weights(step, wpool32_ref, wbrp32_ref, wbra32_ref, wout32_ref,
                               wpool_ref, wbrp_ref, wbra_ref, wout_ref)

    @pl.when(step >= N_CONV)
    def _compute():
        x = x_ref[...]
        m_rows = x.shape[0]
        h = _rms_mod(x, g_ref[...], mod_ref[0, 3:4, :], mod_ref[0, 4:5, :]).astype(BF16)
        p = _dot(h, win_ref[:, 0:O_Q])
        q = _dot(h, win_ref[:, O_Q:O_K])
        k = _dot(h, win_ref[:, O_K:O_V])
        v = _dot(h, win_ref[:, O_V:O_G])
        gl = _dot(h, win_ref[:, O_G:])
        vo_ref[...] = v

        lo = _low_half()
        scale = 1.0 / math.sqrt(HEAD_DIM)
        zeros_halo = jnp.zeros((POOL_HALO, D_POOL), F32)
        d_rows = []
        att_rows = []
        for e in range(m_rows // seq):
            rows = slice(e * seq, (e + 1) * seq)
            p_ext = jnp.concatenate([zeros_halo, p[rows], zeros_halo], axis=0)
            d_rows.append(_pool_delta(p_ext, 0, seq))
            att_pairs = []
            for pr in range(N_PAIRS):
                lanes = slice(pr * LANES, (pr + 1) * LANES)
                kn = _head_norm(k[rows, lanes], kg_ref[...])
                ko_ref[rows, lanes] = kn
                qn = _head_norm(q[rows, lanes], qg_ref[...]) * scale
                kb = kn.astype(BF16)
                vb = v[rows, lanes].astype(BF16)
                halves = []
                for half in range(2):
                    qm = jnp.where(lo, qn, 0.0) if half == 0 else jnp.where(lo, 0.0, qn)
                    s = _dot_nt(qm.astype(BF16), kb)
                    (pm,) = _softmax_parts([s])
                    halves.append(_dot(pm, vb))
                att_pairs.append(jnp.where(lo, halves[0], halves[1]))
            att_rows.append(jnp.concatenate(att_pairs, axis=1))
        d = jnp.concatenate(d_rows, axis=0)
        att = jnp.concatenate(att_rows, axis=0)
        a = _pool_branch(d, wpool_ref, pscale_ref, wbrp_ref)
        b = _dot(att.astype(BF16), wbra_ref[...])
        o_ref[...] = _merge_out(x, mod_ref[0, 5:6, :], gl, a, b, wout_ref)


def _ctx_mixer(x_all, n_tok, mod3, g, w_in, qg2, kg2, w_pool, pscale, wbrp, wbra, wout, seq):
    tm = CTX_TB * seq
    row = lambda s: (_tile_index(s), 0)
    return pl.pallas_call(
        functools.partial(_ctx_mixer_kernel, seq=seq),
        grid=(N_CONV + n_tok // tm,),
        in_specs=[pl.BlockSpec((tm, D_MODEL), row),
                  _const_spec((1, N_MOD, D_MODEL)), _const_spec((1, D_MODEL)),
                  _chunk_spec(w_in), _const_spec((1, LANES)), _const_spec((1, LANES)),
                  _const_spec(w_pool.shape), _const_spec((1, D_POOL)),
                  _chunk_spec(wbrp), _chunk_spec(wbra), _chunk_spec(wout)],
        out_specs=[pl.BlockSpec((tm, D_MODEL), row),
                   pl.BlockSpec((tm, D_ATT), row),
                   pl.BlockSpec((tm, D_ATT), row)],
        out_shape=[jax.ShapeDtypeStruct((n_tok, D_MODEL), F32),
                   jax.ShapeDtypeStruct((n_tok, D_ATT), F32),
                   jax.ShapeDtypeStruct((n_tok, D_ATT), F32)],
        scratch_shapes=[pltpu.VMEM(w_in.shape, BF16)] + _mixer_weight_scratch(),
        compiler_params=pltpu.CompilerParams(dimension_semantics=("arbitrary",),
                                             vmem_limit_bytes=VMEM_LIMIT),
        name="ctx_mixer",
    )(x_all, mod3, g, w_in, qg2, kg2, w_pool, pscale, wbrp, wbra, wout)


def _build_bias_table(rpb_ref, bias_ref):
    lane = lax.broadcasted_iota(jnp.int32, (GRID_W, LANES), 1)
    qc = lax.broadcasted_iota(jnp.int32, (GRID_W, LANES), 0)
    kc = lane & (GRID_W - 1)
    c0 = jnp.clip(qc - NA_KW // 2, 0, GRID_W - NA_KW)
    col_ok = (kc >= c0) & (kc < c0 + NA_KW)
    first_half = lane < GRID_W
    shift_a = LANES - (NA_KW - 1)
    shift_b = (GRID_W - (NA_KW - 1)) % LANES
    for h in range(N_HEADS):
        for i in range(2 * NA_KH - 2):
            va = jnp.broadcast_to(rpb_ref[h, i:i + 1, :], (GRID_W, LANES))
            vb = jnp.broadcast_to(rpb_ref[h, i + 1:i + 2, :], (GRID_W, LANES))
            ra = pltpu.roll(va, shift_a, 1, stride=1, stride_axis=0)
            rb = pltpu.roll(vb, shift_b, 1, stride=1, stride_axis=0)
            bias_ref[h, i] = jnp.where(col_ok, jnp.where(first_half, ra, rb), NEG)


LAT_TM = LAT_ROWS * GRID_W


def _lat_project_stages(x, halo_prev, halo_next, j, mod_ref, g_ref, qg_ref, pscale_ref, wpq_ref, wg_ref,
                        wpool_ref, wbrp_ref, *, seq):
    st = {"gates": []}
    half_pool = D_MODEL // 2
    half_att = D_ATT // 2

    def gate_blocks(lo_blk, hi_blk):
        for blk in range(lo_blk, hi_blk):
            cols = slice(blk * MXU_COLS, (blk + 1) * MXU_COLS)
            st["gates"].append(jax.nn.sigmoid(_dot(st["h"], wg_ref[:, cols])))

    def queries(lo_col, hi_col):
        q = _dot(st["h"], wpq_ref[:, D_POOL + lo_col:D_POOL + hi_col])
        scale = 1.0 / math.sqrt(HEAD_DIM)
        return [(_head_norm(q[:, c:c + LANES], qg_ref[...]) * scale).astype(BF16)
                for c in range(0, hi_col - lo_col, LANES)]

    def piece0():
        g = g_ref[...]
        shift = mod_ref[0, 3:4, :]
        scale_mod = mod_ref[0, 4:5, :]
        st["h"] = _rms_mod(x, g, shift, scale_mod).astype(BF16)
        st["h_halo"] = _rms_mod(jnp.concatenate([halo_prev, halo_next], axis=0), g, shift,
                                scale_mod).astype(BF16)

    def piece1():
        p_halo = _dot(st.pop("h_halo"), wpq_ref[:, :D_POOL])
        st["p_ext"] = jnp.concatenate(
            [p_halo[:POOL_HALO], _dot(st["h"], wpq_ref[:, :D_POOL]), p_halo[POOL_HALO:]], axis=0)
        st["qn"] = queries(0, half_att)

    def piece2():
        pos = j * LAT_TM - POOL_HALO + lax.broadcasted_iota(jnp.int32, (LAT_TM + 2 * POOL_HALO, 1), 0)
        p_ext = jnp.where((pos >= 0) & (pos < seq), st.pop("p_ext"), 0.0)
        st["d"] = _pool_delta(p_ext, j * LAT_TM, seq).astype(BF16)
        st["qn"] = jnp.concatenate(st["qn"] + queries(half_att, D_ATT), axis=1)
        gate_blocks(0, 1)

    def piece3():
        st["y"] = (_dot(st.pop("d"), wpool_ref[...]) * pscale_ref[...]).astype(BF16)
        gate_blocks(1, 2)

    def piece4():
        y = st.pop("y")
        st["a"] = jnp.concatenate([_dot(y, wbrp_ref[:, :half_pool]), _dot(y, wbrp_ref[:, half_pool:])], axis=1)

    def piece5():
        gate_blocks(2, 4)
        st["ga"] = jnp.concatenate(st["gates"][:4], axis=1) * st.pop("a")

    def piece7():
        gate_blocks(6, 8)
        st.pop("h")
        st["gb"] = jnp.concatenate(st.pop("gates")[4:], axis=1)

    return [piece0, piece1, piece2, piece3, piece4, piece5, functools.partial(gate_blocks, 4, 6), piece7], st


def _lat_attention(qn, j, fillers, k_ref, v_ref, ck_ref, cv_ref, bias_ref, *, n_rows):
    n_keys = LAT_KEY_ROWS * GRID_W
    r0q = j * LAT_ROWS
    u0 = jnp.clip(r0q - NA_KH // 2, 0, n_rows - LAT_KEY_ROWS)
    key0 = pl.multiple_of(u0 * GRID_W, GRID_W)
    log2_w = GRID_W.bit_length() - 1
    lo = _low_half()
    zero = jnp.zeros((), BF16)
    one = jnp.ones((), BF16)

    lane_row = lax.broadcasted_iota(jnp.int32, (1, LANES), 1) & (HEAD_DIM - 1)
    q_tile_row = jnp.right_shift(lax.broadcasted_iota(jnp.int32, (LAT_TM, 1), 0), log2_w)
    q_hot = jnp.where(lane_row == q_tile_row, 1.0, 0.0).astype(BF16)
    k_row = u0 + jnp.right_shift(lax.broadcasted_iota(jnp.int32, (n_keys, 1), 0), log2_w)
    first = jnp.clip(r0q + lane_row - NA_KH // 2, 0, n_rows - NA_KH)
    outside = (lane_row < LAT_ROWS) & ((k_row < first) | (k_row >= first + NA_KH))
    k_mask = jnp.where(outside, NEG, 0.0).astype(BF16)

    def own(head, mine, other):
        return jnp.where(lo, mine, other) if head % 2 == 0 else jnp.where(lo, other, mine)

    def scores(head):
        lanes = slice((head // 2) * LANES, (head // 2 + 1) * LANES)
        qp = qn[:, lanes]
        s_loc = _dot_nt(own(head, qp, q_hot), own(head, k_ref[0, pl.ds(key0, n_keys), lanes], k_mask))
        return s_loc, _dot_nt(own(head, qp, zero), ck_ref[0, :, lanes])

    def weighted_values(head, s_loc, s_ctx):
        lanes = slice((head // 2) * LANES, (head // 2 + 1) * LANES)
        bias_rows = []
        for jr in range(LAT_ROWS):
            blocks = []
            for pi in range(LAT_KEY_ROWS // 2):
                idx = jnp.clip(u0 - r0q + 2 * pi - jr + NA_KH - 1, 0, 2 * NA_KH - 3)
                blocks.append(bias_ref[head, idx])
            bias_rows.append(jnp.concatenate(blocks, axis=1))
        s_loc = s_loc + jnp.concatenate(bias_rows, axis=0)
        m = jnp.maximum(s_loc.max(axis=-1, keepdims=True), s_ctx.max(axis=-1, keepdims=True))
        e_loc = jnp.exp(s_loc - m).astype(BF16)
        e_ctx = jnp.exp(s_ctx - m).astype(BF16)
        return (_dot(e_loc, own(head, v_ref[0, pl.ds(key0, n_keys), lanes], one))
                + _dot(e_ctx, own(head, cv_ref[0, :, lanes], one)))

    outs = []
    cur = scores(0)
    for head in range(N_HEADS):
        nxt = scores(head + 1) if head + 1 < N_HEADS else None
        fillers[head]()
        outs.append(weighted_values(head, *cur))
        cur = nxt
    att_pairs = []
    for pr in range(N_PAIRS):
        even, odd = outs[2 * pr], outs[2 * pr + 1]
        num = jnp.where(lo, even, odd)
        den = pltpu.roll(jnp.where(lo, odd, even), HEAD_DIM, 1)
        att_pairs.append(num / den)
    return jnp.concatenate(att_pairs, axis=1)


def _lat_merge(att, gated_a, gate_b, x, gate2, wbra_ref, wout_ref):
    merged = gated_a + gate_b * _dot(att.astype(BF16), wbra_ref[...])
    return x + gate2 * _dot(merged.astype(BF16), wout_ref[...])


def _lat_mixer_kernel(xa_ref, xb_ref, xh_ref, moda_ref, modb_ref, g_ref, wpq32_ref, wg32_ref, qg_ref,
                      k_ref, v_ref, ck_ref, cv_ref, rpb_ref, wpool32_ref, pscale_ref, wbrp32_ref,
                      wbra32_ref, wout32_ref, o_ref, wpq_ref, wg_ref, bias_ref, wpool_ref, wbrp_ref,
                      wbra_ref, wout_ref, q_s, ga_s, gb_s, x_s, *, seq, n_rows, n_tiles):
    step = pl.program_id(0)
    nb = seq // LAT_TM
    project = functools.partial(_lat_project_stages, g_ref=g_ref, qg_ref=qg_ref, pscale_ref=pscale_ref,
                                wpq_ref=wpq_ref, wg_ref=wg_ref, wpool_ref=wpool_ref, wbrp_ref=wbrp_ref,
                                seq=seq)
    attention = functools.partial(_lat_attention, k_ref=k_ref, v_ref=v_ref, ck_ref=ck_ref, cv_ref=cv_ref,
                                  bias_ref=bias_ref, n_rows=n_rows)

    def project_next(i):
        tile = jnp.minimum(2 * i + 2, n_tiles - 1)
        return project(xb_ref[...], xa_ref[LAT_TM - POOL_HALO:, :], xh_ref[...], lax.rem(tile, nb), modb_ref)

    def keep_for_next_step(proj):
        q_s[...] = proj["qn"]
        ga_s[...] = proj["ga"]
        gb_s[...] = proj["gb"]
        x_s[...] = xb_ref[...]

    @pl.when(step < N_CONV)
    def _convert():
        _convert_chunk(step, wpq32_ref, wpq_ref)
        _convert_chunk(step, wg32_ref, wg_ref)
        _convert_mixer_weights(step, wpool32_ref, wbrp32_ref, wbra32_ref, wout32_ref,
                               wpool_ref, wbrp_ref, wbra_ref, wout_ref)

        @pl.when(step == 0)
        def _bias():
            _build_bias_table(rpb_ref, bias_ref)

        @pl.when(step == N_CONV - 1)
        def _first_tile():
            stages, proj = project_next(-1)
            for stage in stages:
                stage()
            keep_for_next_step(proj)

    @pl.when(step >= N_CONV)
    def _compute():
        i = step - N_CONV
        j0 = lax.rem(2 * i, nb)
        gate2 = moda_ref[0, 5:6, :]

        xa = xa_ref[...]
        pieces1, proj1 = project(xa, x_s[LAT_TM - POOL_HALO:, :], xb_ref[:POOL_HALO, :], j0 + 1, moda_ref)
        att0 = attention(q_s[...], j0, pieces1)

        def merge_first():
            o_ref[:LAT_TM, :] = _lat_merge(att0, ga_s[...], gb_s[...], x_s[...], gate2, wbra_ref, wout_ref)

        pieces2, proj2 = project_next(i)
        att1 = attention(proj1["qn"], j0 + 1, [lambda: (merge_first(), pieces2[0]())] + pieces2[1:])
        o_ref[LAT_TM:, :] = _lat_merge(att1, proj1["ga"], proj1["gb"], xa, gate2, wbra_ref, wout_ref)
        keep_for_next_step(proj2)


def _lat_mixer(x_all, tok0, n_batch, seq, mod3, g, w_in, qg2, k3, v3, ck3, cv3, rpb_pad, w_pool, pscale,
               wbrp, wbra, wout):
    tm = LAT_TM
    n_rows = seq // GRID_W
    nb = seq // tm
    n_tiles = n_batch * nb
    assert nb % 2 == 0
    tile0 = tok0 // tm
    halo_per_tile = tm // POOL_HALO

    def tile(s, k):
        return jnp.clip(2 * (s - N_CONV) + k, 0, n_tiles - 1)

    seq_map = lambda s: (tile(s, 0) // nb, 0, 0)
    return pl.pallas_call(
        functools.partial(_lat_mixer_kernel, seq=seq, n_rows=n_rows, n_tiles=n_tiles),
        grid=(N_CONV + n_tiles // 2,),
        in_specs=[pl.BlockSpec((tm, D_MODEL), lambda s: (tile0 + tile(s, 1), 0)),
                  pl.BlockSpec((tm, D_MODEL), lambda s: (tile0 + tile(s, 2), 0)),
                  pl.BlockSpec((POOL_HALO, D_MODEL), lambda s: ((tile0 + tile(s, 3)) * halo_per_tile, 0)),
                  pl.BlockSpec((1, N_MOD, D_MODEL), lambda s: (1 + tile(s, 0) // nb, 0, 0)),
                  pl.BlockSpec((1, N_MOD, D_MODEL), lambda s: (1 + tile(s, 2) // nb, 0, 0)),
                  _const_spec((1, D_MODEL)),
                  _chunk_spec(w_in, (O_K, 0)), _chunk_spec(w_in, (2 * D_MODEL, O_G // (2 * D_MODEL))),
                  _const_spec((1, LANES)),
                  pl.BlockSpec((1, seq, D_ATT), seq_map), pl.BlockSpec((1, seq, D_ATT), seq_map),
                  pl.BlockSpec((1,) + ck3.shape[1:], seq_map), pl.BlockSpec((1,) + cv3.shape[1:], seq_map),
                  _const_spec(rpb_pad.shape),
                  _const_spec(w_pool.shape), _const_spec((1, D_POOL)),
                  _chunk_spec(wbrp), _chunk_spec(wbra), _chunk_spec(wout)],
        out_specs=pl.BlockSpec((2 * tm, D_MODEL), lambda s: (_tile_index(s), 0)),
        out_shape=jax.ShapeDtypeStruct((n_batch * seq, D_MODEL), F32),
        scratch_shapes=[pltpu.VMEM((D_MODEL, O_K), BF16), pltpu.VMEM((D_MODEL, 2 * D_MODEL), BF16),
                        pltpu.VMEM((N_HEADS, 2 * NA_KH - 2, GRID_W, LANES), F32)] + _mixer_weight_scratch()
                       + [pltpu.VMEM((tm, D_ATT), BF16), pltpu.VMEM((tm, D_MODEL), F32),
                          pltpu.VMEM((tm, D_MODEL), F32), pltpu.VMEM((tm, D_MODEL), F32)],
        compiler_params=pltpu.CompilerParams(dimension_semantics=("arbitrary",),
                                             vmem_limit_bytes=VMEM_LIMIT),
        name="lat_mixer",
    )(x_all, x_all, x_all, mod3, mod3, g, w_in, w_in, qg2, k3, v3, ck3, cv3, rpb_pad, w_pool, pscale,
      wbrp, wbra, wout)


def kernel(x_prompt, x_sample, cache_k, cache_v, c, c_ctx, w_ada, b_ada, g_ff1, w_ff1_in, w_ff1_out,
           g_mix, w_in, q_gain, k_gain, w_pool, pool_scale, rpb, w_br_pool, w_br_att, w_out, g_ff2,
           w_ff2_in, w_ff2_out):
    n_ctx, seq_ctx, _ = x_prompt.shape
    n_lat, seq_lat, _ = x_sample.shape
    depth = w_ada.shape[0]
    assert depth == 1 and n_lat + 1 <= 8
    l = 0

    xp = x_prompt.reshape(n_ctx * seq_ctx, D_MODEL)
    xs = x_sample.reshape(n_lat * seq_lat, D_MODEL)
    t_ctx = xp.shape[0]
    cond8 = jnp.zeros((8, D_MODEL), F32).at[0].set(c_ctx).at[1:1 + n_lat].set(c)
    mod3 = _adaln(cond8, w_ada[l], b_ada[l]).reshape(8, N_MOD, D_MODEL)

    pscale = pool_scale[l].reshape(1, D_POOL)
    qg2 = jnp.tile(q_gain[l], 2).reshape(1, LANES)
    kg2 = jnp.tile(k_gain[l], 2).reshape(1, LANES)
    g1, gm, g2 = g_ff1[l].reshape(1, -1), g_mix[l].reshape(1, -1), g_ff2[l].reshape(1, -1)
    rpb_pad = jnp.pad(rpb[l], ((0, 0), (0, 0), (0, LANES - rpb.shape[-1])))
    ck3 = cache_k[:, l].reshape(n_lat, -1, D_ATT).astype(BF16)
    cv3 = cache_v[:, l].reshape(n_lat, -1, D_ATT).astype(BF16)

    x1, k_s, v_s = _ffn(xp, xs, mod3, seq_lat, 0, g1, w_ff1_in[l], w_ff1_out[l], "ffn1",
                        kv=(gm, w_in[l], kg2))
    x2p, k_p, v_p = _ctx_mixer(x1, t_ctx, mod3, gm, w_in[l], qg2, kg2, w_pool[l], pscale,
                               w_br_pool[l], w_br_att[l], w_out[l], seq_ctx)
    x2s = _lat_mixer(x1, t_ctx, n_lat, seq_lat, mod3, gm, w_in[l], qg2,
                     k_s.reshape(n_lat, seq_lat, D_ATT), v_s.reshape(n_lat, seq_lat, D_ATT), ck3, cv3,
                     rpb_pad, w_pool[l], pscale, w_br_pool[l], w_br_att[l], w_out[l])
    yp, ys = _ffn(x2p, x2s, mod3, seq_lat, 6, g2, w_ff2_in[l], w_ff2_out[l], "ffn2", split_out=True)

    new_k = k_p.reshape(n_ctx, 1, seq_ctx, N_HEADS, HEAD_DIM)
    new_v = v_p.reshape(n_ctx, 1, seq_ctx, N_HEADS, HEAD_DIM)
    return (yp.reshape(x_prompt.shape), ys.reshape(x_sample.shape), new_k, new_v)
```

```python
import functools
import math

import jax
import jax.numpy as jnp
from jax import lax
from jax.experimental import pallas as pl
from jax.experimental.pallas import tpu as pltpu

F32 = jnp.float32
BF16 = jnp.bfloat16

D_MODEL = 1024
N_HEADS = 8
HEAD_DIM = 64
D_ATT = N_HEADS * HEAD_DIM
D_POOL = 512
POOL_WINDOWS = (2, 4, 8, 16)
POOL_HALO = 8
D_FF = 2816
N_MOD = 9
GRID_W = 64
NA_KH = 8
NA_KW = 16
EPS = 1e-6
NEG = -1e30

LANES = 128
N_PAIRS = D_ATT // LANES
MXU_COLS = 256
FF_CHUNKS = ((0, 768), (768, 768), (1536, 768), (2304, 512))
FFN_TM = 512
CTX_TB = 2
LAT_ROWS = 4
LAT_KEY_ROWS = 12
N_CONV = 8
FFN_N_CONV = 16
VMEM_LIMIT = 56 * 1024 * 1024

O_Q = D_POOL
O_K = D_POOL + D_ATT
O_V = D_POOL + 2 * D_ATT
O_G = D_POOL + 3 * D_ATT


def _dot(a, b):
    return jnp.dot(a, b, preferred_element_type=F32)


def _dot_nt(a, b):
    return lax.dot_general(a, b, (((1,), (1,)), ((), ())), preferred_element_type=F32)


def _rms_mod(x, g, shift, scale):
    ms = jnp.mean(x * x, axis=-1, keepdims=True)
    y = x * lax.rsqrt(ms + EPS) * g
    return y * (1.0 + scale) + shift


def _low_half():
    return lax.broadcasted_iota(jnp.int32, (1, LANES), 1) < HEAD_DIM


def _head_norm(t, gain2):
    lo = _low_half()
    sq = t * t
    s0 = jnp.sum(jnp.where(lo, sq, 0.0), axis=-1, keepdims=True)
    s1 = jnp.sum(jnp.where(lo, 0.0, sq), axis=-1, keepdims=True)
    r0 = lax.rsqrt(s0 * (1.0 / HEAD_DIM) + EPS)
    r1 = lax.rsqrt(s1 * (1.0 / HEAD_DIM) + EPS)
    return t * jnp.where(lo, r0, r1) * gain2


def _softmax_parts(parts):
    m = parts[0].max(axis=-1, keepdims=True)
    for s in parts[1:]:
        m = jnp.maximum(m, s.max(axis=-1, keepdims=True))
    es = [jnp.exp(s - m) for s in parts]
    l = es[0].sum(axis=-1, keepdims=True)
    for e in es[1:]:
        l = l + e.sum(axis=-1, keepdims=True)
    inv = 1.0 / l
    return [(e * inv).astype(BF16) for e in es]


def _pool_delta(p_ext, pos0, seq_len):
    te = p_ext.shape[0]
    tm = te - 2 * POOL_HALO
    t = pos0 + lax.broadcasted_iota(jnp.int32, (tm, 1), 0)
    outs = []
    for g, w in enumerate(POOL_WINDOWS):
        a = p_ext[:, g * LANES:(g + 1) * LANES]
        s = a
        step = 1
        while step < w:
            s = s + pltpu.roll(s, step, 0)
            step *= 2
        ahead = w // 2 - 1
        if ahead:
            s = pltpu.roll(s, te - ahead, 0)
        win = s[POOL_HALO:POOL_HALO + tm]
        cnt = (jnp.minimum(t + w // 2, seq_len) - jnp.maximum(t - w // 2, 0)).astype(F32)
        outs.append(win / cnt - a[POOL_HALO:POOL_HALO + tm])
    return jnp.concatenate(outs, axis=1)


def _pool_branch(d, wpool_ref, pscale_ref, wbrp_ref):
    y = _dot(d.astype(BF16), wpool_ref[...]) * pscale_ref[...]
    return _dot(y.astype(BF16), wbrp_ref[...])


def _merge_out(x, gate2, gl, a, b, wout_ref):
    gates = jax.nn.sigmoid(gl)
    merged = gates[:, :D_MODEL] * a + gates[:, D_MODEL:] * b
    return x + gate2 * _dot(merged.astype(BF16), wout_ref[...])


def _chunk_spec(w, col_block=None, n_conv=N_CONV):
    rows = w.shape[0] // n_conv
    assert rows * n_conv == w.shape[0] and rows % 16 == 0
    width, cidx = col_block if col_block else (w.shape[1], 0)
    return pl.BlockSpec((rows, width), lambda s, *_: (jnp.minimum(s, n_conv - 1), cidx))


def _convert_chunk(step, src_ref, dst_ref):
    rows = src_ref.shape[0]
    r0 = pl.multiple_of(step * rows, rows)
    dst_ref[pl.ds(r0, rows), :] = src_ref[...].astype(BF16)


def _const_spec(shape):
    nd = len(shape)
    return pl.BlockSpec(shape, lambda *_: (0,) * nd)


def _tile_index(step):
    return jnp.maximum(step - N_CONV, 0)


def _adaln_kernel(cond_ref, w_ref, b_ref, o_ref):
    c = cond_ref[...]
    s = c * jax.nn.sigmoid(c)
    o_ref[...] = _dot(s.astype(BF16), w_ref[...].astype(BF16)) + b_ref[...]


def _adaln(cond8, w_ada, b_ada):
    n = w_ada.shape[1]
    tn = 1152
    return pl.pallas_call(
        _adaln_kernel,
        grid=(n // tn,),
        in_specs=[pl.BlockSpec((8, D_MODEL), lambda i: (0, 0)),
                  pl.BlockSpec((D_MODEL, tn), lambda i: (0, i)),
                  pl.BlockSpec((1, tn), lambda i: (0, i))],
        out_specs=pl.BlockSpec((8, tn), lambda i: (0, i)),
        out_shape=jax.ShapeDtypeStruct((8, n), F32),
        compiler_params=pltpu.CompilerParams(dimension_semantics=("arbitrary",),
                                             vmem_limit_bytes=VMEM_LIMIT),
        name="adaln",
    )(cond8, w_ada, b_ada.reshape(1, n))


def _ffn_kernel(*refs, mod_off, n_a, emit_kv, split_out):
    refs = list(refs)
    xa_ref, xb_ref, mod_ref, modn_ref, g_ref, wgu32_ref, wdn32_ref = refs[:7]
    del refs[:7]
    if emit_kv:
        gm_ref, wkv32_ref, kg_ref = refs[:3]
        del refs[:3]
    out_refs = refs[:2 if split_out else 1]
    del refs[:len(out_refs)]
    if emit_kv:
        ko_ref, vo_ref = refs[:2]
        del refs[:2]
    wgu_ref, wdn_ref = refs[:2]
    handover = (refs[2:5], refs[5:8])
    wkv_ref = refs[8] if emit_kv else None

    step = pl.program_id(0)

    def ff_chunk(h, ci):
        c0, cw = FF_CHUNKS[ci]
        a = _dot(h, wgu_ref[:, c0:c0 + cw])
        u = _dot(h, wgu_ref[:, D_FF + c0:D_FF + c0 + cw])
        t = (a * jax.nn.sigmoid(a) * u).astype(BF16)
        return _dot(t, wdn_ref[c0:c0 + cw, :])

    def start_next_tile(h_ref, x_ref, acc_ref):
        x_next = jnp.where(step - FFN_N_CONV + 1 < n_a, xa_ref[...], xb_ref[...])
        shift = modn_ref[0, mod_off:mod_off + 1, :]
        scale = modn_ref[0, mod_off + 1:mod_off + 2, :]
        h_next = _rms_mod(x_next, g_ref[...], shift, scale).astype(BF16)
        x_ref[...] = x_next
        h_ref[...] = h_next
        acc_ref[...] = ff_chunk(h_next, 0)

    @pl.when(step < FFN_N_CONV)
    def _convert():
        _convert_chunk(step, wgu32_ref, wgu_ref)
        _convert_chunk(step, wdn32_ref, wdn_ref)
        if emit_kv:
            _convert_chunk(step, wkv32_ref, wkv_ref)

        @pl.when(step == FFN_N_CONV - 1)
        def _first_tile():
            start_next_tile(*handover[0])

    def finish_tile(cur, start_next):
        h_ref, x_ref, acc_ref = cur
        is_a = step - FFN_N_CONV < n_a
        h = h_ref[...]
        gate = mod_ref[0, mod_off + 2:mod_off + 3, :]
        acc = acc_ref[...]
        for ci in range(1, len(FF_CHUNKS)):
            acc = acc + ff_chunk(h, ci)
        y = x_ref[...] + (0.5 * gate) * acc
        start_next()
        if split_out:
            @pl.when(is_a)
            def _store_a():
                out_refs[0][...] = y

            @pl.when(jnp.logical_not(is_a))
            def _store_b():
                out_refs[1][...] = y
        else:
            out_refs[0][...] = y

        if emit_kv:
            @pl.when(jnp.logical_not(is_a))
            def _kv():
                h2 = _rms_mod(y, gm_ref[...], mod_ref[0, 3:4, :], mod_ref[0, 4:5, :]).astype(BF16)
                k = _dot(h2, wkv_ref[:, :D_ATT])
                vo_ref[...] = _dot(h2, wkv_ref[:, D_ATT:]).astype(BF16)
                for pr in range(N_PAIRS):
                    lanes = slice(pr * LANES, (pr + 1) * LANES)
                    ko_ref[:, lanes] = _head_norm(k[:, lanes], kg_ref[...]).astype(BF16)

    for parity in range(2):
        @pl.when((step >= FFN_N_CONV) & (lax.rem(step - FFN_N_CONV, 2) == parity))
        def _compute(parity=parity):
            finish_tile(handover[parity], functools.partial(start_next_tile, *handover[1 - parity]))


def _ffn(xa, xb, mod3, seq_b, mod_off, g, wgu, wdn, name, kv=None, split_out=False):
    tm = FFN_TM
    n_a, n_b = xa.shape[0] // tm, xb.shape[0] // tm
    emit_kv = kv is not None
    chunk_spec = functools.partial(_chunk_spec, n_conv=FFN_N_CONV)

    def tile(s, ahead=0):
        return jnp.clip(s - FFN_N_CONV + ahead, 0, n_a + n_b - 1)

    def a_map(s, ahead=0):
        return (jnp.minimum(tile(s, ahead), n_a - 1), 0)

    def b_map(s, ahead=0):
        return (jnp.maximum(tile(s, ahead) - n_a, 0), 0)

    def mod_map(s, ahead=0):
        t = tile(s, ahead)
        return (jnp.where(t < n_a, 0, 1 + (jnp.maximum(t - n_a, 0) * tm) // seq_b), 0, 0)

    in_specs = [pl.BlockSpec((tm, D_MODEL), functools.partial(a_map, ahead=1)),
                pl.BlockSpec((tm, D_MODEL), functools.partial(b_map, ahead=1)),
                pl.BlockSpec((1, N_MOD, D_MODEL), mod_map),
                pl.BlockSpec((1, N_MOD, D_MODEL), functools.partial(mod_map, ahead=1)),
                _const_spec((1, D_MODEL)), chunk_spec(wgu), chunk_spec(wdn)]
    args = [xa, xb, mod3, mod3, g, wgu, wdn]
    scratch = [pltpu.VMEM(wgu.shape, BF16), pltpu.VMEM(wdn.shape, BF16)]
    scratch += [pltpu.VMEM((tm, D_MODEL), BF16), pltpu.VMEM((tm, D_MODEL), F32),
                pltpu.VMEM((tm, D_MODEL), F32)] * 2
    if emit_kv:
        g_mix, w_in, kg2 = kv
        in_specs += [_const_spec((1, D_MODEL)), chunk_spec(w_in, (2 * D_ATT, O_K // (2 * D_ATT))),
                     _const_spec((1, LANES))]
        args += [g_mix, w_in, kg2]
        scratch.append(pltpu.VMEM((D_MODEL, 2 * D_ATT), BF16))
    if split_out:
        out_specs = [pl.BlockSpec((tm, D_MODEL), a_map), pl.BlockSpec((tm, D_MODEL), b_map)]
        out_shape = [jax.ShapeDtypeStruct(xa.shape, F32), jax.ShapeDtypeStruct(xb.shape, F32)]
    else:
        out_specs = [pl.BlockSpec((tm, D_MODEL), lambda s: (tile(s), 0))]
        out_shape = [jax.ShapeDtypeStruct((xa.shape[0] + xb.shape[0], D_MODEL), F32)]
    if emit_kv:
        out_specs += [pl.BlockSpec((tm, D_ATT), b_map)] * 2
        out_shape += [jax.ShapeDtypeStruct((xb.shape[0], D_ATT), BF16)] * 2
    return pl.pallas_call(
        functools.partial(_ffn_kernel, mod_off=mod_off, n_a=n_a, emit_kv=emit_kv, split_out=split_out),
        grid=(FFN_N_CONV + n_a + n_b,),
        in_specs=in_specs, out_specs=out_specs, out_shape=out_shape, scratch_shapes=scratch,
        compiler_params=pltpu.CompilerParams(dimension_semantics=("arbitrary",),
                                             vmem_limit_bytes=VMEM_LIMIT),
        name=name,
    )(*args)


def _convert_mixer_weights(step, wpool32_ref, wbrp32_ref, wbra32_ref, wout32_ref,
                           wpool_ref, wbrp_ref, wbra_ref, wout_ref):
    _convert_chunk(step, wbrp32_ref, wbrp_ref)
    _convert_chunk(step, wbra32_ref, wbra_ref)
    _convert_chunk(step, wout32_ref, wout_ref)

    @pl.when(step == 0)
    def _pool_weight():
        wpool_ref[...] = jnp.zeros(wpool_ref.shape, BF16)
        for g in range(len(POOL_WINDOWS)):
            blk = slice(g * LANES, (g + 1) * LANES)
            wpool_ref[blk, blk] = wpool32_ref[g].astype(BF16)


def _mixer_weight_scratch():
    return [pltpu.VMEM((D_POOL, D_POOL), BF16), pltpu.VMEM((D_POOL, D_MODEL), BF16),
            pltpu.VMEM((D_ATT, D_MODEL), BF16), pltpu.VMEM((D_MODEL, D_MODEL), BF16)]


def _ctx_mixer_kernel(x_ref, mod_ref, g_ref, win32_ref, qg_ref, kg_ref, wpool32_ref, pscale_ref,
                      wbrp32_ref, wbra32_ref, wout32_ref, o_ref, ko_ref, vo_ref,
                      win_ref, wpool_ref, wbrp_ref, wbra_ref, wout_ref, *, seq):
    step = pl.program_id(0)

    @pl.when(step < N_CONV)
    def _convert():
        _convert_chunk(step, win32_ref, win_ref)
        _convert_mixer_weights(step, wpool32_ref, wbrp32_ref, wbra32_ref, wout32_ref,
                               wpool_ref, wbrp_ref, wbra_ref, wout_ref)

    @pl.when(step >= N_CONV)
    def _compute():
        x = x_ref[...]
        m_rows = x.shape[0]
        h = _rms_mod(x, g_ref[...], mod_ref[0, 3:4, :], mod_ref[0, 4:5, :]).astype(BF16)
        p = _dot(h, win_ref[:, 0:O_Q])
        q = _dot(h, win_ref[:, O_Q:O_K])
        k = _dot(h, win_ref[:, O_K:O_V])
        v = _dot(h, win_ref[:, O_V:O_G])
        gl = _dot(h, win_ref[:, O_G:])
        vo_ref[...] = v

        lo = _low_half()
        scale = 1.0 / math.sqrt(HEAD_DIM)
        zeros_halo = jnp.zeros((POOL_HALO, D_POOL), F32)
        d_rows = []
        att_rows = []
        for e in range(m_rows // seq):
            rows = slice(e * seq, (e + 1) * seq)
            p_ext = jnp.concatenate([zeros_halo, p[rows], zeros_halo], axis=0)
            d_rows.append(_pool_delta(p_ext, 0, seq))
            att_pairs = []
            for pr in range(N_PAIRS):
                lanes = slice(pr * LANES, (pr + 1) * LANES)
                kn = _head_norm(k[rows, lanes], kg_ref[...])
                ko_ref[rows, lanes] = kn
                qn = _head_norm(q[rows, lanes], qg_ref[...]) * scale
                kb = kn.astype(BF16)
                vb = v[rows, lanes].astype(BF16)
                halves = []
                for half in range(2):
                    qm = jnp.where(lo, qn, 0.0) if half == 0 else jnp.where(lo, 0.0, qn)
                    s = _dot_nt(qm.astype(BF16), kb)
                    (pm,) = _softmax_parts([s])
                    halves.append(_dot(pm, vb))
                att_pairs.append(jnp.where(lo, halves[0], halves[1]))
            att_rows.append(jnp.concatenate(att_pairs, axis=1))
        d = jnp.concatenate(d_rows, axis=0)
        att = jnp.concatenate(att_rows, axis=0)
        a = _pool_branch(d, wpool_ref, pscale_ref, wbrp_ref)
        b = _dot(att.astype(BF16), wbra_ref[...])
        o_ref[...] = _merge_out(x, mod_ref[0, 5:6, :], gl, a, b, wout_ref)


def _ctx_mixer(x_all, n_tok, mod3, g, w_in, qg2, kg2, w_pool, pscale, wbrp, wbra, wout, seq):
    tm = CTX_TB * seq
    row = lambda s: (_tile_index(s), 0)
    return pl.pallas_call(
        functools.partial(_ctx_mixer_kernel, seq=seq),
        grid=(N_CONV + n_tok // tm,),
        in_specs=[pl.BlockSpec((tm, D_MODEL), row),
                  _const_spec((1, N_MOD, D_MODEL)), _const_spec((1, D_MODEL)),
                  _chunk_spec(w_in), _const_spec((1, LANES)), _const_spec((1, LANES)),
                  _const_spec(w_pool.shape), _const_spec((1, D_POOL)),
                  _chunk_spec(wbrp), _chunk_spec(wbra), _chunk_spec(wout)],
        out_specs=[pl.BlockSpec((tm, D_MODEL), row),
                   pl.BlockSpec((tm, D_ATT), row),
                   pl.BlockSpec((tm, D_ATT), row)],
        out_shape=[jax.ShapeDtypeStruct((n_tok, D_MODEL), F32),
                   jax.ShapeDtypeStruct((n_tok, D_ATT), F32),
                   jax.ShapeDtypeStruct((n_tok, D_ATT), F32)],
        scratch_shapes=[pltpu.VMEM(w_in.shape, BF16)] + _mixer_weight_scratch(),
        compiler_params=pltpu.CompilerParams(dimension_semantics=("arbitrary",),
                                             vmem_limit_bytes=VMEM_LIMIT),
        name="ctx_mixer",
    )(x_all, mod3, g, w_in, qg2, kg2, w_pool, pscale, wbrp, wbra, wout)


def _build_bias_table(rpb_ref, bias_ref):
    lane = lax.broadcasted_iota(jnp.int32, (GRID_W, LANES), 1)
    qc = lax.broadcasted_iota(jnp.int32, (GRID_W, LANES), 0)
    kc = lane & (GRID_W - 1)
    c0 = jnp.clip(qc - NA_KW // 2, 0, GRID_W - NA_KW)
    col_ok = (kc >= c0) & (kc < c0 + NA_KW)
    first_half = lane < GRID_W
    shift_a = LANES - (NA_KW - 1)
    shift_b = (GRID_W - (NA_KW - 1)) % LANES
    for h in range(N_HEADS):
        for i in range(2 * NA_KH - 2):
            va = jnp.broadcast_to(rpb_ref[h, i:i + 1, :], (GRID_W, LANES))
            vb = jnp.broadcast_to(rpb_ref[h, i + 1:i + 2, :], (GRID_W, LANES))
            ra = pltpu.roll(va, shift_a, 1, stride=1, stride_axis=0)
            rb = pltpu.roll(vb, shift_b, 1, stride=1, stride_axis=0)
            bias_ref[h, i] = jnp.where(col_ok, jnp.where(first_half, ra, rb), NEG)


LAT_TM = LAT_ROWS * GRID_W


def _lat_project_stages(x, halo_prev, halo_next, j, mod_ref, g_ref, qg_ref, pscale_ref, wpq_ref, wg_ref,
                        wpool_ref, wbrp_ref, *, seq):
    st = {"gates": []}
    half_pool = D_MODEL // 2
    half_att = D_ATT // 2

    def gate_blocks(lo_blk, hi_blk):
        for blk in range(lo_blk, hi_blk):
            cols = slice(blk * MXU_COLS, (blk + 1) * MXU_COLS)
            st["gates"].append(jax.nn.sigmoid(_dot(st["h"], wg_ref[:, cols])))

    def queries(lo_col, hi_col):
        q = _dot(st["h"], wpq_ref[:, D_POOL + lo_col:D_POOL + hi_col])
        scale = 1.0 / math.sqrt(HEAD_DIM)
        return [(_head_norm(q[:, c:c + LANES], qg_ref[...]) * scale).astype(BF16)
                for c in range(0, hi_col - lo_col, LANES)]

    def piece0():
        g = g_ref[...]
        shift = mod_ref[0, 3:4, :]
        scale_mod = mod_ref[0, 4:5, :]
        st["h"] = _rms_mod(x, g, shift, scale_mod).astype(BF16)
        st["h_halo"] = _rms_mod(jnp.concatenate([halo_prev, halo_next], axis=0), g, shift,
                                scale_mod).astype(BF16)

    def piece1():
        p_halo = _dot(st.pop("h_halo"), wpq_ref[:, :D_POOL])
        st["p_ext"] = jnp.concatenate(
            [p_halo[:POOL_HALO], _dot(st["h"], wpq_ref[:, :D_POOL]), p_halo[POOL_HALO:]], axis=0)
        st["qn"] = queries(0, half_att)

    def piece2():
        pos = j * LAT_TM - POOL_HALO + lax.broadcasted_iota(jnp.int32, (LAT_TM + 2 * POOL_HALO, 1), 0)
        p_ext = jnp.where((pos >= 0) & (pos < seq), st.pop("p_ext"), 0.0)
        st["d"] = _pool_delta(p_ext, j * LAT_TM, seq).astype(BF16)
        st["qn"] = jnp.concatenate(st["qn"] + queries(half_att, D_ATT), axis=1)
        gate_blocks(0, 1)

    def piece3():
        st["y"] = (_dot(st.pop("d"), wpool_ref[...]) * pscale_ref[...]).astype(BF16)
        gate_blocks(1, 2)

    def piece4():
        y = st.pop("y")
        st["a"] = jnp.concatenate([_dot(y, wbrp_ref[:, :half_pool]), _dot(y, wbrp_ref[:, half_pool:])], axis=1)

    def piece5():
        gate_blocks(2, 4)
        st["ga"] = jnp.concatenate(st["gates"][:4], axis=1) * st.pop("a")

    def piece7():
        gate_blocks(6, 8)
        st.pop("h")
        st["gb"] = jnp.concatenate(st.pop("gates")[4:], axis=1)

    return [piece0, piece1, piece2, piece3, piece4, piece5, functools.partial(gate_blocks, 4, 6), piece7], st


def _lat_attention(qn, j, fillers, k_ref, v_ref, ck_ref, cv_ref, bias_ref, *, n_rows):
    n_keys = LAT_KEY_ROWS * GRID_W
    r0q = j * LAT_ROWS
    u0 = jnp.clip(r0q - NA_KH // 2, 0, n_rows - LAT_KEY_ROWS)
    key0 = pl.multiple_of(u0 * GRID_W, GRID_W)
    log2_w = GRID_W.bit_length() - 1
    lo = _low_half()
    zero = jnp.zeros((), BF16)
    one = jnp.ones((), BF16)

    lane_row = lax.broadcasted_iota(jnp.int32, (1, LANES), 1) & (HEAD_DIM - 1)
    q_tile_row = jnp.right_shift(lax.broadcasted_iota(jnp.int32, (LAT_TM, 1), 0), log2_w)
    q_hot = jnp.where(lane_row == q_tile_row, 1.0, 0.0).astype(BF16)
    k_row = u0 + jnp.right_shift(lax.broadcasted_iota(jnp.int32, (n_keys, 1), 0), log2_w)
    first = jnp.clip(r0q + lane_row - NA_KH // 2, 0, n_rows - NA_KH)
    outside = (lane_row < LAT_ROWS) & ((k_row < first) | (k_row >= first + NA_KH))
    k_mask = jnp.where(outside, NEG, 0.0).astype(BF16)

    def own(head, mine, other):
        return jnp.where(lo, mine, other) if head % 2 == 0 else jnp.where(lo, other, mine)

    def scores(head):
        lanes = slice((head // 2) * LANES, (head // 2 + 1) * LANES)
        qp = qn[:, lanes]
        s_loc = _dot_nt(own(head, qp, q_hot), own(head, k_ref[0, pl.ds(key0, n_keys), lanes], k_mask))
        return s_loc, _dot_nt(own(head, qp, zero), ck_ref[0, :, lanes])

    def weighted_values(head, s_loc, s_ctx):
        lanes = slice((head // 2) * LANES, (head // 2 + 1) * LANES)
        bias_rows = []
        for jr in range(LAT_ROWS):
            blocks = []
            for pi in range(LAT_KEY_ROWS // 2):
                idx = jnp.clip(u0 - r0q + 2 * pi - jr + NA_KH - 1, 0, 2 * NA_KH - 3)
                blocks.append(bias_ref[head, idx])
            bias_rows.append(jnp.concatenate(blocks, axis=1))
        s_loc = s_loc + jnp.concatenate(bias_rows, axis=0)
        m = jnp.maximum(s_loc.max(axis=-1, keepdims=True), s_ctx.max(axis=-1, keepdims=True))
        e_loc = jnp.exp(s_loc - m).astype(BF16)
        e_ctx = jnp.exp(s_ctx - m).astype(BF16)
        return (_dot(e_loc, own(head, v_ref[0, pl.ds(key0, n_keys), lanes], one))
                + _dot(e_ctx, own(head, cv_ref[0, :, lanes], one)))

    outs = []
    cur = scores(0)
    for head in range(N_HEADS):
        nxt = scores(head + 1) if head + 1 < N_HEADS else None
        fillers[head]()
        outs.append(weighted_values(head, *cur))
        cur = nxt
    att_pairs = []
    for pr in range(N_PAIRS):
        even, odd = outs[2 * pr], outs[2 * pr + 1]
        num = jnp.where(lo, even, odd)
        den = pltpu.roll(jnp.where(lo, odd, even), HEAD_DIM, 1)
        att_pairs.append(num / den)
    return jnp.concatenate(att_pairs, axis=1)


def _lat_merge(att, gated_a, gate_b, x, gate2, wbra_ref, wout_ref):
    merged = gated_a + gate_b * _dot(att.astype(BF16), wbra_ref[...])
    return x + gate2 * _dot(merged.astype(BF16), wout_ref[...])


def _lat_mixer_kernel(xa_ref, xb_ref, xh_ref, moda_ref, modb_ref, g_ref, wpq32_ref, wg32_ref, qg_ref,
                      k_ref, v_ref, ck_ref, cv_ref, rpb_ref, wpool32_ref, pscale_ref, wbrp32_ref,
                      wbra32_ref, wout32_ref, o_ref, wpq_ref, wg_ref, bias_ref, wpool_ref, wbrp_ref,
                      wbra_ref, wout_ref, q_s, ga_s, gb_s, x_s, *, seq, n_rows, n_tiles):
    step = pl.program_id(0)
    nb = seq // LAT_TM
    project = functools.partial(_lat_project_stages, g_ref=g_ref, qg_ref=qg_ref, pscale_ref=pscale_ref,
                                wpq_ref=wpq_ref, wg_ref=wg_ref, wpool_ref=wpool_ref, wbrp_ref=wbrp_ref,
                                seq=seq)
    attention = functools.partial(_lat_attention, k_ref=k_ref, v_ref=v_ref, ck_ref=ck_ref, cv_ref=cv_ref,
                                  bias_ref=bias_ref, n_rows=n_rows)

    def project_next(i):
        tile = jnp.minimum(2 * i + 2, n_tiles - 1)
        return project(xb_ref[...], xa_ref[LAT_TM - POOL_HALO:, :], xh_ref[...], lax.rem(tile, nb), modb_ref)

    def keep_for_next_step(proj):
        q_s[...] = proj["qn"]
        ga_s[...] = proj["ga"]
        gb_s[...] = proj["gb"]
        x_s[...] = xb_ref[...]

    @pl.when(step < N_CONV)
    def _convert():
        _convert_chunk(step, wpq32_ref, wpq_ref)
        _convert_chunk(step, wg32_ref, wg_ref)
        _convert_mixer_weights(step, wpool32_ref, wbrp32_ref, wbra32_ref, wout32_ref,
                               wpool_ref, wbrp_ref, wbra_ref, wout_ref)

        @pl.when(step == 0)
        def _bias():
            _build_bias_table(rpb_ref, bias_ref)

        @pl.when(step == N_CONV - 1)
        def _first_tile():
            stages, proj = project_next(-1)
            for stage in stages:
                stage()
            keep_for_next_step(proj)

    @pl.when(step >= N_CONV)
    def _compute():
        i = step - N_CONV
        j0 = lax.rem(2 * i, nb)
        gate2 = moda_ref[0, 5:6, :]

        xa = xa_ref[...]
        pieces1, proj1 = project(xa, x_s[LAT_TM - POOL_HALO:, :], xb_ref[:POOL_HALO, :], j0 + 1, moda_ref)
        att0 = attention(q_s[...], j0, pieces1)

        def merge_first():
            o_ref[:LAT_TM, :] = _lat_merge(att0, ga_s[...], gb_s[...], x_s[...], gate2, wbra_ref, wout_ref)

        pieces2, proj2 = project_next(i)
        att1 = attention(proj1["qn"], j0 + 1, [lambda: (merge_first(), pieces2[0]())] + pieces2[1:])
        o_ref[LAT_TM:, :] = _lat_merge(att1, proj1["ga"], proj1["gb"], xa, gate2, wbra_ref, wout_ref)
        keep_for_next_step(proj2)


def _lat_mixer(x_all, tok0, n_batch, seq, mod3, g, w_in, qg2, k3, v3, ck3, cv3, rpb_pad, w_pool, pscale,
               wbrp, wbra, wout):
    tm = LAT_TM
    n_rows = seq // GRID_W
    nb = seq // tm
    n_tiles = n_batch * nb
    assert nb % 2 == 0
    tile0 = tok0 // tm
    halo_per_tile = tm // POOL_HALO

    def tile(s, k):
        return jnp.clip(2 * (s - N_CONV) + k, 0, n_tiles - 1)

    seq_map = lambda s: (tile(s, 0) // nb, 0, 0)
    return pl.pallas_call(
        functools.partial(_lat_mixer_kernel, seq=seq, n_rows=n_rows, n_tiles=n_tiles),
        grid=(N_CONV + n_tiles // 2,),
        in_specs=[pl.BlockSpec((tm, D_MODEL), lambda s: (tile0 + tile(s, 1), 0)),
                  pl.BlockSpec((tm, D_MODEL), lambda s: (tile0 + tile(s, 2), 0)),
                  pl.BlockSpec((POOL_HALO, D_MODEL), lambda s: ((tile0 + tile(s, 3)) * halo_per_tile, 0)),
                  pl.BlockSpec((1, N_MOD, D_MODEL), lambda s: (1 + tile(s, 0) // nb, 0, 0)),
                  pl.BlockSpec((1, N_MOD, D_MODEL), lambda s: (1 + tile(s, 2) // nb, 0, 0)),
                  _const_spec((1, D_MODEL)),
                  _chunk_spec(w_in, (O_K, 0)), _chunk_spec(w_in, (2 * D_MODEL, O_G // (2 * D_MODEL))),
                  _const_spec((1, LANES)),
                  pl.BlockSpec((1, seq, D_ATT), seq_map), pl.BlockSpec((1, seq, D_ATT), seq_map),
                  pl.BlockSpec((1,) + ck3.shape[1:], seq_map), pl.BlockSpec((1,) + cv3.shape[1:], seq_map),
                  _const_spec(rpb_pad.shape),
                  _const_spec(w_pool.shape), _const_spec((1, D_POOL)),
                  _chunk_spec(wbrp), _chunk_spec(wbra), _chunk_spec(wout)],
        out_specs=pl.BlockSpec((2 * tm, D_MODEL), lambda s: (_tile_index(s), 0)),
        out_shape=jax.ShapeDtypeStruct((n_batch * seq, D_MODEL), F32),
        scratch_shapes=[pltpu.VMEM((D_MODEL, O_K), BF16), pltpu.VMEM((D_MODEL, 2 * D_MODEL), BF16),
                        pltpu.VMEM((N_HEADS, 2 * NA_KH - 2, GRID_W, LANES), F32)] + _mixer_weight_scratch()
                       + [pltpu.VMEM((tm, D_ATT), BF16), pltpu.VMEM((tm, D_MODEL), F32),
                          pltpu.VMEM((tm, D_MODEL), F32), pltpu.VMEM((tm, D_MODEL), F32)],
        compiler_params=pltpu.CompilerParams(dimension_semantics=("arbitrary",),
                                             vmem_limit_bytes=VMEM_LIMIT),
        name="lat_mixer",
    )(x_all, x_all, x_all, mod3, mod3, g, w_in, w_in, qg2, k3, v3, ck3, cv3, rpb_pad, w_pool, pscale,
      wbrp, wbra, wout)


def kernel(x_prompt, x_sample, cache_k, cache_v, c, c_ctx, w_ada, b_ada, g_ff1, w_ff1_in, w_ff1_out,
           g_mix, w_in, q_gain, k_gain, w_pool, pool_scale, rpb, w_br_pool, w_br_att, w_out, g_ff2,
           w_ff2_in, w_ff2_out):
    n_ctx, seq_ctx, _ = x_prompt.shape
    n_lat, seq_lat, _ = x_sample.shape
    depth = w_ada.shape[0]
    assert depth == 1 and n_lat + 1 <= 8
    l = 0

    xp = x_prompt.reshape(n_ctx * seq_ctx, D_MODEL)
    xs = x_sample.reshape(n_lat * seq_lat, D_MODEL)
    t_ctx = xp.shape[0]
    cond8 = jnp.zeros((8, D_MODEL), F32).at[0].set(c_ctx).at[1:1 + n_lat].set(c)
    mod3 = _adaln(cond8, w_ada[l], b_ada[l]).reshape(8, N_MOD, D_MODEL)

    pscale = pool_scale[l].reshape(1, D_POOL)
    qg2 = jnp.tile(q_gain[l], 2).reshape(1, LANES)
    kg2 = jnp.tile(k_gain[l], 2).reshape(1, LANES)
    g1, gm, g2 = g_ff1[l].reshape(1, -1), g_mix[l].reshape(1, -1), g_ff2[l].reshape(1, -1)
    rpb_pad = jnp.pad(rpb[l], ((0, 0), (0, 0), (0, LANES - rpb.shape[-1])))
    ck3 = cache_k[:, l].reshape(n_lat, -1, D_ATT).astype(BF16)
    cv3 = cache_v[:, l].reshape(n_lat, -1, D_ATT).astype(BF16)

    x1, k_s, v_s = _ffn(xp, xs, mod3, seq_lat, 0, g1, w_ff1_in[l], w_ff1_out[l], "ffn1",
                        kv=(gm, w_in[l], kg2))
    x2p, k_p, v_p = _ctx_mixer(x1, t_ctx, mod3, gm, w_in[l], qg2, kg2, w_pool[l], pscale,
                               w_br_pool[l], w_br_att[l], w_out[l], seq_ctx)
    x2s = _lat_mixer(x1, t_ctx, n_lat, seq_lat, mod3, gm, w_in[l], qg2,
                     k_s.reshape(n_lat, seq_lat, D_ATT), v_s.reshape(n_lat, seq_lat, D_ATT), ck3, cv3,
                     rpb_pad, w_pool[l], pscale, w_br_pool[l], w_br_att[l], w_out[l])
    yp, ys = _ffn(x2p, x2s, mod3, seq_lat, 6, g2, w_ff2_in[l], w_ff2_out[l], "ffn2", split_out=True)

    new_k = k_p.reshape(n_ctx, 1, seq_ctx, N_HEADS, HEAD_DIM)
    new_v = v_p.reshape(n_ctx, 1, seq_ctx, N_HEADS, HEAD_DIM)
    return (yp.reshape(x_prompt.shape), ys.reshape(x_sample.shape), new_k, new_v)
```

```python
import functools
import math

import jax
import jax.numpy as jnp
from jax import lax
from jax.experimental import pallas as pl
from jax.experimental.pallas import tpu as pltpu

F32 = jnp.float32
BF16 = jnp.bfloat16

D_MODEL = 1024
N_HEADS = 8
HEAD_DIM = 64
D_ATT = N_HEADS * HEAD_DIM
D_POOL = 512
POOL_WINDOWS = (2, 4, 8, 16)
POOL_HALO = 8
D_FF = 2816
N_MOD = 9
GRID_W = 64
NA_KH = 8
NA_KW = 16
EPS = 1e-6
NEG = -1e30

LANES = 128
N_PAIRS = D_ATT // LANES
MXU_COLS = 256
FF_CHUNKS = ((0, 768), (768, 768), (1536, 768), (2304, 512))
FFN_TM = 512
CTX_TB = 2
LAT_ROWS = 4
LAT_KEY_ROWS = 12
N_CONV = 8
VMEM_LIMIT = 56 * 1024 * 1024

O_Q = D_POOL
O_K = D_POOL + D_ATT
O_V = D_POOL + 2 * D_ATT
O_G = D_POOL + 3 * D_ATT


def _dot(a, b):
    return jnp.dot(a, b, preferred_element_type=F32)


def _dot_nt(a, b):
    return lax.dot_general(a, b, (((1,), (1,)), ((), ())), preferred_element_type=F32)


def _rms_mod(x, g, shift, scale):
    ms = jnp.mean(x * x, axis=-1, keepdims=True)
    y = x * lax.rsqrt(ms + EPS) * g
    return y * (1.0 + scale) + shift


def _low_half():
    return lax.broadcasted_iota(jnp.int32, (1, LANES), 1) < HEAD_DIM


def _head_norm(t, gain2):
    lo = _low_half()
    sq = t * t
    s0 = jnp.sum(jnp.where(lo, sq, 0.0), axis=-1, keepdims=True)
    s1 = jnp.sum(jnp.where(lo, 0.0, sq), axis=-1, keepdims=True)
    r0 = lax.rsqrt(s0 * (1.0 / HEAD_DIM) + EPS)
    r1 = lax.rsqrt(s1 * (1.0 / HEAD_DIM) + EPS)
    return t * jnp.where(lo, r0, r1) * gain2


def _softmax_parts(parts):
    m = parts[0].max(axis=-1, keepdims=True)
    for s in parts[1:]:
        m = jnp.maximum(m, s.max(axis=-1, keepdims=True))
    es = [jnp.exp(s - m) for s in parts]
    l = es[0].sum(axis=-1, keepdims=True)
    for e in es[1:]:
        l = l + e.sum(axis=-1, keepdims=True)
    inv = 1.0 / l
    return [(e * inv).astype(BF16) for e in es]


def _pool_delta(p_ext, pos0, seq_len):
    te = p_ext.shape[0]
    tm = te - 2 * POOL_HALO
    t = pos0 + lax.broadcasted_iota(jnp.int32, (tm, 1), 0)
    outs = []
    for g, w in enumerate(POOL_WINDOWS):
        a = p_ext[:, g * LANES:(g + 1) * LANES]
        s = a
        step = 1
        while step < w:
            s = s + pltpu.roll(s, step, 0)
            step *= 2
        ahead = w // 2 - 1
        if ahead:
            s = pltpu.roll(s, te - ahead, 0)
        win = s[POOL_HALO:POOL_HALO + tm]
        cnt = (jnp.minimum(t + w // 2, seq_len) - jnp.maximum(t - w // 2, 0)).astype(F32)
        outs.append(win / cnt - a[POOL_HALO:POOL_HALO + tm])
    return jnp.concatenate(outs, axis=1)


def _pool_branch(d, wpool_ref, pscale_ref, wbrp_ref):
    y = _dot(d.astype(BF16), wpool_ref[...]) * pscale_ref[...]
    return _dot(y.astype(BF16), wbrp_ref[...])


def _merge_out(x, gate2, gl, a, b, wout_ref):
    gates = jax.nn.sigmoid(gl)
    merged = gates[:, :D_MODEL] * a + gates[:, D_MODEL:] * b
    return x + gate2 * _dot(merged.astype(BF16), wout_ref[...])


def _chunk_spec(w, col_block=None, n_conv=N_CONV):
    rows = w.shape[0] // n_conv
    assert rows * n_conv == w.shape[0] and rows % 16 == 0
    width, cidx = col_block if col_block else (w.shape[1], 0)
    return pl.BlockSpec((rows, width), lambda s, *_: (jnp.minimum(s, n_conv - 1), cidx))


def _convert_chunk(step, src_ref, dst_ref):
    rows = src_ref.shape[0]
    r0 = pl.multiple_of(step * rows, rows)
    dst_ref[pl.ds(r0, rows), :] = src_ref[...].astype(BF16)


def _const_spec(shape):
    nd = len(shape)
    return pl.BlockSpec(shape, lambda *_: (0,) * nd)


def _tile_index(step):
    return jnp.maximum(step - N_CONV, 0)


def _adaln_kernel(cond_ref, w_ref, b_ref, o_ref):
    c = cond_ref[...]
    s = c * jax.nn.sigmoid(c)
    o_ref[...] = _dot(s.astype(BF16), w_ref[...].astype(BF16)) + b_ref[...]


def _adaln(cond8, w_ada, b_ada):
    n = w_ada.shape[1]
    tn = 1152
    return pl.pallas_call(
        _adaln_kernel,
        grid=(n // tn,),
        in_specs=[pl.BlockSpec((8, D_MODEL), lambda i: (0, 0)),
                  pl.BlockSpec((D_MODEL, tn), lambda i: (0, i)),
                  pl.BlockSpec((1, tn), lambda i: (0, i))],
        out_specs=pl.BlockSpec((8, tn), lambda i: (0, i)),
        out_shape=jax.ShapeDtypeStruct((8, n), F32),
        compiler_params=pltpu.CompilerParams(dimension_semantics=("arbitrary",),
                                             vmem_limit_bytes=VMEM_LIMIT),
        name="adaln",
    )(cond8, w_ada, b_ada.reshape(1, n))


def _ffn_kernel(*refs, mod_off, n_a, emit_kv, split_out):
    refs = list(refs)
    xa_ref, xb_ref, mod_ref, g_ref, wgu32_ref, wdn32_ref = refs[:6]
    del refs[:6]
    if emit_kv:
        gm_ref, wkv32_ref, kg_ref = refs[:3]
        del refs[:3]
    out_refs = refs[:2 if split_out else 1]
    del refs[:len(out_refs)]
    if emit_kv:
        ko_ref, vo_ref = refs[:2]
        del refs[:2]
    wgu_ref, wdn_ref = refs[:2]
    wkv_ref = refs[2] if emit_kv else None

    step = pl.program_id(0)

    @pl.when(step < N_CONV)
    def _convert():
        _convert_chunk(step, wgu32_ref, wgu_ref)
        _convert_chunk(step, wdn32_ref, wdn_ref)
        if emit_kv:
            _convert_chunk(step, wkv32_ref, wkv_ref)

    @pl.when(step >= N_CONV)
    def _compute():
        is_a = step - N_CONV < n_a
        x = jnp.where(is_a, xa_ref[...], xb_ref[...])
        shift = mod_ref[0, mod_off:mod_off + 1, :]
        scale = mod_ref[0, mod_off + 1:mod_off + 2, :]
        gate = mod_ref[0, mod_off + 2:mod_off + 3, :]
        h = _rms_mod(x, g_ref[...], shift, scale).astype(BF16)
        acc = None
        for c0, cw in FF_CHUNKS:
            a = _dot(h, wgu_ref[:, c0:c0 + cw])
            u = _dot(h, wgu_ref[:, D_FF + c0:D_FF + c0 + cw])
            t = (a * jax.nn.sigmoid(a) * u).astype(BF16)
            part = _dot(t, wdn_ref[c0:c0 + cw, :])
            acc = part if acc is None else acc + part
        y = x + (0.5 * gate) * acc
        if split_out:
            @pl.when(is_a)
            def _store_a():
                out_refs[0][...] = y

            @pl.when(jnp.logical_not(is_a))
            def _store_b():
                out_refs[1][...] = y
        else:
            out_refs[0][...] = y

        if emit_kv:
            @pl.when(jnp.logical_not(is_a))
            def _kv():
                h2 = _rms_mod(y, gm_ref[...], mod_ref[0, 3:4, :], mod_ref[0, 4:5, :]).astype(BF16)
                k = _dot(h2, wkv_ref[:, :D_ATT])
                vo_ref[...] = _dot(h2, wkv_ref[:, D_ATT:]).astype(BF16)
                for pr in range(N_PAIRS):
                    lanes = slice(pr * LANES, (pr + 1) * LANES)
                    ko_ref[:, lanes] = _head_norm(k[:, lanes], kg_ref[...]).astype(BF16)


def _ffn(xa, xb, mod3, seq_b, mod_off, g, wgu, wdn, name, kv=None, split_out=False):
    tm = FFN_TM
    n_a, n_b = xa.shape[0] // tm, xb.shape[0] // tm
    emit_kv = kv is not None
    tile = _tile_index
    a_map = lambda s: (jnp.minimum(tile(s), n_a - 1), 0)
    b_map = lambda s: (jnp.maximum(tile(s) - n_a, 0), 0)
    mod_map = lambda s: (jnp.where(tile(s) < n_a, 0, 1 + (jnp.maximum(tile(s) - n_a, 0) * tm) // seq_b), 0, 0)

    in_specs = [pl.BlockSpec((tm, D_MODEL), a_map), pl.BlockSpec((tm, D_MODEL), b_map),
                pl.BlockSpec((1, N_MOD, D_MODEL), mod_map), _const_spec((1, D_MODEL)),
                _chunk_spec(wgu), _chunk_spec(wdn)]
    args = [xa, xb, mod3, g, wgu, wdn]
    scratch = [pltpu.VMEM(wgu.shape, BF16), pltpu.VMEM(wdn.shape, BF16)]
    if emit_kv:
        g_mix, w_in, kg2 = kv
        in_specs += [_const_spec((1, D_MODEL)), _chunk_spec(w_in, (2 * D_ATT, O_K // (2 * D_ATT))),
                     _const_spec((1, LANES))]
        args += [g_mix, w_in, kg2]
        scratch.append(pltpu.VMEM((D_MODEL, 2 * D_ATT), BF16))
    if split_out:
        out_specs = [pl.BlockSpec((tm, D_MODEL), a_map), pl.BlockSpec((tm, D_MODEL), b_map)]
        out_shape = [jax.ShapeDtypeStruct(xa.shape, F32), jax.ShapeDtypeStruct(xb.shape, F32)]
    else:
        out_specs = [pl.BlockSpec((tm, D_MODEL), lambda s: (tile(s), 0))]
        out_shape = [jax.ShapeDtypeStruct((xa.shape[0] + xb.shape[0], D_MODEL), F32)]
    if emit_kv:
        out_specs += [pl.BlockSpec((tm, D_ATT), b_map)] * 2
        out_shape += [jax.ShapeDtypeStruct((xb.shape[0], D_ATT), BF16)] * 2
    return pl.pallas_call(
        functools.partial(_ffn_kernel, mod_off=mod_off, n_a=n_a, emit_kv=emit_kv, split_out=split_out),
        grid=(N_CONV + n_a + n_b,),
        in_specs=in_specs, out_specs=out_specs, out_shape=out_shape, scratch_shapes=scratch,
        compiler_params=pltpu.CompilerParams(dimension_semantics=("arbitrary",),
                                             vmem_limit_bytes=VMEM_LIMIT),
        name=name,
    )(*args)


def _convert_mixer_weights(step, wpool32_ref, wbrp32_ref, wbra32_ref, wout32_ref,
                           wpool_ref, wbrp_ref, wbra_ref, wout_ref):
    _convert_chunk(step, wbrp32_ref, wbrp_ref)
    _convert_chunk(step, wbra32_ref, wbra_ref)
    _convert_chunk(step, wout32_ref, wout_ref)

    @pl.when(step == 0)
    def _pool_weight():
        wpool_ref[...] = jnp.zeros(wpool_ref.shape, BF16)
        for g in range(len(POOL_WINDOWS)):
            blk = slice(g * LANES, (g + 1) * LANES)
            wpool_ref[blk, blk] = wpool32_ref[g].astype(BF16)


def _mixer_weight_scratch():
    return [pltpu.VMEM((D_POOL, D_POOL), BF16), pltpu.VMEM((D_POOL, D_MODEL), BF16),
            pltpu.VMEM((D_ATT, D_MODEL), BF16), pltpu.VMEM((D_MODEL, D_MODEL), BF16)]


def _heads_to_sublanes(t):
    heads = jnp.stack([t[:, h * HEAD_DIM:(h + 1) * HEAD_DIM] for h in range(N_HEADS)], axis=0)
    return jnp.swapaxes(heads, 0, 1)


def _ctx_mixer_kernel(x_ref, mod_ref, g_ref, win32_ref, qg_ref, kg_ref, wpool32_ref, pscale_ref,
                      wbrp32_ref, wbra32_ref, wout32_ref, o_ref, ko_ref, vo_ref,
                      win_ref, wpool_ref, wbrp_ref, wbra_ref, wout_ref, *, seq):
    step = pl.program_id(0)

    @pl.when(step < N_CONV)
    def _convert():
        _convert_chunk(step, win32_ref, win_ref)
        _convert_mixer_weights(step, wpool32_ref, wbrp32_ref, wbra32_ref, wout32_ref,
                               wpool_ref, wbrp_ref, wbra_ref, wout_ref)

    @pl.when(step >= N_CONV)
    def _compute():
        x = x_ref[...]
        m_rows = x.shape[0]
        h = _rms_mod(x, g_ref[...], mod_ref[0, 3:4, :], mod_ref[0, 4:5, :]).astype(BF16)
        p = _dot(h, win_ref[:, 0:O_Q])
        q = _dot(h, win_ref[:, O_Q:O_K])
        k = _dot(h, win_ref[:, O_K:O_V])
        v = _dot(h, win_ref[:, O_V:O_G])
        gl = _dot(h, win_ref[:, O_G:])
        vo_ref[...] = _heads_to_sublanes(v)

        lo = _low_half()
        scale = 1.0 / math.sqrt(HEAD_DIM)
        zeros_halo = jnp.zeros((POOL_HALO, D_POOL), F32)
        d_rows = []
        att_rows = []
        for e in range(m_rows // seq):
            rows = slice(e * seq, (e + 1) * seq)
            p_ext = jnp.concatenate([zeros_halo, p[rows], zeros_halo], axis=0)
            d_rows.append(_pool_delta(p_ext, 0, seq))
            att_pairs = []
            kn_pairs = []
            for pr in range(N_PAIRS):
                lanes = slice(pr * LANES, (pr + 1) * LANES)
                kn = _head_norm(k[rows, lanes], kg_ref[...])
                kn_pairs.append(kn)
                qn = _head_norm(q[rows, lanes], qg_ref[...]) * scale
                kb = kn.astype(BF16)
                vb = v[rows, lanes].astype(BF16)
                halves = []
                for half in range(2):
                    qm = jnp.where(lo, qn, 0.0) if half == 0 else jnp.where(lo, 0.0, qn)
                    s = _dot_nt(qm.astype(BF16), kb)
                    (pm,) = _softmax_parts([s])
                    halves.append(_dot(pm, vb))
                att_pairs.append(jnp.where(lo, halves[0], halves[1]))
            att_rows.append(jnp.concatenate(att_pairs, axis=1))
            ko_ref[rows] = _heads_to_sublanes(jnp.concatenate(kn_pairs, axis=1))
        d = jnp.concatenate(d_rows, axis=0)
        att = jnp.concatenate(att_rows, axis=0)
        a = _pool_branch(d, wpool_ref, pscale_ref, wbrp_ref)
        b = _dot(att.astype(BF16), wbra_ref[...])
        o_ref[...] = _merge_out(x, mod_ref[0, 5:6, :], gl, a, b, wout_ref)


def _ctx_mixer(x_all, n_tok, mod3, g, w_in, qg2, kg2, w_pool, pscale, wbrp, wbra, wout, seq):
    tm = CTX_TB * seq
    row = lambda s: (_tile_index(s), 0)
    return pl.pallas_call(
        functools.partial(_ctx_mixer_kernel, seq=seq),
        grid=(N_CONV + n_tok // tm,),
        in_specs=[pl.BlockSpec((tm, D_MODEL), row),
                  _const_spec((1, N_MOD, D_MODEL)), _const_spec((1, D_MODEL)),
                  _chunk_spec(w_in), _const_spec((1, LANES)), _const_spec((1, LANES)),
                  _const_spec(w_pool.shape), _const_spec((1, D_POOL)),
                  _chunk_spec(wbrp), _chunk_spec(wbra), _chunk_spec(wout)],
        out_specs=[pl.BlockSpec((tm, D_MODEL), row),
                   pl.BlockSpec((tm, N_HEADS, HEAD_DIM), lambda s: (_tile_index(s), 0, 0)),
                   pl.BlockSpec((tm, N_HEADS, HEAD_DIM), lambda s: (_tile_index(s), 0, 0))],
        out_shape=[jax.ShapeDtypeStruct((n_tok, D_MODEL), F32),
                   jax.ShapeDtypeStruct((n_tok, N_HEADS, HEAD_DIM), F32),
                   jax.ShapeDtypeStruct((n_tok, N_HEADS, HEAD_DIM), F32)],
        scratch_shapes=[pltpu.VMEM(w_in.shape, BF16)] + _mixer_weight_scratch(),
        compiler_params=pltpu.CompilerParams(dimension_semantics=("arbitrary",),
                                             vmem_limit_bytes=VMEM_LIMIT),
        name="ctx_mixer",
    )(x_all, mod3, g, w_in, qg2, kg2, w_pool, pscale, wbrp, wbra, wout)


def _build_bias_table(rpb_ref, bias_ref):
    lane = lax.broadcasted_iota(jnp.int32, (GRID_W, LANES), 1)
    qc = lax.broadcasted_iota(jnp.int32, (GRID_W, LANES), 0)
    kc = lane & (GRID_W - 1)
    c0 = jnp.clip(qc - NA_KW // 2, 0, GRID_W - NA_KW)
    col_ok = (kc >= c0) & (kc < c0 + NA_KW)
    first_half = lane < GRID_W
    shift_a = LANES - (NA_KW - 1)
    shift_b = (GRID_W - (NA_KW - 1)) % LANES
    for h in range(N_HEADS):
        for i in range(2 * NA_KH - 2):
            va = jnp.broadcast_to(rpb_ref[h, i:i + 1, :], (GRID_W, LANES))
            vb = jnp.broadcast_to(rpb_ref[h, i + 1:i + 2, :], (GRID_W, LANES))
            ra = pltpu.roll(va, shift_a, 1, stride=1, stride_axis=0)
            rb = pltpu.roll(vb, shift_b, 1, stride=1, stride_axis=0)
            bias_ref[h, i] = jnp.where(col_ok, jnp.where(first_half, ra, rb), NEG)


LAT_TM = LAT_ROWS * GRID_W


def _lat_project_stages(x, halo_prev, halo_next, j, mod_ref, g_ref, qg_ref, pscale_ref, wpq_ref, wg_ref,
                        wpool_ref, wbrp_ref, *, seq):
    st = {"gates": []}
    half_pool = D_MODEL // 2
    half_att = D_ATT // 2

    def gate_blocks(lo_blk, hi_blk):
        for blk in range(lo_blk, hi_blk):
            cols = slice(blk * MXU_COLS, (blk + 1) * MXU_COLS)
            st["gates"].append(jax.nn.sigmoid(_dot(st["h"], wg_ref[:, cols])))

    def queries(lo_col, hi_col):
        q = _dot(st["h"], wpq_ref[:, D_POOL + lo_col:D_POOL + hi_col])
        scale = 1.0 / math.sqrt(HEAD_DIM)
        return [(_head_norm(q[:, c:c + LANES], qg_ref[...]) * scale).astype(BF16)
                for c in range(0, hi_col - lo_col, LANES)]

    def piece0():
        g = g_ref[...]
        shift = mod_ref[0, 3:4, :]
        scale_mod = mod_ref[0, 4:5, :]
        st["h"] = _rms_mod(x, g, shift, scale_mod).astype(BF16)
        st["h_halo"] = _rms_mod(jnp.concatenate([halo_prev, halo_next], axis=0), g, shift,
                                scale_mod).astype(BF16)

    def piece1():
        p_halo = _dot(st.pop("h_halo"), wpq_ref[:, :D_POOL])
        st["p_ext"] = jnp.concatenate(
            [p_halo[:POOL_HALO], _dot(st["h"], wpq_ref[:, :D_POOL]), p_halo[POOL_HALO:]], axis=0)
        st["qn"] = queries(0, half_att)

    def piece2():
        pos = j * LAT_TM - POOL_HALO + lax.broadcasted_iota(jnp.int32, (LAT_TM + 2 * POOL_HALO, 1), 0)
        p_ext = jnp.where((pos >= 0) & (pos < seq), st.pop("p_ext"), 0.0)
        st["d"] = _pool_delta(p_ext, j * LAT_TM, seq).astype(BF16)
        st["qn"] = jnp.concatenate(st["qn"] + queries(half_att, D_ATT), axis=1)
        gate_blocks(0, 1)

    def piece3():
        st["y"] = (_dot(st.pop("d"), wpool_ref[...]) * pscale_ref[...]).astype(BF16)
        gate_blocks(1, 2)

    def piece4():
        y = st.pop("y")
        st["a"] = jnp.concatenate([_dot(y, wbrp_ref[:, :half_pool]), _dot(y, wbrp_ref[:, half_pool:])], axis=1)

    def piece5():
        gate_blocks(2, 4)
        st["ga"] = jnp.concatenate(st["gates"][:4], axis=1) * st.pop("a")

    def piece7():
        gate_blocks(6, 8)
        st.pop("h")
        st["gb"] = jnp.concatenate(st.pop("gates")[4:], axis=1)

    return [piece0, piece1, piece2, piece3, piece4, piece5, functools.partial(gate_blocks, 4, 6), piece7], st


def _lat_attention(qn, j, fillers, k_ref, v_ref, ck_ref, cv_ref, bias_ref, *, n_rows):
    n_keys = LAT_KEY_ROWS * GRID_W
    r0q = j * LAT_ROWS
    u0 = jnp.clip(r0q - NA_KH // 2, 0, n_rows - LAT_KEY_ROWS)
    key0 = pl.multiple_of(u0 * GRID_W, GRID_W)
    log2_w = GRID_W.bit_length() - 1
    lo = _low_half()
    zero = jnp.zeros((), BF16)
    one = jnp.ones((), BF16)

    lane_row = lax.broadcasted_iota(jnp.int32, (1, LANES), 1) & (HEAD_DIM - 1)
    q_tile_row = jnp.right_shift(lax.broadcasted_iota(jnp.int32, (LAT_TM, 1), 0), log2_w)
    q_hot = jnp.where(lane_row == q_tile_row, 1.0, 0.0).astype(BF16)
    k_row = u0 + jnp.right_shift(lax.broadcasted_iota(jnp.int32, (n_keys, 1), 0), log2_w)
    first = jnp.clip(r0q + lane_row - NA_KH // 2, 0, n_rows - NA_KH)
    outside = (lane_row < LAT_ROWS) & ((k_row < first) | (k_row >= first + NA_KH))
    k_mask = jnp.where(outside, NEG, 0.0).astype(BF16)

    def own(head, mine, other):
        return jnp.where(lo, mine, other) if head % 2 == 0 else jnp.where(lo, other, mine)

    def scores(head):
        lanes = slice((head // 2) * LANES, (head // 2 + 1) * LANES)
        qp = qn[:, lanes]
        s_loc = _dot_nt(own(head, qp, q_hot), own(head, k_ref[0, pl.ds(key0, n_keys), lanes], k_mask))
        return s_loc, _dot_nt(own(head, qp, zero), ck_ref[0, :, lanes])

    def weighted_values(head, s_loc, s_ctx):
        lanes = slice((head // 2) * LANES, (head // 2 + 1) * LANES)
        bias_rows = []
        for jr in range(LAT_ROWS):
            blocks = []
            for pi in range(LAT_KEY_ROWS // 2):
                idx = jnp.clip(u0 - r0q + 2 * pi - jr + NA_KH - 1, 0, 2 * NA_KH - 3)
                blocks.append(bias_ref[head, idx])
            bias_rows.append(jnp.concatenate(blocks, axis=1))
        s_loc = s_loc + jnp.concatenate(bias_rows, axis=0)
        m = jnp.maximum(s_loc.max(axis=-1, keepdims=True), s_ctx.max(axis=-1, keepdims=True))
        e_loc = jnp.exp(s_loc - m).astype(BF16)
        e_ctx = jnp.exp(s_ctx - m).astype(BF16)
        return (_dot(e_loc, own(head, v_ref[0, pl.ds(key0, n_keys), lanes], one))
                + _dot(e_ctx, own(head, cv_ref[0, :, lanes], one)))

    outs = []
    cur = scores(0)
    for head in range(N_HEADS):
        nxt = scores(head + 1) if head + 1 < N_HEADS else None
        fillers[head]()
        outs.append(weighted_values(head, *cur))
        cur = nxt
    att_pairs = []
    for pr in range(N_PAIRS):
        even, odd = outs[2 * pr], outs[2 * pr + 1]
        num = jnp.where(lo, even, odd)
        den = pltpu.roll(jnp.where(lo, odd, even), HEAD_DIM, 1)
        att_pairs.append(num / den)
    return jnp.concatenate(att_pairs, axis=1)


def _lat_merge(att, gated_a, gate_b, x, gate2, wbra_ref, wout_ref):
    merged = gated_a + gate_b * _dot(att.astype(BF16), wbra_ref[...])
    return x + gate2 * _dot(merged.astype(BF16), wout_ref[...])


def _lat_mixer_kernel(xa_ref, xb_ref, xh_ref, moda_ref, modb_ref, g_ref, wpq32_ref, wg32_ref, qg_ref,
                      k_ref, v_ref, ck_ref, cv_ref, rpb_ref, wpool32_ref, pscale_ref, wbrp32_ref,
                      wbra32_ref, wout32_ref, o_ref, wpq_ref, wg_ref, bias_ref, wpool_ref, wbrp_ref,
                      wbra_ref, wout_ref, q_s, ga_s, gb_s, x_s, *, seq, n_rows, n_tiles):
    step = pl.program_id(0)
    nb = seq // LAT_TM
    project = functools.partial(_lat_project_stages, g_ref=g_ref, qg_ref=qg_ref, pscale_ref=pscale_ref,
                                wpq_ref=wpq_ref, wg_ref=wg_ref, wpool_ref=wpool_ref, wbrp_ref=wbrp_ref,
                                seq=seq)
    attention = functools.partial(_lat_attention, k_ref=k_ref, v_ref=v_ref, ck_ref=ck_ref, cv_ref=cv_ref,
                                  bias_ref=bias_ref, n_rows=n_rows)

    def project_next(i):
        tile = jnp.minimum(2 * i + 2, n_tiles - 1)
        return project(xb_ref[...], xa_ref[LAT_TM - POOL_HALO:, :], xh_ref[...], lax.rem(tile, nb), modb_ref)

    def keep_for_next_step(proj):
        q_s[...] = proj["qn"]
        ga_s[...] = proj["ga"]
        gb_s[...] = proj["gb"]
        x_s[...] = xb_ref[...]

    @pl.when(step < N_CONV)
    def _convert():
        _convert_chunk(step, wpq32_ref, wpq_ref)
        _convert_chunk(step, wg32_ref, wg_ref)
        _convert_mixer_weights(step, wpool32_ref, wbrp32_ref, wbra32_ref, wout32_ref,
                               wpool_ref, wbrp_ref, wbra_ref, wout_ref)

        @pl.when(step == 0)
        def _bias():
            _build_bias_table(rpb_ref, bias_ref)

        @pl.when(step == N_CONV - 1)
        def _first_tile():
            stages, proj = project_next(-1)
            for stage in stages:
                stage()
            keep_for_next_step(proj)

    @pl.when(step >= N_CONV)
    def _compute():
        i = step - N_CONV
        j0 = lax.rem(2 * i, nb)
        gate2 = moda_ref[0, 5:6, :]

        xa = xa_ref[...]
        pieces1, proj1 = project(xa, x_s[LAT_TM - POOL_HALO:, :], xb_ref[:POOL_HALO, :], j0 + 1, moda_ref)
        att0 = attention(q_s[...], j0, pieces1)

        def merge_first():
            o_ref[:LAT_TM, :] = _lat_merge(att0, ga_s[...], gb_s[...], x_s[...], gate2, wbra_ref, wout_ref)

        pieces2, proj2 = project_next(i)
        att1 = attention(proj1["qn"], j0 + 1, [lambda: (merge_first(), pieces2[0]())] + pieces2[1:])
        o_ref[LAT_TM:, :] = _lat_merge(att1, proj1["ga"], proj1["gb"], xa, gate2, wbra_ref, wout_ref)
        keep_for_next_step(proj2)


def _lat_mixer(x_all, tok0, n_batch, seq, mod3, g, w_in, qg2, k3, v3, ck3, cv3, rpb_pad, w_pool, pscale,
               wbrp, wbra, wout):
    tm = LAT_TM
    n_rows = seq // GRID_W
    nb = seq // tm
    n_tiles = n_batch * nb
    assert nb % 2 == 0
    tile0 = tok0 // tm
    halo_per_tile = tm // POOL_HALO

    def tile(s, k):
        return jnp.clip(2 * (s - N_CONV) + k, 0, n_tiles - 1)

    seq_map = lambda s: (tile(s, 0) // nb, 0, 0)
    return pl.pallas_call(
        functools.partial(_lat_mixer_kernel, seq=seq, n_rows=n_rows, n_tiles=n_tiles),
        grid=(N_CONV + n_tiles // 2,),
        in_specs=[pl.BlockSpec((tm, D_MODEL), lambda s: (tile0 + tile(s, 1), 0)),
                  pl.BlockSpec((tm, D_MODEL), lambda s: (tile0 + tile(s, 2), 0)),
                  pl.BlockSpec((POOL_HALO, D_MODEL), lambda s: ((tile0 + tile(s, 3)) * halo_per_tile, 0)),
                  pl.BlockSpec((1, N_MOD, D_MODEL), lambda s: (1 + tile(s, 0) // nb, 0, 0)),
                  pl.BlockSpec((1, N_MOD, D_MODEL), lambda s: (1 + tile(s, 2) // nb, 0, 0)),
                  _const_spec((1, D_MODEL)),
                  _chunk_spec(w_in, (O_K, 0)), _chunk_spec(w_in, (2 * D_MODEL, O_G // (2 * D_MODEL))),
                  _const_spec((1, LANES)),
                  pl.BlockSpec((1, seq, D_ATT), seq_map), pl.BlockSpec((1, seq, D_ATT), seq_map),
                  pl.BlockSpec((1,) + ck3.shape[1:], seq_map), pl.BlockSpec((1,) + cv3.shape[1:], seq_map),
                  _const_spec(rpb_pad.shape),
                  _const_spec(w_pool.shape), _const_spec((1, D_POOL)),
                  _chunk_spec(wbrp), _chunk_spec(wbra), _chunk_spec(wout)],
        out_specs=pl.BlockSpec((2 * tm, D_MODEL), lambda s: (_tile_index(s), 0)),
        out_shape=jax.ShapeDtypeStruct((n_batch * seq, D_MODEL), F32),
        scratch_shapes=[pltpu.VMEM((D_MODEL, O_K), BF16), pltpu.VMEM((D_MODEL, 2 * D_MODEL), BF16),
                        pltpu.VMEM((N_HEADS, 2 * NA_KH - 2, GRID_W, LANES), F32)] + _mixer_weight_scratch()
                       + [pltpu.VMEM((tm, D_ATT), BF16), pltpu.VMEM((tm, D_MODEL), F32),
                          pltpu.VMEM((tm, D_MODEL), F32), pltpu.VMEM((tm, D_MODEL), F32)],
        compiler_params=pltpu.CompilerParams(dimension_semantics=("arbitrary",),
                                             vmem_limit_bytes=VMEM_LIMIT),
        name="lat_mixer",
    )(x_all, x_all, x_all, mod3, mod3, g, w_in, w_in, qg2, k3, v3, ck3, cv3, rpb_pad, w_pool, pscale,
      wbrp, wbra, wout)


def kernel(x_prompt, x_sample, cache_k, cache_v, c, c_ctx, w_ada, b_ada, g_ff1, w_ff1_in, w_ff1_out,
           g_mix, w_in, q_gain, k_gain, w_pool, pool_scale, rpb, w_br_pool, w_br_att, w_out, g_ff2,
           w_ff2_in, w_ff2_out):
    n_ctx, seq_ctx, _ = x_prompt.shape
    n_lat, seq_lat, _ = x_sample.shape
    depth = w_ada.shape[0]
    assert depth == 1 and n_lat + 1 <= 8
    l = 0

    xp = x_prompt.reshape(n_ctx * seq_ctx, D_MODEL)
    xs = x_sample.reshape(n_lat * seq_lat, D_MODEL)
    t_ctx = xp.shape[0]
    cond8 = jnp.zeros((8, D_MODEL), F32).at[0].set(c_ctx).at[1:1 + n_lat].set(c)
    mod3 = _adaln(cond8, w_ada[l], b_ada[l]).reshape(8, N_MOD, D_MODEL)

    pscale = pool_scale[l].reshape(1, D_POOL)
    qg2 = jnp.tile(q_gain[l], 2).reshape(1, LANES)
    kg2 = jnp.tile(k_gain[l], 2).reshape(1, LANES)
    g1, gm, g2 = g_ff1[l].reshape(1, -1), g_mix[l].reshape(1, -1), g_ff2[l].reshape(1, -1)
    rpb_pad = jnp.pad(rpb[l], ((0, 0), (0, 0), (0, LANES - rpb.shape[-1])))
    ck3 = cache_k[:, l].reshape(n_lat, -1, D_ATT).astype(BF16)
    cv3 = cache_v[:, l].reshape(n_lat, -1, D_ATT).astype(BF16)

    x1, k_s, v_s = _ffn(xp, xs, mod3, seq_lat, 0, g1, w_ff1_in[l], w_ff1_out[l], "ffn1",
                        kv=(gm, w_in[l], kg2))
    x2p, k_p, v_p = _ctx_mixer(x1, t_ctx, mod3, gm, w_in[l], qg2, kg2, w_pool[l], pscale,
                               w_br_pool[l], w_br_att[l], w_out[l], seq_ctx)
    x2s = _lat_mixer(x1, t_ctx, n_lat, seq_lat, mod3, gm, w_in[l], qg2,
                     k_s.reshape(n_lat, seq_lat, D_ATT), v_s.reshape(n_lat, seq_lat, D_ATT), ck3, cv3,
                     rpb_pad, w_pool[l], pscale, w_br_pool[l], w_br_att[l], w_out[l])
    yp, ys = _ffn(x2p, x2s, mod3, seq_lat, 6, g2, w_ff2_in[l], w_ff2_out[l], "ffn2", split_out=True)

    new_k = k_p.reshape(n_ctx, 1, seq_ctx, N_HEADS, HEAD_DIM)
    new_v = v_p.reshape(n_ctx, 1, seq_ctx, N_HEADS, HEAD_DIM)
    return (yp.reshape(x_prompt.shape), ys.reshape(x_sample.shape), new_k, new_v)
```

```python
import functools
import math

import jax
import jax.numpy as jnp
from jax import lax
from jax.experimental import pallas as pl
from jax.experimental.pallas import tpu as pltpu

F32 = jnp.float32
BF16 = jnp.bfloat16

D_MODEL = 1024
N_HEADS = 8
HEAD_DIM = 64
D_ATT = N_HEADS * HEAD_DIM
D_POOL = 512
POOL_WINDOWS = (2, 4, 8, 16)
POOL_HALO = 8
D_FF = 2816
N_MOD = 9
GRID_W = 64
NA_KH = 8
NA_KW = 16
EPS = 1e-6
NEG = -1e30

LANES = 128
N_PAIRS = D_ATT // LANES
MXU_COLS = 256
FF_CHUNKS = ((0, 768), (768, 768), (1536, 768), (2304, 512))
FFN_TM = 512
CTX_TB = 2
PROJ_TM = 512
LAT_ROWS = 4
LAT_TM = LAT_ROWS * GRID_W
LAT_TILES = 2
LAT_KEY_ROWS = 12
SCORE_LOOKAHEAD = 2
N_CONV = 8
VMEM_LIMIT = 56 * 1024 * 1024

O_Q = D_POOL
O_K = D_POOL + D_ATT
O_V = D_POOL + 2 * D_ATT
O_G = D_POOL + 3 * D_ATT


def _dot(a, b):
    return jnp.dot(a, b, preferred_element_type=F32)


def _dot_nt(a, b):
    return lax.dot_general(a, b, (((1,), (1,)), ((), ())), preferred_element_type=F32)


def _rms_mod(x, g, shift, scale):
    ms = jnp.mean(x * x, axis=-1, keepdims=True)
    y = x * lax.rsqrt(ms + EPS) * g
    return y * (1.0 + scale) + shift


def _low_half():
    return lax.broadcasted_iota(jnp.int32, (1, LANES), 1) < HEAD_DIM


def _head_norm(t, gain2):
    lo = _low_half()
    sq = t * t
    s0 = jnp.sum(jnp.where(lo, sq, 0.0), axis=-1, keepdims=True)
    s1 = jnp.sum(jnp.where(lo, 0.0, sq), axis=-1, keepdims=True)
    r0 = lax.rsqrt(s0 * (1.0 / HEAD_DIM) + EPS)
    r1 = lax.rsqrt(s1 * (1.0 / HEAD_DIM) + EPS)
    return t * jnp.where(lo, r0, r1) * gain2


def _softmax_parts(parts):
    m = parts[0].max(axis=-1, keepdims=True)
    for s in parts[1:]:
        m = jnp.maximum(m, s.max(axis=-1, keepdims=True))
    es = [jnp.exp(s - m) for s in parts]
    l = es[0].sum(axis=-1, keepdims=True)
    for e in es[1:]:
        l = l + e.sum(axis=-1, keepdims=True)
    inv = 1.0 / l
    return [(e * inv).astype(BF16) for e in es]


def _pool_delta(p_ext, pos0, seq_len):
    te = p_ext.shape[0]
    tm = te - 2 * POOL_HALO
    t = (pos0 + lax.broadcasted_iota(jnp.int32, (tm, 1), 0)).astype(F32)
    outs = []
    for g, w in enumerate(POOL_WINDOWS):
        a = p_ext[:, g * LANES:(g + 1) * LANES]
        s = a
        step = 1
        while step < w:
            s = s + pltpu.roll(s, step, 0)
            step *= 2
        ahead = w // 2 - 1
        if ahead:
            s = pltpu.roll(s, te - ahead, 0)
        win = s[POOL_HALO:POOL_HALO + tm]
        cnt = jnp.minimum(t + float(w // 2), float(seq_len)) - jnp.maximum(t - float(w // 2), 0.0)
        outs.append(win / cnt - a[POOL_HALO:POOL_HALO + tm])
    return jnp.concatenate(outs, axis=1)


def _pool_branch(d, wpool_ref, pscale_ref, wbrp_ref):
    y = _dot(d.astype(BF16), wpool_ref[...]) * pscale_ref[...]
    return _dot(y.astype(BF16), wbrp_ref[...])


def _merge_out(x, gate2, gl, a, b, wout_ref):
    gates = jax.nn.sigmoid(gl)
    merged = gates[:, :D_MODEL] * a + gates[:, D_MODEL:] * b
    return x + gate2 * _dot(merged.astype(BF16), wout_ref[...])


def _chunk_spec(w, col_block=None):
    rows = w.shape[0] // N_CONV
    assert rows * N_CONV == w.shape[0] and rows % 16 == 0
    width, cidx = col_block if col_block else (w.shape[1], 0)
    return pl.BlockSpec((rows, width), lambda s, *_: (jnp.minimum(s, N_CONV - 1), cidx))


def _convert_chunk(step, src_ref, dst_ref):
    rows = src_ref.shape[0]
    r0 = pl.multiple_of(step * rows, rows)
    dst_ref[pl.ds(r0, rows), :] = src_ref[...].astype(BF16)


def _const_spec(shape):
    nd = len(shape)
    return pl.BlockSpec(shape, lambda *_: (0,) * nd)


def _tile_index(step):
    return jnp.maximum(step - N_CONV, 0)


def _params():
    return pltpu.CompilerParams(dimension_semantics=("arbitrary",), vmem_limit_bytes=VMEM_LIMIT)


def _adaln_kernel(cond_ref, w_ref, b_ref, o_ref):
    c = cond_ref[...]
    s = c * jax.nn.sigmoid(c)
    o_ref[...] = _dot(s.astype(BF16), w_ref[...].astype(BF16)) + b_ref[...]


def _adaln(cond8, w_ada, b_ada):
    n = w_ada.shape[1]
    tn = 1152
    return pl.pallas_call(
        _adaln_kernel,
        grid=(n // tn,),
        in_specs=[pl.BlockSpec((8, D_MODEL), lambda i: (0, 0)),
                  pl.BlockSpec((D_MODEL, tn), lambda i: (0, i)),
                  pl.BlockSpec((1, tn), lambda i: (0, i))],
        out_specs=pl.BlockSpec((8, tn), lambda i: (0, i)),
        out_shape=jax.ShapeDtypeStruct((8, n), F32),
        compiler_params=_params(),
        name="adaln",
    )(cond8, w_ada, b_ada.reshape(1, n))


def _ffn_kernel(*refs, mod_off, n_a, split_out):
    xa_ref, xb_ref, mod_ref, g_ref, wgu32_ref, wdn32_ref = refs[:6]
    n_out = 2 if split_out else 1
    out_refs = refs[6:6 + n_out]
    wgu_ref, wdn_ref = refs[6 + n_out:]
    step = pl.program_id(0)

    @pl.when(step < N_CONV)
    def _convert():
        _convert_chunk(step, wgu32_ref, wgu_ref)
        _convert_chunk(step, wdn32_ref, wdn_ref)

    @pl.when(step >= N_CONV)
    def _compute():
        is_a = step - N_CONV < n_a
        x = jnp.where(is_a, xa_ref[...], xb_ref[...])
        shift = mod_ref[0, mod_off:mod_off + 1, :]
        scale = mod_ref[0, mod_off + 1:mod_off + 2, :]
        gate = mod_ref[0, mod_off + 2:mod_off + 3, :]
        h = _rms_mod(x, g_ref[...], shift, scale).astype(BF16)
        acc = None
        for c0, cw in FF_CHUNKS:
            a = _dot(h, wgu_ref[:, c0:c0 + cw])
            u = _dot(h, wgu_ref[:, D_FF + c0:D_FF + c0 + cw])
            t = (a * jax.nn.sigmoid(a) * u).astype(BF16)
            part = _dot(t, wdn_ref[c0:c0 + cw, :])
            acc = part if acc is None else acc + part
        y = x + (0.5 * gate) * acc
        if split_out:
            @pl.when(is_a)
            def _store_a():
                out_refs[0][...] = y

            @pl.when(jnp.logical_not(is_a))
            def _store_b():
                out_refs[1][...] = y
        else:
            out_refs[0][...] = y


def _ffn(xa, xb, mod3, seq_b, mod_off, g, wgu, wdn, name, split_out=False):
    tm = FFN_TM
    n_a, n_b = xa.shape[0] // tm, xb.shape[0] // tm
    tile = _tile_index
    a_map = lambda s: (jnp.minimum(tile(s), n_a - 1), 0)
    b_map = lambda s: (jnp.maximum(tile(s) - n_a, 0), 0)
    mod_map = lambda s: (jnp.where(tile(s) < n_a, 0, 1 + (jnp.maximum(tile(s) - n_a, 0) * tm) // seq_b), 0, 0)
    if split_out:
        out_specs = [pl.BlockSpec((tm, D_MODEL), a_map), pl.BlockSpec((tm, D_MODEL), b_map)]
        out_shape = [jax.ShapeDtypeStruct(xa.shape, F32), jax.ShapeDtypeStruct(xb.shape, F32)]
    else:
        out_specs = [pl.BlockSpec((tm, D_MODEL), lambda s: (tile(s), 0))]
        out_shape = [jax.ShapeDtypeStruct((xa.shape[0] + xb.shape[0], D_MODEL), F32)]
    return pl.pallas_call(
        functools.partial(_ffn_kernel, mod_off=mod_off, n_a=n_a, split_out=split_out),
        grid=(N_CONV + n_a + n_b,),
        in_specs=[pl.BlockSpec((tm, D_MODEL), a_map), pl.BlockSpec((tm, D_MODEL), b_map),
                  pl.BlockSpec((1, N_MOD, D_MODEL), mod_map), _const_spec((1, D_MODEL)),
                  _chunk_spec(wgu), _chunk_spec(wdn)],
        out_specs=out_specs, out_shape=out_shape,
        scratch_shapes=[pltpu.VMEM(wgu.shape, BF16), pltpu.VMEM(wdn.shape, BF16)],
        compiler_params=_params(),
        name=name,
    )(xa, xb, mod3, g, wgu, wdn)


def _convert_mixer_weights(step, wpool32_ref, wbrp32_ref, wbra32_ref, wout32_ref,
                           wpool_ref, wbrp_ref, wbra_ref, wout_ref):
    _convert_chunk(step, wbrp32_ref, wbrp_ref)
    _convert_chunk(step, wbra32_ref, wbra_ref)
    _convert_chunk(step, wout32_ref, wout_ref)

    @pl.when(step == 0)
    def _pool_weight():
        wpool_ref[...] = jnp.zeros(wpool_ref.shape, BF16)
        for g in range(len(POOL_WINDOWS)):
            blk = slice(g * LANES, (g + 1) * LANES)
            wpool_ref[blk, blk] = wpool32_ref[g].astype(BF16)


def _mixer_weight_scratch():
    return [pltpu.VMEM((D_POOL, D_POOL), BF16), pltpu.VMEM((D_POOL, D_MODEL), BF16),
            pltpu.VMEM((D_ATT, D_MODEL), BF16), pltpu.VMEM((D_MODEL, D_MODEL), BF16)]


def _heads_to_sublanes(t):
    heads = jnp.stack([t[:, h * HEAD_DIM:(h + 1) * HEAD_DIM] for h in range(N_HEADS)], axis=0)
    return jnp.swapaxes(heads, 0, 1)


def _ctx_mixer_kernel(x_ref, mod_ref, g_ref, win32_ref, qg_ref, kg_ref, wpool32_ref, pscale_ref,
                      wbrp32_ref, wbra32_ref, wout32_ref, o_ref, ko_ref, vo_ref,
                      win_ref, wpool_ref, wbrp_ref, wbra_ref, wout_ref, *, seq):
    step = pl.program_id(0)

    @pl.when(step < N_CONV)
    def _convert():
        _convert_chunk(step, win32_ref, win_ref)
        _convert_mixer_weights(step, wpool32_ref, wbrp32_ref, wbra32_ref, wout32_ref,
                               wpool_ref, wbrp_ref, wbra_ref, wout_ref)

    @pl.when(step >= N_CONV)
    def _compute():
        x = x_ref[...]
        m_rows = x.shape[0]
        h = _rms_mod(x, g_ref[...], mod_ref[0, 3:4, :], mod_ref[0, 4:5, :]).astype(BF16)
        p = _dot(h, win_ref[:, 0:O_Q])
        q = _dot(h, win_ref[:, O_Q:O_K])
        k = _dot(h, win_ref[:, O_K:O_V])
        v = _dot(h, win_ref[:, O_V:O_G])
        gl = _dot(h, win_ref[:, O_G:])
        vo_ref[...] = _heads_to_sublanes(v)

        lo = _low_half()
        scale = 1.0 / math.sqrt(HEAD_DIM)
        zeros_halo = jnp.zeros((POOL_HALO, D_POOL), F32)
        d_rows = []
        att_rows = []
        for e in range(m_rows // seq):
            rows = slice(e * seq, (e + 1) * seq)
            p_ext = jnp.concatenate([zeros_halo, p[rows], zeros_halo], axis=0)
            d_rows.append(_pool_delta(p_ext, 0, seq))
            att_pairs = []
            kn_pairs = []
            for pr in range(N_PAIRS):
                lanes = slice(pr * LANES, (pr + 1) * LANES)
                kn = _head_norm(k[rows, lanes], kg_ref[...])
                kn_pairs.append(kn)
                qn = _head_norm(q[rows, lanes], qg_ref[...]) * scale
                kb = kn.astype(BF16)
                vb = v[rows, lanes].astype(BF16)
                halves = []
                for half in range(2):
                    qm = jnp.where(lo, qn, 0.0) if half == 0 else jnp.where(lo, 0.0, qn)
                    s = _dot_nt(qm.astype(BF16), kb)
                    (pm,) = _softmax_parts([s])
                    halves.append(_dot(pm, vb))
                att_pairs.append(jnp.where(lo, halves[0], halves[1]))
            att_rows.append(jnp.concatenate(att_pairs, axis=1))
            ko_ref[rows] = _heads_to_sublanes(jnp.concatenate(kn_pairs, axis=1))
        d = jnp.concatenate(d_rows, axis=0)
        att = jnp.concatenate(att_rows, axis=0)
        a = _pool_branch(d, wpool_ref, pscale_ref, wbrp_ref)
        b = _dot(att.astype(BF16), wbra_ref[...])
        o_ref[...] = _merge_out(x, mod_ref[0, 5:6, :], gl, a, b, wout_ref)


def _ctx_mixer(x_all, n_tok, mod3, g, w_in, qg2, kg2, w_pool, pscale, wbrp, wbra, wout, seq):
    tm = CTX_TB * seq
    row = lambda s: (_tile_index(s), 0)
    heads = lambda s: (_tile_index(s), 0, 0)
    return pl.pallas_call(
        functools.partial(_ctx_mixer_kernel, seq=seq),
        grid=(N_CONV + n_tok // tm,),
        in_specs=[pl.BlockSpec((tm, D_MODEL), row),
                  _const_spec((1, N_MOD, D_MODEL)), _const_spec((1, D_MODEL)),
                  _chunk_spec(w_in), _const_spec((1, LANES)), _const_spec((1, LANES)),
                  _const_spec(w_pool.shape), _const_spec((1, D_POOL)),
                  _chunk_spec(wbrp), _chunk_spec(wbra), _chunk_spec(wout)],
        out_specs=[pl.BlockSpec((tm, D_MODEL), row),
                   pl.BlockSpec((tm, N_HEADS, HEAD_DIM), heads),
                   pl.BlockSpec((tm, N_HEADS, HEAD_DIM), heads)],
        out_shape=[jax.ShapeDtypeStruct((n_tok, D_MODEL), F32),
                   jax.ShapeDtypeStruct((n_tok, N_HEADS, HEAD_DIM), F32),
                   jax.ShapeDtypeStruct((n_tok, N_HEADS, HEAD_DIM), F32)],
        scratch_shapes=[pltpu.VMEM(w_in.shape, BF16)] + _mixer_weight_scratch(),
        compiler_params=_params(),
        name="ctx_mixer",
    )(x_all, mod3, g, w_in, qg2, kg2, w_pool, pscale, wbrp, wbra, wout)


def _lat_proj_kernel(x_ref, mod_ref, g_ref, win32_ref, qg_ref, kg_ref,
                     p_ref, q_ref, k_ref, v_ref, ga_ref, gb_ref, win_ref):
    step = pl.program_id(0)

    @pl.when(step < N_CONV)
    def _convert():
        _convert_chunk(step, win32_ref, win_ref)

    @pl.when(step >= N_CONV)
    def _compute():
        h = _rms_mod(x_ref[...], g_ref[...], mod_ref[0, 3:4, :], mod_ref[0, 4:5, :]).astype(BF16)
        p_ref[...] = _dot(h, win_ref[:, 0:O_Q])
        q = _dot(h, win_ref[:, O_Q:O_K])
        k = _dot(h, win_ref[:, O_K:O_V])
        v_ref[...] = _dot(h, win_ref[:, O_V:O_G]).astype(BF16)
        scale = 1.0 / math.sqrt(HEAD_DIM)
        for pr in range(N_PAIRS):
            lanes = slice(pr * LANES, (pr + 1) * LANES)
            q_ref[:, lanes] = (_head_norm(q[:, lanes], qg_ref[...]) * scale).astype(BF16)
            k_ref[:, lanes] = _head_norm(k[:, lanes], kg_ref[...]).astype(BF16)
        ga_ref[...] = jax.nn.sigmoid(_dot(h, win_ref[:, O_G:O_G + D_MODEL]))
        gb_ref[...] = jax.nn.sigmoid(_dot(h, win_ref[:, O_G + D_MODEL:]))


def _lat_proj(x_all, tok0, n_tok, seq, mod3, g, w_in, qg2, kg2):
    tm = PROJ_TM
    tile0 = tok0 // tm
    row = lambda s: (_tile_index(s), 0)
    shapes = [(D_POOL, F32), (D_ATT, BF16), (D_ATT, BF16), (D_ATT, BF16), (D_MODEL, F32), (D_MODEL, F32)]
    return pl.pallas_call(
        _lat_proj_kernel,
        grid=(N_CONV + n_tok // tm,),
        in_specs=[pl.BlockSpec((tm, D_MODEL), lambda s: (tile0 + _tile_index(s), 0)),
                  pl.BlockSpec((1, N_MOD, D_MODEL), lambda s: (1 + (_tile_index(s) * tm) // seq, 0, 0)),
                  _const_spec((1, D_MODEL)), _chunk_spec(w_in),
                  _const_spec((1, LANES)), _const_spec((1, LANES))],
        out_specs=[pl.BlockSpec((tm, width), row) for width, _ in shapes],
        out_shape=[jax.ShapeDtypeStruct((n_tok, width), dtype) for width, dtype in shapes],
        scratch_shapes=[pltpu.VMEM(w_in.shape, BF16)],
        compiler_params=_params(),
        name="lat_proj",
    )(x_all, mod3, g, w_in, qg2, kg2)


def _build_bias_table(rpb_ref, bias_ref):
    lane = lax.broadcasted_iota(jnp.int32, (GRID_W, LANES), 1)
    qc = lax.broadcasted_iota(jnp.int32, (GRID_W, LANES), 0)
    kc = lane & (GRID_W - 1)
    c0 = jnp.clip(qc - NA_KW // 2, 0, GRID_W - NA_KW)
    col_ok = (kc >= c0) & (kc < c0 + NA_KW)
    first_half = lane < GRID_W
    shift_a = LANES - (NA_KW - 1)
    shift_b = (GRID_W - (NA_KW - 1)) % LANES
    for h in range(N_HEADS):
        for i in range(2 * NA_KH - 2):
            va = jnp.broadcast_to(rpb_ref[h, i:i + 1, :], (GRID_W, LANES))
            vb = jnp.broadcast_to(rpb_ref[h, i + 1:i + 2, :], (GRID_W, LANES))
            ra = pltpu.roll(va, shift_a, 1, stride=1, stride_axis=0)
            rb = pltpu.roll(vb, shift_b, 1, stride=1, stride_axis=0)
            bias_ref[h, i] = jnp.where(col_ok, jnp.where(first_half, ra, rb), NEG)


def _lat_attention(qn, j, fillers, k_ref, v_ref, ck_ref, cv_ref, bias_ref, *, n_rows):
    n_keys = LAT_KEY_ROWS * GRID_W
    r0q = j * LAT_ROWS
    u0 = jnp.clip(r0q - NA_KH // 2, 0, n_rows - LAT_KEY_ROWS)
    key0 = pl.multiple_of(u0 * GRID_W, GRID_W)
    log2_w = GRID_W.bit_length() - 1
    lo = _low_half()
    zero = jnp.zeros((), BF16)
    one = jnp.ones((), BF16)

    lane_row = lax.broadcasted_iota(jnp.int32, (1, LANES), 1) & (HEAD_DIM - 1)
    q_tile_row = jnp.right_shift(lax.broadcasted_iota(jnp.int32, (LAT_TM, 1), 0), log2_w)
    q_hot = jnp.where(lane_row == q_tile_row, 1.0, 0.0).astype(BF16)
    k_row = u0 + jnp.right_shift(lax.broadcasted_iota(jnp.int32, (n_keys, 1), 0), log2_w)
    first = jnp.clip(r0q + lane_row - NA_KH // 2, 0, n_rows - NA_KH)
    outside = (lane_row < LAT_ROWS) & ((k_row < first) | (k_row >= first + NA_KH))
    k_mask = jnp.where(outside, NEG, 0.0).astype(BF16)

    def own(head, mine, other):
        return jnp.where(lo, mine, other) if head % 2 == 0 else jnp.where(lo, other, mine)

    def scores(head):
        lanes = slice((head // 2) * LANES, (head // 2 + 1) * LANES)
        qp = qn[:, lanes]
        s_loc = _dot_nt(own(head, qp, q_hot), own(head, k_ref[0, pl.ds(key0, n_keys), lanes], k_mask))
        return s_loc, _dot_nt(own(head, qp, zero), ck_ref[0, :, lanes])

    def weights(head, s_loc, s_ctx):
        bias_rows = []
        for jr in range(LAT_ROWS):
            blocks = []
            for pi in range(LAT_KEY_ROWS // 2):
                idx = jnp.clip(u0 - r0q + 2 * pi - jr + NA_KH - 1, 0, 2 * NA_KH - 3)
                blocks.append(bias_ref[head, idx])
            bias_rows.append(jnp.concatenate(blocks, axis=1))
        s_loc = s_loc + jnp.concatenate(bias_rows, axis=0)
        m = jnp.maximum(s_loc.max(axis=-1, keepdims=True), s_ctx.max(axis=-1, keepdims=True))
        return jnp.concatenate([jnp.exp(s_loc - m).astype(BF16), jnp.exp(s_ctx - m).astype(BF16)], axis=1)

    def weighted_values(head, e):
        lanes = slice((head // 2) * LANES, (head // 2 + 1) * LANES)
        values = jnp.concatenate([own(head, v_ref[0, pl.ds(key0, n_keys), lanes], one),
                                  own(head, cv_ref[0, :, lanes], one)], axis=0)
        return _dot(e, values)

    outs = []
    pending_scores = [scores(head) for head in range(SCORE_LOOKAHEAD)]
    pending_weights = [weights(0, *pending_scores.pop(0))]
    for head in range(N_HEADS):
        if head + SCORE_LOOKAHEAD < N_HEADS:
            pending_scores.append(scores(head + SCORE_LOOKAHEAD))
        fillers[head]()
        if head + 1 < N_HEADS:
            pending_weights.append(weights(head + 1, *pending_scores.pop(0)))
        outs.append(weighted_values(head, pending_weights.pop(0)))
    att_pairs = []
    for pr in range(N_PAIRS):
        even, odd = outs[2 * pr], outs[2 * pr + 1]
        num = jnp.where(lo, even, odd)
        den = pltpu.roll(jnp.where(lo, odd, even), HEAD_DIM, 1)
        att_pairs.append(num / den)
    return jnp.concatenate(att_pairs, axis=1)


def _lat_mixer_kernel(x_ref, q_ref, ga_ref, gb_ref, p_ref, pp_ref, pn_ref, mod_ref, k_ref, v_ref,
                      ck_ref, cv_ref, rpb_ref, wpool32_ref, pscale_ref, wbrp32_ref, wbra32_ref, wout32_ref,
                      o_ref, bias_ref, wpool_ref, wbrp_ref, wbra_ref, wout_ref, *, seq, n_rows):
    step = pl.program_id(0)
    tm = LAT_TILES * LAT_TM

    @pl.when(step < N_CONV)
    def _convert():
        _convert_mixer_weights(step, wpool32_ref, wbrp32_ref, wbra32_ref, wout32_ref,
                               wpool_ref, wbrp_ref, wbra_ref, wout_ref)

        @pl.when(step == 0)
        def _bias():
            _build_bias_table(rpb_ref, bias_ref)

    @pl.when(step >= N_CONV)
    def _compute():
        blk = lax.rem(step - N_CONV, seq // tm)
        gate2 = mod_ref[0, 5:6, :]
        attention = functools.partial(_lat_attention, k_ref=k_ref, v_ref=v_ref, ck_ref=ck_ref, cv_ref=cv_ref,
                                      bias_ref=bias_ref, n_rows=n_rows)
        st = {}

        def pool_delta():
            p_ext = jnp.concatenate([pp_ref[...], p_ref[...], pn_ref[...]], axis=0)
            pos = blk * tm - POOL_HALO + lax.broadcasted_iota(jnp.int32, (tm + 2 * POOL_HALO, 1), 0)
            p_ext = jnp.where((pos >= 0) & (pos < seq), p_ext, 0.0)
            st["d"] = _pool_delta(p_ext, blk * tm, seq).astype(BF16)

        def pool_weight():
            st["y"] = (_dot(st.pop("d"), wpool_ref[...]) * pscale_ref[...]).astype(BF16)

        def pool_out():
            st["a"] = _dot(st.pop("y"), wbrp_ref[...])

        def merge_pieces(t, att):
            rows = slice(t * LAT_TM, (t + 1) * LAT_TM)

            def attention_branch():
                st["b"] = _dot(att.astype(BF16), wbra_ref[...])

            def gate():
                merged = ga_ref[rows, :] * st["a"][rows] + gb_ref[rows, :] * st.pop("b")
                st["merged"] = merged.astype(BF16)

            def out():
                o_ref[rows, :] = x_ref[rows, :] + gate2 * _dot(st.pop("merged"), wout_ref[...])

            return [attention_branch, gate, out]

        idle = lambda: None
        fillers = [pool_delta, pool_weight, pool_out] + [idle] * (N_HEADS - 3)
        for t in range(LAT_TILES):
            att = attention(q_ref[t * LAT_TM:(t + 1) * LAT_TM, :], blk * LAT_TILES + t, fillers)
            pieces = merge_pieces(t, att)
            fillers = [idle] + pieces + [idle] * (N_HEADS - 1 - len(pieces))
        for piece in pieces:
            piece()


def _lat_mixer(x_all, tok0, n_batch, seq, mod3, q, ga, gb, p, k3, v3, ck3, cv3, rpb_pad, w_pool, pscale,
               wbrp, wbra, wout):
    tm = LAT_TILES * LAT_TM
    n_rows = seq // GRID_W
    nb = seq // tm
    tile0 = tok0 // tm
    halo_per_tile = tm // POOL_HALO
    n_halo = p.shape[0] // POOL_HALO
    tile = _tile_index
    row = lambda s: (tile(s), 0)
    seq_map = lambda s: (tile(s) // nb, 0, 0)
    return pl.pallas_call(
        functools.partial(_lat_mixer_kernel, seq=seq, n_rows=n_rows),
        grid=(N_CONV + n_batch * nb,),
        in_specs=[pl.BlockSpec((tm, D_MODEL), lambda s: (tile0 + tile(s), 0)),
                  pl.BlockSpec((tm, D_ATT), row),
                  pl.BlockSpec((tm, D_MODEL), row), pl.BlockSpec((tm, D_MODEL), row),
                  pl.BlockSpec((tm, D_POOL), row),
                  pl.BlockSpec((POOL_HALO, D_POOL), lambda s: (jnp.maximum(tile(s) * halo_per_tile - 1, 0), 0)),
                  pl.BlockSpec((POOL_HALO, D_POOL),
                               lambda s: (jnp.minimum((tile(s) + 1) * halo_per_tile, n_halo - 1), 0)),
                  pl.BlockSpec((1, N_MOD, D_MODEL), lambda s: (1 + tile(s) // nb, 0, 0)),
                  pl.BlockSpec((1, seq, D_ATT), seq_map), pl.BlockSpec((1, seq, D_ATT), seq_map),
                  pl.BlockSpec((1,) + ck3.shape[1:], seq_map), pl.BlockSpec((1,) + cv3.shape[1:], seq_map),
                  _const_spec(rpb_pad.shape),
                  _const_spec(w_pool.shape), _const_spec((1, D_POOL)),
                  _chunk_spec(wbrp), _chunk_spec(wbra), _chunk_spec(wout)],
        out_specs=pl.BlockSpec((tm, D_MODEL), row),
        out_shape=jax.ShapeDtypeStruct((n_batch * seq, D_MODEL), F32),
        scratch_shapes=[pltpu.VMEM((N_HEADS, 2 * NA_KH - 2, GRID_W, LANES), F32)] + _mixer_weight_scratch(),
        compiler_params=_params(),
        name="lat_mixer",
    )(x_all, q, ga, gb, p, p, p, mod3, k3, v3, ck3, cv3, rpb_pad, w_pool, pscale, wbrp, wbra, wout)


def kernel(x_prompt, x_sample, cache_k, cache_v, c, c_ctx, w_ada, b_ada, g_ff1, w_ff1_in, w_ff1_out,
           g_mix, w_in, q_gain, k_gain, w_pool, pool_scale, rpb, w_br_pool, w_br_att, w_out, g_ff2,
           w_ff2_in, w_ff2_out):
    n_ctx, seq_ctx, _ = x_prompt.shape
    n_lat, seq_lat, _ = x_sample.shape
    depth = w_ada.shape[0]
    assert depth == 1 and n_lat + 1 <= 8
    l = 0

    xp = x_prompt.reshape(n_ctx * seq_ctx, D_MODEL)
    xs = x_sample.reshape(n_lat * seq_lat, D_MODEL)
    t_ctx, t_lat = xp.shape[0], xs.shape[0]
    cond8 = jnp.zeros((8, D_MODEL), F32).at[0].set(c_ctx).at[1:1 + n_lat].set(c)
    mod3 = _adaln(cond8, w_ada[l], b_ada[l]).reshape(8, N_MOD, D_MODEL)

    pscale = pool_scale[l].reshape(1, D_POOL)
    qg2 = jnp.tile(q_gain[l], 2).reshape(1, LANES)
    kg2 = jnp.tile(k_gain[l], 2).reshape(1, LANES)
    g1, gm, g2 = g_ff1[l].reshape(1, -1), g_mix[l].reshape(1, -1), g_ff2[l].reshape(1, -1)
    rpb_pad = jnp.pad(rpb[l], ((0, 0), (0, 0), (0, LANES - rpb.shape[-1])))
    ck3 = cache_k[:, l].reshape(n_lat, -1, D_ATT).astype(BF16)
    cv3 = cache_v[:, l].reshape(n_lat, -1, D_ATT).astype(BF16)

    (x1,) = _ffn(xp, xs, mod3, seq_lat, 0, g1, w_ff1_in[l], w_ff1_out[l], "ffn1")
    x2p, k_p, v_p = _ctx_mixer(x1, t_ctx, mod3, gm, w_in[l], qg2, kg2, w_pool[l], pscale,
                               w_br_pool[l], w_br_att[l], w_out[l], seq_ctx)
    p_s, q_s, k_s, v_s, ga_s, gb_s = _lat_proj(x1, t_ctx, t_lat, seq_lat, mod3, gm, w_in[l], qg2, kg2)
    x2s = _lat_mixer(x1, t_ctx, n_lat, seq_lat, mod3, q_s, ga_s, gb_s, p_s,
                     k_s.reshape(n_lat, seq_lat, D_ATT), v_s.reshape(n_lat, seq_lat, D_ATT), ck3, cv3,
                     rpb_pad, w_pool[l], pscale, w_br_pool[l], w_br_att[l], w_out[l])
    yp, ys = _ffn(x2p, x2s, mod3, seq_lat, 6, g2, w_ff2_in[l], w_ff2_out[l], "ffn2", split_out=True)

    new_k = k_p.reshape(n_ctx, 1, seq_ctx, N_HEADS, HEAD_DIM)
    new_v = v_p.reshape(n_ctx, 1, seq_ctx, N_HEADS, HEAD_DIM)
    return (yp.reshape(x_prompt.shape), ys.reshape(x_sample.shape), new_k, new_v)
```

```python
import functools
import math

import jax
import jax.numpy as jnp
from jax import lax
from jax.experimental import pallas as pl
from jax.experimental.pallas import tpu as pltpu

F32 = jnp.float32
BF16 = jnp.bfloat16

D_MODEL = 1024
N_HEADS = 8
HEAD_DIM = 64
D_ATT = N_HEADS * HEAD_DIM
D_POOL = 512
POOL_WINDOWS = (2, 4, 8, 16)
POOL_HALO = 8
D_FF = 2816
N_MOD = 9
GRID_W = 64
NA_KH = 8
NA_KW = 16
EPS = 1e-6
NEG = -1e30

LANES = 128
N_PAIRS = D_ATT // LANES
MXU_COLS = 256
FF_CHUNKS = ((0, 768), (768, 768), (1536, 768), (2304, 512))
ADALN_TN = 2304
FFN_TM = 512
CTX_TB = 2
PROJ_TM = 512
LAT_ROWS = 4
LAT_TM = LAT_ROWS * GRID_W
LAT_TILES = 2
LAT_KEY_ROWS = 12
SCORE_LOOKAHEAD = 2
N_CONV = 8
VMEM_LIMIT = 56 * 1024 * 1024

O_Q = D_POOL
O_K = D_POOL + D_ATT
O_V = D_POOL + 2 * D_ATT
O_G = D_POOL + 3 * D_ATT


def _dot(a, b):
    return jnp.dot(a, b, preferred_element_type=F32)


def _dot_nt(a, b):
    return lax.dot_general(a, b, (((1,), (1,)), ((), ())), preferred_element_type=F32)


def _rms_mod(x, g, shift, scale):
    ms = jnp.mean(x * x, axis=-1, keepdims=True)
    y = x * lax.rsqrt(ms + EPS) * g
    return y * (1.0 + scale) + shift


def _low_half():
    return lax.broadcasted_iota(jnp.int32, (1, LANES), 1) < HEAD_DIM


def _head_norm(t, gain2):
    lo = _low_half()
    sq = t * t
    s0 = jnp.sum(jnp.where(lo, sq, 0.0), axis=-1, keepdims=True)
    s1 = jnp.sum(jnp.where(lo, 0.0, sq), axis=-1, keepdims=True)
    r0 = lax.rsqrt(s0 * (1.0 / HEAD_DIM) + EPS)
    r1 = lax.rsqrt(s1 * (1.0 / HEAD_DIM) + EPS)
    return t * jnp.where(lo, r0, r1) * gain2


def _softmax_parts(parts):
    m = parts[0].max(axis=-1, keepdims=True)
    for s in parts[1:]:
        m = jnp.maximum(m, s.max(axis=-1, keepdims=True))
    es = [jnp.exp(s - m) for s in parts]
    l = es[0].sum(axis=-1, keepdims=True)
    for e in es[1:]:
        l = l + e.sum(axis=-1, keepdims=True)
    inv = 1.0 / l
    return [(e * inv).astype(BF16) for e in es]


def _pool_delta(p_ext, pos0, seq_len):
    te = p_ext.shape[0]
    tm = te - 2 * POOL_HALO
    t = (pos0 + lax.broadcasted_iota(jnp.int32, (tm, 1), 0)).astype(F32)
    outs = []
    for g, w in enumerate(POOL_WINDOWS):
        a = p_ext[:, g * LANES:(g + 1) * LANES]
        s = a
        step = 1
        while step < w:
            s = s + pltpu.roll(s, step, 0)
            step *= 2
        ahead = w // 2 - 1
        if ahead:
            s = pltpu.roll(s, te - ahead, 0)
        win = s[POOL_HALO:POOL_HALO + tm]
        cnt = jnp.minimum(t + float(w // 2), float(seq_len)) - jnp.maximum(t - float(w // 2), 0.0)
        outs.append(win / cnt - a[POOL_HALO:POOL_HALO + tm])
    return jnp.concatenate(outs, axis=1)


def _pool_branch(d, wpool_ref, pscale_ref, wbrp_ref):
    y = _dot(d.astype(BF16), wpool_ref[...]) * pscale_ref[...]
    return _dot(y.astype(BF16), wbrp_ref[...])


def _merge_out(x, gate2, gl, a, b, wout_ref):
    gates = jax.nn.sigmoid(gl)
    merged = gates[:, :D_MODEL] * a + gates[:, D_MODEL:] * b
    return x + gate2 * _dot(merged.astype(BF16), wout_ref[...])


def _chunk_spec(w, col_block=None):
    rows = w.shape[0] // N_CONV
    assert rows * N_CONV == w.shape[0] and rows % 16 == 0
    width, cidx = col_block if col_block else (w.shape[1], 0)
    return pl.BlockSpec((rows, width), lambda s, *_: (jnp.minimum(s, N_CONV - 1), cidx))


def _convert_chunk(step, src_ref, dst_ref):
    rows = src_ref.shape[0]
    r0 = pl.multiple_of(step * rows, rows)
    dst_ref[pl.ds(r0, rows), :] = src_ref[...].astype(BF16)


def _const_spec(shape):
    nd = len(shape)
    return pl.BlockSpec(shape, lambda *_: (0,) * nd)


def _tile_index(step):
    return jnp.maximum(step - N_CONV, 0)


def _params():
    return pltpu.CompilerParams(dimension_semantics=("arbitrary",), vmem_limit_bytes=VMEM_LIMIT)


def _adaln_kernel(cond_ref, w_ref, b_ref, o_ref):
    c = cond_ref[...]
    s = c * jax.nn.sigmoid(c)
    o_ref[...] = _dot(s.astype(BF16), w_ref[...].astype(BF16)) + b_ref[...]


def _adaln(cond8, w_ada, b_ada):
    n = w_ada.shape[1]
    tn = ADALN_TN
    return pl.pallas_call(
        _adaln_kernel,
        grid=(n // tn,),
        in_specs=[pl.BlockSpec((8, D_MODEL), lambda i: (0, 0)),
                  pl.BlockSpec((D_MODEL, tn), lambda i: (0, i)),
                  pl.BlockSpec((1, tn), lambda i: (0, i))],
        out_specs=pl.BlockSpec((8, tn), lambda i: (0, i)),
        out_shape=jax.ShapeDtypeStruct((8, n), F32),
        compiler_params=_params(),
        name="adaln",
    )(cond8, w_ada, b_ada.reshape(1, n))


def _ffn_kernel(*refs, mod_off, n_a, split_out):
    xa_ref, xb_ref, mod_ref, g_ref, wgu32_ref, wdn32_ref = refs[:6]
    n_out = 2 if split_out else 1
    out_refs = refs[6:6 + n_out]
    wgu_ref, wdn_ref = refs[6 + n_out:]
    step = pl.program_id(0)

    @pl.when(step < N_CONV)
    def _convert():
        _convert_chunk(step, wgu32_ref, wgu_ref)
        _convert_chunk(step, wdn32_ref, wdn_ref)

    @pl.when(step >= N_CONV)
    def _compute():
        is_a = step - N_CONV < n_a
        shift = mod_ref[0, mod_off:mod_off + 1, :]
        scale = mod_ref[0, mod_off + 1:mod_off + 2, :]
        gate = mod_ref[0, mod_off + 2:mod_off + 3, :]
        x = jnp.where(is_a, xa_ref[...], xb_ref[...])
        h = _rms_mod(x, g_ref[...], shift, scale).astype(BF16)
        acc = None
        for c0, cw in FF_CHUNKS:
            a = _dot(h, wgu_ref[:, c0:c0 + cw])
            u = _dot(h, wgu_ref[:, D_FF + c0:D_FF + c0 + cw])
            t = (a * jax.nn.sigmoid(a) * u).astype(BF16)
            part = _dot(t, wdn_ref[c0:c0 + cw, :])
            acc = part if acc is None else acc + part
        y = x + (0.5 * gate) * acc
        if split_out:
            @pl.when(is_a)
            def _store_a():
                out_refs[0][...] = y

            @pl.when(jnp.logical_not(is_a))
            def _store_b():
                out_refs[1][...] = y
        else:
            out_refs[0][...] = y


def _ffn(xa, xb, mod3, seq_b, mod_off, g, wgu, wdn, name, split_out=False):
    tm = FFN_TM
    n_a, n_b = xa.shape[0] // tm, xb.shape[0] // tm
    tile = _tile_index
    a_map = lambda s: (jnp.minimum(tile(s), n_a - 1), 0)
    b_map = lambda s: (jnp.maximum(tile(s) - n_a, 0), 0)
    mod_map = lambda s: (jnp.where(tile(s) < n_a, 0, 1 + (jnp.maximum(tile(s) - n_a, 0) * tm) // seq_b), 0, 0)
    if split_out:
        out_specs = [pl.BlockSpec((tm, D_MODEL), a_map), pl.BlockSpec((tm, D_MODEL), b_map)]
        out_shape = [jax.ShapeDtypeStruct(xa.shape, F32), jax.ShapeDtypeStruct(xb.shape, F32)]
    else:
        out_specs = [pl.BlockSpec((tm, D_MODEL), lambda s: (tile(s), 0))]
        out_shape = [jax.ShapeDtypeStruct((xa.shape[0] + xb.shape[0], D_MODEL), F32)]
    return pl.pallas_call(
        functools.partial(_ffn_kernel, mod_off=mod_off, n_a=n_a, split_out=split_out),
        grid=(N_CONV + n_a + n_b,),
        in_specs=[pl.BlockSpec((tm, D_MODEL), a_map), pl.BlockSpec((tm, D_MODEL), b_map),
                  pl.BlockSpec((1, N_MOD, D_MODEL), mod_map), _const_spec((1, D_MODEL)),
                  _chunk_spec(wgu), _chunk_spec(wdn)],
        out_specs=out_specs, out_shape=out_shape,
        scratch_shapes=[pltpu.VMEM(wgu.shape, BF16), pltpu.VMEM(wdn.shape, BF16)],
        compiler_params=_params(),
        name=name,
    )(xa, xb, mod3, g, wgu, wdn)


def _convert_mixer_weights(step, wpool32_ref, wbrp32_ref, wbra32_ref, wout32_ref,
                           wpool_ref, wbrp_ref, wbra_ref, wout_ref):
    _convert_chunk(step, wbrp32_ref, wbrp_ref)
    _convert_chunk(step, wbra32_ref, wbra_ref)
    _convert_chunk(step, wout32_ref, wout_ref)

    @pl.when(step == 0)
    def _pool_weight():
        wpool_ref[...] = jnp.zeros(wpool_ref.shape, BF16)
        for g in range(len(POOL_WINDOWS)):
            blk = slice(g * LANES, (g + 1) * LANES)
            wpool_ref[blk, blk] = wpool32_ref[g].astype(BF16)


def _mixer_weight_scratch():
    return [pltpu.VMEM((D_POOL, D_POOL), BF16), pltpu.VMEM((D_POOL, D_MODEL), BF16),
            pltpu.VMEM((D_ATT, D_MODEL), BF16), pltpu.VMEM((D_MODEL, D_MODEL), BF16)]


def _heads_to_sublanes(t):
    heads = jnp.stack([t[:, h * HEAD_DIM:(h + 1) * HEAD_DIM] for h in range(N_HEADS)], axis=0)
    return jnp.swapaxes(heads, 0, 1)


def _ctx_mixer_kernel(x_ref, mod_ref, g_ref, win32_ref, qg_ref, kg_ref, wpool32_ref, pscale_ref,
                      wbrp32_ref, wbra32_ref, wout32_ref, o_ref, ko_ref, vo_ref,
                      win_ref, wpool_ref, wbrp_ref, wbra_ref, wout_ref, *, seq):
    step = pl.program_id(0)

    @pl.when(step < N_CONV)
    def _convert():
        _convert_chunk(step, win32_ref, win_ref)
        _convert_mixer_weights(step, wpool32_ref, wbrp32_ref, wbra32_ref, wout32_ref,
                               wpool_ref, wbrp_ref, wbra_ref, wout_ref)

    @pl.when(step >= N_CONV)
    def _compute():
        x = x_ref[...]
        m_rows = x.shape[0]
        h = _rms_mod(x, g_ref[...], mod_ref[0, 3:4, :], mod_ref[0, 4:5, :]).astype(BF16)
        p = _dot(h, win_ref[:, 0:O_Q])
        q = _dot(h, win_ref[:, O_Q:O_K])
        k = _dot(h, win_ref[:, O_K:O_V])
        v = _dot(h, win_ref[:, O_V:O_G])
        gl = _dot(h, win_ref[:, O_G:])
        vo_ref[...] = _heads_to_sublanes(v)

        lo = _low_half()
        scale = 1.0 / math.sqrt(HEAD_DIM)
        zeros_halo = jnp.zeros((POOL_HALO, D_POOL), F32)
        d_rows = []
        att_rows = []
        for e in range(m_rows // seq):
            rows = slice(e * seq, (e + 1) * seq)
            p_ext = jnp.concatenate([zeros_halo, p[rows], zeros_halo], axis=0)
            d_rows.append(_pool_delta(p_ext, 0, seq))
            att_pairs = []
            kn_pairs = []
            for pr in range(N_PAIRS):
                lanes = slice(pr * LANES, (pr + 1) * LANES)
                kn = _head_norm(k[rows, lanes], kg_ref[...])
                kn_pairs.append(kn)
                qn = _head_norm(q[rows, lanes], qg_ref[...]) * scale
                kb = kn.astype(BF16)
                vb = v[rows, lanes].astype(BF16)
                halves = []
                for half in range(2):
                    qm = jnp.where(lo, qn, 0.0) if half == 0 else jnp.where(lo, 0.0, qn)
                    s = _dot_nt(qm.astype(BF16), kb)
                    (pm,) = _softmax_parts([s])
                    halves.append(_dot(pm, vb))
                att_pairs.append(jnp.where(lo, halves[0], halves[1]))
            att_rows.append(jnp.concatenate(att_pairs, axis=1))
            ko_ref[rows] = _heads_to_sublanes(jnp.concatenate(kn_pairs, axis=1))
        d = jnp.concatenate(d_rows, axis=0)
        att = jnp.concatenate(att_rows, axis=0)
        a = _pool_branch(d, wpool_ref, pscale_ref, wbrp_ref)
        b = _dot(att.astype(BF16), wbra_ref[...])
        o_ref[...] = _merge_out(x, mod_ref[0, 5:6, :], gl, a, b, wout_ref)


def _ctx_mixer(x_all, n_tok, mod3, g, w_in, qg2, kg2, w_pool, pscale, wbrp, wbra, wout, seq):
    tm = CTX_TB * seq
    row = lambda s: (_tile_index(s), 0)
    heads = lambda s: (_tile_index(s), 0, 0)
    return pl.pallas_call(
        functools.partial(_ctx_mixer_kernel, seq=seq),
        grid=(N_CONV + n_tok // tm,),
        in_specs=[pl.BlockSpec((tm, D_MODEL), row),
                  _const_spec((1, N_MOD, D_MODEL)), _const_spec((1, D_MODEL)),
                  _chunk_spec(w_in), _const_spec((1, LANES)), _const_spec((1, LANES)),
                  _const_spec(w_pool.shape), _const_spec((1, D_POOL)),
                  _chunk_spec(wbrp), _chunk_spec(wbra), _chunk_spec(wout)],
        out_specs=[pl.BlockSpec((tm, D_MODEL), row),
                   pl.BlockSpec((tm, N_HEADS, HEAD_DIM), heads),
                   pl.BlockSpec((tm, N_HEADS, HEAD_DIM), heads)],
        out_shape=[jax.ShapeDtypeStruct((n_tok, D_MODEL), F32),
                   jax.ShapeDtypeStruct((n_tok, N_HEADS, HEAD_DIM), F32),
                   jax.ShapeDtypeStruct((n_tok, N_HEADS, HEAD_DIM), F32)],
        scratch_shapes=[pltpu.VMEM(w_in.shape, BF16)] + _mixer_weight_scratch(),
        compiler_params=_params(),
        name="ctx_mixer",
    )(x_all, mod3, g, w_in, qg2, kg2, w_pool, pscale, wbrp, wbra, wout)


def _lat_proj_kernel(x_ref, mod_ref, g_ref, win32_ref, qg_ref, kg_ref,
                     h_ref, p_ref, q_ref, k_ref, v_ref, win_ref):
    step = pl.program_id(0)

    @pl.when(step < N_CONV)
    def _convert():
        _convert_chunk(step, win32_ref, win_ref)

    @pl.when(step >= N_CONV)
    def _compute():
        h = _rms_mod(x_ref[...], g_ref[...], mod_ref[0, 3:4, :], mod_ref[0, 4:5, :]).astype(BF16)
        h_ref[...] = h
        p_ref[...] = _dot(h, win_ref[:, 0:O_Q])
        q = _dot(h, win_ref[:, O_Q:O_K])
        k = _dot(h, win_ref[:, O_K:O_V])
        v_ref[...] = _dot(h, win_ref[:, O_V:O_G]).astype(BF16)
        scale = 1.0 / math.sqrt(HEAD_DIM)
        for pr in range(N_PAIRS):
            lanes = slice(pr * LANES, (pr + 1) * LANES)
            q_ref[:, lanes] = (_head_norm(q[:, lanes], qg_ref[...]) * scale).astype(BF16)
            k_ref[:, lanes] = _head_norm(k[:, lanes], kg_ref[...]).astype(BF16)


def _lat_proj(x_all, tok0, n_tok, seq, mod3, g, w_in, qg2, kg2):
    tm = PROJ_TM
    tile0 = tok0 // tm
    row = lambda s: (_tile_index(s), 0)
    shapes = [(D_MODEL, BF16), (D_POOL, F32), (D_ATT, BF16), (D_ATT, BF16), (D_ATT, BF16)]
    return pl.pallas_call(
        _lat_proj_kernel,
        grid=(N_CONV + n_tok // tm,),
        in_specs=[pl.BlockSpec((tm, D_MODEL), lambda s: (tile0 + _tile_index(s), 0)),
                  pl.BlockSpec((1, N_MOD, D_MODEL), lambda s: (1 + (_tile_index(s) * tm) // seq, 0, 0)),
                  _const_spec((1, D_MODEL)), _chunk_spec(w_in, (O_G, 0)),
                  _const_spec((1, LANES)), _const_spec((1, LANES))],
        out_specs=[pl.BlockSpec((tm, width), row) for width, _ in shapes],
        out_shape=[jax.ShapeDtypeStruct((n_tok, width), dtype) for width, dtype in shapes],
        scratch_shapes=[pltpu.VMEM((D_MODEL, O_G), BF16)],
        compiler_params=_params(),
        name="lat_proj",
    )(x_all, mod3, g, w_in, qg2, kg2)


def _build_bias_table(rpb_ref, bias_ref):
    lane = lax.broadcasted_iota(jnp.int32, (GRID_W, LANES), 1)
    qc = lax.broadcasted_iota(jnp.int32, (GRID_W, LANES), 0)
    kc = lane & (GRID_W - 1)
    c0 = jnp.clip(qc - NA_KW // 2, 0, GRID_W - NA_KW)
    col_ok = (kc >= c0) & (kc < c0 + NA_KW)
    first_half = lane < GRID_W
    shift_a = LANES - (NA_KW - 1)
    shift_b = (GRID_W - (NA_KW - 1)) % LANES
    for h in range(N_HEADS):
        for i in range(2 * NA_KH - 2):
            va = jnp.broadcast_to(rpb_ref[h, i:i + 1, :], (GRID_W, LANES))
            vb = jnp.broadcast_to(rpb_ref[h, i + 1:i + 2, :], (GRID_W, LANES))
            ra = pltpu.roll(va, shift_a, 1, stride=1, stride_axis=0)
            rb = pltpu.roll(vb, shift_b, 1, stride=1, stride_axis=0)
            bias_ref[h, i] = jnp.where(col_ok, jnp.where(first_half, ra, rb), NEG)


def _lat_attention(qn, j, fillers, k_ref, v_ref, ck_ref, cv_ref, bias_ref, *, n_rows):
    n_keys = LAT_KEY_ROWS * GRID_W
    r0q = j * LAT_ROWS
    u0 = jnp.clip(r0q - NA_KH // 2, 0, n_rows - LAT_KEY_ROWS)
    key0 = pl.multiple_of(u0 * GRID_W, GRID_W)
    log2_w = GRID_W.bit_length() - 1
    lo = _low_half()
    zero = jnp.zeros((), BF16)
    one = jnp.ones((), BF16)

    lane_row = lax.broadcasted_iota(jnp.int32, (1, LANES), 1) & (HEAD_DIM - 1)
    q_tile_row = jnp.right_shift(lax.broadcasted_iota(jnp.int32, (LAT_TM, 1), 0), log2_w)
    q_hot = jnp.where(lane_row == q_tile_row, 1.0, 0.0).astype(BF16)
    k_row = u0 + jnp.right_shift(lax.broadcasted_iota(jnp.int32, (n_keys, 1), 0), log2_w)
    first = jnp.clip(r0q + lane_row - NA_KH // 2, 0, n_rows - NA_KH)
    outside = (lane_row < LAT_ROWS) & ((k_row < first) | (k_row >= first + NA_KH))
    k_mask = jnp.where(outside, NEG, 0.0).astype(BF16)

    def own(head, mine, other):
        return jnp.where(lo, mine, other) if head % 2 == 0 else jnp.where(lo, other, mine)

    def scores(head):
        lanes = slice((head // 2) * LANES, (head // 2 + 1) * LANES)
        qp = qn[:, lanes]
        s_loc = _dot_nt(own(head, qp, q_hot), own(head, k_ref[0, pl.ds(key0, n_keys), lanes], k_mask))
        return s_loc, _dot_nt(own(head, qp, zero), ck_ref[0, :, lanes])

    def weights(head, s_loc, s_ctx):
        bias_rows = []
        for jr in range(LAT_ROWS):
            blocks = []
            for pi in range(LAT_KEY_ROWS // 2):
                idx = jnp.clip(u0 - r0q + 2 * pi - jr + NA_KH - 1, 0, 2 * NA_KH - 3)
                blocks.append(bias_ref[head, idx])
            bias_rows.append(jnp.concatenate(blocks, axis=1))
        s_loc = s_loc + jnp.concatenate(bias_rows, axis=0)
        m = jnp.maximum(s_loc.max(axis=-1, keepdims=True), s_ctx.max(axis=-1, keepdims=True))
        return jnp.concatenate([jnp.exp(s_loc - m).astype(BF16), jnp.exp(s_ctx - m).astype(BF16)], axis=1)

    def weighted_values(head, e):
        lanes = slice((head // 2) * LANES, (head // 2 + 1) * LANES)
        values = jnp.concatenate([own(head, v_ref[0, pl.ds(key0, n_keys), lanes], one),
                                  own(head, cv_ref[0, :, lanes], one)], axis=0)
        return _dot(e, values)

    outs = []
    pending_scores = [scores(head) for head in range(SCORE_LOOKAHEAD)]
    pending_weights = [weights(0, *pending_scores.pop(0))]
    for head in range(N_HEADS):
        if head + SCORE_LOOKAHEAD < N_HEADS:
            pending_scores.append(scores(head + SCORE_LOOKAHEAD))
        fillers[head]()
        if head + 1 < N_HEADS:
            pending_weights.append(weights(head + 1, *pending_scores.pop(0)))
        outs.append(weighted_values(head, pending_weights.pop(0)))
    att_pairs = []
    for pr in range(N_PAIRS):
        even, odd = outs[2 * pr], outs[2 * pr + 1]
        num = jnp.where(lo, even, odd)
        den = pltpu.roll(jnp.where(lo, odd, even), HEAD_DIM, 1)
        att_pairs.append(num / den)
    return jnp.concatenate(att_pairs, axis=1)


def _lat_mixer_kernel(x_ref, h_ref, q_ref, p_ref, pp_ref, pn_ref, mod_ref, k_ref, v_ref,
                      ck_ref, cv_ref, rpb_ref, wg32_ref, wpool32_ref, pscale_ref, wbrp32_ref, wbra32_ref,
                      wout32_ref, o_ref, bias_ref, wg_ref, wpool_ref, wbrp_ref, wbra_ref, wout_ref,
                      *, seq, n_rows):
    step = pl.program_id(0)
    tm = LAT_TILES * LAT_TM

    @pl.when(step < N_CONV)
    def _convert():
        _convert_chunk(step, wg32_ref, wg_ref)
        _convert_mixer_weights(step, wpool32_ref, wbrp32_ref, wbra32_ref, wout32_ref,
                               wpool_ref, wbrp_ref, wbra_ref, wout_ref)

        @pl.when(step == 0)
        def _bias():
            _build_bias_table(rpb_ref, bias_ref)

    @pl.when(step >= N_CONV)
    def _compute():
        blk = lax.rem(step - N_CONV, seq // tm)
        gate2 = mod_ref[0, 5:6, :]
        attention = functools.partial(_lat_attention, k_ref=k_ref, v_ref=v_ref, ck_ref=ck_ref, cv_ref=cv_ref,
                                      bias_ref=bias_ref, n_rows=n_rows)
        st = {"gates": []}
        n_gate_blocks = 2 * D_MODEL // MXU_COLS

        def gate_block():
            cols = slice(len(st["gates"]) * MXU_COLS, (len(st["gates"]) + 1) * MXU_COLS)
            st["gates"].append(jax.nn.sigmoid(_dot(h_ref[...], wg_ref[:, cols])))

        def pool_delta():
            p_ext = jnp.concatenate([pp_ref[...], p_ref[...], pn_ref[...]], axis=0)
            pos = blk * tm - POOL_HALO + lax.broadcasted_iota(jnp.int32, (tm + 2 * POOL_HALO, 1), 0)
            p_ext = jnp.where((pos >= 0) & (pos < seq), p_ext, 0.0)
            st["d"] = _pool_delta(p_ext, blk * tm, seq).astype(BF16)
            gate_block()

        def pool_weight():
            st["y"] = (_dot(st.pop("d"), wpool_ref[...]) * pscale_ref[...]).astype(BF16)
            gate_block()

        def pool_out():
            st["a"] = _dot(st.pop("y"), wbrp_ref[...])
            gate_block()

        def merge_pieces(t, att):
            rows = slice(t * LAT_TM, (t + 1) * LAT_TM)

            def attention_branch():
                st["b"] = _dot(att.astype(BF16), wbra_ref[...])

            def gate():
                gates = jnp.concatenate([g[rows] for g in st["gates"]], axis=1)
                merged = gates[:, :D_MODEL] * st["a"][rows] + gates[:, D_MODEL:] * st.pop("b")
                st["merged"] = merged.astype(BF16)

            def out():
                o_ref[rows, :] = x_ref[rows, :] + gate2 * _dot(st.pop("merged"), wout_ref[...])

            return [attention_branch, gate, out]

        idle = lambda: None
        assert n_gate_blocks == N_HEADS
        fillers = [pool_delta, pool_weight, pool_out] + [gate_block] * (N_HEADS - 3)
        for t in range(LAT_TILES):
            att = attention(q_ref[t * LAT_TM:(t + 1) * LAT_TM, :], blk * LAT_TILES + t, fillers)
            pieces = merge_pieces(t, att)
            fillers = [idle] + pieces + [idle] * (N_HEADS - 1 - len(pieces))
        for piece in pieces:
            piece()


def _lat_mixer(x_all, tok0, n_batch, seq, mod3, h, q, p, k3, v3, ck3, cv3, rpb_pad, w_in, w_pool, pscale,
               wbrp, wbra, wout):
    tm = LAT_TILES * LAT_TM
    n_rows = seq // GRID_W
    nb = seq // tm
    tile0 = tok0 // tm
    halo_per_tile = tm // POOL_HALO
    n_halo = p.shape[0] // POOL_HALO
    tile = _tile_index
    row = lambda s: (tile(s), 0)
    seq_map = lambda s: (tile(s) // nb, 0, 0)
    return pl.pallas_call(
        functools.partial(_lat_mixer_kernel, seq=seq, n_rows=n_rows),
        grid=(N_CONV + n_batch * nb,),
        in_specs=[pl.BlockSpec((tm, D_MODEL), lambda s: (tile0 + tile(s), 0)),
                  pl.BlockSpec((tm, D_MODEL), row),
                  pl.BlockSpec((tm, D_ATT), row),
                  pl.BlockSpec((tm, D_POOL), row),
                  pl.BlockSpec((POOL_HALO, D_POOL), lambda s: (jnp.maximum(tile(s) * halo_per_tile - 1, 0), 0)),
                  pl.BlockSpec((POOL_HALO, D_POOL),
                               lambda s: (jnp.minimum((tile(s) + 1) * halo_per_tile, n_halo - 1), 0)),
                  pl.BlockSpec((1, N_MOD, D_MODEL), lambda s: (1 + tile(s) // nb, 0, 0)),
                  pl.BlockSpec((1, seq, D_ATT), seq_map), pl.BlockSpec((1, seq, D_ATT), seq_map),
                  pl.BlockSpec((1,) + ck3.shape[1:], seq_map), pl.BlockSpec((1,) + cv3.shape[1:], seq_map),
                  _const_spec(rpb_pad.shape),
                  _chunk_spec(w_in, (2 * D_MODEL, O_G // (2 * D_MODEL))),
                  _const_spec(w_pool.shape), _const_spec((1, D_POOL)),
                  _chunk_spec(wbrp), _chunk_spec(wbra), _chunk_spec(wout)],
        out_specs=pl.BlockSpec((tm, D_MODEL), row),
        out_shape=jax.ShapeDtypeStruct((n_batch * seq, D_MODEL), F32),
        scratch_shapes=[pltpu.VMEM((N_HEADS, 2 * NA_KH - 2, GRID_W, LANES), F32),
                        pltpu.VMEM((D_MODEL, 2 * D_MODEL), BF16)] + _mixer_weight_scratch(),
        compiler_params=_params(),
        name="lat_mixer",
    )(x_all, h, q, p, p, p, mod3, k3, v3, ck3, cv3, rpb_pad, w_in, w_pool, pscale, wbrp, wbra, wout)


def kernel(x_prompt, x_sample, cache_k, cache_v, c, c_ctx, w_ada, b_ada, g_ff1, w_ff1_in, w_ff1_out,
           g_mix, w_in, q_gain, k_gain, w_pool, pool_scale, rpb, w_br_pool, w_br_att, w_out, g_ff2,
           w_ff2_in, w_ff2_out):
    n_ctx, seq_ctx, _ = x_prompt.shape
    n_lat, seq_lat, _ = x_sample.shape
    depth = w_ada.shape[0]
    assert depth == 1 and n_lat + 1 <= 8
    l = 0

    xp = x_prompt.reshape(n_ctx * seq_ctx, D_MODEL)
    xs = x_sample.reshape(n_lat * seq_lat, D_MODEL)
    t_ctx, t_lat = xp.shape[0], xs.shape[0]
    cond8 = jnp.zeros((8, D_MODEL), F32).at[0].set(c_ctx).at[1:1 + n_lat].set(c)
    mod3 = _adaln(cond8, w_ada[l], b_ada[l]).reshape(8, N_MOD, D_MODEL)

    pscale = pool_scale[l].reshape(1, D_POOL)
    qg2 = jnp.tile(q_gain[l], 2).reshape(1, LANES)
    kg2 = jnp.tile(k_gain[l], 2).reshape(1, LANES)
    g1, gm, g2 = g_ff1[l].reshape(1, -1), g_mix[l].reshape(1, -1), g_ff2[l].reshape(1, -1)
    rpb_pad = jnp.pad(rpb[l], ((0, 0), (0, 0), (0, LANES - rpb.shape[-1])))
    ck3 = cache_k[:, l].reshape(n_lat, -1, D_ATT).astype(BF16)
    cv3 = cache_v[:, l].reshape(n_lat, -1, D_ATT).astype(BF16)

    (x1,) = _ffn(xp, xs, mod3, seq_lat, 0, g1, w_ff1_in[l], w_ff1_out[l], "ffn1")
    x2p, k_p, v_p = _ctx_mixer(x1, t_ctx, mod3, gm, w_in[l], qg2, kg2, w_pool[l], pscale,
                               w_br_pool[l], w_br_att[l], w_out[l], seq_ctx)
    h_s, p_s, q_s, k_s, v_s = _lat_proj(x1, t_ctx, t_lat, seq_lat, mod3, gm, w_in[l], qg2, kg2)
    x2s = _lat_mixer(x1, t_ctx, n_lat, seq_lat, mod3, h_s, q_s, p_s,
                     k_s.reshape(n_lat, seq_lat, D_ATT), v_s.reshape(n_lat, seq_lat, D_ATT), ck3, cv3,
                     rpb_pad, w_in[l], w_pool[l], pscale, w_br_pool[l], w_br_att[l], w_out[l])
    yp, ys = _ffn(x2p, x2s, mod3, seq_lat, 6, g2, w_ff2_in[l], w_ff2_out[l], "ffn2", split_out=True)

    new_k = k_p.reshape(n_ctx, 1, seq_ctx, N_HEADS, HEAD_DIM)
    new_v = v_p.reshape(n_ctx, 1, seq_ctx, N_HEADS, HEAD_DIM)
    return (yp.reshape(x_prompt.shape), ys.reshape(x_sample.shape), new_k, new_v)
```

```python
import functools
import math

import jax
import jax.numpy as jnp
from jax import lax
from jax.experimental import pallas as pl
from jax.experimental.pallas import tpu as pltpu

F32 = jnp.float32
BF16 = jnp.bfloat16

D_MODEL = 1024
N_HEADS = 8
HEAD_DIM = 64
D_ATT = N_HEADS * HEAD_DIM
D_POOL = 512
POOL_WINDOWS = (2, 4, 8, 16)
POOL_HALO = 8
D_FF = 2816
N_MOD = 9
GRID_W = 64
NA_KH = 8
NA_KW = 16
EPS = 1e-6
NEG = -1e30

LANES = 128
BF16_ROWS = 16
N_PAIRS = D_ATT // LANES
MXU_COLS = 256
FF_CHUNKS = ((0, 768), (768, 768), (1536, 768), (2304, 512))
ADALN_TN = 2304
FFN_TM = 512
CTX_TB = 2
PROJ_TM = 512
LAT_ROWS = 4
LAT_TM = LAT_ROWS * GRID_W
LAT_TILES = 2
LAT_KEY_ROWS = 12
SCORE_LOOKAHEAD = 2
N_CONV = 8
PRE_CHUNKS = 16
VMEM_LIMIT = 56 * 1024 * 1024

O_Q = D_POOL
O_K = D_POOL + D_ATT
O_V = D_POOL + 2 * D_ATT
O_G = D_POOL + 3 * D_ATT


def _dot(a, b):
    return jnp.dot(a, b, preferred_element_type=F32)


def _dot_nt(a, b):
    return lax.dot_general(a, b, (((1,), (1,)), ((), ())), preferred_element_type=F32)


def _rms_mod(x, g, shift, scale):
    ms = jnp.mean(x * x, axis=-1, keepdims=True)
    y = x * lax.rsqrt(ms + EPS) * g
    return y * (1.0 + scale) + shift


def _low_half():
    return lax.broadcasted_iota(jnp.int32, (1, LANES), 1) < HEAD_DIM


def _head_norm(t, gain2):
    lo = _low_half()
    sq = t * t
    s0 = jnp.sum(jnp.where(lo, sq, 0.0), axis=-1, keepdims=True)
    s1 = jnp.sum(jnp.where(lo, 0.0, sq), axis=-1, keepdims=True)
    r0 = lax.rsqrt(s0 * (1.0 / HEAD_DIM) + EPS)
    r1 = lax.rsqrt(s1 * (1.0 / HEAD_DIM) + EPS)
    return t * jnp.where(lo, r0, r1) * gain2


def _own_lanes(head, mine, other):
    lo = _low_half()
    return jnp.where(lo, mine, other) if head % 2 == 0 else jnp.where(lo, other, mine)


def _pipelined_heads(scores, weights, weighted_values, fillers):
    lo = _low_half()
    outs = []
    pending_scores = [scores(head) for head in range(SCORE_LOOKAHEAD)]
    pending_weights = [weights(0, *pending_scores.pop(0))]
    for head in range(N_HEADS):
        if head + SCORE_LOOKAHEAD < N_HEADS:
            pending_scores.append(scores(head + SCORE_LOOKAHEAD))
        fillers[head]()
        if head + 1 < N_HEADS:
            pending_weights.append(weights(head + 1, *pending_scores.pop(0)))
        outs.append(weighted_values(head, pending_weights.pop(0)))
    att_pairs = []
    for pr in range(N_PAIRS):
        even, odd = outs[2 * pr], outs[2 * pr + 1]
        num = jnp.where(lo, even, odd)
        den = pltpu.roll(jnp.where(lo, odd, even), HEAD_DIM, 1)
        att_pairs.append(num / den)
    return jnp.concatenate(att_pairs, axis=1)


def _pool_delta(p_ext, pos0, seq_len):
    te = p_ext.shape[0]
    tm = te - 2 * POOL_HALO
    t = (pos0 + lax.broadcasted_iota(jnp.int32, (tm, 1), 0)).astype(F32)
    outs = []
    for g, w in enumerate(POOL_WINDOWS):
        a = p_ext[:, g * LANES:(g + 1) * LANES]
        s = a
        step = 1
        while step < w:
            s = s + pltpu.roll(s, step, 0)
            step *= 2
        ahead = w // 2 - 1
        if ahead:
            s = pltpu.roll(s, te - ahead, 0)
        win = s[POOL_HALO:POOL_HALO + tm]
        cnt = jnp.minimum(t + float(w // 2), float(seq_len)) - jnp.maximum(t - float(w // 2), 0.0)
        outs.append(win / cnt - a[POOL_HALO:POOL_HALO + tm])
    return jnp.concatenate(outs, axis=1)


def _const_spec(shape):
    nd = len(shape)
    return pl.BlockSpec(shape, lambda *_: (0,) * nd)


def _resident_spec(w, col_block=None):
    width, cidx = col_block if col_block else (w.shape[1], 0)
    return pl.BlockSpec((w.shape[0], width), lambda *_: (0, cidx), pipeline_mode=pl.Buffered(1))


def _params():
    return pltpu.CompilerParams(dimension_semantics=("arbitrary",), vmem_limit_bytes=VMEM_LIMIT)


def _build_pool_weight(wpool32_ref, wpool_ref):
    wpool_ref[...] = jnp.zeros(wpool_ref.shape, BF16)
    for g in range(len(POOL_WINDOWS)):
        blk = slice(g * LANES, (g + 1) * LANES)
        wpool_ref[blk, blk] = wpool32_ref[g].astype(BF16)


def _adaln_kernel(cond_ref, w_ref, b_ref, o_ref):
    c = cond_ref[...]
    s = c * jax.nn.sigmoid(c)
    o_ref[...] = _dot(s.astype(BF16), w_ref[...].astype(BF16)) + b_ref[...]


def _adaln(cond8, w_ada, b_ada):
    n = w_ada.shape[1]
    tn = ADALN_TN
    return pl.pallas_call(
        _adaln_kernel,
        grid=(n // tn,),
        in_specs=[pl.BlockSpec((8, D_MODEL), lambda i: (0, 0)),
                  pl.BlockSpec((D_MODEL, tn), lambda i: (0, i)),
                  pl.BlockSpec((1, tn), lambda i: (0, i))],
        out_specs=pl.BlockSpec((8, tn), lambda i: (0, i)),
        out_shape=jax.ShapeDtypeStruct((8, n), F32),
        compiler_params=_params(),
        name="adaln",
    )(cond8, w_ada, b_ada.reshape(1, n))


def _ffn_kernel(*refs, mod_off, n_a, split_out, n_conv, n_pre):
    refs = list(refs)
    xa_ref, xb_ref, mod_ref, g_ref, wgu_in, wdn_in = refs[:6]
    pre_src = refs[6:6 + n_pre]
    n_out = 2 if split_out else 1
    out_refs = refs[6 + n_pre:6 + n_pre + n_out]
    pre_dst = refs[6 + n_pre + n_out:6 + 2 * n_pre + n_out]
    wgu_ref, wdn_ref = refs[6 + 2 * n_pre + n_out:] if n_conv else (wgu_in, wdn_in)
    step = pl.program_id(0)

    if n_conv:
        @pl.when(step < n_conv)
        def _convert():
            for src, dst in ((wgu_in, wgu_ref), (wdn_in, wdn_ref)):
                rows = src.shape[0]
                dst[pl.ds(pl.multiple_of(step * rows, rows), rows), :] = src[...].astype(BF16)

    @pl.when(step >= n_conv)
    def _compute():
        tile = step - n_conv
        is_a = tile < n_a
        shift = mod_ref[0, mod_off:mod_off + 1, :]
        scale = mod_ref[0, mod_off + 1:mod_off + 2, :]
        gate = mod_ref[0, mod_off + 2:mod_off + 3, :]
        x = jnp.where(is_a, xa_ref[...], xb_ref[...])
        h = _rms_mod(x, g_ref[...], shift, scale).astype(BF16)
        acc = None
        for c0, cw in FF_CHUNKS:
            a = _dot(h, wgu_ref[:, c0:c0 + cw])
            u = _dot(h, wgu_ref[:, D_FF + c0:D_FF + c0 + cw])
            t = (a * jax.nn.sigmoid(a) * u).astype(BF16)
            part = _dot(t, wdn_ref[c0:c0 + cw, :])
            acc = part if acc is None else acc + part
        y = x + (0.5 * gate) * acc
        if split_out:
            @pl.when(is_a)
            def _store_a():
                out_refs[0][...] = y

            @pl.when(jnp.logical_not(is_a))
            def _store_b():
                out_refs[1][...] = y
        else:
            out_refs[0][...] = y

        if n_pre:
            @pl.when(tile < PRE_CHUNKS)
            def _preconvert():
                for src, dst in zip(pre_src, pre_dst):
                    dst[...] = src[...].astype(BF16)


def _ffn(xa, xb, mod3, seq_b, mod_off, g, wgu, wdn, name, split_out=False, convert=False, preconvert=()):
    tm = FFN_TM
    n_a, n_b = xa.shape[0] // tm, xb.shape[0] // tm
    n_conv = N_CONV if convert else 0
    assert n_a + n_b >= PRE_CHUNKS
    tile = lambda s: jnp.maximum(s - n_conv, 0)
    a_map = lambda s: (jnp.minimum(tile(s), n_a - 1), 0)
    b_map = lambda s: (jnp.maximum(tile(s) - n_a, 0), 0)
    mod_map = lambda s: (jnp.where(tile(s) < n_a, 0, 1 + (jnp.maximum(tile(s) - n_a, 0) * tm) // seq_b), 0, 0)

    def chunk_spec(w, n_chunks, index):
        rows = w.shape[0] // n_chunks
        assert rows * n_chunks == w.shape[0] and rows % BF16_ROWS == 0
        return pl.BlockSpec((rows, w.shape[1]), lambda s: (jnp.minimum(index(s), n_chunks - 1), 0))

    if convert:
        w_specs = [chunk_spec(wgu, N_CONV, lambda s: s), chunk_spec(wdn, N_CONV, lambda s: s)]
        scratch = [pltpu.VMEM(wgu.shape, BF16), pltpu.VMEM(wdn.shape, BF16)]
    else:
        w_specs = [_resident_spec(wgu), _resident_spec(wdn)]
        scratch = []
    pre_specs = lambda: [chunk_spec(w, PRE_CHUNKS, tile) for w in preconvert]
    if split_out:
        out_specs = [pl.BlockSpec((tm, D_MODEL), a_map), pl.BlockSpec((tm, D_MODEL), b_map)]
        out_shape = [jax.ShapeDtypeStruct(xa.shape, F32), jax.ShapeDtypeStruct(xb.shape, F32)]
    else:
        out_specs = [pl.BlockSpec((tm, D_MODEL), lambda s: (tile(s), 0))]
        out_shape = [jax.ShapeDtypeStruct((xa.shape[0] + xb.shape[0], D_MODEL), F32)]
    return pl.pallas_call(
        functools.partial(_ffn_kernel, mod_off=mod_off, n_a=n_a, split_out=split_out, n_conv=n_conv,
                          n_pre=len(preconvert)),
        grid=(n_conv + n_a + n_b,),
        in_specs=[pl.BlockSpec((tm, D_MODEL), a_map), pl.BlockSpec((tm, D_MODEL), b_map),
                  pl.BlockSpec((1, N_MOD, D_MODEL), mod_map), _const_spec((1, D_MODEL))] + w_specs + pre_specs(),
        out_specs=out_specs + pre_specs(),
        out_shape=out_shape + [jax.ShapeDtypeStruct(w.shape, BF16) for w in preconvert],
        scratch_shapes=scratch,
        compiler_params=_params(),
        name=name,
    )(xa, xb, mod3, g, wgu, wdn, *preconvert)


def _heads_to_sublanes(t):
    heads = jnp.stack([t[:, h * HEAD_DIM:(h + 1) * HEAD_DIM] for h in range(N_HEADS)], axis=0)
    return jnp.swapaxes(heads, 0, 1)


def _branch_fillers(st, h_block, delta_fn, gate_cols, wpool_ref, pscale_ref, wbrp_ref):
    assert 2 * D_MODEL // MXU_COLS == N_HEADS
    st["gates"] = []

    def gate_block():
        c0 = len(st["gates"]) * MXU_COLS
        st["gates"].append(jax.nn.sigmoid(_dot(h_block(), gate_cols(c0, c0 + MXU_COLS))))

    def pool_delta():
        st["d"] = delta_fn().astype(BF16)
        gate_block()

    def pool_weight():
        st["y"] = (_dot(st.pop("d"), wpool_ref[...]) * pscale_ref[...]).astype(BF16)
        gate_block()

    def pool_out():
        st["a"] = _dot(st.pop("y"), wbrp_ref[...])
        gate_block()

    return [pool_delta, pool_weight, pool_out] + [gate_block] * (N_HEADS - 3)


def _merge_fillers(st, rows, att, x_ref, o_ref, gate2, wbra_ref, wout_ref):
    def attention_branch():
        st["b"] = _dot(att.astype(BF16), wbra_ref[...])

    def gate():
        gates = jnp.concatenate([g[rows] for g in st["gates"]], axis=1)
        merged = gates[:, :D_MODEL] * st["a"][rows] + gates[:, D_MODEL:] * st.pop("b")
        st["merged"] = merged.astype(BF16)

    def out():
        o_ref[rows, :] = x_ref[rows, :] + gate2 * _dot(st.pop("merged"), wout_ref[...])

    return [attention_branch, gate, out]


def _attend_tiles(n_tiles, tile_rows, attention, first_fillers, merge_fillers):
    idle = lambda: None
    fillers = first_fillers
    for t in range(n_tiles):
        rows = slice(t * tile_rows, (t + 1) * tile_rows)
        pieces = merge_fillers(rows, attention(t, fillers))
        fillers = [idle] + pieces + [idle] * (N_HEADS - 1 - len(pieces))
    for piece in pieces:
        piece()


def _ctx_mixer_kernel(x_ref, mod_ref, g_ref, win_ref, qg_ref, kg_ref, wpool32_ref, pscale_ref,
                      wbrp_ref, wbra_ref, wout_ref, o_ref, ko_ref, vo_ref, wpool_ref, *, seq):
    @pl.when(pl.program_id(0) == 0)
    def _first_step():
        _build_pool_weight(wpool32_ref, wpool_ref)

    m_rows = x_ref.shape[0]
    h = _rms_mod(x_ref[...], g_ref[...], mod_ref[0, 3:4, :], mod_ref[0, 4:5, :]).astype(BF16)
    p = _dot(h, win_ref[:, 0:O_Q])
    q = _dot(h, win_ref[:, O_Q:O_K])
    k = _dot(h, win_ref[:, O_K:O_V])
    v = _dot(h, win_ref[:, O_V:O_G])
    vo_ref[...] = _heads_to_sublanes(v)
    scale = 1.0 / math.sqrt(HEAD_DIM)
    zero = jnp.zeros((), BF16)
    one = jnp.ones((), BF16)
    st = {}

    def pool_delta():
        zeros_halo = jnp.zeros((POOL_HALO, D_POOL), F32)
        return jnp.concatenate(
            [_pool_delta(jnp.concatenate([zeros_halo, p[e * seq:(e + 1) * seq], zeros_halo], axis=0), 0, seq)
             for e in range(m_rows // seq)], axis=0)

    fillers = _branch_fillers(st, lambda: h, pool_delta, lambda c0, c1: win_ref[:, O_G + c0:O_G + c1],
                              wpool_ref, pscale_ref, wbrp_ref)

    def attention(e, fillers):
        rows = slice(e * seq, (e + 1) * seq)
        qn, kn, vb = [], [], []
        for pr in range(N_PAIRS):
            lanes = slice(pr * LANES, (pr + 1) * LANES)
            kn.append(_head_norm(k[rows, lanes], kg_ref[...]))
            qn.append((_head_norm(q[rows, lanes], qg_ref[...]) * scale).astype(BF16))
            vb.append(v[rows, lanes].astype(BF16))
        ko_ref[rows] = _heads_to_sublanes(jnp.concatenate(kn, axis=1))
        kb = [t.astype(BF16) for t in kn]

        def scores(head):
            return (_dot_nt(_own_lanes(head, qn[head // 2], zero), kb[head // 2]),)

        def weights(head, s):
            return jnp.exp(s - s.max(axis=-1, keepdims=True)).astype(BF16)

        def weighted_values(head, e_):
            return _dot(e_, _own_lanes(head, vb[head // 2], one))

        return _pipelined_heads(scores, weights, weighted_values, fillers)

    merge = functools.partial(_merge_fillers, st, x_ref=x_ref, o_ref=o_ref, gate2=mod_ref[0, 5:6, :],
                              wbra_ref=wbra_ref, wout_ref=wout_ref)
    _attend_tiles(m_rows // seq, seq, attention, fillers, lambda rows, att: merge(rows, att))


def _ctx_mixer(x_all, n_tok, mod3, g, w_in, qg2, kg2, w_pool, pscale, wbrp, wbra, wout, seq):
    tm = CTX_TB * seq
    row = lambda s: (s, 0)
    heads = lambda s: (s, 0, 0)
    return pl.pallas_call(
        functools.partial(_ctx_mixer_kernel, seq=seq),
        grid=(n_tok // tm,),
        in_specs=[pl.BlockSpec((tm, D_MODEL), row),
                  _const_spec((1, N_MOD, D_MODEL)), _const_spec((1, D_MODEL)),
                  _resident_spec(w_in), _const_spec((1, LANES)), _const_spec((1, LANES)),
                  _const_spec(w_pool.shape), _const_spec((1, D_POOL)),
                  _resident_spec(wbrp), _resident_spec(wbra), _resident_spec(wout)],
        out_specs=[pl.BlockSpec((tm, D_MODEL), row),
                   pl.BlockSpec((tm, N_HEADS, HEAD_DIM), heads),
                   pl.BlockSpec((tm, N_HEADS, HEAD_DIM), heads)],
        out_shape=[jax.ShapeDtypeStruct((n_tok, D_MODEL), F32),
                   jax.ShapeDtypeStruct((n_tok, N_HEADS, HEAD_DIM), F32),
                   jax.ShapeDtypeStruct((n_tok, N_HEADS, HEAD_DIM), F32)],
        scratch_shapes=[pltpu.VMEM((D_POOL, D_POOL), BF16)],
        compiler_params=_params(),
        name="ctx_mixer",
    )(x_all, mod3, g, w_in, qg2, kg2, w_pool, pscale, wbrp, wbra, wout)


def _lat_proj_kernel(x_ref, mod_ref, g_ref, win_ref, qg_ref, kg_ref, h_ref, p_ref, q_ref, k_ref, v_ref):
    h = _rms_mod(x_ref[...], g_ref[...], mod_ref[0, 3:4, :], mod_ref[0, 4:5, :]).astype(BF16)
    h_ref[...] = h
    p_ref[...] = _dot(h, win_ref[:, 0:O_Q])
    q = _dot(h, win_ref[:, O_Q:O_K])
    k = _dot(h, win_ref[:, O_K:O_V])
    v_ref[...] = _dot(h, win_ref[:, O_V:O_G]).astype(BF16)
    scale = 1.0 / math.sqrt(HEAD_DIM)
    for pr in range(N_PAIRS):
        lanes = slice(pr * LANES, (pr + 1) * LANES)
        q_ref[:, lanes] = (_head_norm(q[:, lanes], qg_ref[...]) * scale).astype(BF16)
        k_ref[:, lanes] = _head_norm(k[:, lanes], kg_ref[...]).astype(BF16)


def _lat_proj(x_all, tok0, n_tok, seq, mod3, g, w_in, qg2, kg2):
    tm = PROJ_TM
    tile0 = tok0 // tm
    row = lambda s: (s, 0)
    shapes = [(D_MODEL, BF16), (D_POOL, F32), (D_ATT, BF16), (D_ATT, BF16), (D_ATT, BF16)]
    return pl.pallas_call(
        _lat_proj_kernel,
        grid=(n_tok // tm,),
        in_specs=[pl.BlockSpec((tm, D_MODEL), lambda s: (tile0 + s, 0)),
                  pl.BlockSpec((1, N_MOD, D_MODEL), lambda s: (1 + (s * tm) // seq, 0, 0)),
                  _const_spec((1, D_MODEL)), _resident_spec(w_in, (O_G, 0)),
                  _const_spec((1, LANES)), _const_spec((1, LANES))],
        out_specs=[pl.BlockSpec((tm, width), row) for width, _ in shapes],
        out_shape=[jax.ShapeDtypeStruct((n_tok, width), dtype) for width, dtype in shapes],
        compiler_params=_params(),
        name="lat_proj",
    )(x_all, mod3, g, w_in, qg2, kg2)


def _build_bias_table(rpb_ref, bias_ref):
    lane = lax.broadcasted_iota(jnp.int32, (GRID_W, LANES), 1)
    qc = lax.broadcasted_iota(jnp.int32, (GRID_W, LANES), 0)
    kc = lane & (GRID_W - 1)
    c0 = jnp.clip(qc - NA_KW // 2, 0, GRID_W - NA_KW)
    col_ok = (kc >= c0) & (kc < c0 + NA_KW)
    first_half = lane < GRID_W
    shift_a = LANES - (NA_KW - 1)
    shift_b = (GRID_W - (NA_KW - 1)) % LANES
    for h in range(N_HEADS):
        for i in range(2 * NA_KH - 2):
            va = jnp.broadcast_to(rpb_ref[h, i:i + 1, :], (GRID_W, LANES))
            vb = jnp.broadcast_to(rpb_ref[h, i + 1:i + 2, :], (GRID_W, LANES))
            ra = pltpu.roll(va, shift_a, 1, stride=1, stride_axis=0)
            rb = pltpu.roll(vb, shift_b, 1, stride=1, stride_axis=0)
            bias_ref[h, i] = jnp.where(col_ok, jnp.where(first_half, ra, rb), NEG)


def _lat_attention(qn, j, fillers, k_ref, v_ref, ck_ref, cv_ref, bias_ref, *, n_rows):
    n_keys = LAT_KEY_ROWS * GRID_W
    r0q = j * LAT_ROWS
    u0 = jnp.clip(r0q - NA_KH // 2, 0, n_rows - LAT_KEY_ROWS)
    key0 = pl.multiple_of(u0 * GRID_W, GRID_W)
    log2_w = GRID_W.bit_length() - 1
    zero = jnp.zeros((), BF16)
    one = jnp.ones((), BF16)

    lane_row = lax.broadcasted_iota(jnp.int32, (1, LANES), 1) & (HEAD_DIM - 1)
    q_tile_row = jnp.right_shift(lax.broadcasted_iota(jnp.int32, (LAT_TM, 1), 0), log2_w)
    q_hot = jnp.where(lane_row == q_tile_row, 1.0, 0.0).astype(BF16)
    k_row = u0 + jnp.right_shift(lax.broadcasted_iota(jnp.int32, (n_keys, 1), 0), log2_w)
    first = jnp.clip(r0q + lane_row - NA_KH // 2, 0, n_rows - NA_KH)
    outside = (lane_row < LAT_ROWS) & ((k_row < first) | (k_row >= first + NA_KH))
    k_mask = jnp.where(outside, NEG, 0.0).astype(BF16)

    def scores(head):
        lanes = slice((head // 2) * LANES, (head // 2 + 1) * LANES)
        qp = qn[:, lanes]
        s_loc = _dot_nt(_own_lanes(head, qp, q_hot),
                        _own_lanes(head, k_ref[0, pl.ds(key0, n_keys), lanes], k_mask))
        return s_loc, _dot_nt(_own_lanes(head, qp, zero), ck_ref[0, :, lanes])

    def weights(head, s_loc, s_ctx):
        bias_rows = []
        for jr in range(LAT_ROWS):
            blocks = []
            for pi in range(LAT_KEY_ROWS // 2):
                idx = jnp.clip(u0 - r0q + 2 * pi - jr + NA_KH - 1, 0, 2 * NA_KH - 3)
                blocks.append(bias_ref[head, idx])
            bias_rows.append(jnp.concatenate(blocks, axis=1))
        s_loc = s_loc + jnp.concatenate(bias_rows, axis=0)
        m = jnp.maximum(s_loc.max(axis=-1, keepdims=True), s_ctx.max(axis=-1, keepdims=True))
        return jnp.concatenate([jnp.exp(s_loc - m).astype(BF16), jnp.exp(s_ctx - m).astype(BF16)], axis=1)

    def weighted_values(head, e):
        lanes = slice((head // 2) * LANES, (head // 2 + 1) * LANES)
        values = jnp.concatenate([_own_lanes(head, v_ref[0, pl.ds(key0, n_keys), lanes], one),
                                  _own_lanes(head, cv_ref[0, :, lanes], one)], axis=0)
        return _dot(e, values)

    return _pipelined_heads(scores, weights, weighted_values, fillers)


def _lat_mixer_kernel(x_ref, h_ref, q_ref, p_ref, pp_ref, pn_ref, mod_ref, k_ref, v_ref, ck_ref, cv_ref,
                      rpb_ref, wg_ref, wpool32_ref, pscale_ref, wbrp_ref, wbra_ref, wout_ref,
                      o_ref, bias_ref, wpool_ref, *, seq, n_rows):
    step = pl.program_id(0)
    tm = LAT_TILES * LAT_TM

    @pl.when(step == 0)
    def _first_step():
        _build_pool_weight(wpool32_ref, wpool_ref)
        _build_bias_table(rpb_ref, bias_ref)

    blk = lax.rem(step, seq // tm)
    st = {}

    def p_ext():
        pe = jnp.concatenate([pp_ref[...], p_ref[...], pn_ref[...]], axis=0)
        pos = blk * tm - POOL_HALO + lax.broadcasted_iota(jnp.int32, (tm + 2 * POOL_HALO, 1), 0)
        return jnp.where((pos >= 0) & (pos < seq), pe, 0.0)

    first_fillers = _branch_fillers(st, lambda: h_ref[...], lambda: _pool_delta(p_ext(), blk * tm, seq),
                                    lambda c0, c1: wg_ref[:, c0:c1], wpool_ref, pscale_ref, wbrp_ref)

    def attention(t, fillers):
        return _lat_attention(q_ref[t * LAT_TM:(t + 1) * LAT_TM, :], blk * LAT_TILES + t, fillers,
                              k_ref, v_ref, ck_ref, cv_ref, bias_ref, n_rows=n_rows)

    merge = functools.partial(_merge_fillers, st, x_ref=x_ref, o_ref=o_ref, gate2=mod_ref[0, 5:6, :],
                              wbra_ref=wbra_ref, wout_ref=wout_ref)
    _attend_tiles(LAT_TILES, LAT_TM, attention, first_fillers, lambda rows, att: merge(rows, att))


def _lat_mixer(x_all, tok0, n_batch, seq, mod3, h, q, p, k3, v3, ck3, cv3, rpb_pad, w_in, w_pool, pscale,
               wbrp, wbra, wout):
    tm = LAT_TILES * LAT_TM
    n_rows = seq // GRID_W
    nb = seq // tm
    tile0 = tok0 // tm
    halo_per_tile = tm // POOL_HALO
    n_halo = p.shape[0] // POOL_HALO
    row = lambda s: (s, 0)
    seq_map = lambda s: (s // nb, 0, 0)
    return pl.pallas_call(
        functools.partial(_lat_mixer_kernel, seq=seq, n_rows=n_rows),
        grid=(n_batch * nb,),
        in_specs=[pl.BlockSpec((tm, D_MODEL), lambda s: (tile0 + s, 0)),
                  pl.BlockSpec((tm, D_MODEL), row),
                  pl.BlockSpec((tm, D_ATT), row),
                  pl.BlockSpec((tm, D_POOL), row),
                  pl.BlockSpec((POOL_HALO, D_POOL), lambda s: (jnp.maximum(s * halo_per_tile - 1, 0), 0)),
                  pl.BlockSpec((POOL_HALO, D_POOL),
                               lambda s: (jnp.minimum((s + 1) * halo_per_tile, n_halo - 1), 0)),
                  pl.BlockSpec((1, N_MOD, D_MODEL), lambda s: (1 + s // nb, 0, 0)),
                  pl.BlockSpec((1, seq, D_ATT), seq_map), pl.BlockSpec((1, seq, D_ATT), seq_map),
                  pl.BlockSpec((1,) + ck3.shape[1:], seq_map), pl.BlockSpec((1,) + cv3.shape[1:], seq_map),
                  _const_spec(rpb_pad.shape),
                  _resident_spec(w_in, (2 * D_MODEL, O_G // (2 * D_MODEL))),
                  _const_spec(w_pool.shape), _const_spec((1, D_POOL)),
                  _resident_spec(wbrp), _resident_spec(wbra), _resident_spec(wout)],
        out_specs=pl.BlockSpec((tm, D_MODEL), row),
        out_shape=jax.ShapeDtypeStruct((n_batch * seq, D_MODEL), F32),
        scratch_shapes=[pltpu.VMEM((N_HEADS, 2 * NA_KH - 2, GRID_W, LANES), F32),
                        pltpu.VMEM((D_POOL, D_POOL), BF16)],
        compiler_params=_params(),
        name="lat_mixer",
    )(x_all, h, q, p, p, p, mod3, k3, v3, ck3, cv3, rpb_pad, w_in, w_pool, pscale, wbrp, wbra, wout)


def kernel(x_prompt, x_sample, cache_k, cache_v, c, c_ctx, w_ada, b_ada, g_ff1, w_ff1_in, w_ff1_out,
           g_mix, w_in, q_gain, k_gain, w_pool, pool_scale, rpb, w_br_pool, w_br_att, w_out, g_ff2,
           w_ff2_in, w_ff2_out):
    n_ctx, seq_ctx, _ = x_prompt.shape
    n_lat, seq_lat, _ = x_sample.shape
    depth = w_ada.shape[0]
    assert depth == 1 and n_lat + 1 <= 8
    l = 0

    xp = x_prompt.reshape(n_ctx * seq_ctx, D_MODEL)
    xs = x_sample.reshape(n_lat * seq_lat, D_MODEL)
    t_ctx, t_lat = xp.shape[0], xs.shape[0]
    cond8 = jnp.concatenate([c_ctx[None], c, jnp.zeros((7 - n_lat, D_MODEL), F32)], axis=0)
    mod3 = _adaln(cond8, w_ada[l], b_ada[l]).reshape(8, N_MOD, D_MODEL)

    pscale = pool_scale[l].reshape(1, D_POOL)
    qg2 = jnp.tile(q_gain[l], 2).reshape(1, LANES)
    kg2 = jnp.tile(k_gain[l], 2).reshape(1, LANES)
    g1, gm, g2 = g_ff1[l].reshape(1, -1), g_mix[l].reshape(1, -1), g_ff2[l].reshape(1, -1)
    rpb_pad = jnp.pad(rpb[l], ((0, 0), (0, 0), (0, LANES - rpb.shape[-1])))
    ck3 = cache_k[:, l].reshape(n_lat, -1, D_ATT).astype(BF16)
    cv3 = cache_v[:, l].reshape(n_lat, -1, D_ATT).astype(BF16)

    x1, w2gu, w2dn, win, wbrp, wbra, wout = _ffn(
        xp, xs, mod3, seq_lat, 0, g1, w_ff1_in[l], w_ff1_out[l], "ffn1", convert=True,
        preconvert=(w_ff2_in[l], w_ff2_out[l], w_in[l], w_br_pool[l], w_br_att[l], w_out[l]))
    x2p, k_p, v_p = _ctx_mixer(x1, t_ctx, mod3, gm, win, qg2, kg2, w_pool[l], pscale, wbrp, wbra, wout, seq_ctx)
    h_s, p_s, q_s, k_s, v_s = _lat_proj(x1, t_ctx, t_lat, seq_lat, mod3, gm, win, qg2, kg2)
    x2s = _lat_mixer(x1, t_ctx, n_lat, seq_lat, mod3, h_s, q_s, p_s,
                     k_s.reshape(n_lat, seq_lat, D_ATT), v_s.reshape(n_lat, seq_lat, D_ATT), ck3, cv3,
                     rpb_pad, win, w_pool[l], pscale, wbrp, wbra, wout)
    yp, ys = _ffn(x2p, x2s, mod3, seq_lat, 6, g2, w2gu, w2dn, "ffn2", split_out=True)

    new_k = k_p.reshape(n_ctx, 1, seq_ctx, N_HEADS, HEAD_DIM)
    new_v = v_p.reshape(n_ctx, 1, seq_ctx, N_HEADS, HEAD_DIM)
    return (yp.reshape(x_prompt.shape), ys.reshape(x_sample.shape), new_k, new_v)
```

```python
import functools
import math

import jax
import jax.numpy as jnp
from jax import lax
from jax.experimental import pallas as pl
from jax.experimental.pallas import tpu as pltpu

F32 = jnp.float32
BF16 = jnp.bfloat16

D_MODEL = 1024
N_HEADS = 8
HEAD_DIM = 64
D_ATT = N_HEADS * HEAD_DIM
D_POOL = 512
POOL_WINDOWS = (2, 4, 8, 16)
POOL_HALO = 8
D_FF = 2816
N_MOD = 9
GRID_W = 64
NA_KH = 8
NA_KW = 16
EPS = 1e-6
NEG = -1e30

LANES = 128
BF16_ROWS = 16
N_PAIRS = D_ATT // LANES
MXU_COLS = 256
FF_CHUNKS = ((0, 768), (768, 768), (1536, 768), (2304, 512))
ADALN_TN = 2304
FFN_TM = 512
CTX_TB = 2
PROJ_TM = 512
LAT_ROWS = 4
LAT_TM = LAT_ROWS * GRID_W
LAT_TILES = 2
LAT_KEY_ROWS = 12
SCORE_LOOKAHEAD = 2
N_CONV = 8
PRE_CHUNKS = 16
VMEM_LIMIT = 56 * 1024 * 1024

Q_SCALE = 1.0 / math.sqrt(HEAD_DIM)

O_Q = D_POOL
O_K = D_POOL + D_ATT
O_V = D_POOL + 2 * D_ATT
O_G = D_POOL + 3 * D_ATT


def _dot(a, b):
    return jnp.dot(a, b, preferred_element_type=F32)


def _dot_nt(a, b):
    return lax.dot_general(a, b, (((1,), (1,)), ((), ())), preferred_element_type=F32)


def _rms_mod(x, g, shift, scale):
    ms = jnp.mean(x * x, axis=-1, keepdims=True)
    y = x * lax.rsqrt(ms + EPS) * g
    return y * (1.0 + scale) + shift


def _low_half():
    return lax.broadcasted_iota(jnp.int32, (1, LANES), 1) < HEAD_DIM


def _head_norm(t, gain2):
    lo = _low_half()
    sq = t * t
    s0 = jnp.sum(jnp.where(lo, sq, 0.0), axis=-1, keepdims=True)
    s1 = jnp.sum(jnp.where(lo, 0.0, sq), axis=-1, keepdims=True)
    r0 = lax.rsqrt(s0 * (1.0 / HEAD_DIM) + EPS)
    r1 = lax.rsqrt(s1 * (1.0 / HEAD_DIM) + EPS)
    return t * jnp.where(lo, r0, r1) * gain2


def _own_lanes(head, mine, other):
    lo = _low_half()
    return jnp.where(lo, mine, other) if head % 2 == 0 else jnp.where(lo, other, mine)


def _pipelined_heads(scores, weights, weighted_values, fillers):
    lo = _low_half()
    outs = []
    pending_scores = [scores(head) for head in range(SCORE_LOOKAHEAD)]
    pending_weights = [weights(0, *pending_scores.pop(0))]
    for head in range(N_HEADS):
        if head + SCORE_LOOKAHEAD < N_HEADS:
            pending_scores.append(scores(head + SCORE_LOOKAHEAD))
        fillers[head]()
        if head + 1 < N_HEADS:
            pending_weights.append(weights(head + 1, *pending_scores.pop(0)))
        outs.append(weighted_values(head, pending_weights.pop(0)))
    att_pairs = []
    for pr in range(N_PAIRS):
        even, odd = outs[2 * pr], outs[2 * pr + 1]
        num = jnp.where(lo, even, odd)
        den = pltpu.roll(jnp.where(lo, odd, even), HEAD_DIM, 1)
        att_pairs.append(num / den)
    return jnp.concatenate(att_pairs, axis=1)


def _pool_delta(p_ext, g, pos0, seq_len):
    w = POOL_WINDOWS[g]
    te = p_ext.shape[0]
    tm = te - 2 * POOL_HALO
    t = (pos0 + lax.broadcasted_iota(jnp.int32, (tm, 1), 0)).astype(F32)
    s = p_ext
    step = 1
    while step < w:
        s = s + pltpu.roll(s, step, 0)
        step *= 2
    ahead = w // 2 - 1
    if ahead:
        s = pltpu.roll(s, te - ahead, 0)
    win = s[POOL_HALO:POOL_HALO + tm]
    cnt = jnp.minimum(t + float(w // 2), float(seq_len)) - jnp.maximum(t - float(w // 2), 0.0)
    return win / cnt - p_ext[POOL_HALO:POOL_HALO + tm]


def _const_spec(shape):
    nd = len(shape)
    return pl.BlockSpec(shape, lambda *_: (0,) * nd)


def _resident_spec(w, col_block=None):
    width, cidx = col_block if col_block else (w.shape[1], 0)
    return pl.BlockSpec((w.shape[0], width), lambda *_: (0, cidx), pipeline_mode=pl.Buffered(1))


def _params():
    return pltpu.CompilerParams(dimension_semantics=("arbitrary",), vmem_limit_bytes=VMEM_LIMIT)


def _build_pool_weight(wpool32_ref, wpool_ref):
    wpool_ref[...] = jnp.zeros(wpool_ref.shape, BF16)
    for g in range(len(POOL_WINDOWS)):
        blk = slice(g * LANES, (g + 1) * LANES)
        wpool_ref[blk, blk] = wpool32_ref[g].astype(BF16)


def _adaln_kernel(cond_ref, w_ref, b_ref, o_ref):
    c = cond_ref[...]
    s = c * jax.nn.sigmoid(c)
    o_ref[...] = _dot(s.astype(BF16), w_ref[...].astype(BF16)) + b_ref[...]


def _adaln(cond8, w_ada, b_ada):
    n = w_ada.shape[1]
    tn = ADALN_TN
    return pl.pallas_call(
        _adaln_kernel,
        grid=(n // tn,),
        in_specs=[pl.BlockSpec((8, D_MODEL), lambda i: (0, 0)),
                  pl.BlockSpec((D_MODEL, tn), lambda i: (0, i)),
                  pl.BlockSpec((1, tn), lambda i: (0, i))],
        out_specs=pl.BlockSpec((8, tn), lambda i: (0, i)),
        out_shape=jax.ShapeDtypeStruct((8, n), F32),
        compiler_params=_params(),
        name="adaln",
    )(cond8, w_ada, b_ada.reshape(1, n))


def _ffn_kernel(*refs, mod_off, n_a, split_out, n_conv, n_pre):
    refs = list(refs)
    xa_ref, xb_ref, mod_ref, g_ref, wgu_in, wdn_in = refs[:6]
    pre_src = refs[6:6 + n_pre]
    n_out = 2 if split_out else 1
    out_refs = refs[6 + n_pre:6 + n_pre + n_out]
    pre_dst = refs[6 + n_pre + n_out:6 + 2 * n_pre + n_out]
    wgu_ref, wdn_ref = refs[6 + 2 * n_pre + n_out:] if n_conv else (wgu_in, wdn_in)
    step = pl.program_id(0)

    if n_conv:
        @pl.when(step < n_conv)
        def _convert():
            for src, dst in ((wgu_in, wgu_ref), (wdn_in, wdn_ref)):
                rows = src.shape[0]
                dst[pl.ds(pl.multiple_of(step * rows, rows), rows), :] = src[...].astype(BF16)

    @pl.when(step >= n_conv)
    def _compute():
        tile = step - n_conv
        is_a = tile < n_a
        shift = mod_ref[0, mod_off:mod_off + 1, :]
        scale = mod_ref[0, mod_off + 1:mod_off + 2, :]
        gate = mod_ref[0, mod_off + 2:mod_off + 3, :]
        x = jnp.where(is_a, xa_ref[...], xb_ref[...])
        h = _rms_mod(x, g_ref[...], shift, scale).astype(BF16)
        acc = None
        for c0, cw in FF_CHUNKS:
            a = _dot(h, wgu_ref[:, c0:c0 + cw])
            u = _dot(h, wgu_ref[:, D_FF + c0:D_FF + c0 + cw])
            t = (a * jax.nn.sigmoid(a) * u).astype(BF16)
            part = _dot(t, wdn_ref[c0:c0 + cw, :])
            acc = part if acc is None else acc + part
        y = x + (0.5 * gate) * acc
        if split_out:
            @pl.when(is_a)
            def _store_a():
                out_refs[0][...] = y

            @pl.when(jnp.logical_not(is_a))
            def _store_b():
                out_refs[1][...] = y
        else:
            out_refs[0][...] = y

        if n_pre:
            @pl.when(tile < PRE_CHUNKS)
            def _preconvert():
                for src, dst in zip(pre_src, pre_dst):
                    dst[...] = src[...].astype(BF16)


def _ffn(xa, xb, mod3, seq_b, mod_off, g, wgu, wdn, name, split_out=False, convert=False, preconvert=()):
    tm = FFN_TM
    n_a, n_b = xa.shape[0] // tm, xb.shape[0] // tm
    n_conv = N_CONV if convert else 0
    assert n_a + n_b >= PRE_CHUNKS
    tile = lambda s: jnp.maximum(s - n_conv, 0)
    a_map = lambda s: (jnp.minimum(tile(s), n_a - 1), 0)
    b_map = lambda s: (jnp.maximum(tile(s) - n_a, 0), 0)
    mod_map = lambda s: (jnp.where(tile(s) < n_a, 0, 1 + (jnp.maximum(tile(s) - n_a, 0) * tm) // seq_b), 0, 0)

    def chunk_spec(w, n_chunks, index):
        rows = w.shape[0] // n_chunks
        assert rows * n_chunks == w.shape[0] and rows % BF16_ROWS == 0
        return pl.BlockSpec((rows, w.shape[1]), lambda s: (jnp.minimum(index(s), n_chunks - 1), 0))

    if convert:
        w_specs = [chunk_spec(wgu, N_CONV, lambda s: s), chunk_spec(wdn, N_CONV, lambda s: s)]
        scratch = [pltpu.VMEM(wgu.shape, BF16), pltpu.VMEM(wdn.shape, BF16)]
    else:
        w_specs = [_resident_spec(wgu), _resident_spec(wdn)]
        scratch = []
    pre_specs = lambda: [chunk_spec(w, PRE_CHUNKS, tile) for w in preconvert]
    if split_out:
        out_specs = [pl.BlockSpec((tm, D_MODEL), a_map), pl.BlockSpec((tm, D_MODEL), b_map)]
        out_shape = [jax.ShapeDtypeStruct(xa.shape, F32), jax.ShapeDtypeStruct(xb.shape, F32)]
    else:
        out_specs = [pl.BlockSpec((tm, D_MODEL), lambda s: (tile(s), 0))]
        out_shape = [jax.ShapeDtypeStruct((xa.shape[0] + xb.shape[0], D_MODEL), F32)]
    return pl.pallas_call(
        functools.partial(_ffn_kernel, mod_off=mod_off, n_a=n_a, split_out=split_out, n_conv=n_conv,
                          n_pre=len(preconvert)),
        grid=(n_conv + n_a + n_b,),
        in_specs=[pl.BlockSpec((tm, D_MODEL), a_map), pl.BlockSpec((tm, D_MODEL), b_map),
                  pl.BlockSpec((1, N_MOD, D_MODEL), mod_map), _const_spec((1, D_MODEL))] + w_specs + pre_specs(),
        out_specs=out_specs + pre_specs(),
        out_shape=out_shape + [jax.ShapeDtypeStruct(w.shape, BF16) for w in preconvert],
        scratch_shapes=scratch,
        compiler_params=_params(),
        name=name,
    )(xa, xb, mod3, g, wgu, wdn, *preconvert)


def _heads_to_sublanes(t):
    heads = jnp.stack([t[:, h * HEAD_DIM:(h + 1) * HEAD_DIM] for h in range(N_HEADS)], axis=0)
    return jnp.swapaxes(heads, 0, 1)


def _gate_fillers(gates, h_rows, gate_cols):
    assert 2 * D_MODEL // MXU_COLS == N_HEADS

    def gate_block():
        c0 = len(gates) * MXU_COLS
        gates.append(jax.nn.sigmoid(_dot(h_rows(), gate_cols(c0, c0 + MXU_COLS))))

    return [gate_block] * N_HEADS


def _pool_fillers(st, delta_fn, wpool_ref, pscale_ref, wbrp_ref):
    deltas = []

    def pool_delta():
        deltas.append(delta_fn(len(deltas)).astype(BF16))

    def pool_weight():
        st["y"] = (_dot(jnp.concatenate(deltas, axis=1), wpool_ref[...]) * pscale_ref[...]).astype(BF16)

    def pool_out():
        st["a"] = _dot(st.pop("y"), wbrp_ref[...])

    return [pool_delta] * len(POOL_WINDOWS) + [pool_weight, pool_out]


def _merge_fillers(st, gates, rows, att, x_ref, o_ref, gate2, wbra_ref, wout_ref):
    def attention_branch():
        st["b"] = _dot(att.astype(BF16), wbra_ref[...])

    def gate():
        g = jnp.concatenate(gates, axis=1)
        merged = g[:, :D_MODEL] * st["a"][rows] + g[:, D_MODEL:] * st.pop("b")
        st["merged"] = merged.astype(BF16)

    def out():
        o_ref[rows, :] = x_ref[rows, :] + gate2 * _dot(st.pop("merged"), wout_ref[...])

    return [attention_branch, gate, out]


def _attend_tiles(n_tiles, tile_rows, attention, gate_fillers, pool_fillers, merge_fillers):
    idle = lambda: None
    extra = pool_fillers + [idle] * (N_HEADS - len(pool_fillers))
    for t in range(n_tiles):
        rows = slice(t * tile_rows, (t + 1) * tile_rows)
        gates = []
        fillers = [lambda g=g, e=e: (g(), e()) for g, e in zip(gate_fillers(rows, gates), extra)]
        pieces = merge_fillers(gates, rows, attention(t, fillers))
        extra = [idle] + pieces + [idle] * (N_HEADS - 1 - len(pieces))
    for piece in pieces:
        piece()


def _ctx_mixer_kernel(x_ref, mod_ref, g_ref, win_ref, qg_ref, kg_ref, wpool32_ref, pscale_ref,
                      wbrp_ref, wbra_ref, wout_ref, o_ref, ko_ref, vo_ref, wpool_ref, *, seq):
    @pl.when(pl.program_id(0) == 0)
    def _first_step():
        _build_pool_weight(wpool32_ref, wpool_ref)

    m_rows = x_ref.shape[0]
    h = _rms_mod(x_ref[...], g_ref[...], mod_ref[0, 3:4, :], mod_ref[0, 4:5, :]).astype(BF16)
    p = _dot(h, win_ref[:, 0:O_Q])
    q = _dot(h, win_ref[:, O_Q:O_K])
    k = _dot(h, win_ref[:, O_K:O_V])
    v = _dot(h, win_ref[:, O_V:O_G])
    vo_ref[...] = _heads_to_sublanes(v)
    scale = Q_SCALE
    zero = jnp.zeros((), BF16)
    one = jnp.ones((), BF16)
    st = {}

    def pool_delta(g):
        lanes = slice(g * LANES, (g + 1) * LANES)
        zeros_halo = jnp.zeros((POOL_HALO, LANES), F32)
        return jnp.concatenate(
            [_pool_delta(jnp.concatenate([zeros_halo, p[e * seq:(e + 1) * seq, lanes], zeros_halo], axis=0),
                         g, 0, seq) for e in range(m_rows // seq)], axis=0)

    def gate_fillers(rows, gates):
        return _gate_fillers(gates, lambda: h[rows], lambda c0, c1: win_ref[:, O_G + c0:O_G + c1])

    def attention(e, fillers):
        rows = slice(e * seq, (e + 1) * seq)
        qn, kn, vb = [], [], []
        for pr in range(N_PAIRS):
            lanes = slice(pr * LANES, (pr + 1) * LANES)
            kn.append(_head_norm(k[rows, lanes], kg_ref[...]))
            qn.append((_head_norm(q[rows, lanes], qg_ref[...]) * scale).astype(BF16))
            vb.append(v[rows, lanes].astype(BF16))
        ko_ref[rows] = _heads_to_sublanes(jnp.concatenate(kn, axis=1))
        kb = [t.astype(BF16) for t in kn]

        def scores(head):
            return (_dot_nt(_own_lanes(head, qn[head // 2], zero), kb[head // 2]),)

        def weights(head, s):
            return jnp.exp(s - s.max(axis=-1, keepdims=True)).astype(BF16)

        def weighted_values(head, e_):
            return _dot(e_, _own_lanes(head, vb[head // 2], one))

        return _pipelined_heads(scores, weights, weighted_values, fillers)

    merge = functools.partial(_merge_fillers, st, x_ref=x_ref, o_ref=o_ref, gate2=mod_ref[0, 5:6, :],
                              wbra_ref=wbra_ref, wout_ref=wout_ref)
    _attend_tiles(m_rows // seq, seq, attention, gate_fillers,
                  _pool_fillers(st, pool_delta, wpool_ref, pscale_ref, wbrp_ref), merge)


def _ctx_mixer(x_all, n_tok, mod3, g, w_in, qg2, kg2, w_pool, pscale, wbrp, wbra, wout, seq):
    tm = CTX_TB * seq
    row = lambda s: (s, 0)
    heads = lambda s: (s, 0, 0)
    return pl.pallas_call(
        functools.partial(_ctx_mixer_kernel, seq=seq),
        grid=(n_tok // tm,),
        in_specs=[pl.BlockSpec((tm, D_MODEL), row),
                  _const_spec((1, N_MOD, D_MODEL)), _const_spec((1, D_MODEL)),
                  _resident_spec(w_in), _const_spec((1, LANES)), _const_spec((1, LANES)),
                  _const_spec(w_pool.shape), _const_spec((1, D_POOL)),
                  _resident_spec(wbrp), _resident_spec(wbra), _resident_spec(wout)],
        out_specs=[pl.BlockSpec((tm, D_MODEL), row),
                   pl.BlockSpec((tm, N_HEADS, HEAD_DIM), heads),
                   pl.BlockSpec((tm, N_HEADS, HEAD_DIM), heads)],
        out_shape=[jax.ShapeDtypeStruct((n_tok, D_MODEL), F32),
                   jax.ShapeDtypeStruct((n_tok, N_HEADS, HEAD_DIM), F32),
                   jax.ShapeDtypeStruct((n_tok, N_HEADS, HEAD_DIM), F32)],
        scratch_shapes=[pltpu.VMEM((D_POOL, D_POOL), BF16)],
        compiler_params=_params(),
        name="ctx_mixer",
    )(x_all, mod3, g, w_in, qg2, kg2, w_pool, pscale, wbrp, wbra, wout)


def _lat_proj_kernel(x_ref, mod_ref, g_ref, win_ref, qg_ref, kg_ref, h_ref, p_ref, q_ref, k_ref, v_ref):
    h = _rms_mod(x_ref[...], g_ref[...], mod_ref[0, 3:4, :], mod_ref[0, 4:5, :]).astype(BF16)
    h_ref[...] = h
    p_ref[...] = _dot(h, win_ref[:, 0:O_Q])
    q = _dot(h, win_ref[:, O_Q:O_K])
    k = _dot(h, win_ref[:, O_K:O_V])
    v_ref[...] = _dot(h, win_ref[:, O_V:O_G]).astype(BF16)
    scale = Q_SCALE
    for pr in range(N_PAIRS):
        lanes = slice(pr * LANES, (pr + 1) * LANES)
        q_ref[:, lanes] = (_head_norm(q[:, lanes], qg_ref[...]) * scale).astype(BF16)
        k_ref[:, lanes] = _head_norm(k[:, lanes], kg_ref[...]).astype(BF16)


def _lat_proj(x_all, tok0, n_tok, seq, mod3, g, w_in, qg2, kg2):
    tm = PROJ_TM
    tile0 = tok0 // tm
    row = lambda s: (s, 0)
    shapes = [(D_MODEL, BF16), (D_POOL, F32), (D_ATT, BF16), (D_ATT, BF16), (D_ATT, BF16)]
    return pl.pallas_call(
        _lat_proj_kernel,
        grid=(n_tok // tm,),
        in_specs=[pl.BlockSpec((tm, D_MODEL), lambda s: (tile0 + s, 0)),
                  pl.BlockSpec((1, N_MOD, D_MODEL), lambda s: (1 + (s * tm) // seq, 0, 0)),
                  _const_spec((1, D_MODEL)), _resident_spec(w_in, (O_G, 0)),
                  _const_spec((1, LANES)), _const_spec((1, LANES))],
        out_specs=[pl.BlockSpec((tm, width), row) for width, _ in shapes],
        out_shape=[jax.ShapeDtypeStruct((n_tok, width), dtype) for width, dtype in shapes],
        compiler_params=_params(),
        name="lat_proj",
    )(x_all, mod3, g, w_in, qg2, kg2)


def _build_bias_table(rpb_ref, bias_ref):
    lane = lax.broadcasted_iota(jnp.int32, (GRID_W, LANES), 1)
    qc = lax.broadcasted_iota(jnp.int32, (GRID_W, LANES), 0)
    kc = lane & (GRID_W - 1)
    c0 = jnp.clip(qc - NA_KW // 2, 0, GRID_W - NA_KW)
    col_ok = (kc >= c0) & (kc < c0 + NA_KW)
    first_half = lane < GRID_W
    shift_a = LANES - (NA_KW - 1)
    shift_b = (GRID_W - (NA_KW - 1)) % LANES
    for h in range(N_HEADS):
        for i in range(2 * NA_KH - 2):
            va = jnp.broadcast_to(rpb_ref[h, i:i + 1, :], (GRID_W, LANES))
            vb = jnp.broadcast_to(rpb_ref[h, i + 1:i + 2, :], (GRID_W, LANES))
            ra = pltpu.roll(va, shift_a, 1, stride=1, stride_axis=0)
            rb = pltpu.roll(vb, shift_b, 1, stride=1, stride_axis=0)
            bias_ref[h, i] = jnp.where(col_ok, jnp.where(first_half, ra, rb), NEG)


def _lat_attention(qn, j, fillers, k_ref, v_ref, ck_ref, cv_ref, bias_ref, *, n_rows):
    n_keys = LAT_KEY_ROWS * GRID_W
    r0q = j * LAT_ROWS
    u0 = jnp.clip(r0q - NA_KH // 2, 0, n_rows - LAT_KEY_ROWS)
    key0 = pl.multiple_of(u0 * GRID_W, GRID_W)
    log2_w = GRID_W.bit_length() - 1
    zero = jnp.zeros((), BF16)
    one = jnp.ones((), BF16)

    lane_row = lax.broadcasted_iota(jnp.int32, (1, LANES), 1) & (HEAD_DIM - 1)
    q_tile_row = jnp.right_shift(lax.broadcasted_iota(jnp.int32, (LAT_TM, 1), 0), log2_w)
    q_hot = jnp.where(lane_row == q_tile_row, 1.0, 0.0).astype(BF16)
    k_row = u0 + jnp.right_shift(lax.broadcasted_iota(jnp.int32, (n_keys, 1), 0), log2_w)
    first = jnp.clip(r0q + lane_row - NA_KH // 2, 0, n_rows - NA_KH)
    outside = (lane_row < LAT_ROWS) & ((k_row < first) | (k_row >= first + NA_KH))
    k_mask = jnp.where(outside, NEG, 0.0).astype(BF16)

    def scores(head):
        lanes = slice((head // 2) * LANES, (head // 2 + 1) * LANES)
        qp = qn[:, lanes]
        s_loc = _dot_nt(_own_lanes(head, qp, q_hot),
                        _own_lanes(head, k_ref[0, pl.ds(key0, n_keys), lanes], k_mask))
        return s_loc, _dot_nt(_own_lanes(head, qp, zero), ck_ref[0, :, lanes])

    def weights(head, s_loc, s_ctx):
        bias_rows = []
        for jr in range(LAT_ROWS):
            blocks = []
            for pi in range(LAT_KEY_ROWS // 2):
                idx = jnp.clip(u0 - r0q + 2 * pi - jr + NA_KH - 1, 0, 2 * NA_KH - 3)
                blocks.append(bias_ref[head, idx])
            bias_rows.append(jnp.concatenate(blocks, axis=1))
        s_loc = s_loc + jnp.concatenate(bias_rows, axis=0)
        m = jnp.maximum(s_loc.max(axis=-1, keepdims=True), s_ctx.max(axis=-1, keepdims=True))
        return jnp.concatenate([jnp.exp(s_loc - m).astype(BF16), jnp.exp(s_ctx - m).astype(BF16)], axis=1)

    def weighted_values(head, e):
        lanes = slice((head // 2) * LANES, (head // 2 + 1) * LANES)
        values = jnp.concatenate([_own_lanes(head, v_ref[0, pl.ds(key0, n_keys), lanes], one),
                                  _own_lanes(head, cv_ref[0, :, lanes], one)], axis=0)
        return _dot(e, values)

    return _pipelined_heads(scores, weights, weighted_values, fillers)


def _lat_mixer_kernel(x_ref, h_ref, q_ref, p_ref, pp_ref, pn_ref, mod_ref, k_ref, v_ref, ck_ref, cv_ref,
                      rpb_ref, wg_ref, wpool32_ref, pscale_ref, wbrp_ref, wbra_ref, wout_ref,
                      o_ref, bias_ref, wpool_ref, *, seq, n_rows):
    step = pl.program_id(0)
    tm = LAT_TILES * LAT_TM

    @pl.when(step == 0)
    def _first_step():
        _build_pool_weight(wpool32_ref, wpool_ref)
        _build_bias_table(rpb_ref, bias_ref)

    blk = lax.rem(step, seq // tm)
    st = {}

    def pool_delta(g):
        lanes = slice(g * LANES, (g + 1) * LANES)
        pe = jnp.concatenate([pp_ref[:, lanes], p_ref[:, lanes], pn_ref[:, lanes]], axis=0)
        pos = blk * tm - POOL_HALO + lax.broadcasted_iota(jnp.int32, (tm + 2 * POOL_HALO, 1), 0)
        return _pool_delta(jnp.where((pos >= 0) & (pos < seq), pe, 0.0), g, blk * tm, seq)

    def gate_fillers(rows, gates):
        return _gate_fillers(gates, lambda: h_ref[rows, :], lambda c0, c1: wg_ref[:, c0:c1])

    pool_fillers = _pool_fillers(st, pool_delta, wpool_ref, pscale_ref, wbrp_ref)

    def attention(t, fillers):
        return _lat_attention(q_ref[t * LAT_TM:(t + 1) * LAT_TM, :], blk * LAT_TILES + t, fillers,
                              k_ref, v_ref, ck_ref, cv_ref, bias_ref, n_rows=n_rows)

    merge = functools.partial(_merge_fillers, st, x_ref=x_ref, o_ref=o_ref, gate2=mod_ref[0, 5:6, :],
                              wbra_ref=wbra_ref, wout_ref=wout_ref)
    _attend_tiles(LAT_TILES, LAT_TM, attention, gate_fillers, pool_fillers, merge)


def _lat_mixer(x_all, tok0, n_batch, seq, mod3, h, q, p, k3, v3, ck3, cv3, rpb_pad, w_in, w_pool, pscale,
               wbrp, wbra, wout):
    tm = LAT_TILES * LAT_TM
    n_rows = seq // GRID_W
    nb = seq // tm
    tile0 = tok0 // tm
    halo_per_tile = tm // POOL_HALO
    n_halo = p.shape[0] // POOL_HALO
    row = lambda s: (s, 0)
    seq_map = lambda s: (s // nb, 0, 0)
    return pl.pallas_call(
        functools.partial(_lat_mixer_kernel, seq=seq, n_rows=n_rows),
        grid=(n_batch * nb,),
        in_specs=[pl.BlockSpec((tm, D_MODEL), lambda s: (tile0 + s, 0)),
                  pl.BlockSpec((tm, D_MODEL), row),
                  pl.BlockSpec((tm, D_ATT), row),
                  pl.BlockSpec((tm, D_POOL), row),
                  pl.BlockSpec((POOL_HALO, D_POOL), lambda s: (jnp.maximum(s * halo_per_tile - 1, 0), 0)),
                  pl.BlockSpec((POOL_HALO, D_POOL),
                               lambda s: (jnp.minimum((s + 1) * halo_per_tile, n_halo - 1), 0)),
                  pl.BlockSpec((1, N_MOD, D_MODEL), lambda s: (1 + s // nb, 0, 0)),
                  pl.BlockSpec((1, seq, D_ATT), seq_map), pl.BlockSpec((1, seq, D_ATT), seq_map),
                  pl.BlockSpec((1,) + ck3.shape[1:], seq_map), pl.BlockSpec((1,) + cv3.shape[1:], seq_map),
                  _const_spec(rpb_pad.shape),
                  _resident_spec(w_in, (2 * D_MODEL, O_G // (2 * D_MODEL))),
                  _const_spec(w_pool.shape), _const_spec((1, D_POOL)),
                  _resident_spec(wbrp), _resident_spec(wbra), _resident_spec(wout)],
        out_specs=pl.BlockSpec((tm, D_MODEL), row),
        out_shape=jax.ShapeDtypeStruct((n_batch * seq, D_MODEL), F32),
        scratch_shapes=[pltpu.VMEM((N_HEADS, 2 * NA_KH - 2, GRID_W, LANES), F32),
                        pltpu.VMEM((D_POOL, D_POOL), BF16)],
        compiler_params=_params(),
        name="lat_mixer",
    )(x_all, h, q, p, p, p, mod3, k3, v3, ck3, cv3, rpb_pad, w_in, w_pool, pscale, wbrp, wbra, wout)


def kernel(x_prompt, x_sample, cache_k, cache_v, c, c_ctx, w_ada, b_ada, g_ff1, w_ff1_in, w_ff1_out,
           g_mix, w_in, q_gain, k_gain, w_pool, pool_scale, rpb, w_br_pool, w_br_att, w_out, g_ff2,
           w_ff2_in, w_ff2_out):
    n_ctx, seq_ctx, _ = x_prompt.shape
    n_lat, seq_lat, _ = x_sample.shape
    depth = w_ada.shape[0]
    assert depth == 1 and n_lat + 1 <= 8
    l = 0

    xp = x_prompt.reshape(n_ctx * seq_ctx, D_MODEL)
    xs = x_sample.reshape(n_lat * seq_lat, D_MODEL)
    t_ctx, t_lat = xp.shape[0], xs.shape[0]
    cond8 = jnp.concatenate([c_ctx[None], c, jnp.zeros((7 - n_lat, D_MODEL), F32)], axis=0)
    mod3 = _adaln(cond8, w_ada[l], b_ada[l]).reshape(8, N_MOD, D_MODEL)

    pscale = pool_scale[l].reshape(1, D_POOL)
    qg2 = jnp.tile(q_gain[l], 2).reshape(1, LANES)
    kg2 = jnp.tile(k_gain[l], 2).reshape(1, LANES)
    g1, gm, g2 = g_ff1[l].reshape(1, -1), g_mix[l].reshape(1, -1), g_ff2[l].reshape(1, -1)
    rpb_pad = jnp.pad(rpb[l], ((0, 0), (0, 0), (0, LANES - rpb.shape[-1])))
    ck3 = cache_k[:, l].reshape(n_lat, -1, D_ATT).astype(BF16)
    cv3 = cache_v[:, l].reshape(n_lat, -1, D_ATT).astype(BF16)

    x1, w2gu, w2dn, win, wbrp, wbra, wout = _ffn(
        xp, xs, mod3, seq_lat, 0, g1, w_ff1_in[l], w_ff1_out[l], "ffn1", convert=True,
        preconvert=(w_ff2_in[l], w_ff2_out[l], w_in[l], w_br_pool[l], w_br_att[l], w_out[l]))
    x2p, k_p, v_p = _ctx_mixer(x1, t_ctx, mod3, gm, win, qg2, kg2, w_pool[l], pscale, wbrp, wbra, wout, seq_ctx)
    h_s, p_s, q_s, k_s, v_s = _lat_proj(x1, t_ctx, t_lat, seq_lat, mod3, gm, win, qg2, kg2)
    x2s = _lat_mixer(x1, t_ctx, n_lat, seq_lat, mod3, h_s, q_s, p_s,
                     k_s.reshape(n_lat, seq_lat, D_ATT), v_s.reshape(n_lat, seq_lat, D_ATT), ck3, cv3,
                     rpb_pad, win, w_pool[l], pscale, wbrp, wbra, wout)
    yp, ys = _ffn(x2p, x2s, mod3, seq_lat, 6, g2, w2gu, w2dn, "ffn2", split_out=True)

    new_k = k_p.reshape(n_ctx, 1, seq_ctx, N_HEADS, HEAD_DIM)
    new_v = v_p.reshape(n_ctx, 1, seq_ctx, N_HEADS, HEAD_DIM)
    return (yp.reshape(x_prompt.shape), ys.reshape(x_sample.shape), new_k, new_v)
```

```python
import functools
import math

import jax
import jax.numpy as jnp
from jax import lax
from jax.experimental import pallas as pl
from jax.experimental.pallas import tpu as pltpu

F32 = jnp.float32
BF16 = jnp.bfloat16

D_MODEL = 1024
N_HEADS = 8
HEAD_DIM = 64
D_ATT = N_HEADS * HEAD_DIM
D_POOL = 512
POOL_WINDOWS = (2, 4, 8, 16)
POOL_HALO = 8
D_FF = 2816
N_MOD = 9
GRID_W = 64
NA_KH = 8
NA_KW = 16
EPS = 1e-6
NEG = -1e30

LANES = 128
BF16_ROWS = 16
N_PAIRS = D_ATT // LANES
MXU_COLS = 256
FF_CHUNKS = ((0, 768), (768, 768), (1536, 768), (2304, 512))
ADALN_TN = 2304
FFN_TM = 512
CTX_TB = 2
PROJ_TM = 512
LAT_ROWS = 4
LAT_TM = LAT_ROWS * GRID_W
LAT_TILES = 2
LAT_KEY_ROWS = 12
SCORE_LOOKAHEAD = 2
N_CONV = 8
PRE_CHUNKS = 16
VMEM_LIMIT = 56 * 1024 * 1024

Q_SCALE = 1.0 / math.sqrt(HEAD_DIM)

O_Q = D_POOL
O_K = D_POOL + D_ATT
O_V = D_POOL + 2 * D_ATT
O_G = D_POOL + 3 * D_ATT


def _dot(a, b):
    return jnp.dot(a, b, preferred_element_type=F32)


def _dot_nt(a, b):
    return lax.dot_general(a, b, (((1,), (1,)), ((), ())), preferred_element_type=F32)


def _rms_mod(x, g, shift, scale):
    ms = jnp.mean(x * x, axis=-1, keepdims=True)
    y = x * lax.rsqrt(ms + EPS) * g
    return y * (1.0 + scale) + shift


def _low_half():
    return lax.broadcasted_iota(jnp.int32, (1, LANES), 1) < HEAD_DIM


def _head_norm(t, gain2):
    lo = _low_half()
    sq = t * t
    s0 = jnp.sum(jnp.where(lo, sq, 0.0), axis=-1, keepdims=True)
    s1 = jnp.sum(jnp.where(lo, 0.0, sq), axis=-1, keepdims=True)
    r0 = lax.rsqrt(s0 * (1.0 / HEAD_DIM) + EPS)
    r1 = lax.rsqrt(s1 * (1.0 / HEAD_DIM) + EPS)
    return t * jnp.where(lo, r0, r1) * gain2


def _own_lanes(head, mine, other):
    lo = _low_half()
    return jnp.where(lo, mine, other) if head % 2 == 0 else jnp.where(lo, other, mine)


def _pipelined_heads(scores, weights, weighted_values, fillers):
    lo = _low_half()
    outs = []
    pending_scores = [scores(head) for head in range(SCORE_LOOKAHEAD)]
    pending_weights = [weights(0, *pending_scores.pop(0))]
    for head in range(N_HEADS):
        if head + SCORE_LOOKAHEAD < N_HEADS:
            pending_scores.append(scores(head + SCORE_LOOKAHEAD))
        fillers[head]()
        if head + 1 < N_HEADS:
            pending_weights.append(weights(head + 1, *pending_scores.pop(0)))
        outs.append(weighted_values(head, pending_weights.pop(0)))
    att_pairs = []
    for pr in range(N_PAIRS):
        even, odd = outs[2 * pr], outs[2 * pr + 1]
        num = jnp.where(lo, even, odd)
        den = pltpu.roll(jnp.where(lo, odd, even), HEAD_DIM, 1)
        att_pairs.append(num / den)
    return jnp.concatenate(att_pairs, axis=1)


def _pool_delta(p_ext, g, pos0, seq_len):
    w = POOL_WINDOWS[g]
    te = p_ext.shape[0]
    tm = te - 2 * POOL_HALO
    t = (pos0 + lax.broadcasted_iota(jnp.int32, (tm, 1), 0)).astype(F32)
    s = p_ext
    step = 1
    while step < w:
        s = s + pltpu.roll(s, step, 0)
        step *= 2
    ahead = w // 2 - 1
    if ahead:
        s = pltpu.roll(s, te - ahead, 0)
    win = s[POOL_HALO:POOL_HALO + tm]
    cnt = jnp.minimum(t + float(w // 2), float(seq_len)) - jnp.maximum(t - float(w // 2), 0.0)
    return win / cnt - p_ext[POOL_HALO:POOL_HALO + tm]


def _const_spec(shape):
    nd = len(shape)
    return pl.BlockSpec(shape, lambda *_: (0,) * nd)


def _resident_spec(w, col_block=None):
    width, cidx = col_block if col_block else (w.shape[1], 0)
    return pl.BlockSpec((w.shape[0], width), lambda *_: (0, cidx), pipeline_mode=pl.Buffered(1))


def _params():
    return pltpu.CompilerParams(dimension_semantics=("arbitrary",), vmem_limit_bytes=VMEM_LIMIT)


def _build_pool_weight(wpool32_ref, wpool_ref):
    wpool_ref[...] = jnp.zeros(wpool_ref.shape, BF16)
    for g in range(len(POOL_WINDOWS)):
        blk = slice(g * LANES, (g + 1) * LANES)
        wpool_ref[blk, blk] = wpool32_ref[g].astype(BF16)


def _adaln_kernel(cond_ref, w_ref, b_ref, o_ref):
    c = cond_ref[...]
    s = c * jax.nn.sigmoid(c)
    o_ref[...] = _dot(s.astype(BF16), w_ref[...].astype(BF16)) + b_ref[...]


def _adaln(cond8, w_ada, b_ada):
    n = w_ada.shape[1]
    tn = ADALN_TN
    return pl.pallas_call(
        _adaln_kernel,
        grid=(n // tn,),
        in_specs=[pl.BlockSpec((8, D_MODEL), lambda i: (0, 0)),
                  pl.BlockSpec((D_MODEL, tn), lambda i: (0, i)),
                  pl.BlockSpec((1, tn), lambda i: (0, i))],
        out_specs=pl.BlockSpec((8, tn), lambda i: (0, i)),
        out_shape=jax.ShapeDtypeStruct((8, n), F32),
        compiler_params=_params(),
        name="adaln",
    )(cond8, w_ada, b_ada.reshape(1, n))


def _ffn_kernel(*refs, mod_off, n_a, split_out, n_conv, n_pre):
    refs = list(refs)
    xa_ref, xb_ref, mod_ref, g_ref, wgu_in, wdn_in = refs[:6]
    pre_src = refs[6:6 + n_pre]
    n_out = 2 if split_out else 1
    out_refs = refs[6 + n_pre:6 + n_pre + n_out]
    pre_dst = refs[6 + n_pre + n_out:6 + 2 * n_pre + n_out]
    wgu_ref, wdn_ref = refs[6 + 2 * n_pre + n_out:] if n_conv else (wgu_in, wdn_in)
    step = pl.program_id(0)

    if n_conv:
        @pl.when(step < n_conv)
        def _convert():
            for src, dst in ((wgu_in, wgu_ref), (wdn_in, wdn_ref)):
                rows = src.shape[0]
                dst[pl.ds(pl.multiple_of(step * rows, rows), rows), :] = src[...].astype(BF16)

    @pl.when(step >= n_conv)
    def _compute():
        tile = step - n_conv
        is_a = tile < n_a
        shift = mod_ref[0, mod_off:mod_off + 1, :]
        scale = mod_ref[0, mod_off + 1:mod_off + 2, :]
        gate = mod_ref[0, mod_off + 2:mod_off + 3, :]
        x = jnp.where(is_a, xa_ref[...], xb_ref[...])
        h = _rms_mod(x, g_ref[...], shift, scale).astype(BF16)
        acc = None
        for c0, cw in FF_CHUNKS:
            a = _dot(h, wgu_ref[:, c0:c0 + cw])
            u = _dot(h, wgu_ref[:, D_FF + c0:D_FF + c0 + cw])
            t = (a * jax.nn.sigmoid(a) * u).astype(BF16)
            part = _dot(t, wdn_ref[c0:c0 + cw, :])
            acc = part if acc is None else acc + part
        y = x + (0.5 * gate) * acc
        if split_out:
            @pl.when(is_a)
            def _store_a():
                out_refs[0][...] = y

            @pl.when(jnp.logical_not(is_a))
            def _store_b():
                out_refs[1][...] = y
        else:
            out_refs[0][...] = y

        if n_pre:
            @pl.when(tile < PRE_CHUNKS)
            def _preconvert():
                for src, dst in zip(pre_src, pre_dst):
                    dst[...] = src[...].astype(BF16)


def _ffn(xa, xb, mod3, seq_b, mod_off, g, wgu, wdn, name, split_out=False, convert=False, preconvert=()):
    tm = FFN_TM
    n_a, n_b = xa.shape[0] // tm, xb.shape[0] // tm
    n_conv = N_CONV if convert else 0
    assert n_a + n_b >= PRE_CHUNKS
    tile = lambda s: jnp.maximum(s - n_conv, 0)
    a_map = lambda s: (jnp.minimum(tile(s), n_a - 1), 0)
    b_map = lambda s: (jnp.maximum(tile(s) - n_a, 0), 0)
    mod_map = lambda s: (jnp.where(tile(s) < n_a, 0, 1 + (jnp.maximum(tile(s) - n_a, 0) * tm) // seq_b), 0, 0)

    def chunk_spec(w, n_chunks, index):
        rows = w.shape[0] // n_chunks
        assert rows * n_chunks == w.shape[0] and rows % BF16_ROWS == 0
        return pl.BlockSpec((rows, w.shape[1]), lambda s: (jnp.minimum(index(s), n_chunks - 1), 0))

    if convert:
        w_specs = [chunk_spec(wgu, N_CONV, lambda s: s), chunk_spec(wdn, N_CONV, lambda s: s)]
        scratch = [pltpu.VMEM(wgu.shape, BF16), pltpu.VMEM(wdn.shape, BF16)]
    else:
        w_specs = [_resident_spec(wgu), _resident_spec(wdn)]
        scratch = []
    pre_specs = lambda: [chunk_spec(w, PRE_CHUNKS, tile) for w in preconvert]
    if split_out:
        out_specs = [pl.BlockSpec((tm, D_MODEL), a_map), pl.BlockSpec((tm, D_MODEL), b_map)]
        out_shape = [jax.ShapeDtypeStruct(xa.shape, F32), jax.ShapeDtypeStruct(xb.shape, F32)]
    else:
        out_specs = [pl.BlockSpec((tm, D_MODEL), lambda s: (tile(s), 0))]
        out_shape = [jax.ShapeDtypeStruct((xa.shape[0] + xb.shape[0], D_MODEL), F32)]
    return pl.pallas_call(
        functools.partial(_ffn_kernel, mod_off=mod_off, n_a=n_a, split_out=split_out, n_conv=n_conv,
                          n_pre=len(preconvert)),
        grid=(n_conv + n_a + n_b,),
        in_specs=[pl.BlockSpec((tm, D_MODEL), a_map), pl.BlockSpec((tm, D_MODEL), b_map),
                  pl.BlockSpec((1, N_MOD, D_MODEL), mod_map), _const_spec((1, D_MODEL))] + w_specs + pre_specs(),
        out_specs=out_specs + pre_specs(),
        out_shape=out_shape + [jax.ShapeDtypeStruct(w.shape, BF16) for w in preconvert],
        scratch_shapes=scratch,
        compiler_params=_params(),
        name=name,
    )(xa, xb, mod3, g, wgu, wdn, *preconvert)


def _heads_to_sublanes(t):
    heads = jnp.stack([t[:, h * HEAD_DIM:(h + 1) * HEAD_DIM] for h in range(N_HEADS)], axis=0)
    return jnp.swapaxes(heads, 0, 1)


def _gate_fillers(gates, h_rows, gate_cols):
    assert 2 * D_MODEL // MXU_COLS == N_HEADS

    def gate_block():
        c0 = len(gates) * MXU_COLS
        gates.append(jax.nn.sigmoid(_dot(h_rows(), gate_cols(c0, c0 + MXU_COLS))))

    return [gate_block] * N_HEADS


def _pool_fillers(st, delta_fn, wpool_ref, pscale_ref, wbrp_ref):
    def pool_delta():
        deltas = [delta_fn(g).astype(BF16) for g in range(len(POOL_WINDOWS))]
        st["d"] = jnp.concatenate(deltas, axis=1)

    def pool_weight():
        st["y"] = (_dot(st.pop("d"), wpool_ref[...]) * pscale_ref[...]).astype(BF16)

    def pool_out():
        st["a"] = _dot(st.pop("y"), wbrp_ref[...])

    return [pool_delta, pool_weight, pool_out]


def _merge_fillers(st, gates, rows, att, x_ref, o_ref, gate2, wbra_ref, wout_ref):
    def attention_branch():
        st["b"] = _dot(att.astype(BF16), wbra_ref[...])

    def gate():
        g = jnp.concatenate(gates, axis=1)
        merged = g[:, :D_MODEL] * st["a"][rows] + g[:, D_MODEL:] * st.pop("b")
        st["merged"] = merged.astype(BF16)

    def out():
        o_ref[rows, :] = x_ref[rows, :] + gate2 * _dot(st.pop("merged"), wout_ref[...])

    return [attention_branch, gate, out]


def _attend_tiles(n_tiles, tile_rows, attention, gate_fillers, pool_fillers, merge_fillers):
    idle = lambda: None
    extra = pool_fillers + [idle] * (N_HEADS - len(pool_fillers))
    for t in range(n_tiles):
        rows = slice(t * tile_rows, (t + 1) * tile_rows)
        gates = []
        fillers = [lambda g=g, e=e: (g(), e()) for g, e in zip(gate_fillers(rows, gates), extra)]
        pieces = merge_fillers(gates, rows, attention(t, fillers))
        extra = [idle] + pieces + [idle] * (N_HEADS - 1 - len(pieces))
    for piece in pieces:
        piece()


def _ctx_mixer_kernel(x_ref, mod_ref, g_ref, win_ref, qg_ref, kg_ref, wpool32_ref, pscale_ref,
                      wbrp_ref, wbra_ref, wout_ref, o_ref, ko_ref, vo_ref, wpool_ref, *, seq):
    @pl.when(pl.program_id(0) == 0)
    def _first_step():
        _build_pool_weight(wpool32_ref, wpool_ref)

    m_rows = x_ref.shape[0]
    h = _rms_mod(x_ref[...], g_ref[...], mod_ref[0, 3:4, :], mod_ref[0, 4:5, :]).astype(BF16)
    p = _dot(h, win_ref[:, 0:O_Q])
    q = _dot(h, win_ref[:, O_Q:O_K])
    k = _dot(h, win_ref[:, O_K:O_V])
    v = _dot(h, win_ref[:, O_V:O_G])
    vo_ref[...] = _heads_to_sublanes(v)
    scale = Q_SCALE
    zero = jnp.zeros((), BF16)
    one = jnp.ones((), BF16)
    st = {}

    def pool_delta(g):
        lanes = slice(g * LANES, (g + 1) * LANES)
        zeros_halo = jnp.zeros((POOL_HALO, LANES), F32)
        return jnp.concatenate(
            [_pool_delta(jnp.concatenate([zeros_halo, p[e * seq:(e + 1) * seq, lanes], zeros_halo], axis=0),
                         g, 0, seq) for e in range(m_rows // seq)], axis=0)

    def gate_fillers(rows, gates):
        return _gate_fillers(gates, lambda: h[rows], lambda c0, c1: win_ref[:, O_G + c0:O_G + c1])

    def attention(e, fillers):
        rows = slice(e * seq, (e + 1) * seq)
        qn, kn, vb = [], [], []
        for pr in range(N_PAIRS):
            lanes = slice(pr * LANES, (pr + 1) * LANES)
            kn.append(_head_norm(k[rows, lanes], kg_ref[...]))
            qn.append((_head_norm(q[rows, lanes], qg_ref[...]) * scale).astype(BF16))
            vb.append(v[rows, lanes].astype(BF16))
        ko_ref[rows] = _heads_to_sublanes(jnp.concatenate(kn, axis=1))
        kb = [t.astype(BF16) for t in kn]

        def scores(head):
            return (_dot_nt(_own_lanes(head, qn[head // 2], zero), kb[head // 2]),)

        def weights(head, s):
            return jnp.exp(s - s.max(axis=-1, keepdims=True)).astype(BF16)

        def weighted_values(head, e_):
            return _dot(e_, _own_lanes(head, vb[head // 2], one))

        return _pipelined_heads(scores, weights, weighted_values, fillers)

    merge = functools.partial(_merge_fillers, st, x_ref=x_ref, o_ref=o_ref, gate2=mod_ref[0, 5:6, :],
                              wbra_ref=wbra_ref, wout_ref=wout_ref)
    _attend_tiles(m_rows // seq, seq, attention, gate_fillers,
                  _pool_fillers(st, pool_delta, wpool_ref, pscale_ref, wbrp_ref), merge)


def _ctx_mixer(x_all, n_tok, mod3, g, w_in, qg2, kg2, w_pool, pscale, wbrp, wbra, wout, seq):
    tm = CTX_TB * seq
    row = lambda s: (s, 0)
    heads = lambda s: (s, 0, 0)
    return pl.pallas_call(
        functools.partial(_ctx_mixer_kernel, seq=seq),
        grid=(n_tok // tm,),
        in_specs=[pl.BlockSpec((tm, D_MODEL), row),
                  _const_spec((1, N_MOD, D_MODEL)), _const_spec((1, D_MODEL)),
                  _resident_spec(w_in), _const_spec((1, LANES)), _const_spec((1, LANES)),
                  _const_spec(w_pool.shape), _const_spec((1, D_POOL)),
                  _resident_spec(wbrp), _resident_spec(wbra), _resident_spec(wout)],
        out_specs=[pl.BlockSpec((tm, D_MODEL), row),
                   pl.BlockSpec((tm, N_HEADS, HEAD_DIM), heads),
                   pl.BlockSpec((tm, N_HEADS, HEAD_DIM), heads)],
        out_shape=[jax.ShapeDtypeStruct((n_tok, D_MODEL), F32),
                   jax.ShapeDtypeStruct((n_tok, N_HEADS, HEAD_DIM), F32),
                   jax.ShapeDtypeStruct((n_tok, N_HEADS, HEAD_DIM), F32)],
        scratch_shapes=[pltpu.VMEM((D_POOL, D_POOL), BF16)],
        compiler_params=_params(),
        name="ctx_mixer",
    )(x_all, mod3, g, w_in, qg2, kg2, w_pool, pscale, wbrp, wbra, wout)


def _lat_proj_kernel(x_ref, mod_ref, g_ref, win_ref, qg_ref, kg_ref, h_ref, p_ref, q_ref, k_ref, v_ref):
    h = _rms_mod(x_ref[...], g_ref[...], mod_ref[0, 3:4, :], mod_ref[0, 4:5, :]).astype(BF16)
    h_ref[...] = h
    p_ref[...] = _dot(h, win_ref[:, 0:O_Q])
    q = _dot(h, win_ref[:, O_Q:O_K])
    k = _dot(h, win_ref[:, O_K:O_V])
    v_ref[...] = _dot(h, win_ref[:, O_V:O_G]).astype(BF16)
    scale = Q_SCALE
    for pr in range(N_PAIRS):
        lanes = slice(pr * LANES, (pr + 1) * LANES)
        q_ref[:, lanes] = (_head_norm(q[:, lanes], qg_ref[...]) * scale).astype(BF16)
        k_ref[:, lanes] = _head_norm(k[:, lanes], kg_ref[...]).astype(BF16)


def _lat_proj(x_all, tok0, n_tok, seq, mod3, g, w_in, qg2, kg2):
    tm = PROJ_TM
    tile0 = tok0 // tm
    row = lambda s: (s, 0)
    shapes = [(D_MODEL, BF16), (D_POOL, F32), (D_ATT, BF16), (D_ATT, BF16), (D_ATT, BF16)]
    return pl.pallas_call(
        _lat_proj_kernel,
        grid=(n_tok // tm,),
        in_specs=[pl.BlockSpec((tm, D_MODEL), lambda s: (tile0 + s, 0)),
                  pl.BlockSpec((1, N_MOD, D_MODEL), lambda s: (1 + (s * tm) // seq, 0, 0)),
                  _const_spec((1, D_MODEL)), _resident_spec(w_in, (O_G, 0)),
                  _const_spec((1, LANES)), _const_spec((1, LANES))],
        out_specs=[pl.BlockSpec((tm, width), row) for width, _ in shapes],
        out_shape=[jax.ShapeDtypeStruct((n_tok, width), dtype) for width, dtype in shapes],
        compiler_params=_params(),
        name="lat_proj",
    )(x_all, mod3, g, w_in, qg2, kg2)


def _build_bias_table(rpb_ref, bias_ref):
    lane = lax.broadcasted_iota(jnp.int32, (GRID_W, LANES), 1)
    qc = lax.broadcasted_iota(jnp.int32, (GRID_W, LANES), 0)
    kc = lane & (GRID_W - 1)
    c0 = jnp.clip(qc - NA_KW // 2, 0, GRID_W - NA_KW)
    col_ok = (kc >= c0) & (kc < c0 + NA_KW)
    first_half = lane < GRID_W
    shift_a = LANES - (NA_KW - 1)
    shift_b = (GRID_W - (NA_KW - 1)) % LANES
    for h in range(N_HEADS):
        for i in range(2 * NA_KH - 2):
            va = jnp.broadcast_to(rpb_ref[h, i:i + 1, :], (GRID_W, LANES))
            vb = jnp.broadcast_to(rpb_ref[h, i + 1:i + 2, :], (GRID_W, LANES))
            ra = pltpu.roll(va, shift_a, 1, stride=1, stride_axis=0)
            rb = pltpu.roll(vb, shift_b, 1, stride=1, stride_axis=0)
            bias_ref[h, i] = jnp.where(col_ok, jnp.where(first_half, ra, rb), NEG)


def _lat_attention(qn, j, fillers, k_ref, v_ref, ck_ref, cv_ref, bias_ref, *, n_rows):
    n_keys = LAT_KEY_ROWS * GRID_W
    r0q = j * LAT_ROWS
    u0 = jnp.clip(r0q - NA_KH // 2, 0, n_rows - LAT_KEY_ROWS)
    key0 = pl.multiple_of(u0 * GRID_W, GRID_W)
    log2_w = GRID_W.bit_length() - 1
    zero = jnp.zeros((), BF16)
    one = jnp.ones((), BF16)

    lane_row = lax.broadcasted_iota(jnp.int32, (1, LANES), 1) & (HEAD_DIM - 1)
    q_tile_row = jnp.right_shift(lax.broadcasted_iota(jnp.int32, (LAT_TM, 1), 0), log2_w)
    q_hot = jnp.where(lane_row == q_tile_row, 1.0, 0.0).astype(BF16)
    k_row = u0 + jnp.right_shift(lax.broadcasted_iota(jnp.int32, (n_keys, 1), 0), log2_w)
    first = jnp.clip(r0q + lane_row - NA_KH // 2, 0, n_rows - NA_KH)
    outside = (lane_row < LAT_ROWS) & ((k_row < first) | (k_row >= first + NA_KH))
    k_mask = jnp.where(outside, NEG, 0.0).astype(BF16)

    def scores(head):
        lanes = slice((head // 2) * LANES, (head // 2 + 1) * LANES)
        qp = qn[:, lanes]
        s_loc = _dot_nt(_own_lanes(head, qp, q_hot),
                        _own_lanes(head, k_ref[0, pl.ds(key0, n_keys), lanes], k_mask))
        return s_loc, _dot_nt(_own_lanes(head, qp, zero), ck_ref[0, :, lanes])

    def weights(head, s_loc, s_ctx):
        bias_rows = []
        for jr in range(LAT_ROWS):
            blocks = []
            for pi in range(LAT_KEY_ROWS // 2):
                idx = jnp.clip(u0 - r0q + 2 * pi - jr + NA_KH - 1, 0, 2 * NA_KH - 3)
                blocks.append(bias_ref[head, idx])
            bias_rows.append(jnp.concatenate(blocks, axis=1))
        s_loc = s_loc + jnp.concatenate(bias_rows, axis=0)
        m = jnp.maximum(s_loc.max(axis=-1, keepdims=True), s_ctx.max(axis=-1, keepdims=True))
        return jnp.concatenate([jnp.exp(s_loc - m).astype(BF16), jnp.exp(s_ctx - m).astype(BF16)], axis=1)

    def weighted_values(head, e):
        lanes = slice((head // 2) * LANES, (head // 2 + 1) * LANES)
        values = jnp.concatenate([_own_lanes(head, v_ref[0, pl.ds(key0, n_keys), lanes], one),
                                  _own_lanes(head, cv_ref[0, :, lanes], one)], axis=0)
        return _dot(e, values)

    return _pipelined_heads(scores, weights, weighted_values, fillers)


def _lat_mixer_kernel(x_ref, h_ref, q_ref, p_ref, pp_ref, pn_ref, mod_ref, k_ref, v_ref, ck_ref, cv_ref,
                      rpb_ref, wg_ref, wpool32_ref, pscale_ref, wbrp_ref, wbra_ref, wout_ref,
                      o_ref, bias_ref, wpool_ref, *, seq, n_rows):
    step = pl.program_id(0)
    tm = LAT_TILES * LAT_TM

    @pl.when(step == 0)
    def _first_step():
        _build_pool_weight(wpool32_ref, wpool_ref)
        _build_bias_table(rpb_ref, bias_ref)

    blk = lax.rem(step, seq // tm)
    st = {}

    def pool_delta(g):
        lanes = slice(g * LANES, (g + 1) * LANES)
        pe = jnp.concatenate([pp_ref[:, lanes], p_ref[:, lanes], pn_ref[:, lanes]], axis=0)
        pos = blk * tm - POOL_HALO + lax.broadcasted_iota(jnp.int32, (tm + 2 * POOL_HALO, 1), 0)
        return _pool_delta(jnp.where((pos >= 0) & (pos < seq), pe, 0.0), g, blk * tm, seq)

    block_gates = []

    def gate_fillers(rows, gates):
        if not block_gates:
            fillers = _gate_fillers(block_gates, lambda: h_ref[...], lambda c0, c1: wg_ref[:, c0:c1])
            return [lambda f=f: (f(), gates.append(block_gates[-1][rows])) for f in fillers]
        gates.extend(g[rows] for g in block_gates)
        return [lambda: None] * N_HEADS

    pool_fillers = _pool_fillers(st, pool_delta, wpool_ref, pscale_ref, wbrp_ref)

    def attention(t, fillers):
        return _lat_attention(q_ref[t * LAT_TM:(t + 1) * LAT_TM, :], blk * LAT_TILES + t, fillers,
                              k_ref, v_ref, ck_ref, cv_ref, bias_ref, n_rows=n_rows)

    merge = functools.partial(_merge_fillers, st, x_ref=x_ref, o_ref=o_ref, gate2=mod_ref[0, 5:6, :],
                              wbra_ref=wbra_ref, wout_ref=wout_ref)
    _attend_tiles(LAT_TILES, LAT_TM, attention, gate_fillers, pool_fillers, merge)


def _lat_mixer(x_all, tok0, n_batch, seq, mod3, h, q, p, k3, v3, ck3, cv3, rpb_pad, w_in, w_pool, pscale,
               wbrp, wbra, wout):
    tm = LAT_TILES * LAT_TM
    n_rows = seq // GRID_W
    nb = seq // tm
    tile0 = tok0 // tm
    halo_per_tile = tm // POOL_HALO
    n_halo = p.shape[0] // POOL_HALO
    row = lambda s: (s, 0)
    seq_map = lambda s: (s // nb, 0, 0)
    return pl.pallas_call(
        functools.partial(_lat_mixer_kernel, seq=seq, n_rows=n_rows),
        grid=(n_batch * nb,),
        in_specs=[pl.BlockSpec((tm, D_MODEL), lambda s: (tile0 + s, 0)),
                  pl.BlockSpec((tm, D_MODEL), row),
                  pl.BlockSpec((tm, D_ATT), row),
                  pl.BlockSpec((tm, D_POOL), row),
                  pl.BlockSpec((POOL_HALO, D_POOL), lambda s: (jnp.maximum(s * halo_per_tile - 1, 0), 0)),
                  pl.BlockSpec((POOL_HALO, D_POOL),
                               lambda s: (jnp.minimum((s + 1) * halo_per_tile, n_halo - 1), 0)),
                  pl.BlockSpec((1, N_MOD, D_MODEL), lambda s: (1 + s // nb, 0, 0)),
                  pl.BlockSpec((1, seq, D_ATT), seq_map), pl.BlockSpec((1, seq, D_ATT), seq_map),
                  pl.BlockSpec((1,) + ck3.shape[1:], seq_map), pl.BlockSpec((1,) + cv3.shape[1:], seq_map),
                  _const_spec(rpb_pad.shape),
                  _resident_spec(w_in, (2 * D_MODEL, O_G // (2 * D_MODEL))),
                  _const_spec(w_pool.shape), _const_spec((1, D_POOL)),
                  _resident_spec(wbrp), _resident_spec(wbra), _resident_spec(wout)],
        out_specs=pl.BlockSpec((tm, D_MODEL), row),
        out_shape=jax.ShapeDtypeStruct((n_batch * seq, D_MODEL), F32),
        scratch_shapes=[pltpu.VMEM((N_HEADS, 2 * NA_KH - 2, GRID_W, LANES), F32),
                        pltpu.VMEM((D_POOL, D_POOL), BF16)],
        compiler_params=_params(),
        name="lat_mixer",
    )(x_all, h, q, p, p, p, mod3, k3, v3, ck3, cv3, rpb_pad, w_in, w_pool, pscale, wbrp, wbra, wout)


def kernel(x_prompt, x_sample, cache_k, cache_v, c, c_ctx, w_ada, b_ada, g_ff1, w_ff1_in, w_ff1_out,
           g_mix, w_in, q_gain, k_gain, w_pool, pool_scale, rpb, w_br_pool, w_br_att, w_out, g_ff2,
           w_ff2_in, w_ff2_out):
    n_ctx, seq_ctx, _ = x_prompt.shape
    n_lat, seq_lat, _ = x_sample.shape
    depth = w_ada.shape[0]
    assert depth == 1 and n_lat + 1 <= 8
    l = 0

    xp = x_prompt.reshape(n_ctx * seq_ctx, D_MODEL)
    xs = x_sample.reshape(n_lat * seq_lat, D_MODEL)
    t_ctx, t_lat = xp.shape[0], xs.shape[0]
    cond8 = jnp.concatenate([c_ctx[None], c, jnp.zeros((7 - n_lat, D_MODEL), F32)], axis=0)
    mod3 = _adaln(cond8, w_ada[l], b_ada[l]).reshape(8, N_MOD, D_MODEL)

    pscale = pool_scale[l].reshape(1, D_POOL)
    qg2 = jnp.tile(q_gain[l], 2).reshape(1, LANES)
    kg2 = jnp.tile(k_gain[l], 2).reshape(1, LANES)
    g1, gm, g2 = g_ff1[l].reshape(1, -1), g_mix[l].reshape(1, -1), g_ff2[l].reshape(1, -1)
    rpb_pad = jnp.pad(rpb[l], ((0, 0), (0, 0), (0, LANES - rpb.shape[-1])))
    ck3 = cache_k[:, l].reshape(n_lat, -1, D_ATT).astype(BF16)
    cv3 = cache_v[:, l].reshape(n_lat, -1, D_ATT).astype(BF16)

    x1, w2gu, w2dn, win, wbrp, wbra, wout = _ffn(
        xp, xs, mod3, seq_lat, 0, g1, w_ff1_in[l], w_ff1_out[l], "ffn1", convert=True,
        preconvert=(w_ff2_in[l], w_ff2_out[l], w_in[l], w_br_pool[l], w_br_att[l], w_out[l]))
    x2p, k_p, v_p = _ctx_mixer(x1, t_ctx, mod3, gm, win, qg2, kg2, w_pool[l], pscale, wbrp, wbra, wout, seq_ctx)
    h_s, p_s, q_s, k_s, v_s = _lat_proj(x1, t_ctx, t_lat, seq_lat, mod3, gm, win, qg2, kg2)
    x2s = _lat_mixer(x1, t_ctx, n_lat, seq_lat, mod3, h_s, q_s, p_s,
                     k_s.reshape(n_lat, seq_lat, D_ATT), v_s.reshape(n_lat, seq_lat, D_ATT), ck3, cv3,
                     rpb_pad, win, w_pool[l], pscale, wbrp, wbra, wout)
    yp, ys = _ffn(x2p, x2s, mod3, seq_lat, 6, g2, w2gu, w2dn, "ffn2", split_out=True)

    new_k = k_p.reshape(n_ctx, 1, seq_ctx, N_HEADS, HEAD_DIM)
    new_v = v_p.reshape(n_ctx, 1, seq_ctx, N_HEADS, HEAD_DIM)
    return (yp.reshape(x_prompt.shape), ys.reshape(x_sample.shape), new_k, new_v)
```

```python
import functools
import math

import jax
import jax.numpy as jnp
from jax import lax
from jax.experimental import pallas as pl
from jax.experimental.pallas import tpu as pltpu

F32 = jnp.float32
BF16 = jnp.bfloat16

D_MODEL = 1024
N_HEADS = 8
HEAD_DIM = 64
D_ATT = N_HEADS * HEAD_DIM
D_POOL = 512
POOL_WINDOWS = (2, 4, 8, 16)
POOL_HALO = 8
D_FF = 2816
N_MOD = 9
GRID_W = 64
NA_KH = 8
NA_KW = 16
EPS = 1e-6
NEG = -1e30

LANES = 128
BF16_ROWS = 16
N_PAIRS = D_ATT // LANES
MXU_COLS = 256
FF_CHUNKS = ((0, 768), (768, 768), (1536, 768), (2304, 512))
ADALN_TN = 2304
FFN_TM = 512
CTX_TB = 2
PROJ_TM = 1024
LAT_ROWS = 4
LAT_TM = LAT_ROWS * GRID_W
LAT_TILES = 2
LAT_KEY_ROWS = 12
SCORE_LOOKAHEAD = 2
N_CONV = 8
PRE_CHUNKS = 16
VMEM_LIMIT = 56 * 1024 * 1024

Q_SCALE = 1.0 / math.sqrt(HEAD_DIM)

O_Q = D_POOL
O_K = D_POOL + D_ATT
O_V = D_POOL + 2 * D_ATT
O_G = D_POOL + 3 * D_ATT


def _dot(a, b):
    return jnp.dot(a, b, preferred_element_type=F32)


def _dot_nt(a, b):
    return lax.dot_general(a, b, (((1,), (1,)), ((), ())), preferred_element_type=F32)


def _rms_mod(x, g, shift, scale):
    ms = jnp.mean(x * x, axis=-1, keepdims=True)
    y = x * lax.rsqrt(ms + EPS) * g
    return y * (1.0 + scale) + shift


def _low_half():
    return lax.broadcasted_iota(jnp.int32, (1, LANES), 1) < HEAD_DIM


def _head_norm(t, gain2):
    lo = _low_half()
    sq = t * t
    s0 = jnp.sum(jnp.where(lo, sq, 0.0), axis=-1, keepdims=True)
    s1 = jnp.sum(jnp.where(lo, 0.0, sq), axis=-1, keepdims=True)
    r0 = lax.rsqrt(s0 * (1.0 / HEAD_DIM) + EPS)
    r1 = lax.rsqrt(s1 * (1.0 / HEAD_DIM) + EPS)
    return t * jnp.where(lo, r0, r1) * gain2


def _own_lanes(head, mine, other):
    lo = _low_half()
    return jnp.where(lo, mine, other) if head % 2 == 0 else jnp.where(lo, other, mine)


def _pipelined_heads(scores, weights, weighted_values, fillers):
    lo = _low_half()
    outs = []
    pending_scores = [scores(head) for head in range(SCORE_LOOKAHEAD)]
    pending_weights = [weights(0, *pending_scores.pop(0))]
    for head in range(N_HEADS):
        if head + SCORE_LOOKAHEAD < N_HEADS:
            pending_scores.append(scores(head + SCORE_LOOKAHEAD))
        fillers[head]()
        if head + 1 < N_HEADS:
            pending_weights.append(weights(head + 1, *pending_scores.pop(0)))
        outs.append(weighted_values(head, pending_weights.pop(0)))
    att_pairs = []
    for pr in range(N_PAIRS):
        even, odd = outs[2 * pr], outs[2 * pr + 1]
        num = jnp.where(lo, even, odd)
        den = pltpu.roll(jnp.where(lo, odd, even), HEAD_DIM, 1)
        att_pairs.append(num / den)
    return jnp.concatenate(att_pairs, axis=1)


def _pool_delta(p_ext, g, pos0, seq_len):
    w = POOL_WINDOWS[g]
    te = p_ext.shape[0]
    tm = te - 2 * POOL_HALO
    t = (pos0 + lax.broadcasted_iota(jnp.int32, (tm, 1), 0)).astype(F32)
    s = p_ext
    step = 1
    while step < w:
        s = s + pltpu.roll(s, step, 0)
        step *= 2
    ahead = w // 2 - 1
    if ahead:
        s = pltpu.roll(s, te - ahead, 0)
    win = s[POOL_HALO:POOL_HALO + tm]
    cnt = jnp.minimum(t + float(w // 2), float(seq_len)) - jnp.maximum(t - float(w // 2), 0.0)
    return win / cnt - p_ext[POOL_HALO:POOL_HALO + tm]


def _const_spec(shape):
    nd = len(shape)
    return pl.BlockSpec(shape, lambda *_: (0,) * nd)


def _resident_spec(w, col_block=None):
    width, cidx = col_block if col_block else (w.shape[1], 0)
    return pl.BlockSpec((w.shape[0], width), lambda *_: (0, cidx), pipeline_mode=pl.Buffered(1))


def _params():
    return pltpu.CompilerParams(dimension_semantics=("arbitrary",), vmem_limit_bytes=VMEM_LIMIT)


def _build_pool_weight(wpool32_ref, wpool_ref):
    wpool_ref[...] = jnp.zeros(wpool_ref.shape, BF16)
    for g in range(len(POOL_WINDOWS)):
        blk = slice(g * LANES, (g + 1) * LANES)
        wpool_ref[blk, blk] = wpool32_ref[g].astype(BF16)


def _adaln_kernel(cond_ref, wlo_ref, whi_ref, b_ref, o_ref):
    c = cond_ref[...]
    s = (c * jax.nn.sigmoid(c)).astype(BF16)
    half = wlo_ref.shape[1]
    o_ref[:, :half] = _dot(s, wlo_ref[...].astype(BF16)) + b_ref[:, :half]
    o_ref[:, half:] = _dot(s, whi_ref[...].astype(BF16)) + b_ref[:, half:]


def _adaln(cond8, w_ada, b_ada):
    n = w_ada.shape[1]
    tn = ADALN_TN
    return pl.pallas_call(
        _adaln_kernel,
        grid=(n // tn,),
        in_specs=[pl.BlockSpec((8, D_MODEL), lambda i: (0, 0)),
                  pl.BlockSpec((D_MODEL, tn // 2), lambda i: (0, 2 * i)),
                  pl.BlockSpec((D_MODEL, tn // 2), lambda i: (0, 2 * i + 1)),
                  pl.BlockSpec((1, tn), lambda i: (0, i))],
        out_specs=pl.BlockSpec((8, tn), lambda i: (0, i)),
        out_shape=jax.ShapeDtypeStruct((8, n), F32),
        compiler_params=_params(),
        name="adaln",
    )(cond8, w_ada, w_ada, b_ada.reshape(1, n))


def _ffn_kernel(*refs, mod_off, n_a, split_out, n_conv, n_pre):
    refs = list(refs)
    xa_ref, xb_ref, mod_ref, g_ref, wgu_in, wdn_in = refs[:6]
    pre_src = refs[6:6 + n_pre]
    n_out = 2 if split_out else 1
    out_refs = refs[6 + n_pre:6 + n_pre + n_out]
    pre_dst = refs[6 + n_pre + n_out:6 + 2 * n_pre + n_out]
    wgu_ref, wdn_ref = refs[6 + 2 * n_pre + n_out:] if n_conv else (wgu_in, wdn_in)
    step = pl.program_id(0)

    if n_conv:
        @pl.when(step < n_conv)
        def _convert():
            for src, dst in ((wgu_in, wgu_ref), (wdn_in, wdn_ref)):
                rows = src.shape[0]
                dst[pl.ds(pl.multiple_of(step * rows, rows), rows), :] = src[...].astype(BF16)

    @pl.when(step >= n_conv)
    def _compute():
        tile = step - n_conv
        is_a = tile < n_a
        shift = mod_ref[0, mod_off:mod_off + 1, :]
        scale = mod_ref[0, mod_off + 1:mod_off + 2, :]
        gate = mod_ref[0, mod_off + 2:mod_off + 3, :]
        x = jnp.where(is_a, xa_ref[...], xb_ref[...])
        h = _rms_mod(x, g_ref[...], shift, scale).astype(BF16)
        acc = None
        for c0, cw in FF_CHUNKS:
            a = _dot(h, wgu_ref[:, c0:c0 + cw])
            u = _dot(h, wgu_ref[:, D_FF + c0:D_FF + c0 + cw])
            t = (a * jax.nn.sigmoid(a) * u).astype(BF16)
            part = _dot(t, wdn_ref[c0:c0 + cw, :])
            acc = part if acc is None else acc + part
        y = x + (0.5 * gate) * acc
        if split_out:
            @pl.when(is_a)
            def _store_a():
                out_refs[0][...] = y

            @pl.when(jnp.logical_not(is_a))
            def _store_b():
                out_refs[1][...] = y
        else:
            out_refs[0][...] = y

        if n_pre:
            @pl.when(tile < PRE_CHUNKS)
            def _preconvert():
                for src, dst in zip(pre_src, pre_dst):
                    dst[...] = src[...].astype(BF16)


def _ffn(xa, xb, mod3, seq_b, mod_off, g, wgu, wdn, name, split_out=False, convert=False, preconvert=()):
    tm = FFN_TM
    n_a, n_b = xa.shape[0] // tm, xb.shape[0] // tm
    n_conv = N_CONV if convert else 0
    assert n_a + n_b >= PRE_CHUNKS
    tile = lambda s: jnp.maximum(s - n_conv, 0)
    a_map = lambda s: (jnp.minimum(tile(s), n_a - 1), 0)
    b_map = lambda s: (jnp.maximum(tile(s) - n_a, 0), 0)
    mod_map = lambda s: (jnp.where(tile(s) < n_a, 0, 1 + (jnp.maximum(tile(s) - n_a, 0) * tm) // seq_b), 0, 0)

    def chunk_spec(w, n_chunks, index):
        rows = w.shape[0] // n_chunks
        assert rows * n_chunks == w.shape[0] and rows % BF16_ROWS == 0
        return pl.BlockSpec((rows, w.shape[1]), lambda s: (jnp.minimum(index(s), n_chunks - 1), 0))

    if convert:
        w_specs = [chunk_spec(wgu, N_CONV, lambda s: s), chunk_spec(wdn, N_CONV, lambda s: s)]
        scratch = [pltpu.VMEM(wgu.shape, BF16), pltpu.VMEM(wdn.shape, BF16)]
    else:
        w_specs = [_resident_spec(wgu), _resident_spec(wdn)]
        scratch = []
    pre_specs = lambda: [chunk_spec(w, PRE_CHUNKS, tile) for w in preconvert]
    if split_out:
        out_specs = [pl.BlockSpec((tm, D_MODEL), a_map), pl.BlockSpec((tm, D_MODEL), b_map)]
        out_shape = [jax.ShapeDtypeStruct(xa.shape, F32), jax.ShapeDtypeStruct(xb.shape, F32)]
    else:
        out_specs = [pl.BlockSpec((tm, D_MODEL), lambda s: (tile(s), 0))]
        out_shape = [jax.ShapeDtypeStruct((xa.shape[0] + xb.shape[0], D_MODEL), F32)]
    return pl.pallas_call(
        functools.partial(_ffn_kernel, mod_off=mod_off, n_a=n_a, split_out=split_out, n_conv=n_conv,
                          n_pre=len(preconvert)),
        grid=(n_conv + n_a + n_b,),
        in_specs=[pl.BlockSpec((tm, D_MODEL), a_map), pl.BlockSpec((tm, D_MODEL), b_map),
                  pl.BlockSpec((1, N_MOD, D_MODEL), mod_map), _const_spec((1, D_MODEL))] + w_specs + pre_specs(),
        out_specs=out_specs + pre_specs(),
        out_shape=out_shape + [jax.ShapeDtypeStruct(w.shape, BF16) for w in preconvert],
        scratch_shapes=scratch,
        compiler_params=_params(),
        name=name,
    )(xa, xb, mod3, g, wgu, wdn, *preconvert)


def _heads_to_sublanes(t):
    heads = jnp.stack([t[:, h * HEAD_DIM:(h + 1) * HEAD_DIM] for h in range(N_HEADS)], axis=0)
    return jnp.swapaxes(heads, 0, 1)


def _gate_fillers(gates, h_rows, gate_cols):
    assert 2 * D_MODEL // MXU_COLS == N_HEADS

    def gate_block():
        c0 = len(gates) * MXU_COLS
        gates.append(jax.nn.sigmoid(_dot(h_rows(), gate_cols(c0, c0 + MXU_COLS))))

    return [gate_block] * N_HEADS


def _pool_fillers(st, delta_fn, wpool_ref, pscale_ref, wbrp_ref):
    def pool_delta():
        deltas = [delta_fn(g).astype(BF16) for g in range(len(POOL_WINDOWS))]
        st["d"] = jnp.concatenate(deltas, axis=1)

    def pool_weight():
        st["y"] = (_dot(st.pop("d"), wpool_ref[...]) * pscale_ref[...]).astype(BF16)

    def pool_out():
        st["a"] = _dot(st.pop("y"), wbrp_ref[...])

    return [pool_delta, pool_weight, pool_out]


def _merge_fillers(st, gates, rows, att, x_ref, o_ref, gate2, wbra_ref, wout_ref):
    def attention_branch():
        st["b"] = _dot(att.astype(BF16), wbra_ref[...])

    def gate():
        g = jnp.concatenate(gates, axis=1)
        merged = g[:, :D_MODEL] * st["a"][rows] + g[:, D_MODEL:] * st.pop("b")
        st["merged"] = merged.astype(BF16)

    def out():
        o_ref[rows, :] = x_ref[rows, :] + gate2 * _dot(st.pop("merged"), wout_ref[...])

    return [attention_branch, gate, out]


def _attend_tiles(n_tiles, tile_rows, attention, gate_fillers, pool_fillers, merge_fillers):
    idle = lambda: None
    extra = pool_fillers + [idle] * (N_HEADS - len(pool_fillers))
    for t in range(n_tiles):
        rows = slice(t * tile_rows, (t + 1) * tile_rows)
        gates = []
        fillers = [lambda g=g, e=e: (g(), e()) for g, e in zip(gate_fillers(rows, gates), extra)]
        pieces = merge_fillers(gates, rows, attention(t, fillers))
        extra = [idle] + pieces + [idle] * (N_HEADS - 1 - len(pieces))
    for piece in pieces:
        piece()


def _ctx_mixer_kernel(x_ref, mod_ref, g_ref, win_ref, qg_ref, kg_ref, wpool32_ref, pscale_ref,
                      wbrp_ref, wbra_ref, wout_ref, o_ref, ko_ref, vo_ref, wpool_ref, *, seq):
    @pl.when(pl.program_id(0) == 0)
    def _first_step():
        _build_pool_weight(wpool32_ref, wpool_ref)

    m_rows = x_ref.shape[0]
    h = _rms_mod(x_ref[...], g_ref[...], mod_ref[0, 3:4, :], mod_ref[0, 4:5, :]).astype(BF16)
    p = _dot(h, win_ref[:, 0:O_Q])
    q = _dot(h, win_ref[:, O_Q:O_K])
    k = _dot(h, win_ref[:, O_K:O_V])
    v = _dot(h, win_ref[:, O_V:O_G])
    vo_ref[...] = _heads_to_sublanes(v)
    scale = Q_SCALE
    zero = jnp.zeros((), BF16)
    one = jnp.ones((), BF16)
    st = {}

    def pool_delta(g):
        lanes = slice(g * LANES, (g + 1) * LANES)
        zeros_halo = jnp.zeros((POOL_HALO, LANES), F32)
        return jnp.concatenate(
            [_pool_delta(jnp.concatenate([zeros_halo, p[e * seq:(e + 1) * seq, lanes], zeros_halo], axis=0),
                         g, 0, seq) for e in range(m_rows // seq)], axis=0)

    def gate_fillers(rows, gates):
        return _gate_fillers(gates, lambda: h[rows], lambda c0, c1: win_ref[:, O_G + c0:O_G + c1])

    def attention(e, fillers):
        rows = slice(e * seq, (e + 1) * seq)
        qn, kn, vb = [], [], []
        for pr in range(N_PAIRS):
            lanes = slice(pr * LANES, (pr + 1) * LANES)
            kn.append(_head_norm(k[rows, lanes], kg_ref[...]))
            qn.append((_head_norm(q[rows, lanes], qg_ref[...]) * scale).astype(BF16))
            vb.append(v[rows, lanes].astype(BF16))
        ko_ref[rows] = _heads_to_sublanes(jnp.concatenate(kn, axis=1))
        kb = [t.astype(BF16) for t in kn]

        def scores(head):
            return (_dot_nt(_own_lanes(head, qn[head // 2], zero), kb[head // 2]),)

        def weights(head, s):
            return jnp.exp(s - s.max(axis=-1, keepdims=True)).astype(BF16)

        def weighted_values(head, e_):
            return _dot(e_, _own_lanes(head, vb[head // 2], one))

        return _pipelined_heads(scores, weights, weighted_values, fillers)

    merge = functools.partial(_merge_fillers, st, x_ref=x_ref, o_ref=o_ref, gate2=mod_ref[0, 5:6, :],
                              wbra_ref=wbra_ref, wout_ref=wout_ref)
    _attend_tiles(m_rows // seq, seq, attention, gate_fillers,
                  _pool_fillers(st, pool_delta, wpool_ref, pscale_ref, wbrp_ref), merge)


def _ctx_mixer(x_all, n_tok, mod3, g, w_in, qg2, kg2, w_pool, pscale, wbrp, wbra, wout, seq):
    tm = CTX_TB * seq
    row = lambda s: (s, 0)
    heads = lambda s: (s, 0, 0)
    return pl.pallas_call(
        functools.partial(_ctx_mixer_kernel, seq=seq),
        grid=(n_tok // tm,),
        in_specs=[pl.BlockSpec((tm, D_MODEL), row),
                  _const_spec((1, N_MOD, D_MODEL)), _const_spec((1, D_MODEL)),
                  _resident_spec(w_in), _const_spec((1, LANES)), _const_spec((1, LANES)),
                  _const_spec(w_pool.shape), _const_spec((1, D_POOL)),
                  _resident_spec(wbrp), _resident_spec(wbra), _resident_spec(wout)],
        out_specs=[pl.BlockSpec((tm, D_MODEL), row),
                   pl.BlockSpec((tm, N_HEADS, HEAD_DIM), heads),
                   pl.BlockSpec((tm, N_HEADS, HEAD_DIM), heads)],
        out_shape=[jax.ShapeDtypeStruct((n_tok, D_MODEL), F32),
                   jax.ShapeDtypeStruct((n_tok, N_HEADS, HEAD_DIM), F32),
                   jax.ShapeDtypeStruct((n_tok, N_HEADS, HEAD_DIM), F32)],
        scratch_shapes=[pltpu.VMEM((D_POOL, D_POOL), BF16)],
        compiler_params=_params(),
        name="ctx_mixer",
    )(x_all, mod3, g, w_in, qg2, kg2, w_pool, pscale, wbrp, wbra, wout)


def _lat_proj_kernel(x_ref, mod_ref, g_ref, win_ref, qg_ref, kg_ref, h_ref, p_ref, q_ref, k_ref, v_ref):
    h = _rms_mod(x_ref[...], g_ref[...], mod_ref[0, 3:4, :], mod_ref[0, 4:5, :]).astype(BF16)
    h_ref[...] = h
    p_ref[...] = _dot(h, win_ref[:, 0:O_Q])
    q = _dot(h, win_ref[:, O_Q:O_K])
    k = _dot(h, win_ref[:, O_K:O_V])
    v_ref[...] = _dot(h, win_ref[:, O_V:O_G]).astype(BF16)
    scale = Q_SCALE
    for pr in range(N_PAIRS):
        lanes = slice(pr * LANES, (pr + 1) * LANES)
        q_ref[:, lanes] = (_head_norm(q[:, lanes], qg_ref[...]) * scale).astype(BF16)
        k_ref[:, lanes] = _head_norm(k[:, lanes], kg_ref[...]).astype(BF16)


def _lat_proj(x_all, tok0, n_tok, seq, mod3, g, w_in, qg2, kg2):
    tm = PROJ_TM
    tile0 = tok0 // tm
    row = lambda s: (s, 0)
    shapes = [(D_MODEL, BF16), (D_POOL, F32), (D_ATT, BF16), (D_ATT, BF16), (D_ATT, BF16)]
    return pl.pallas_call(
        _lat_proj_kernel,
        grid=(n_tok // tm,),
        in_specs=[pl.BlockSpec((tm, D_MODEL), lambda s: (tile0 + s, 0)),
                  pl.BlockSpec((1, N_MOD, D_MODEL), lambda s: (1 + (s * tm) // seq, 0, 0)),
                  _const_spec((1, D_MODEL)), _resident_spec(w_in, (O_G, 0)),
                  _const_spec((1, LANES)), _const_spec((1, LANES))],
        out_specs=[pl.BlockSpec((tm, width), row) for width, _ in shapes],
        out_shape=[jax.ShapeDtypeStruct((n_tok, width), dtype) for width, dtype in shapes],
        compiler_params=_params(),
        name="lat_proj",
    )(x_all, mod3, g, w_in, qg2, kg2)


def _build_bias_table(rpb_ref, bias_ref):
    lane = lax.broadcasted_iota(jnp.int32, (GRID_W, LANES), 1)
    qc = lax.broadcasted_iota(jnp.int32, (GRID_W, LANES), 0)
    kc = lane & (GRID_W - 1)
    c0 = jnp.clip(qc - NA_KW // 2, 0, GRID_W - NA_KW)
    col_ok = (kc >= c0) & (kc < c0 + NA_KW)
    first_half = lane < GRID_W
    shift_a = LANES - (NA_KW - 1)
    shift_b = (GRID_W - (NA_KW - 1)) % LANES
    for h in range(N_HEADS):
        for i in range(2 * NA_KH - 2):
            va = jnp.broadcast_to(rpb_ref[h, i:i + 1, :], (GRID_W, LANES))
            vb = jnp.broadcast_to(rpb_ref[h, i + 1:i + 2, :], (GRID_W, LANES))
            ra = pltpu.roll(va, shift_a, 1, stride=1, stride_axis=0)
            rb = pltpu.roll(vb, shift_b, 1, stride=1, stride_axis=0)
            bias_ref[h, i] = jnp.where(col_ok, jnp.where(first_half, ra, rb), NEG)


def _lat_attention(qn, j, fillers, k_ref, v_ref, ck_ref, cv_ref, bias_ref, *, n_rows):
    n_keys = LAT_KEY_ROWS * GRID_W
    r0q = j * LAT_ROWS
    u0 = jnp.clip(r0q - NA_KH // 2, 0, n_rows - LAT_KEY_ROWS)
    key0 = pl.multiple_of(u0 * GRID_W, GRID_W)
    log2_w = GRID_W.bit_length() - 1
    zero = jnp.zeros((), BF16)
    one = jnp.ones((), BF16)

    lane_row = lax.broadcasted_iota(jnp.int32, (1, LANES), 1) & (HEAD_DIM - 1)
    q_tile_row = jnp.right_shift(lax.broadcasted_iota(jnp.int32, (LAT_TM, 1), 0), log2_w)
    q_hot = jnp.where(lane_row == q_tile_row, 1.0, 0.0).astype(BF16)
    k_row = u0 + jnp.right_shift(lax.broadcasted_iota(jnp.int32, (n_keys, 1), 0), log2_w)
    first = jnp.clip(r0q + lane_row - NA_KH // 2, 0, n_rows - NA_KH)
    outside = (lane_row < LAT_ROWS) & ((k_row < first) | (k_row >= first + NA_KH))
    k_mask = jnp.where(outside, NEG, 0.0).astype(BF16)

    def scores(head):
        lanes = slice((head // 2) * LANES, (head // 2 + 1) * LANES)
        qp = qn[:, lanes]
        s_loc = _dot_nt(_own_lanes(head, qp, q_hot),
                        _own_lanes(head, k_ref[0, pl.ds(key0, n_keys), lanes], k_mask))
        return s_loc, _dot_nt(_own_lanes(head, qp, zero), ck_ref[0, :, lanes])

    def weights(head, s_loc, s_ctx):
        bias_rows = []
        for jr in range(LAT_ROWS):
            blocks = []
            for pi in range(LAT_KEY_ROWS // 2):
                idx = jnp.clip(u0 - r0q + 2 * pi - jr + NA_KH - 1, 0, 2 * NA_KH - 3)
                blocks.append(bias_ref[head, idx])
            bias_rows.append(jnp.concatenate(blocks, axis=1))
        s_loc = s_loc + jnp.concatenate(bias_rows, axis=0)
        m = jnp.maximum(s_loc.max(axis=-1, keepdims=True), s_ctx.max(axis=-1, keepdims=True))
        return jnp.concatenate([jnp.exp(s_loc - m).astype(BF16), jnp.exp(s_ctx - m).astype(BF16)], axis=1)

    def weighted_values(head, e):
        lanes = slice((head // 2) * LANES, (head // 2 + 1) * LANES)
        values = jnp.concatenate([_own_lanes(head, v_ref[0, pl.ds(key0, n_keys), lanes], one),
                                  _own_lanes(head, cv_ref[0, :, lanes], one)], axis=0)
        return _dot(e, values)

    return _pipelined_heads(scores, weights, weighted_values, fillers)


def _lat_mixer_kernel(x_ref, h_ref, q_ref, p_ref, pp_ref, pn_ref, mod_ref, k_ref, v_ref, ck_ref, cv_ref,
                      rpb_ref, wg_ref, wpool32_ref, pscale_ref, wbrp_ref, wbra_ref, wout_ref,
                      o_ref, bias_ref, wpool_ref, *, seq, n_rows):
    step = pl.program_id(0)
    tm = LAT_TILES * LAT_TM

    @pl.when(step == 0)
    def _first_step():
        _build_pool_weight(wpool32_ref, wpool_ref)
        _build_bias_table(rpb_ref, bias_ref)

    blk = lax.rem(step, seq // tm)
    st = {}

    def pool_delta(g):
        lanes = slice(g * LANES, (g + 1) * LANES)
        pe = jnp.concatenate([pp_ref[:, lanes], p_ref[:, lanes], pn_ref[:, lanes]], axis=0)
        pos = blk * tm - POOL_HALO + lax.broadcasted_iota(jnp.int32, (tm + 2 * POOL_HALO, 1), 0)
        return _pool_delta(jnp.where((pos >= 0) & (pos < seq), pe, 0.0), g, blk * tm, seq)

    block_gates = []

    def gate_fillers(rows, gates):
        if not block_gates:
            fillers = _gate_fillers(block_gates, lambda: h_ref[...], lambda c0, c1: wg_ref[:, c0:c1])
            return [lambda f=f: (f(), gates.append(block_gates[-1][rows])) for f in fillers]
        gates.extend(g[rows] for g in block_gates)
        return [lambda: None] * N_HEADS

    pool_fillers = _pool_fillers(st, pool_delta, wpool_ref, pscale_ref, wbrp_ref)

    def attention(t, fillers):
        return _lat_attention(q_ref[t * LAT_TM:(t + 1) * LAT_TM, :], blk * LAT_TILES + t, fillers,
                              k_ref, v_ref, ck_ref, cv_ref, bias_ref, n_rows=n_rows)

    merge = functools.partial(_merge_fillers, st, x_ref=x_ref, o_ref=o_ref, gate2=mod_ref[0, 5:6, :],
                              wbra_ref=wbra_ref, wout_ref=wout_ref)
    _attend_tiles(LAT_TILES, LAT_TM, attention, gate_fillers, pool_fillers, merge)


def _lat_mixer(x_all, tok0, n_batch, seq, mod3, h, q, p, k3, v3, ck3, cv3, rpb_pad, w_in, w_pool, pscale,
               wbrp, wbra, wout):
    tm = LAT_TILES * LAT_TM
    n_rows = seq // GRID_W
    nb = seq // tm
    tile0 = tok0 // tm
    halo_per_tile = tm // POOL_HALO
    n_halo = p.shape[0] // POOL_HALO
    row = lambda s: (s, 0)
    seq_map = lambda s: (s // nb, 0, 0)
    return pl.pallas_call(
        functools.partial(_lat_mixer_kernel, seq=seq, n_rows=n_rows),
        grid=(n_batch * nb,),
        in_specs=[pl.BlockSpec((tm, D_MODEL), lambda s: (tile0 + s, 0)),
                  pl.BlockSpec((tm, D_MODEL), row),
                  pl.BlockSpec((tm, D_ATT), row),
                  pl.BlockSpec((tm, D_POOL), row),
                  pl.BlockSpec((POOL_HALO, D_POOL), lambda s: (jnp.maximum(s * halo_per_tile - 1, 0), 0)),
                  pl.BlockSpec((POOL_HALO, D_POOL),
                               lambda s: (jnp.minimum((s + 1) * halo_per_tile, n_halo - 1), 0)),
                  pl.BlockSpec((1, N_MOD, D_MODEL), lambda s: (1 + s // nb, 0, 0)),
                  pl.BlockSpec((1, seq, D_ATT), seq_map), pl.BlockSpec((1, seq, D_ATT), seq_map),
                  pl.BlockSpec((1,) + ck3.shape[1:], seq_map), pl.BlockSpec((1,) + cv3.shape[1:], seq_map),
                  _const_spec(rpb_pad.shape),
                  _resident_spec(w_in, (2 * D_MODEL, O_G // (2 * D_MODEL))),
                  _const_spec(w_pool.shape), _const_spec((1, D_POOL)),
                  _resident_spec(wbrp), _resident_spec(wbra), _resident_spec(wout)],
        out_specs=pl.BlockSpec((tm, D_MODEL), row),
        out_shape=jax.ShapeDtypeStruct((n_batch * seq, D_MODEL), F32),
        scratch_shapes=[pltpu.VMEM((N_HEADS, 2 * NA_KH - 2, GRID_W, LANES), F32),
                        pltpu.VMEM((D_POOL, D_POOL), BF16)],
        compiler_params=_params(),
        name="lat_mixer",
    )(x_all, h, q, p, p, p, mod3, k3, v3, ck3, cv3, rpb_pad, w_in, w_pool, pscale, wbrp, wbra, wout)


def kernel(x_prompt, x_sample, cache_k, cache_v, c, c_ctx, w_ada, b_ada, g_ff1, w_ff1_in, w_ff1_out,
           g_mix, w_in, q_gain, k_gain, w_pool, pool_scale, rpb, w_br_pool, w_br_att, w_out, g_ff2,
           w_ff2_in, w_ff2_out):
    n_ctx, seq_ctx, _ = x_prompt.shape
    n_lat, seq_lat, _ = x_sample.shape
    depth = w_ada.shape[0]
    assert depth == 1 and n_lat + 1 <= 8
    l = 0

    xp = x_prompt.reshape(n_ctx * seq_ctx, D_MODEL)
    xs = x_sample.reshape(n_lat * seq_lat, D_MODEL)
    t_ctx, t_lat = xp.shape[0], xs.shape[0]
    cond8 = jnp.concatenate([c_ctx[None], c, jnp.zeros((7 - n_lat, D_MODEL), F32)], axis=0)
    mod3 = _adaln(cond8, w_ada[l], b_ada[l]).reshape(8, N_MOD, D_MODEL)

    pscale = pool_scale[l].reshape(1, D_POOL)
    qg2 = jnp.tile(q_gain[l], 2).reshape(1, LANES)
    kg2 = jnp.tile(k_gain[l], 2).reshape(1, LANES)
    g1, gm, g2 = g_ff1[l].reshape(1, -1), g_mix[l].reshape(1, -1), g_ff2[l].reshape(1, -1)
    rpb_pad = jnp.pad(rpb[l], ((0, 0), (0, 0), (0, LANES - rpb.shape[-1])))
    ck3 = cache_k[:, l].reshape(n_lat, -1, D_ATT).astype(BF16)
    cv3 = cache_v[:, l].reshape(n_lat, -1, D_ATT).astype(BF16)

    x1, w2gu, w2dn, win, wbrp, wbra, wout = _ffn(
        xp, xs, mod3, seq_lat, 0, g1, w_ff1_in[l], w_ff1_out[l], "ffn1", convert=True,
        preconvert=(w_ff2_in[l], w_ff2_out[l], w_in[l], w_br_pool[l], w_br_att[l], w_out[l]))
    x2p, k_p, v_p = _ctx_mixer(x1, t_ctx, mod3, gm, win, qg2, kg2, w_pool[l], pscale, wbrp, wbra, wout, seq_ctx)
    h_s, p_s, q_s, k_s, v_s = _lat_proj(x1, t_ctx, t_lat, seq_lat, mod3, gm, win, qg2, kg2)
    x2s = _lat_mixer(x1, t_ctx, n_lat, seq_lat, mod3, h_s, q_s, p_s,
                     k_s.reshape(n_lat, seq_lat, D_ATT), v_s.reshape(n_lat, seq_lat, D_ATT), ck3, cv3,
                     rpb_pad, win, w_pool[l], pscale, wbrp, wbra, wout)
    yp, ys = _ffn(x2p, x2s, mod3, seq_lat, 6, g2, w2gu, w2dn, "ffn2", split_out=True)

    new_k = k_p.reshape(n_ctx, 1, seq_ctx, N_HEADS, HEAD_DIM)
    new_v = v_p.reshape(n_ctx, 1, seq_ctx, N_HEADS, HEAD_DIM)
    return (yp.reshape(x_prompt.shape), ys.reshape(x_sample.shape), new_k, new_v)
```

```python
import functools
import math

import jax
import jax.numpy as jnp
from jax import lax
from jax.experimental import pallas as pl
from jax.experimental.pallas import tpu as pltpu

F32 = jnp.float32
BF16 = jnp.bfloat16

D_MODEL = 1024
N_HEADS = 8
HEAD_DIM = 64
D_ATT = N_HEADS * HEAD_DIM
D_POOL = 512
POOL_WINDOWS = (2, 4, 8, 16)
POOL_HALO = 8
D_FF = 2816
N_MOD = 9
GRID_W = 64
NA_KH = 8
NA_KW = 16
EPS = 1e-6
NEG = -1e30

LANES = 128
BF16_ROWS = 16
N_PAIRS = D_ATT // LANES
MXU_COLS = 256
FF_CHUNKS = ((0, 768), (768, 768), (1536, 768), (2304, 512))
ADALN_TN = 2304
FFN_TM = 512
CTX_TB = 2
PROJ_TM = 1024
LAT_ROWS = 4
LAT_TM = LAT_ROWS * GRID_W
LAT_TILES = 2
LAT_KEY_ROWS = 12
SCORE_LOOKAHEAD = 2
N_CONV = 8
PRE_CHUNKS = 16
VMEM_LIMIT = 56 * 1024 * 1024

Q_SCALE = 1.0 / math.sqrt(HEAD_DIM)

O_Q = D_POOL
O_K = D_POOL + D_ATT
O_V = D_POOL + 2 * D_ATT
O_G = D_POOL + 3 * D_ATT


def _dot(a, b):
    return jnp.dot(a, b, preferred_element_type=F32)


def _dot_nt(a, b):
    return lax.dot_general(a, b, (((1,), (1,)), ((), ())), preferred_element_type=F32)


def _rms_mod(x, g, shift, scale):
    ms = jnp.mean(x * x, axis=-1, keepdims=True)
    y = x * lax.rsqrt(ms + EPS) * g
    return y * (1.0 + scale) + shift


def _low_half():
    return lax.broadcasted_iota(jnp.int32, (1, LANES), 1) < HEAD_DIM


def _head_norm(t, gain2):
    lo = _low_half()
    sq = t * t
    s0 = jnp.sum(jnp.where(lo, sq, 0.0), axis=-1, keepdims=True)
    s1 = jnp.sum(jnp.where(lo, 0.0, sq), axis=-1, keepdims=True)
    r0 = lax.rsqrt(s0 * (1.0 / HEAD_DIM) + EPS)
    r1 = lax.rsqrt(s1 * (1.0 / HEAD_DIM) + EPS)
    return t * jnp.where(lo, r0, r1) * gain2


def _own_lanes(head, mine, other):
    lo = _low_half()
    return jnp.where(lo, mine, other) if head % 2 == 0 else jnp.where(lo, other, mine)


def _pipelined_heads(scores, weights, weighted_values, fillers):
    lo = _low_half()
    outs = []
    pending_scores = [scores(head) for head in range(SCORE_LOOKAHEAD)]
    pending_weights = [weights(0, *pending_scores.pop(0))]
    for head in range(N_HEADS):
        if head + SCORE_LOOKAHEAD < N_HEADS:
            pending_scores.append(scores(head + SCORE_LOOKAHEAD))
        fillers[head]()
        if head + 1 < N_HEADS:
            pending_weights.append(weights(head + 1, *pending_scores.pop(0)))
        outs.append(weighted_values(head, pending_weights.pop(0)))
    att_pairs = []
    for pr in range(N_PAIRS):
        even, odd = outs[2 * pr], outs[2 * pr + 1]
        num = jnp.where(lo, even, odd)
        den = pltpu.roll(jnp.where(lo, odd, even), HEAD_DIM, 1)
        att_pairs.append(num / den)
    return jnp.concatenate(att_pairs, axis=1)


def _pool_delta(p_ext, g, pos0, seq_len):
    w = POOL_WINDOWS[g]
    te = p_ext.shape[0]
    tm = te - 2 * POOL_HALO
    t = (pos0 + lax.broadcasted_iota(jnp.int32, (tm, 1), 0)).astype(F32)
    s = p_ext
    step = 1
    while step < w:
        s = s + pltpu.roll(s, step, 0)
        step *= 2
    ahead = w // 2 - 1
    if ahead:
        s = pltpu.roll(s, te - ahead, 0)
    win = s[POOL_HALO:POOL_HALO + tm]
    cnt = jnp.minimum(t + float(w // 2), float(seq_len)) - jnp.maximum(t - float(w // 2), 0.0)
    return win / cnt - p_ext[POOL_HALO:POOL_HALO + tm]


def _const_spec(shape):
    nd = len(shape)
    return pl.BlockSpec(shape, lambda *_: (0,) * nd)


def _resident_spec(w, col_block=None):
    width, cidx = col_block if col_block else (w.shape[1], 0)
    return pl.BlockSpec((w.shape[0], width), lambda *_: (0, cidx), pipeline_mode=pl.Buffered(1))


def _params():
    return pltpu.CompilerParams(dimension_semantics=("arbitrary",), vmem_limit_bytes=VMEM_LIMIT)


def _build_pool_weight(wpool32_ref, wpool_ref):
    wpool_ref[...] = jnp.zeros(wpool_ref.shape, BF16)
    for g in range(len(POOL_WINDOWS)):
        blk = slice(g * LANES, (g + 1) * LANES)
        wpool_ref[blk, blk] = wpool32_ref[g].astype(BF16)


def _adaln_kernel(cond_ref, wlo_ref, whi_ref, b_ref, o_ref):
    c = cond_ref[...]
    s = (c * jax.nn.sigmoid(c)).astype(BF16)
    half = wlo_ref.shape[1]
    o_ref[:, :half] = _dot(s, wlo_ref[...].astype(BF16)) + b_ref[:, :half]
    o_ref[:, half:] = _dot(s, whi_ref[...].astype(BF16)) + b_ref[:, half:]


def _adaln(cond8, w_ada, b_ada):
    n = w_ada.shape[1]
    tn = ADALN_TN
    return pl.pallas_call(
        _adaln_kernel,
        grid=(n // tn,),
        in_specs=[pl.BlockSpec((8, D_MODEL), lambda i: (0, 0)),
                  pl.BlockSpec((D_MODEL, tn // 2), lambda i: (0, 2 * i)),
                  pl.BlockSpec((D_MODEL, tn // 2), lambda i: (0, 2 * i + 1)),
                  pl.BlockSpec((1, tn), lambda i: (0, i))],
        out_specs=pl.BlockSpec((8, tn), lambda i: (0, i)),
        out_shape=jax.ShapeDtypeStruct((8, n), F32),
        compiler_params=_params(),
        name="adaln",
    )(cond8, w_ada, w_ada, b_ada.reshape(1, n))


def _ffn_kernel(*refs, mod_off, n_a, split_out, n_conv, n_pre):
    refs = list(refs)
    xa_ref, xb_ref, mod_ref, g_ref, wgu_in, wdn_in = refs[:6]
    pre_src = refs[6:6 + n_pre]
    n_out = 2 if split_out else 1
    out_refs = refs[6 + n_pre:6 + n_pre + n_out]
    pre_dst = refs[6 + n_pre + n_out:6 + 2 * n_pre + n_out]
    wgu_ref, wdn_ref = refs[6 + 2 * n_pre + n_out:] if n_conv else (wgu_in, wdn_in)
    step = pl.program_id(0)

    if n_conv:
        @pl.when(step < n_conv)
        def _convert():
            for src, dst in ((wgu_in, wgu_ref), (wdn_in, wdn_ref)):
                rows = src.shape[0]
                dst[pl.ds(pl.multiple_of(step * rows, rows), rows), :] = src[...].astype(BF16)

    tile = step - n_conv

    def half_step(x_ref, o_ref):
        shift = mod_ref[0, mod_off:mod_off + 1, :]
        scale = mod_ref[0, mod_off + 1:mod_off + 2, :]
        gate = mod_ref[0, mod_off + 2:mod_off + 3, :]
        x = x_ref[...]
        h = _rms_mod(x, g_ref[...], shift, scale).astype(BF16)
        acc = None
        for c0, cw in FF_CHUNKS:
            a = _dot(h, wgu_ref[:, c0:c0 + cw])
            u = _dot(h, wgu_ref[:, D_FF + c0:D_FF + c0 + cw])
            t = (a * jax.nn.sigmoid(a) * u).astype(BF16)
            part = _dot(t, wdn_ref[c0:c0 + cw, :])
            acc = part if acc is None else acc + part
        o_ref[...] = x + (0.5 * gate) * acc

    @pl.when((tile >= 0) & (tile < n_a))
    def _tokens_a():
        half_step(xa_ref, out_refs[0])

    @pl.when(tile >= n_a)
    def _tokens_b():
        half_step(xb_ref, out_refs[-1])

    if n_pre:
        @pl.when((tile >= 0) & (tile < PRE_CHUNKS))
        def _preconvert():
            for src, dst in zip(pre_src, pre_dst):
                dst[...] = src[...].astype(BF16)


def _ffn(xa, xb, mod3, seq_b, mod_off, g, wgu, wdn, name, split_out=False, convert=False, preconvert=()):
    tm = FFN_TM
    n_a, n_b = xa.shape[0] // tm, xb.shape[0] // tm
    n_conv = N_CONV if convert else 0
    assert n_a + n_b >= PRE_CHUNKS
    tile = lambda s: jnp.maximum(s - n_conv, 0)
    a_map = lambda s: (jnp.minimum(tile(s), n_a - 1), 0)
    b_map = lambda s: (jnp.maximum(tile(s) - n_a, 0), 0)
    mod_map = lambda s: (jnp.where(tile(s) < n_a, 0, 1 + (jnp.maximum(tile(s) - n_a, 0) * tm) // seq_b), 0, 0)

    def chunk_spec(w, n_chunks, index):
        rows = w.shape[0] // n_chunks
        assert rows * n_chunks == w.shape[0] and rows % BF16_ROWS == 0
        return pl.BlockSpec((rows, w.shape[1]), lambda s: (jnp.minimum(index(s), n_chunks - 1), 0))

    if convert:
        w_specs = [chunk_spec(wgu, N_CONV, lambda s: s), chunk_spec(wdn, N_CONV, lambda s: s)]
        scratch = [pltpu.VMEM(wgu.shape, BF16), pltpu.VMEM(wdn.shape, BF16)]
    else:
        w_specs = [_resident_spec(wgu), _resident_spec(wdn)]
        scratch = []
    pre_specs = lambda: [chunk_spec(w, PRE_CHUNKS, tile) for w in preconvert]
    if split_out:
        out_specs = [pl.BlockSpec((tm, D_MODEL), a_map), pl.BlockSpec((tm, D_MODEL), b_map)]
        out_shape = [jax.ShapeDtypeStruct(xa.shape, F32), jax.ShapeDtypeStruct(xb.shape, F32)]
    else:
        out_specs = [pl.BlockSpec((tm, D_MODEL), lambda s: (tile(s), 0))]
        out_shape = [jax.ShapeDtypeStruct((xa.shape[0] + xb.shape[0], D_MODEL), F32)]
    return pl.pallas_call(
        functools.partial(_ffn_kernel, mod_off=mod_off, n_a=n_a, split_out=split_out, n_conv=n_conv,
                          n_pre=len(preconvert)),
        grid=(n_conv + n_a + n_b,),
        in_specs=[pl.BlockSpec((tm, D_MODEL), a_map), pl.BlockSpec((tm, D_MODEL), b_map),
                  pl.BlockSpec((1, N_MOD, D_MODEL), mod_map), _const_spec((1, D_MODEL))] + w_specs + pre_specs(),
        out_specs=out_specs + pre_specs(),
        out_shape=out_shape + [jax.ShapeDtypeStruct(w.shape, BF16) for w in preconvert],
        scratch_shapes=scratch,
        compiler_params=_params(),
        name=name,
    )(xa, xb, mod3, g, wgu, wdn, *preconvert)


def _heads_to_sublanes(t):
    heads = jnp.stack([t[:, h * HEAD_DIM:(h + 1) * HEAD_DIM] for h in range(N_HEADS)], axis=0)
    return jnp.swapaxes(heads, 0, 1)


def _gate_fillers(gates, h_rows, gate_cols):
    assert 2 * D_MODEL // MXU_COLS == N_HEADS

    def gate_block():
        c0 = len(gates) * MXU_COLS
        gates.append(jax.nn.sigmoid(_dot(h_rows(), gate_cols(c0, c0 + MXU_COLS))))

    return [gate_block] * N_HEADS


def _pool_fillers(st, delta_fn, wpool_ref, pscale_ref, wbrp_ref):
    def pool_delta():
        deltas = [delta_fn(g).astype(BF16) for g in range(len(POOL_WINDOWS))]
        st["d"] = jnp.concatenate(deltas, axis=1)

    def pool_weight():
        st["y"] = (_dot(st.pop("d"), wpool_ref[...]) * pscale_ref[...]).astype(BF16)

    def pool_out():
        st["a"] = _dot(st.pop("y"), wbrp_ref[...])

    return [pool_delta, pool_weight, pool_out]


def _merge_fillers(st, gates, rows, att, x_ref, o_ref, gate2, wbra_ref, wout_ref):
    def attention_branch():
        st["b"] = _dot(att.astype(BF16), wbra_ref[...])

    def gate():
        g = jnp.concatenate(gates, axis=1)
        merged = g[:, :D_MODEL] * st["a"][rows] + g[:, D_MODEL:] * st.pop("b")
        st["merged"] = merged.astype(BF16)

    def out():
        o_ref[rows, :] = x_ref[rows, :] + gate2 * _dot(st.pop("merged"), wout_ref[...])

    return [attention_branch, gate, out]


def _attend_tiles(n_tiles, tile_rows, attention, gate_fillers, pool_fillers, merge_fillers):
    idle = lambda: None
    extra = pool_fillers + [idle] * (N_HEADS - len(pool_fillers))
    for t in range(n_tiles):
        rows = slice(t * tile_rows, (t + 1) * tile_rows)
        gates = []
        fillers = [lambda g=g, e=e: (g(), e()) for g, e in zip(gate_fillers(rows, gates), extra)]
        pieces = merge_fillers(gates, rows, attention(t, fillers))
        extra = [idle] + pieces + [idle] * (N_HEADS - 1 - len(pieces))
    for piece in pieces:
        piece()


def _ctx_mixer_kernel(x_ref, mod_ref, g_ref, win_ref, qg_ref, kg_ref, wpool32_ref, pscale_ref,
                      wbrp_ref, wbra_ref, wout_ref, o_ref, ko_ref, vo_ref, wpool_ref, *, seq):
    @pl.when(pl.program_id(0) == 0)
    def _first_step():
        _build_pool_weight(wpool32_ref, wpool_ref)

    m_rows = x_ref.shape[0]
    h = _rms_mod(x_ref[...], g_ref[...], mod_ref[0, 3:4, :], mod_ref[0, 4:5, :]).astype(BF16)
    p = _dot(h, win_ref[:, 0:O_Q])
    q = _dot(h, win_ref[:, O_Q:O_K])
    k = _dot(h, win_ref[:, O_K:O_V])
    v = _dot(h, win_ref[:, O_V:O_G])
    vo_ref[...] = _heads_to_sublanes(v)
    scale = Q_SCALE
    zero = jnp.zeros((), BF16)
    one = jnp.ones((), BF16)
    st = {}

    def pool_delta(g):
        lanes = slice(g * LANES, (g + 1) * LANES)
        zeros_halo = jnp.zeros((POOL_HALO, LANES), F32)
        return jnp.concatenate(
            [_pool_delta(jnp.concatenate([zeros_halo, p[e * seq:(e + 1) * seq, lanes], zeros_halo], axis=0),
                         g, 0, seq) for e in range(m_rows // seq)], axis=0)

    def gate_fillers(rows, gates):
        return _gate_fillers(gates, lambda: h[rows], lambda c0, c1: win_ref[:, O_G + c0:O_G + c1])

    def attention(e, fillers):
        rows = slice(e * seq, (e + 1) * seq)
        qn, kn, vb = [], [], []
        for pr in range(N_PAIRS):
            lanes = slice(pr * LANES, (pr + 1) * LANES)
            kn.append(_head_norm(k[rows, lanes], kg_ref[...]))
            qn.append((_head_norm(q[rows, lanes], qg_ref[...]) * scale).astype(BF16))
            vb.append(v[rows, lanes].astype(BF16))
        ko_ref[rows] = _heads_to_sublanes(jnp.concatenate(kn, axis=1))
        kb = [t.astype(BF16) for t in kn]

        def scores(head):
            return (_dot_nt(_own_lanes(head, qn[head // 2], zero), kb[head // 2]),)

        def weights(head, s):
            return jnp.exp(s - s.max(axis=-1, keepdims=True)).astype(BF16)

        def weighted_values(head, e_):
            return _dot(e_, _own_lanes(head, vb[head // 2], one))

        return _pipelined_heads(scores, weights, weighted_values, fillers)

    merge = functools.partial(_merge_fillers, st, x_ref=x_ref, o_ref=o_ref, gate2=mod_ref[0, 5:6, :],
                              wbra_ref=wbra_ref, wout_ref=wout_ref)
    _attend_tiles(m_rows // seq, seq, attention, gate_fillers,
                  _pool_fillers(st, pool_delta, wpool_ref, pscale_ref, wbrp_ref), merge)


def _ctx_mixer(x_all, n_tok, mod3, g, w_in, qg2, kg2, w_pool, pscale, wbrp, wbra, wout, seq):
    tm = CTX_TB * seq
    row = lambda s: (s, 0)
    heads = lambda s: (s, 0, 0)
    return pl.pallas_call(
        functools.partial(_ctx_mixer_kernel, seq=seq),
        grid=(n_tok // tm,),
        in_specs=[pl.BlockSpec((tm, D_MODEL), row),
                  _const_spec((1, N_MOD, D_MODEL)), _const_spec((1, D_MODEL)),
                  _resident_spec(w_in), _const_spec((1, LANES)), _const_spec((1, LANES)),
                  _const_spec(w_pool.shape), _const_spec((1, D_POOL)),
                  _resident_spec(wbrp), _resident_spec(wbra), _resident_spec(wout)],
        out_specs=[pl.BlockSpec((tm, D_MODEL), row),
                   pl.BlockSpec((tm, N_HEADS, HEAD_DIM), heads),
                   pl.BlockSpec((tm, N_HEADS, HEAD_DIM), heads)],
        out_shape=[jax.ShapeDtypeStruct((n_tok, D_MODEL), F32),
                   jax.ShapeDtypeStruct((n_tok, N_HEADS, HEAD_DIM), F32),
                   jax.ShapeDtypeStruct((n_tok, N_HEADS, HEAD_DIM), F32)],
        scratch_shapes=[pltpu.VMEM((D_POOL, D_POOL), BF16)],
        compiler_params=_params(),
        name="ctx_mixer",
    )(x_all, mod3, g, w_in, qg2, kg2, w_pool, pscale, wbrp, wbra, wout)


def _lat_proj_kernel(x_ref, mod_ref, g_ref, win_ref, qg_ref, kg_ref, h_ref, p_ref, q_ref, k_ref, v_ref):
    h = _rms_mod(x_ref[...], g_ref[...], mod_ref[0, 3:4, :], mod_ref[0, 4:5, :]).astype(BF16)
    h_ref[...] = h
    p_ref[...] = _dot(h, win_ref[:, 0:O_Q])
    q = _dot(h, win_ref[:, O_Q:O_K])
    k = _dot(h, win_ref[:, O_K:O_V])
    v_ref[...] = _dot(h, win_ref[:, O_V:O_G]).astype(BF16)
    scale = Q_SCALE
    for pr in range(N_PAIRS):
        lanes = slice(pr * LANES, (pr + 1) * LANES)
        q_ref[:, lanes] = (_head_norm(q[:, lanes], qg_ref[...]) * scale).astype(BF16)
        k_ref[:, lanes] = _head_norm(k[:, lanes], kg_ref[...]).astype(BF16)


def _lat_proj(x_all, tok0, n_tok, seq, mod3, g, w_in, qg2, kg2):
    tm = PROJ_TM
    tile0 = tok0 // tm
    row = lambda s: (s, 0)
    shapes = [(D_MODEL, BF16), (D_POOL, F32), (D_ATT, BF16), (D_ATT, BF16), (D_ATT, BF16)]
    return pl.pallas_call(
        _lat_proj_kernel,
        grid=(n_tok // tm,),
        in_specs=[pl.BlockSpec((tm, D_MODEL), lambda s: (tile0 + s, 0)),
                  pl.BlockSpec((1, N_MOD, D_MODEL), lambda s: (1 + (s * tm) // seq, 0, 0)),
                  _const_spec((1, D_MODEL)), _resident_spec(w_in, (O_G, 0)),
                  _const_spec((1, LANES)), _const_spec((1, LANES))],
        out_specs=[pl.BlockSpec((tm, width), row) for width, _ in shapes],
        out_shape=[jax.ShapeDtypeStruct((n_tok, width), dtype) for width, dtype in shapes],
        compiler_params=_params(),
        name="lat_proj",
    )(x_all, mod3, g, w_in, qg2, kg2)


def _build_bias_table(rpb_ref, bias_ref):
    lane = lax.broadcasted_iota(jnp.int32, (GRID_W, LANES), 1)
    qc = lax.broadcasted_iota(jnp.int32, (GRID_W, LANES), 0)
    kc = lane & (GRID_W - 1)
    c0 = jnp.clip(qc - NA_KW // 2, 0, GRID_W - NA_KW)
    col_ok = (kc >= c0) & (kc < c0 + NA_KW)
    first_half = lane < GRID_W
    shift_a = LANES - (NA_KW - 1)
    shift_b = (GRID_W - (NA_KW - 1)) % LANES
    for h in range(N_HEADS):
        for i in range(2 * NA_KH - 2):
            va = jnp.broadcast_to(rpb_ref[h, i:i + 1, :], (GRID_W, LANES))
            vb = jnp.broadcast_to(rpb_ref[h, i + 1:i + 2, :], (GRID_W, LANES))
            ra = pltpu.roll(va, shift_a, 1, stride=1, stride_axis=0)
            rb = pltpu.roll(vb, shift_b, 1, stride=1, stride_axis=0)
            bias_ref[h, i] = jnp.where(col_ok, jnp.where(first_half, ra, rb), NEG)


def _lat_attention(qn, j, fillers, k_ref, v_ref, ck_ref, cv_ref, bias_ref, *, n_rows):
    n_keys = LAT_KEY_ROWS * GRID_W
    r0q = j * LAT_ROWS
    u0 = jnp.clip(r0q - NA_KH // 2, 0, n_rows - LAT_KEY_ROWS)
    key0 = pl.multiple_of(u0 * GRID_W, GRID_W)
    log2_w = GRID_W.bit_length() - 1
    zero = jnp.zeros((), BF16)
    one = jnp.ones((), BF16)

    lane_row = lax.broadcasted_iota(jnp.int32, (1, LANES), 1) & (HEAD_DIM - 1)
    q_tile_row = jnp.right_shift(lax.broadcasted_iota(jnp.int32, (LAT_TM, 1), 0), log2_w)
    q_hot = jnp.where(lane_row == q_tile_row, 1.0, 0.0).astype(BF16)
    k_row = u0 + jnp.right_shift(lax.broadcasted_iota(jnp.int32, (n_keys, 1), 0), log2_w)
    first = jnp.clip(r0q + lane_row - NA_KH // 2, 0, n_rows - NA_KH)
    outside = (lane_row < LAT_ROWS) & ((k_row < first) | (k_row >= first + NA_KH))
    k_mask = jnp.where(outside, NEG, 0.0).astype(BF16)

    def scores(head):
        lanes = slice((head // 2) * LANES, (head // 2 + 1) * LANES)
        qp = qn[:, lanes]
        s_loc = _dot_nt(_own_lanes(head, qp, q_hot),
                        _own_lanes(head, k_ref[0, pl.ds(key0, n_keys), lanes], k_mask))
        return s_loc, _dot_nt(_own_lanes(head, qp, zero), ck_ref[0, :, lanes])

    def weights(head, s_loc, s_ctx):
        bias_rows = []
        for jr in range(LAT_ROWS):
            blocks = []
            for pi in range(LAT_KEY_ROWS // 2):
                idx = jnp.clip(u0 - r0q + 2 * pi - jr + NA_KH - 1, 0, 2 * NA_KH - 3)
                blocks.append(bias_ref[head, idx])
            bias_rows.append(jnp.concatenate(blocks, axis=1))
        s_loc = s_loc + jnp.concatenate(bias_rows, axis=0)
        m = jnp.maximum(s_loc.max(axis=-1, keepdims=True), s_ctx.max(axis=-1, keepdims=True))
        return jnp.concatenate([jnp.exp(s_loc - m).astype(BF16), jnp.exp(s_ctx - m).astype(BF16)], axis=1)

    def weighted_values(head, e):
        lanes = slice((head // 2) * LANES, (head // 2 + 1) * LANES)
        values = jnp.concatenate([_own_lanes(head, v_ref[0, pl.ds(key0, n_keys), lanes], one),
                                  _own_lanes(head, cv_ref[0, :, lanes], one)], axis=0)
        return _dot(e, values)

    return _pipelined_heads(scores, weights, weighted_values, fillers)


def _lat_mixer_kernel(x_ref, h_ref, q_ref, p_ref, pp_ref, pn_ref, mod_ref, k_ref, v_ref, ck_ref, cv_ref,
                      rpb_ref, wg_ref, wpool32_ref, pscale_ref, wbrp_ref, wbra_ref, wout_ref,
                      o_ref, bias_ref, wpool_ref, *, seq, n_rows):
    step = pl.program_id(0)
    tm = LAT_TILES * LAT_TM

    @pl.when(step == 0)
    def _first_step():
        _build_pool_weight(wpool32_ref, wpool_ref)
        _build_bias_table(rpb_ref, bias_ref)

    blk = lax.rem(step, seq // tm)
    st = {}

    def pool_delta(g):
        lanes = slice(g * LANES, (g + 1) * LANES)
        pe = jnp.concatenate([pp_ref[:, lanes], p_ref[:, lanes], pn_ref[:, lanes]], axis=0)
        pos = blk * tm - POOL_HALO + lax.broadcasted_iota(jnp.int32, (tm + 2 * POOL_HALO, 1), 0)
        return _pool_delta(jnp.where((pos >= 0) & (pos < seq), pe, 0.0), g, blk * tm, seq)

    block_gates = []

    def gate_fillers(rows, gates):
        if not block_gates:
            fillers = _gate_fillers(block_gates, lambda: h_ref[...], lambda c0, c1: wg_ref[:, c0:c1])
            return [lambda f=f: (f(), gates.append(block_gates[-1][rows])) for f in fillers]
        gates.extend(g[rows] for g in block_gates)
        return [lambda: None] * N_HEADS

    pool_fillers = _pool_fillers(st, pool_delta, wpool_ref, pscale_ref, wbrp_ref)

    def attention(t, fillers):
        return _lat_attention(q_ref[t * LAT_TM:(t + 1) * LAT_TM, :], blk * LAT_TILES + t, fillers,
                              k_ref, v_ref, ck_ref, cv_ref, bias_ref, n_rows=n_rows)

    merge = functools.partial(_merge_fillers, st, x_ref=x_ref, o_ref=o_ref, gate2=mod_ref[0, 5:6, :],
                              wbra_ref=wbra_ref, wout_ref=wout_ref)
    _attend_tiles(LAT_TILES, LAT_TM, attention, gate_fillers, pool_fillers, merge)


def _lat_mixer(x_all, tok0, n_batch, seq, mod3, h, q, p, k3, v3, ck3, cv3, rpb_pad, w_in, w_pool, pscale,
               wbrp, wbra, wout):
    tm = LAT_TILES * LAT_TM
    n_rows = seq // GRID_W
    nb = seq // tm
    tile0 = tok0 // tm
    halo_per_tile = tm // POOL_HALO
    n_halo = p.shape[0] // POOL_HALO
    row = lambda s: (s, 0)
    seq_map = lambda s: (s // nb, 0, 0)
    return pl.pallas_call(
        functools.partial(_lat_mixer_kernel, seq=seq, n_rows=n_rows),
        grid=(n_batch * nb,),
        in_specs=[pl.BlockSpec((tm, D_MODEL), lambda s: (tile0 + s, 0)),
                  pl.BlockSpec((tm, D_MODEL), row),
                  pl.BlockSpec((tm, D_ATT), row),
                  pl.BlockSpec((tm, D_POOL), row),
                  pl.BlockSpec((POOL_HALO, D_POOL), lambda s: (jnp.maximum(s * halo_per_tile - 1, 0), 0)),
                  pl.BlockSpec((POOL_HALO, D_POOL),
                               lambda s: (jnp.minimum((s + 1) * halo_per_tile, n_halo - 1), 0)),
                  pl.BlockSpec((1, N_MOD, D_MODEL), lambda s: (1 + s // nb, 0, 0)),
                  pl.BlockSpec((1, seq, D_ATT), seq_map), pl.BlockSpec((1, seq, D_ATT), seq_map),
                  pl.BlockSpec((1,) + ck3.shape[1:], seq_map), pl.BlockSpec((1,) + cv3.shape[1:], seq_map),
                  _const_spec(rpb_pad.shape),
                  _resident_spec(w_in, (2 * D_MODEL, O_G // (2 * D_MODEL))),
                  _const_spec(w_pool.shape), _const_spec((1, D_POOL)),
                  _resident_spec(wbrp), _resident_spec(wbra), _resident_spec(wout)],
        out_specs=pl.BlockSpec((tm, D_MODEL), row),
        out_shape=jax.ShapeDtypeStruct((n_batch * seq, D_MODEL), F32),
        scratch_shapes=[pltpu.VMEM((N_HEADS, 2 * NA_KH - 2, GRID_W, LANES), F32),
                        pltpu.VMEM((D_POOL, D_POOL), BF16)],
        compiler_params=_params(),
        name="lat_mixer",
    )(x_all, h, q, p, p, p, mod3, k3, v3, ck3, cv3, rpb_pad, w_in, w_pool, pscale, wbrp, wbra, wout)


def kernel(x_prompt, x_sample, cache_k, cache_v, c, c_ctx, w_ada, b_ada, g_ff1, w_ff1_in, w_ff1_out,
           g_mix, w_in, q_gain, k_gain, w_pool, pool_scale, rpb, w_br_pool, w_br_att, w_out, g_ff2,
           w_ff2_in, w_ff2_out):
    n_ctx, seq_ctx, _ = x_prompt.shape
    n_lat, seq_lat, _ = x_sample.shape
    depth = w_ada.shape[0]
    assert depth == 1 and n_lat + 1 <= 8
    l = 0

    xp = x_prompt.reshape(n_ctx * seq_ctx, D_MODEL)
    xs = x_sample.reshape(n_lat * seq_lat, D_MODEL)
    t_ctx, t_lat = xp.shape[0], xs.shape[0]
    cond8 = jnp.concatenate([c_ctx[None], c, jnp.zeros((7 - n_lat, D_MODEL), F32)], axis=0)
    mod3 = _adaln(cond8, w_ada[l], b_ada[l]).reshape(8, N_MOD, D_MODEL)

    pscale = pool_scale[l].reshape(1, D_POOL)
    qg2 = jnp.tile(q_gain[l], 2).reshape(1, LANES)
    kg2 = jnp.tile(k_gain[l], 2).reshape(1, LANES)
    g1, gm, g2 = g_ff1[l].reshape(1, -1), g_mix[l].reshape(1, -1), g_ff2[l].reshape(1, -1)
    rpb_pad = jnp.pad(rpb[l], ((0, 0), (0, 0), (0, LANES - rpb.shape[-1])))
    ck3 = cache_k[:, l].reshape(n_lat, -1, D_ATT).astype(BF16)
    cv3 = cache_v[:, l].reshape(n_lat, -1, D_ATT).astype(BF16)

    x1, w2gu, w2dn, win, wbrp, wbra, wout = _ffn(
        xp, xs, mod3, seq_lat, 0, g1, w_ff1_in[l], w_ff1_out[l], "ffn1", convert=True,
        preconvert=(w_ff2_in[l], w_ff2_out[l], w_in[l], w_br_pool[l], w_br_att[l], w_out[l]))
    x2p, k_p, v_p = _ctx_mixer(x1, t_ctx, mod3, gm, win, qg2, kg2, w_pool[l], pscale, wbrp, wbra, wout, seq_ctx)
    h_s, p_s, q_s, k_s, v_s = _lat_proj(x1, t_ctx, t_lat, seq_lat, mod3, gm, win, qg2, kg2)
    x2s = _lat_mixer(x1, t_ctx, n_lat, seq_lat, mod3, h_s, q_s, p_s,
                     k_s.reshape(n_lat, seq_lat, D_ATT), v_s.reshape(n_lat, seq_lat, D_ATT), ck3, cv3,
                     rpb_pad, win, w_pool[l], pscale, wbrp, wbra, wout)
    yp, ys = _ffn(x2p, x2s, mod3, seq_lat, 6, g2, w2gu, w2dn, "ffn2", split_out=True)

    new_k = k_p.reshape(n_ctx, 1, seq_ctx, N_HEADS, HEAD_DIM)
    new_v = v_p.reshape(n_ctx, 1, seq_ctx, N_HEADS, HEAD_DIM)
    return (yp.reshape(x_prompt.shape), ys.reshape(x_sample.shape), new_k, new_v)
```

```python
import functools
import math

import jax
import jax.numpy as jnp
from jax import lax
from jax.experimental import pallas as pl
from jax.experimental.pallas import tpu as pltpu

F32 = jnp.float32
BF16 = jnp.bfloat16

D_MODEL = 1024
N_HEADS = 8
HEAD_DIM = 64
D_ATT = N_HEADS * HEAD_DIM
D_POOL = 512
POOL_WINDOWS = (2, 4, 8, 16)
POOL_HALO = 8
D_FF = 2816
N_MOD = 9
GRID_W = 64
NA_KH = 8
NA_KW = 16
EPS = 1e-6
NEG = -1e30

LANES = 128
BF16_ROWS = 16
N_PAIRS = D_ATT // LANES
MXU_COLS = 256
FF_CHUNKS = ((0, 768), (768, 768), (1536, 768), (2304, 512))
ADALN_TN = 2304
ADALN_STREAMS = 3
FFN_TM = 512
CTX_TB = 2
PROJ_TM = 1024
LAT_ROWS = 4
LAT_TM = LAT_ROWS * GRID_W
LAT_TILES = 2
LAT_KEY_ROWS = 12
SCORE_LOOKAHEAD = 2
N_CONV = 8
PRE_CHUNKS = 16
VMEM_LIMIT = 56 * 1024 * 1024

Q_SCALE = 1.0 / math.sqrt(HEAD_DIM)

O_Q = D_POOL
O_K = D_POOL + D_ATT
O_V = D_POOL + 2 * D_ATT
O_G = D_POOL + 3 * D_ATT


def _dot(a, b):
    return jnp.dot(a, b, preferred_element_type=F32)


def _dot_nt(a, b):
    return lax.dot_general(a, b, (((1,), (1,)), ((), ())), preferred_element_type=F32)


def _rms_mod(x, g, shift, scale):
    ms = jnp.mean(x * x, axis=-1, keepdims=True)
    y = x * lax.rsqrt(ms + EPS) * g
    return y * (1.0 + scale) + shift


def _low_half():
    return lax.broadcasted_iota(jnp.int32, (1, LANES), 1) < HEAD_DIM


def _head_norm(t, gain2):
    lo = _low_half()
    sq = t * t
    s0 = jnp.sum(jnp.where(lo, sq, 0.0), axis=-1, keepdims=True)
    s1 = jnp.sum(jnp.where(lo, 0.0, sq), axis=-1, keepdims=True)
    r0 = lax.rsqrt(s0 * (1.0 / HEAD_DIM) + EPS)
    r1 = lax.rsqrt(s1 * (1.0 / HEAD_DIM) + EPS)
    return t * jnp.where(lo, r0, r1) * gain2


def _own_lanes(head, mine, other):
    lo = _low_half()
    return jnp.where(lo, mine, other) if head % 2 == 0 else jnp.where(lo, other, mine)


def _pipelined_heads(scores, weights, weighted_values, fillers):
    lo = _low_half()
    outs = []
    pending_scores = [scores(head) for head in range(SCORE_LOOKAHEAD)]
    pending_weights = [weights(0, *pending_scores.pop(0))]
    for head in range(N_HEADS):
        if head + SCORE_LOOKAHEAD < N_HEADS:
            pending_scores.append(scores(head + SCORE_LOOKAHEAD))
        fillers[head]()
        if head + 1 < N_HEADS:
            pending_weights.append(weights(head + 1, *pending_scores.pop(0)))
        outs.append(weighted_values(head, pending_weights.pop(0)))
    att_pairs = []
    for pr in range(N_PAIRS):
        even, odd = outs[2 * pr], outs[2 * pr + 1]
        num = jnp.where(lo, even, odd)
        den = pltpu.roll(jnp.where(lo, odd, even), HEAD_DIM, 1)
        att_pairs.append(num / den)
    return jnp.concatenate(att_pairs, axis=1)


def _pool_delta(p_ext, g, pos0, seq_len):
    w = POOL_WINDOWS[g]
    te = p_ext.shape[0]
    tm = te - 2 * POOL_HALO
    t = (pos0 + lax.broadcasted_iota(jnp.int32, (tm, 1), 0)).astype(F32)
    s = p_ext
    step = 1
    while step < w:
        s = s + pltpu.roll(s, step, 0)
        step *= 2
    ahead = w // 2 - 1
    if ahead:
        s = pltpu.roll(s, te - ahead, 0)
    win = s[POOL_HALO:POOL_HALO + tm]
    cnt = jnp.minimum(t + float(w // 2), float(seq_len)) - jnp.maximum(t - float(w // 2), 0.0)
    return win / cnt - p_ext[POOL_HALO:POOL_HALO + tm]


def _const_spec(shape):
    nd = len(shape)
    return pl.BlockSpec(shape, lambda *_: (0,) * nd)


def _resident_spec(w, col_block=None):
    width, cidx = col_block if col_block else (w.shape[1], 0)
    return pl.BlockSpec((w.shape[0], width), lambda *_: (0, cidx), pipeline_mode=pl.Buffered(1))


def _params():
    return pltpu.CompilerParams(dimension_semantics=("arbitrary",), vmem_limit_bytes=VMEM_LIMIT)


def _build_pool_weight(wpool32_ref, wpool_ref):
    wpool_ref[...] = jnp.zeros(wpool_ref.shape, BF16)
    for g in range(len(POOL_WINDOWS)):
        blk = slice(g * LANES, (g + 1) * LANES)
        wpool_ref[blk, blk] = wpool32_ref[g].astype(BF16)


def _adaln_kernel(cond_ref, *refs):
    w_refs, b_ref, o_ref = refs[:-2], refs[-2], refs[-1]
    c = cond_ref[...]
    s = (c * jax.nn.sigmoid(c)).astype(BF16)
    width = w_refs[0].shape[1]
    for k, w_ref in enumerate(w_refs):
        cols = slice(k * width, (k + 1) * width)
        o_ref[:, cols] = _dot(s, w_ref[...].astype(BF16)) + b_ref[:, cols]


def _adaln(cond8, w_ada, b_ada):
    n = w_ada.shape[1]
    tn = ADALN_TN
    width = tn // ADALN_STREAMS
    assert width * ADALN_STREAMS == tn and width % LANES == 0

    def part_spec(k):
        return pl.BlockSpec((D_MODEL, width), lambda i: (0, ADALN_STREAMS * i + k))

    return pl.pallas_call(
        _adaln_kernel,
        grid=(n // tn,),
        in_specs=[pl.BlockSpec((8, D_MODEL), lambda i: (0, 0))]
                 + [part_spec(k) for k in range(ADALN_STREAMS)]
                 + [pl.BlockSpec((1, tn), lambda i: (0, i))],
        out_specs=pl.BlockSpec((8, tn), lambda i: (0, i)),
        out_shape=jax.ShapeDtypeStruct((8, n), F32),
        compiler_params=_params(),
        name="adaln",
    )(cond8, *([w_ada] * ADALN_STREAMS), b_ada.reshape(1, n))


def _ffn_kernel(*refs, mod_off, n_a, split_out, n_conv, n_pre):
    refs = list(refs)
    xa_ref, xb_ref, mod_ref, g_ref, wgu_in, wdn_in = refs[:6]
    pre_src = refs[6:6 + n_pre]
    n_out = 2 if split_out else 1
    out_refs = refs[6 + n_pre:6 + n_pre + n_out]
    pre_dst = refs[6 + n_pre + n_out:6 + 2 * n_pre + n_out]
    wgu_ref, wdn_ref = refs[6 + 2 * n_pre + n_out:] if n_conv else (wgu_in, wdn_in)
    step = pl.program_id(0)

    if n_conv:
        @pl.when(step < n_conv)
        def _convert():
            for src, dst in ((wgu_in, wgu_ref), (wdn_in, wdn_ref)):
                rows = src.shape[0]
                dst[pl.ds(pl.multiple_of(step * rows, rows), rows), :] = src[...].astype(BF16)

    tile = step - n_conv

    def half_step(x_ref, o_ref):
        shift = mod_ref[0, mod_off:mod_off + 1, :]
        scale = mod_ref[0, mod_off + 1:mod_off + 2, :]
        gate = mod_ref[0, mod_off + 2:mod_off + 3, :]
        x = x_ref[...]
        h = _rms_mod(x, g_ref[...], shift, scale).astype(BF16)
        acc = None
        for c0, cw in FF_CHUNKS:
            a = _dot(h, wgu_ref[:, c0:c0 + cw])
            u = _dot(h, wgu_ref[:, D_FF + c0:D_FF + c0 + cw])
            t = (a * jax.nn.sigmoid(a) * u).astype(BF16)
            part = _dot(t, wdn_ref[c0:c0 + cw, :])
            acc = part if acc is None else acc + part
        o_ref[...] = x + (0.5 * gate) * acc

    @pl.when((tile >= 0) & (tile < n_a))
    def _tokens_a():
        half_step(xa_ref, out_refs[0])

    @pl.when(tile >= n_a)
    def _tokens_b():
        half_step(xb_ref, out_refs[-1])

    if n_pre:
        @pl.when((tile >= 0) & (tile < PRE_CHUNKS))
        def _preconvert():
            for src, dst in zip(pre_src, pre_dst):
                dst[...] = src[...].astype(BF16)


def _ffn(xa, xb, mod3, seq_b, mod_off, g, wgu, wdn, name, split_out=False, convert=False, preconvert=()):
    tm = FFN_TM
    n_a, n_b = xa.shape[0] // tm, xb.shape[0] // tm
    n_conv = N_CONV if convert else 0
    assert n_a + n_b >= PRE_CHUNKS
    tile = lambda s: jnp.maximum(s - n_conv, 0)
    a_map = lambda s: (jnp.minimum(tile(s), n_a - 1), 0)
    b_map = lambda s: (jnp.maximum(tile(s) - n_a, 0), 0)
    mod_map = lambda s: (jnp.where(tile(s) < n_a, 0, 1 + (jnp.maximum(tile(s) - n_a, 0) * tm) // seq_b), 0, 0)

    def chunk_spec(w, n_chunks, index):
        rows = w.shape[0] // n_chunks
        assert rows * n_chunks == w.shape[0] and rows % BF16_ROWS == 0
        return pl.BlockSpec((rows, w.shape[1]), lambda s: (jnp.minimum(index(s), n_chunks - 1), 0))

    if convert:
        w_specs = [chunk_spec(wgu, N_CONV, lambda s: s), chunk_spec(wdn, N_CONV, lambda s: s)]
        scratch = [pltpu.VMEM(wgu.shape, BF16), pltpu.VMEM(wdn.shape, BF16)]
    else:
        w_specs = [_resident_spec(wgu), _resident_spec(wdn)]
        scratch = []
    pre_specs = lambda: [chunk_spec(w, PRE_CHUNKS, tile) for w in preconvert]
    if split_out:
        out_specs = [pl.BlockSpec((tm, D_MODEL), a_map), pl.BlockSpec((tm, D_MODEL), b_map)]
        out_shape = [jax.ShapeDtypeStruct(xa.shape, F32), jax.ShapeDtypeStruct(xb.shape, F32)]
    else:
        out_specs = [pl.BlockSpec((tm, D_MODEL), lambda s: (tile(s), 0))]
        out_shape = [jax.ShapeDtypeStruct((xa.shape[0] + xb.shape[0], D_MODEL), F32)]
    return pl.pallas_call(
        functools.partial(_ffn_kernel, mod_off=mod_off, n_a=n_a, split_out=split_out, n_conv=n_conv,
                          n_pre=len(preconvert)),
        grid=(n_conv + n_a + n_b,),
        in_specs=[pl.BlockSpec((tm, D_MODEL), a_map), pl.BlockSpec((tm, D_MODEL), b_map),
                  pl.BlockSpec((1, N_MOD, D_MODEL), mod_map), _const_spec((1, D_MODEL))] + w_specs + pre_specs(),
        out_specs=out_specs + pre_specs(),
        out_shape=out_shape + [jax.ShapeDtypeStruct(w.shape, BF16) for w in preconvert],
        scratch_shapes=scratch,
        compiler_params=_params(),
        name=name,
    )(xa, xb, mod3, g, wgu, wdn, *preconvert)


def _heads_to_sublanes(t):
    heads = jnp.stack([t[:, h * HEAD_DIM:(h + 1) * HEAD_DIM] for h in range(N_HEADS)], axis=0)
    return jnp.swapaxes(heads, 0, 1)


def _gate_fillers(gates, h_rows, gate_cols):
    assert 2 * D_MODEL // MXU_COLS == N_HEADS

    def gate_block():
        c0 = len(gates) * MXU_COLS
        gates.append(jax.nn.sigmoid(_dot(h_rows(), gate_cols(c0, c0 + MXU_COLS))))

    return [gate_block] * N_HEADS


def _pool_fillers(st, delta_fn, wpool_ref, pscale_ref, wbrp_ref):
    def pool_delta():
        deltas = [delta_fn(g).astype(BF16) for g in range(len(POOL_WINDOWS))]
        st["d"] = jnp.concatenate(deltas, axis=1)

    def pool_weight():
        st["y"] = (_dot(st.pop("d"), wpool_ref[...]) * pscale_ref[...]).astype(BF16)

    def pool_out():
        st["a"] = _dot(st.pop("y"), wbrp_ref[...])

    return [pool_delta, pool_weight, pool_out]


def _merge_fillers(st, gates, rows, att, x_ref, o_ref, gate2, wbra_ref, wout_ref):
    def attention_branch():
        st["b"] = _dot(att.astype(BF16), wbra_ref[...])

    def gate():
        g = jnp.concatenate(gates, axis=1)
        merged = g[:, :D_MODEL] * st["a"][rows] + g[:, D_MODEL:] * st.pop("b")
        st["merged"] = merged.astype(BF16)

    def out():
        o_ref[rows, :] = x_ref[rows, :] + gate2 * _dot(st.pop("merged"), wout_ref[...])

    return [attention_branch, gate, out]


def _attend_tiles(n_tiles, tile_rows, attention, gate_fillers, pool_fillers, merge_fillers):
    idle = lambda: None
    extra = pool_fillers + [idle] * (N_HEADS - len(pool_fillers))
    for t in range(n_tiles):
        rows = slice(t * tile_rows, (t + 1) * tile_rows)
        gates = []
        fillers = [lambda g=g, e=e: (g(), e()) for g, e in zip(gate_fillers(rows, gates), extra)]
        pieces = merge_fillers(gates, rows, attention(t, fillers))
        extra = [idle] + pieces + [idle] * (N_HEADS - 1 - len(pieces))
    for piece in pieces:
        piece()


def _ctx_mixer_kernel(x_ref, mod_ref, g_ref, win_ref, qg_ref, kg_ref, wpool32_ref, pscale_ref,
                      wbrp_ref, wbra_ref, wout_ref, o_ref, ko_ref, vo_ref, wpool_ref, *, seq):
    @pl.when(pl.program_id(0) == 0)
    def _first_step():
        _build_pool_weight(wpool32_ref, wpool_ref)

    m_rows = x_ref.shape[0]
    h = _rms_mod(x_ref[...], g_ref[...], mod_ref[0, 3:4, :], mod_ref[0, 4:5, :]).astype(BF16)
    p = _dot(h, win_ref[:, 0:O_Q])
    q = _dot(h, win_ref[:, O_Q:O_K])
    k = _dot(h, win_ref[:, O_K:O_V])
    v = _dot(h, win_ref[:, O_V:O_G])
    vo_ref[...] = _heads_to_sublanes(v)
    scale = Q_SCALE
    zero = jnp.zeros((), BF16)
    one = jnp.ones((), BF16)
    st = {}

    def pool_delta(g):
        lanes = slice(g * LANES, (g + 1) * LANES)
        zeros_halo = jnp.zeros((POOL_HALO, LANES), F32)
        return jnp.concatenate(
            [_pool_delta(jnp.concatenate([zeros_halo, p[e * seq:(e + 1) * seq, lanes], zeros_halo], axis=0),
                         g, 0, seq) for e in range(m_rows // seq)], axis=0)

    def gate_fillers(rows, gates):
        return _gate_fillers(gates, lambda: h[rows], lambda c0, c1: win_ref[:, O_G + c0:O_G + c1])

    def attention(e, fillers):
        rows = slice(e * seq, (e + 1) * seq)
        qn, kn, vb = [], [], []
        for pr in range(N_PAIRS):
            lanes = slice(pr * LANES, (pr + 1) * LANES)
            kn.append(_head_norm(k[rows, lanes], kg_ref[...]))
            qn.append((_head_norm(q[rows, lanes], qg_ref[...]) * scale).astype(BF16))
            vb.append(v[rows, lanes].astype(BF16))
        ko_ref[rows] = _heads_to_sublanes(jnp.concatenate(kn, axis=1))
        kb = [t.astype(BF16) for t in kn]

        def scores(head):
            return (_dot_nt(_own_lanes(head, qn[head // 2], zero), kb[head // 2]),)

        def weights(head, s):
            return jnp.exp(s - s.max(axis=-1, keepdims=True)).astype(BF16)

        def weighted_values(head, e_):
            return _dot(e_, _own_lanes(head, vb[head // 2], one))

        return _pipelined_heads(scores, weights, weighted_values, fillers)

    merge = functools.partial(_merge_fillers, st, x_ref=x_ref, o_ref=o_ref, gate2=mod_ref[0, 5:6, :],
                              wbra_ref=wbra_ref, wout_ref=wout_ref)
    _attend_tiles(m_rows // seq, seq, attention, gate_fillers,
                  _pool_fillers(st, pool_delta, wpool_ref, pscale_ref, wbrp_ref), merge)


def _ctx_mixer(x_all, n_tok, mod3, g, w_in, qg2, kg2, w_pool, pscale, wbrp, wbra, wout, seq):
    tm = CTX_TB * seq
    row = lambda s: (s, 0)
    heads = lambda s: (s, 0, 0)
    return pl.pallas_call(
        functools.partial(_ctx_mixer_kernel, seq=seq),
        grid=(n_tok // tm,),
        in_specs=[pl.BlockSpec((tm, D_MODEL), row),
                  _const_spec((1, N_MOD, D_MODEL)), _const_spec((1, D_MODEL)),
                  _resident_spec(w_in), _const_spec((1, LANES)), _const_spec((1, LANES)),
                  _const_spec(w_pool.shape), _const_spec((1, D_POOL)),
                  _resident_spec(wbrp), _resident_spec(wbra), _resident_spec(wout)],
        out_specs=[pl.BlockSpec((tm, D_MODEL), row),
                   pl.BlockSpec((tm, N_HEADS, HEAD_DIM), heads),
                   pl.BlockSpec((tm, N_HEADS, HEAD_DIM), heads)],
        out_shape=[jax.ShapeDtypeStruct((n_tok, D_MODEL), F32),
                   jax.ShapeDtypeStruct((n_tok, N_HEADS, HEAD_DIM), F32),
                   jax.ShapeDtypeStruct((n_tok, N_HEADS, HEAD_DIM), F32)],
        scratch_shapes=[pltpu.VMEM((D_POOL, D_POOL), BF16)],
        compiler_params=_params(),
        name="ctx_mixer",
    )(x_all, mod3, g, w_in, qg2, kg2, w_pool, pscale, wbrp, wbra, wout)


def _lat_proj_kernel(x_ref, mod_ref, g_ref, win_ref, qg_ref, kg_ref, h_ref, p_ref, q_ref, k_ref, v_ref):
    h = _rms_mod(x_ref[...], g_ref[...], mod_ref[0, 3:4, :], mod_ref[0, 4:5, :]).astype(BF16)
    h_ref[...] = h
    p_ref[...] = _dot(h, win_ref[:, 0:O_Q])
    q = _dot(h, win_ref[:, O_Q:O_K])
    k = _dot(h, win_ref[:, O_K:O_V])
    v_ref[...] = _dot(h, win_ref[:, O_V:O_G]).astype(BF16)
    scale = Q_SCALE
    for pr in range(N_PAIRS):
        lanes = slice(pr * LANES, (pr + 1) * LANES)
        q_ref[:, lanes] = (_head_norm(q[:, lanes], qg_ref[...]) * scale).astype(BF16)
        k_ref[:, lanes] = _head_norm(k[:, lanes], kg_ref[...]).astype(BF16)


def _lat_proj(x_all, tok0, n_tok, seq, mod3, g, w_in, qg2, kg2):
    tm = PROJ_TM
    tile0 = tok0 // tm
    row = lambda s: (s, 0)
    shapes = [(D_MODEL, BF16), (D_POOL, F32), (D_ATT, BF16), (D_ATT, BF16), (D_ATT, BF16)]
    return pl.pallas_call(
        _lat_proj_kernel,
        grid=(n_tok // tm,),
        in_specs=[pl.BlockSpec((tm, D_MODEL), lambda s: (tile0 + s, 0)),
                  pl.BlockSpec((1, N_MOD, D_MODEL), lambda s: (1 + (s * tm) // seq, 0, 0)),
                  _const_spec((1, D_MODEL)), _resident_spec(w_in, (O_G, 0)),
                  _const_spec((1, LANES)), _const_spec((1, LANES))],
        out_specs=[pl.BlockSpec((tm, width), row) for width, _ in shapes],
        out_shape=[jax.ShapeDtypeStruct((n_tok, width), dtype) for width, dtype in shapes],
        compiler_params=_params(),
        name="lat_proj",
    )(x_all, mod3, g, w_in, qg2, kg2)


def _build_bias_table(rpb_ref, bias_ref):
    lane = lax.broadcasted_iota(jnp.int32, (GRID_W, LANES), 1)
    qc = lax.broadcasted_iota(jnp.int32, (GRID_W, LANES), 0)
    kc = lane & (GRID_W - 1)
    c0 = jnp.clip(qc - NA_KW // 2, 0, GRID_W - NA_KW)
    col_ok = (kc >= c0) & (kc < c0 + NA_KW)
    first_half = lane < GRID_W
    shift_a = LANES - (NA_KW - 1)
    shift_b = (GRID_W - (NA_KW - 1)) % LANES
    for h in range(N_HEADS):
        for i in range(2 * NA_KH - 2):
            va = jnp.broadcast_to(rpb_ref[h, i:i + 1, :], (GRID_W, LANES))
            vb = jnp.broadcast_to(rpb_ref[h, i + 1:i + 2, :], (GRID_W, LANES))
            ra = pltpu.roll(va, shift_a, 1, stride=1, stride_axis=0)
            rb = pltpu.roll(vb, shift_b, 1, stride=1, stride_axis=0)
            bias_ref[h, i] = jnp.where(col_ok, jnp.where(first_half, ra, rb), NEG)


def _lat_attention(qn, j, fillers, k_ref, v_ref, ck_ref, cv_ref, bias_ref, *, n_rows):
    n_keys = LAT_KEY_ROWS * GRID_W
    r0q = j * LAT_ROWS
    u0 = jnp.clip(r0q - NA_KH // 2, 0, n_rows - LAT_KEY_ROWS)
    key0 = pl.multiple_of(u0 * GRID_W, GRID_W)
    log2_w = GRID_W.bit_length() - 1
    zero = jnp.zeros((), BF16)
    one = jnp.ones((), BF16)

    lane_row = lax.broadcasted_iota(jnp.int32, (1, LANES), 1) & (HEAD_DIM - 1)
    q_tile_row = jnp.right_shift(lax.broadcasted_iota(jnp.int32, (LAT_TM, 1), 0), log2_w)
    q_hot = jnp.where(lane_row == q_tile_row, 1.0, 0.0).astype(BF16)
    k_row = u0 + jnp.right_shift(lax.broadcasted_iota(jnp.int32, (n_keys, 1), 0), log2_w)
    first = jnp.clip(r0q + lane_row - NA_KH // 2, 0, n_rows - NA_KH)
    outside = (lane_row < LAT_ROWS) & ((k_row < first) | (k_row >= first + NA_KH))
    k_mask = jnp.where(outside, NEG, 0.0).astype(BF16)

    def scores(head):
        lanes = slice((head // 2) * LANES, (head // 2 + 1) * LANES)
        qp = qn[:, lanes]
        s_loc = _dot_nt(_own_lanes(head, qp, q_hot),
                        _own_lanes(head, k_ref[0, pl.ds(key0, n_keys), lanes], k_mask))
        return s_loc, _dot_nt(_own_lanes(head, qp, zero), ck_ref[0, :, lanes])

    def weights(head, s_loc, s_ctx):
        bias_rows = []
        for jr in range(LAT_ROWS):
            blocks = []
            for pi in range(LAT_KEY_ROWS // 2):
                idx = jnp.clip(u0 - r0q + 2 * pi - jr + NA_KH - 1, 0, 2 * NA_KH - 3)
                blocks.append(bias_ref[head, idx])
            bias_rows.append(jnp.concatenate(blocks, axis=1))
        s_loc = s_loc + jnp.concatenate(bias_rows, axis=0)
        m = jnp.maximum(s_loc.max(axis=-1, keepdims=True), s_ctx.max(axis=-1, keepdims=True))
        return jnp.concatenate([jnp.exp(s_loc - m).astype(BF16), jnp.exp(s_ctx - m).astype(BF16)], axis=1)

    def weighted_values(head, e):
        lanes = slice((head // 2) * LANES, (head // 2 + 1) * LANES)
        values = jnp.concatenate([_own_lanes(head, v_ref[0, pl.ds(key0, n_keys), lanes], one),
                                  _own_lanes(head, cv_ref[0, :, lanes], one)], axis=0)
        return _dot(e, values)

    return _pipelined_heads(scores, weights, weighted_values, fillers)


def _lat_mixer_kernel(x_ref, h_ref, q_ref, p_ref, pp_ref, pn_ref, mod_ref, k_ref, v_ref, ck_ref, cv_ref,
                      rpb_ref, wg_ref, wpool32_ref, pscale_ref, wbrp_ref, wbra_ref, wout_ref,
                      o_ref, bias_ref, wpool_ref, *, seq, n_rows):
    step = pl.program_id(0)
    tm = LAT_TILES * LAT_TM

    @pl.when(step == 0)
    def _first_step():
        _build_pool_weight(wpool32_ref, wpool_ref)
        _build_bias_table(rpb_ref, bias_ref)

    blk = lax.rem(step, seq // tm)
    st = {}

    def pool_delta(g):
        lanes = slice(g * LANES, (g + 1) * LANES)
        pe = jnp.concatenate([pp_ref[:, lanes], p_ref[:, lanes], pn_ref[:, lanes]], axis=0)
        pos = blk * tm - POOL_HALO + lax.broadcasted_iota(jnp.int32, (tm + 2 * POOL_HALO, 1), 0)
        return _pool_delta(jnp.where((pos >= 0) & (pos < seq), pe, 0.0), g, blk * tm, seq)

    block_gates = []

    def gate_fillers(rows, gates):
        if not block_gates:
            fillers = _gate_fillers(block_gates, lambda: h_ref[...], lambda c0, c1: wg_ref[:, c0:c1])
            return [lambda f=f: (f(), gates.append(block_gates[-1][rows])) for f in fillers]
        gates.extend(g[rows] for g in block_gates)
        return [lambda: None] * N_HEADS

    pool_fillers = _pool_fillers(st, pool_delta, wpool_ref, pscale_ref, wbrp_ref)

    def attention(t, fillers):
        return _lat_attention(q_ref[t * LAT_TM:(t + 1) * LAT_TM, :], blk * LAT_TILES + t, fillers,
                              k_ref, v_ref, ck_ref, cv_ref, bias_ref, n_rows=n_rows)

    merge = functools.partial(_merge_fillers, st, x_ref=x_ref, o_ref=o_ref, gate2=mod_ref[0, 5:6, :],
                              wbra_ref=wbra_ref, wout_ref=wout_ref)
    _attend_tiles(LAT_TILES, LAT_TM, attention, gate_fillers, pool_fillers, merge)


def _lat_mixer(x_all, tok0, n_batch, seq, mod3, h, q, p, k3, v3, ck3, cv3, rpb_pad, w_in, w_pool, pscale,
               wbrp, wbra, wout):
    tm = LAT_TILES * LAT_TM
    n_rows = seq // GRID_W
    nb = seq // tm
    tile0 = tok0 // tm
    halo_per_tile = tm // POOL_HALO
    n_halo = p.shape[0] // POOL_HALO
    row = lambda s: (s, 0)
    seq_map = lambda s: (s // nb, 0, 0)
    return pl.pallas_call(
        functools.partial(_lat_mixer_kernel, seq=seq, n_rows=n_rows),
        grid=(n_batch * nb,),
        in_specs=[pl.BlockSpec((tm, D_MODEL), lambda s: (tile0 + s, 0)),
                  pl.BlockSpec((tm, D_MODEL), row),
                  pl.BlockSpec((tm, D_ATT), row),
                  pl.BlockSpec((tm, D_POOL), row),
                  pl.BlockSpec((POOL_HALO, D_POOL), lambda s: (jnp.maximum(s * halo_per_tile - 1, 0), 0)),
                  pl.BlockSpec((POOL_HALO, D_POOL),
                               lambda s: (jnp.minimum((s + 1) * halo_per_tile, n_halo - 1), 0)),
                  pl.BlockSpec((1, N_MOD, D_MODEL), lambda s: (1 + s // nb, 0, 0)),
                  pl.BlockSpec((1, seq, D_ATT), seq_map), pl.BlockSpec((1, seq, D_ATT), seq_map),
                  pl.BlockSpec((1,) + ck3.shape[1:], seq_map), pl.BlockSpec((1,) + cv3.shape[1:], seq_map),
                  _const_spec(rpb_pad.shape),
                  _resident_spec(w_in, (2 * D_MODEL, O_G // (2 * D_MODEL))),
                  _const_spec(w_pool.shape), _const_spec((1, D_POOL)),
                  _resident_spec(wbrp), _resident_spec(wbra), _resident_spec(wout)],
        out_specs=pl.BlockSpec((tm, D_MODEL), row),
        out_shape=jax.ShapeDtypeStruct((n_batch * seq, D_MODEL), F32),
        scratch_shapes=[pltpu.VMEM((N_HEADS, 2 * NA_KH - 2, GRID_W, LANES), F32),
                        pltpu.VMEM((D_POOL, D_POOL), BF16)],
        compiler_params=_params(),
        name="lat_mixer",
    )(x_all, h, q, p, p, p, mod3, k3, v3, ck3, cv3, rpb_pad, w_in, w_pool, pscale, wbrp, wbra, wout)


def kernel(x_prompt, x_sample, cache_k, cache_v, c, c_ctx, w_ada, b_ada, g_ff1, w_ff1_in, w_ff1_out,
           g_mix, w_in, q_gain, k_gain, w_pool, pool_scale, rpb, w_br_pool, w_br_att, w_out, g_ff2,
           w_ff2_in, w_ff2_out):
    n_ctx, seq_ctx, _ = x_prompt.shape
    n_lat, seq_lat, _ = x_sample.shape
    depth = w_ada.shape[0]
    assert depth == 1 and n_lat + 1 <= 8
    l = 0

    xp = x_prompt.reshape(n_ctx * seq_ctx, D_MODEL)
    xs = x_sample.reshape(n_lat * seq_lat, D_MODEL)
    t_ctx, t_lat = xp.shape[0], xs.shape[0]
    cond8 = jnp.concatenate([c_ctx[None], c, jnp.zeros((7 - n_lat, D_MODEL), F32)], axis=0)
    mod3 = _adaln(cond8, w_ada[l], b_ada[l]).reshape(8, N_MOD, D_MODEL)

    pscale = pool_scale[l].reshape(1, D_POOL)
    qg2 = jnp.tile(q_gain[l], 2).reshape(1, LANES)
    kg2 = jnp.tile(k_gain[l], 2).reshape(1, LANES)
    g1, gm, g2 = g_ff1[l].reshape(1, -1), g_mix[l].reshape(1, -1), g_ff2[l].reshape(1, -1)
    rpb_pad = jnp.pad(rpb[l], ((0, 0), (0, 0), (0, LANES - rpb.shape[-1])))
    ck3 = cache_k[:, l].reshape(n_lat, -1, D_ATT).astype(BF16)
    cv3 = cache_v[:, l].reshape(n_lat, -1, D_ATT).astype(BF16)

    x1, w2gu, w2dn, win, wbrp, wbra, wout = _ffn(
        xp, xs, mod3, seq_lat, 0, g1, w_ff1_in[l], w_ff1_out[l], "ffn1", convert=True,
        preconvert=(w_ff2_in[l], w_ff2_out[l], w_in[l], w_br_pool[l], w_br_att[l], w_out[l]))
    x2p, k_p, v_p = _ctx_mixer(x1, t_ctx, mod3, gm, win, qg2, kg2, w_pool[l], pscale, wbrp, wbra, wout, seq_ctx)
    h_s, p_s, q_s, k_s, v_s = _lat_proj(x1, t_ctx, t_lat, seq_lat, mod3, gm, win, qg2, kg2)
    x2s = _lat_mixer(x1, t_ctx, n_lat, seq_lat, mod3, h_s, q_s, p_s,
                     k_s.reshape(n_lat, seq_lat, D_ATT), v_s.reshape(n_lat, seq_lat, D_ATT), ck3, cv3,
                     rpb_pad, win, w_pool[l], pscale, wbrp, wbra, wout)
    yp, ys = _ffn(x2p, x2s, mod3, seq_lat, 6, g2, w2gu, w2dn, "ffn2", split_out=True)

    new_k = k_p.reshape(n_ctx, 1, seq_ctx, N_HEADS, HEAD_DIM)
    new_v = v_p.reshape(n_ctx, 1, seq_ctx, N_HEADS, HEAD_DIM)
    return (yp.reshape(x_prompt.shape), ys.reshape(x_sample.shape), new_k, new_v)
```

```python
import functools
import math

import jax
import jax.numpy as jnp
from jax import lax
from jax.experimental import pallas as pl
from jax.experimental.pallas import tpu as pltpu

F32 = jnp.float32
BF16 = jnp.bfloat16

D_MODEL = 1024
N_HEADS = 8
HEAD_DIM = 64
D_ATT = N_HEADS * HEAD_DIM
D_POOL = 512
POOL_WINDOWS = (2, 4, 8, 16)
POOL_HALO = 8
D_FF = 2816
N_MOD = 9
GRID_W = 64
NA_KH = 8
NA_KW = 16
EPS = 1e-6
NEG = -1e30

LANES = 128
BF16_ROWS = 16
N_PAIRS = D_ATT // LANES
MXU_COLS = 256
FF_CHUNKS = ((0, 768), (768, 768), (1536, 768), (2304, 512))
ADALN_TN = 2304
FFN_TM = 512
CTX_TB = 2
PROJ_TM = 1024
LAT_ROWS = 4
LAT_TM = LAT_ROWS * GRID_W
LAT_TILES = 2
LAT_KEY_ROWS = 12
SCORE_LOOKAHEAD = 2
N_CONV = 8
PRE_CHUNKS = 16
VMEM_LIMIT = 56 * 1024 * 1024

Q_SCALE = 1.0 / math.sqrt(HEAD_DIM)

O_Q = D_POOL
O_K = D_POOL + D_ATT
O_V = D_POOL + 2 * D_ATT
O_G = D_POOL + 3 * D_ATT


def _dot(a, b):
    return jnp.dot(a, b, preferred_element_type=F32)


def _dot_nt(a, b):
    return lax.dot_general(a, b, (((1,), (1,)), ((), ())), preferred_element_type=F32)


def _rms_mod(x, g, shift, scale):
    ms = jnp.mean(x * x, axis=-1, keepdims=True)
    return x * lax.rsqrt(ms + EPS) * (g * (1.0 + scale)) + shift


def _low_half():
    return lax.broadcasted_iota(jnp.int32, (1, LANES), 1) < HEAD_DIM


def _head_norm(t, gain2):
    lo = _low_half()
    sq = t * t
    s0 = jnp.sum(jnp.where(lo, sq, 0.0), axis=-1, keepdims=True)
    s1 = jnp.sum(jnp.where(lo, 0.0, sq), axis=-1, keepdims=True)
    r0 = lax.rsqrt(s0 * (1.0 / HEAD_DIM) + EPS)
    r1 = lax.rsqrt(s1 * (1.0 / HEAD_DIM) + EPS)
    return t * jnp.where(lo, r0, r1) * gain2


def _own_lanes(head, mine, other):
    lo = _low_half()
    return jnp.where(lo, mine, other) if head % 2 == 0 else jnp.where(lo, other, mine)


def _pipelined_heads(scores, weights, weighted_values, fillers):
    lo = _low_half()
    outs = []
    pending_scores = [scores(head) for head in range(SCORE_LOOKAHEAD)]
    pending_weights = [weights(0, *pending_scores.pop(0))]
    for head in range(N_HEADS):
        if head + SCORE_LOOKAHEAD < N_HEADS:
            pending_scores.append(scores(head + SCORE_LOOKAHEAD))
        fillers[head]()
        if head + 1 < N_HEADS:
            pending_weights.append(weights(head + 1, *pending_scores.pop(0)))
        outs.append(weighted_values(head, pending_weights.pop(0)))
    att_pairs = []
    for pr in range(N_PAIRS):
        even, odd = outs[2 * pr], outs[2 * pr + 1]
        num = jnp.where(lo, even, odd)
        den = pltpu.roll(jnp.where(lo, odd, even), HEAD_DIM, 1)
        att_pairs.append(num / den)
    return jnp.concatenate(att_pairs, axis=1)


def _pool_delta(p_ext, g, pos0, seq_len):
    w = POOL_WINDOWS[g]
    te = p_ext.shape[0]
    tm = te - 2 * POOL_HALO
    t = (pos0 + lax.broadcasted_iota(jnp.int32, (tm, 1), 0)).astype(F32)
    s = p_ext
    step = 1
    while step < w:
        s = s + pltpu.roll(s, step, 0)
        step *= 2
    ahead = w // 2 - 1
    if ahead:
        s = pltpu.roll(s, te - ahead, 0)
    win = s[POOL_HALO:POOL_HALO + tm]
    cnt = jnp.minimum(t + float(w // 2), float(seq_len)) - jnp.maximum(t - float(w // 2), 0.0)
    return win / cnt - p_ext[POOL_HALO:POOL_HALO + tm]


def _const_spec(shape):
    nd = len(shape)
    return pl.BlockSpec(shape, lambda *_: (0,) * nd)


def _resident_spec(w, col_block=None):
    width, cidx = col_block if col_block else (w.shape[1], 0)
    return pl.BlockSpec((w.shape[0], width), lambda *_: (0, cidx), pipeline_mode=pl.Buffered(1))


def _params():
    return pltpu.CompilerParams(dimension_semantics=("arbitrary",), vmem_limit_bytes=VMEM_LIMIT)


def _build_pool_weight(wpool32_ref, wpool_ref):
    wpool_ref[...] = jnp.zeros(wpool_ref.shape, BF16)
    for g in range(len(POOL_WINDOWS)):
        blk = slice(g * LANES, (g + 1) * LANES)
        wpool_ref[blk, blk] = wpool32_ref[g].astype(BF16)


def _adaln_kernel(cond_ref, wlo_ref, whi_ref, b_ref, o_ref):
    c = cond_ref[...]
    s = (c * jax.nn.sigmoid(c)).astype(BF16)
    half = wlo_ref.shape[1]
    o_ref[:, :half] = _dot(s, wlo_ref[...].astype(BF16)) + b_ref[:, :half]
    o_ref[:, half:] = _dot(s, whi_ref[...].astype(BF16)) + b_ref[:, half:]


def _adaln(cond8, w_ada, b_ada):
    n = w_ada.shape[1]
    tn = ADALN_TN
    return pl.pallas_call(
        _adaln_kernel,
        grid=(n // tn,),
        in_specs=[pl.BlockSpec((8, D_MODEL), lambda i: (0, 0)),
                  pl.BlockSpec((D_MODEL, tn // 2), lambda i: (0, 2 * i)),
                  pl.BlockSpec((D_MODEL, tn // 2), lambda i: (0, 2 * i + 1)),
                  pl.BlockSpec((1, tn), lambda i: (0, i))],
        out_specs=pl.BlockSpec((8, tn), lambda i: (0, i)),
        out_shape=jax.ShapeDtypeStruct((8, n), F32),
        compiler_params=_params(),
        name="adaln",
    )(cond8, w_ada, w_ada, b_ada.reshape(1, n))


def _ffn_kernel(*refs, mod_off, n_a, split_out, n_conv, n_pre):
    refs = list(refs)
    xa_ref, xb_ref, mod_ref, g_ref, wgu_in, wdn_in = refs[:6]
    pre_src = refs[6:6 + n_pre]
    n_out = 2 if split_out else 1
    out_refs = refs[6 + n_pre:6 + n_pre + n_out]
    pre_dst = refs[6 + n_pre + n_out:6 + 2 * n_pre + n_out]
    wgu_ref, wdn_ref = refs[6 + 2 * n_pre + n_out:] if n_conv else (wgu_in, wdn_in)
    step = pl.program_id(0)

    if n_conv:
        @pl.when(step < n_conv)
        def _convert():
            for src, dst in ((wgu_in, wgu_ref), (wdn_in, wdn_ref)):
                rows = src.shape[0]
                dst[pl.ds(pl.multiple_of(step * rows, rows), rows), :] = src[...].astype(BF16)

    tile = step - n_conv

    def half_step(x_ref, o_ref):
        shift = mod_ref[0, mod_off:mod_off + 1, :]
        scale = mod_ref[0, mod_off + 1:mod_off + 2, :]
        gate = mod_ref[0, mod_off + 2:mod_off + 3, :]
        x = x_ref[...]
        h = _rms_mod(x, g_ref[...], shift, scale).astype(BF16)
        acc = None
        for c0, cw in FF_CHUNKS:
            a = _dot(h, wgu_ref[:, c0:c0 + cw])
            u = _dot(h, wgu_ref[:, D_FF + c0:D_FF + c0 + cw])
            t = (a * jax.nn.sigmoid(a) * u).astype(BF16)
            part = _dot(t, wdn_ref[c0:c0 + cw, :])
            acc = part if acc is None else acc + part
        o_ref[...] = x + (0.5 * gate) * acc

    @pl.when((tile >= 0) & (tile < n_a))
    def _tokens_a():
        half_step(xa_ref, out_refs[0])

    @pl.when(tile >= n_a)
    def _tokens_b():
        half_step(xb_ref, out_refs[-1])

    if n_pre:
        @pl.when((tile >= 0) & (tile < PRE_CHUNKS))
        def _preconvert():
            for src, dst in zip(pre_src, pre_dst):
                dst[...] = src[...].astype(BF16)


def _ffn(xa, xb, mod3, seq_b, mod_off, g, wgu, wdn, name, split_out=False, convert=False, preconvert=()):
    tm = FFN_TM
    n_a, n_b = xa.shape[0] // tm, xb.shape[0] // tm
    n_conv = N_CONV if convert else 0
    assert n_a + n_b >= PRE_CHUNKS
    tile = lambda s: jnp.maximum(s - n_conv, 0)
    a_map = lambda s: (jnp.minimum(tile(s), n_a - 1), 0)
    b_map = lambda s: (jnp.maximum(tile(s) - n_a, 0), 0)
    mod_map = lambda s: (jnp.where(tile(s) < n_a, 0, 1 + (jnp.maximum(tile(s) - n_a, 0) * tm) // seq_b), 0, 0)

    def chunk_spec(w, n_chunks, index):
        rows = w.shape[0] // n_chunks
        assert rows * n_chunks == w.shape[0] and rows % BF16_ROWS == 0
        return pl.BlockSpec((rows, w.shape[1]), lambda s: (jnp.minimum(index(s), n_chunks - 1), 0))

    if convert:
        w_specs = [chunk_spec(wgu, N_CONV, lambda s: s), chunk_spec(wdn, N_CONV, lambda s: s)]
        scratch = [pltpu.VMEM(wgu.shape, BF16), pltpu.VMEM(wdn.shape, BF16)]
    else:
        w_specs = [_resident_spec(wgu), _resident_spec(wdn)]
        scratch = []
    pre_specs = lambda: [chunk_spec(w, PRE_CHUNKS, tile) for w in preconvert]
    if split_out:
        out_specs = [pl.BlockSpec((tm, D_MODEL), a_map), pl.BlockSpec((tm, D_MODEL), b_map)]
        out_shape = [jax.ShapeDtypeStruct(xa.shape, F32), jax.ShapeDtypeStruct(xb.shape, F32)]
    else:
        out_specs = [pl.BlockSpec((tm, D_MODEL), lambda s: (tile(s), 0))]
        out_shape = [jax.ShapeDtypeStruct((xa.shape[0] + xb.shape[0], D_MODEL), F32)]
    return pl.pallas_call(
        functools.partial(_ffn_kernel, mod_off=mod_off, n_a=n_a, split_out=split_out, n_conv=n_conv,
                          n_pre=len(preconvert)),
        grid=(n_conv + n_a + n_b,),
        in_specs=[pl.BlockSpec((tm, D_MODEL), a_map), pl.BlockSpec((tm, D_MODEL), b_map),
                  pl.BlockSpec((1, N_MOD, D_MODEL), mod_map), _const_spec((1, D_MODEL))] + w_specs + pre_specs(),
        out_specs=out_specs + pre_specs(),
        out_shape=out_shape + [jax.ShapeDtypeStruct(w.shape, BF16) for w in preconvert],
        scratch_shapes=scratch,
        compiler_params=_params(),
        name=name,
    )(xa, xb, mod3, g, wgu, wdn, *preconvert)


def _heads_to_sublanes(t):
    heads = jnp.stack([t[:, h * HEAD_DIM:(h + 1) * HEAD_DIM] for h in range(N_HEADS)], axis=0)
    return jnp.swapaxes(heads, 0, 1)


def _gate_fillers(gates, h_rows, gate_cols):
    assert 2 * D_MODEL // MXU_COLS == N_HEADS

    def gate_block():
        c0 = len(gates) * MXU_COLS
        gates.append(jax.nn.sigmoid(_dot(h_rows(), gate_cols(c0, c0 + MXU_COLS))))

    return [gate_block] * N_HEADS


def _pool_fillers(st, delta_fn, wpool_ref, pscale_ref, wbrp_ref):
    def pool_delta():
        deltas = [delta_fn(g).astype(BF16) for g in range(len(POOL_WINDOWS))]
        st["d"] = jnp.concatenate(deltas, axis=1)

    def pool_weight():
        st["y"] = (_dot(st.pop("d"), wpool_ref[...]) * pscale_ref[...]).astype(BF16)

    def pool_out():
        st["a"] = _dot(st.pop("y"), wbrp_ref[...])

    return [pool_delta, pool_weight, pool_out]


def _merge_fillers(st, gates, rows, att, x_ref, o_ref, gate2, wbra_ref, wout_ref):
    def attention_branch():
        st["b"] = _dot(att.astype(BF16), wbra_ref[...])

    def gate():
        g = jnp.concatenate(gates, axis=1)
        merged = g[:, :D_MODEL] * st["a"][rows] + g[:, D_MODEL:] * st.pop("b")
        st["merged"] = merged.astype(BF16)

    def out():
        o_ref[rows, :] = x_ref[rows, :] + gate2 * _dot(st.pop("merged"), wout_ref[...])

    return [attention_branch, gate, out]


def _attend_tiles(n_tiles, tile_rows, attention, gate_fillers, pool_fillers, merge_fillers):
    idle = lambda: None
    extra = pool_fillers + [idle] * (N_HEADS - len(pool_fillers))
    for t in range(n_tiles):
        rows = slice(t * tile_rows, (t + 1) * tile_rows)
        gates = []
        fillers = [lambda g=g, e=e: (g(), e()) for g, e in zip(gate_fillers(rows, gates), extra)]
        pieces = merge_fillers(gates, rows, attention(t, fillers))
        extra = [idle] + pieces + [idle] * (N_HEADS - 1 - len(pieces))
    for piece in pieces:
        piece()


def _ctx_mixer_kernel(x_ref, mod_ref, g_ref, win_ref, qg_ref, kg_ref, wpool32_ref, pscale_ref,
                      wbrp_ref, wbra_ref, wout_ref, o_ref, ko_ref, vo_ref, wpool_ref, *, seq):
    @pl.when(pl.program_id(0) == 0)
    def _first_step():
        _build_pool_weight(wpool32_ref, wpool_ref)

    m_rows = x_ref.shape[0]
    h = _rms_mod(x_ref[...], g_ref[...], mod_ref[0, 3:4, :], mod_ref[0, 4:5, :]).astype(BF16)
    p = _dot(h, win_ref[:, 0:O_Q])
    q = _dot(h, win_ref[:, O_Q:O_K])
    k = _dot(h, win_ref[:, O_K:O_V])
    v = _dot(h, win_ref[:, O_V:O_G])
    vo_ref[...] = _heads_to_sublanes(v)
    scale = Q_SCALE
    zero = jnp.zeros((), BF16)
    one = jnp.ones((), BF16)
    st = {}

    def pool_delta(g):
        lanes = slice(g * LANES, (g + 1) * LANES)
        zeros_halo = jnp.zeros((POOL_HALO, LANES), F32)
        return jnp.concatenate(
            [_pool_delta(jnp.concatenate([zeros_halo, p[e * seq:(e + 1) * seq, lanes], zeros_halo], axis=0),
                         g, 0, seq) for e in range(m_rows // seq)], axis=0)

    def gate_fillers(rows, gates):
        return _gate_fillers(gates, lambda: h[rows], lambda c0, c1: win_ref[:, O_G + c0:O_G + c1])

    def attention(e, fillers):
        rows = slice(e * seq, (e + 1) * seq)
        qn, kn, vb = [], [], []
        for pr in range(N_PAIRS):
            lanes = slice(pr * LANES, (pr + 1) * LANES)
            kn.append(_head_norm(k[rows, lanes], kg_ref[...]))
            qn.append((_head_norm(q[rows, lanes], qg_ref[...]) * scale).astype(BF16))
            vb.append(v[rows, lanes].astype(BF16))
        ko_ref[rows] = _heads_to_sublanes(jnp.concatenate(kn, axis=1))
        kb = [t.astype(BF16) for t in kn]

        def scores(head):
            return (_dot_nt(_own_lanes(head, qn[head // 2], zero), kb[head // 2]),)

        def weights(head, s):
            return jnp.exp(s - s.max(axis=-1, keepdims=True)).astype(BF16)

        def weighted_values(head, e_):
            return _dot(e_, _own_lanes(head, vb[head // 2], one))

        return _pipelined_heads(scores, weights, weighted_values, fillers)

    merge = functools.partial(_merge_fillers, st, x_ref=x_ref, o_ref=o_ref, gate2=mod_ref[0, 5:6, :],
                              wbra_ref=wbra_ref, wout_ref=wout_ref)
    _attend_tiles(m_rows // seq, seq, attention, gate_fillers,
                  _pool_fillers(st, pool_delta, wpool_ref, pscale_ref, wbrp_ref), merge)


def _ctx_mixer(x_all, n_tok, mod3, g, w_in, qg2, kg2, w_pool, pscale, wbrp, wbra, wout, seq):
    tm = CTX_TB * seq
    row = lambda s: (s, 0)
    heads = lambda s: (s, 0, 0)
    return pl.pallas_call(
        functools.partial(_ctx_mixer_kernel, seq=seq),
        grid=(n_tok // tm,),
        in_specs=[pl.BlockSpec((tm, D_MODEL), row),
                  _const_spec((1, N_MOD, D_MODEL)), _const_spec((1, D_MODEL)),
                  _resident_spec(w_in), _const_spec((1, LANES)), _const_spec((1, LANES)),
                  _const_spec(w_pool.shape), _const_spec((1, D_POOL)),
                  _resident_spec(wbrp), _resident_spec(wbra), _resident_spec(wout)],
        out_specs=[pl.BlockSpec((tm, D_MODEL), row),
                   pl.BlockSpec((tm, N_HEADS, HEAD_DIM), heads),
                   pl.BlockSpec((tm, N_HEADS, HEAD_DIM), heads)],
        out_shape=[jax.ShapeDtypeStruct((n_tok, D_MODEL), F32),
                   jax.ShapeDtypeStruct((n_tok, N_HEADS, HEAD_DIM), F32),
                   jax.ShapeDtypeStruct((n_tok, N_HEADS, HEAD_DIM), F32)],
        scratch_shapes=[pltpu.VMEM((D_POOL, D_POOL), BF16)],
        compiler_params=_params(),
        name="ctx_mixer",
    )(x_all, mod3, g, w_in, qg2, kg2, w_pool, pscale, wbrp, wbra, wout)


def _lat_proj_kernel(x_ref, mod_ref, g_ref, win_ref, qg_ref, kg_ref, h_ref, p_ref, q_ref, k_ref, v_ref):
    h = _rms_mod(x_ref[...], g_ref[...], mod_ref[0, 3:4, :], mod_ref[0, 4:5, :]).astype(BF16)
    h_ref[...] = h
    p_ref[...] = _dot(h, win_ref[:, 0:O_Q])
    q = _dot(h, win_ref[:, O_Q:O_K])
    k = _dot(h, win_ref[:, O_K:O_V])
    v_ref[...] = _dot(h, win_ref[:, O_V:O_G]).astype(BF16)
    scale = Q_SCALE
    for pr in range(N_PAIRS):
        lanes = slice(pr * LANES, (pr + 1) * LANES)
        q_ref[:, lanes] = (_head_norm(q[:, lanes], qg_ref[...]) * scale).astype(BF16)
        k_ref[:, lanes] = _head_norm(k[:, lanes], kg_ref[...]).astype(BF16)


def _lat_proj(x_all, tok0, n_tok, seq, mod3, g, w_in, qg2, kg2):
    tm = PROJ_TM
    tile0 = tok0 // tm
    row = lambda s: (s, 0)
    shapes = [(D_MODEL, BF16), (D_POOL, F32), (D_ATT, BF16), (D_ATT, BF16), (D_ATT, BF16)]
    return pl.pallas_call(
        _lat_proj_kernel,
        grid=(n_tok // tm,),
        in_specs=[pl.BlockSpec((tm, D_MODEL), lambda s: (tile0 + s, 0)),
                  pl.BlockSpec((1, N_MOD, D_MODEL), lambda s: (1 + (s * tm) // seq, 0, 0)),
                  _const_spec((1, D_MODEL)), _resident_spec(w_in, (O_G, 0)),
                  _const_spec((1, LANES)), _const_spec((1, LANES))],
        out_specs=[pl.BlockSpec((tm, width), row) for width, _ in shapes],
        out_shape=[jax.ShapeDtypeStruct((n_tok, width), dtype) for width, dtype in shapes],
        compiler_params=_params(),
        name="lat_proj",
    )(x_all, mod3, g, w_in, qg2, kg2)


def _build_bias_table(rpb_ref, bias_ref):
    lane = lax.broadcasted_iota(jnp.int32, (GRID_W, LANES), 1)
    qc = lax.broadcasted_iota(jnp.int32, (GRID_W, LANES), 0)
    kc = lane & (GRID_W - 1)
    c0 = jnp.clip(qc - NA_KW // 2, 0, GRID_W - NA_KW)
    col_ok = (kc >= c0) & (kc < c0 + NA_KW)
    first_half = lane < GRID_W
    shift_a = LANES - (NA_KW - 1)
    shift_b = (GRID_W - (NA_KW - 1)) % LANES
    for h in range(N_HEADS):
        for i in range(2 * NA_KH - 2):
            va = jnp.broadcast_to(rpb_ref[h, i:i + 1, :], (GRID_W, LANES))
            vb = jnp.broadcast_to(rpb_ref[h, i + 1:i + 2, :], (GRID_W, LANES))
            ra = pltpu.roll(va, shift_a, 1, stride=1, stride_axis=0)
            rb = pltpu.roll(vb, shift_b, 1, stride=1, stride_axis=0)
            bias_ref[h, i] = jnp.where(col_ok, jnp.where(first_half, ra, rb), NEG)


def _lat_attention(qn, j, fillers, k_ref, v_ref, ck_ref, cv_ref, bias_ref, *, n_rows):
    n_keys = LAT_KEY_ROWS * GRID_W
    r0q = j * LAT_ROWS
    u0 = jnp.clip(r0q - NA_KH // 2, 0, n_rows - LAT_KEY_ROWS)
    key0 = pl.multiple_of(u0 * GRID_W, GRID_W)
    log2_w = GRID_W.bit_length() - 1
    zero = jnp.zeros((), BF16)
    one = jnp.ones((), BF16)

    lane_row = lax.broadcasted_iota(jnp.int32, (1, LANES), 1) & (HEAD_DIM - 1)
    q_tile_row = jnp.right_shift(lax.broadcasted_iota(jnp.int32, (LAT_TM, 1), 0), log2_w)
    q_hot = jnp.where(lane_row == q_tile_row, 1.0, 0.0).astype(BF16)
    k_row = u0 + jnp.right_shift(lax.broadcasted_iota(jnp.int32, (n_keys, 1), 0), log2_w)
    first = jnp.clip(r0q + lane_row - NA_KH // 2, 0, n_rows - NA_KH)
    outside = (lane_row < LAT_ROWS) & ((k_row < first) | (k_row >= first + NA_KH))
    k_mask = jnp.where(outside, NEG, 0.0).astype(BF16)

    def scores(head):
        lanes = slice((head // 2) * LANES, (head // 2 + 1) * LANES)
        qp = qn[:, lanes]
        s_loc = _dot_nt(_own_lanes(head, qp, q_hot),
                        _own_lanes(head, k_ref[0, pl.ds(key0, n_keys), lanes], k_mask))
        return s_loc, _dot_nt(_own_lanes(head, qp, zero), ck_ref[0, :, lanes])

    def weights(head, s_loc, s_ctx):
        bias_rows = []
        for jr in range(LAT_ROWS):
            blocks = []
            for pi in range(LAT_KEY_ROWS // 2):
                idx = jnp.clip(u0 - r0q + 2 * pi - jr + NA_KH - 1, 0, 2 * NA_KH - 3)
                blocks.append(bias_ref[head, idx])
            bias_rows.append(jnp.concatenate(blocks, axis=1))
        s_loc = s_loc + jnp.concatenate(bias_rows, axis=0)
        m = jnp.maximum(s_loc.max(axis=-1, keepdims=True), s_ctx.max(axis=-1, keepdims=True))
        return jnp.concatenate([jnp.exp(s_loc - m).astype(BF16), jnp.exp(s_ctx - m).astype(BF16)], axis=1)

    def weighted_values(head, e):
        lanes = slice((head // 2) * LANES, (head // 2 + 1) * LANES)
        values = jnp.concatenate([_own_lanes(head, v_ref[0, pl.ds(key0, n_keys), lanes], one),
                                  _own_lanes(head, cv_ref[0, :, lanes], one)], axis=0)
        return _dot(e, values)

    return _pipelined_heads(scores, weights, weighted_values, fillers)


def _lat_mixer_kernel(x_ref, h_ref, q_ref, p_ref, pp_ref, pn_ref, mod_ref, k_ref, v_ref, ck_ref, cv_ref,
                      rpb_ref, wg_ref, wpool32_ref, pscale_ref, wbrp_ref, wbra_ref, wout_ref,
                      o_ref, bias_ref, wpool_ref, *, seq, n_rows):
    step = pl.program_id(0)
    tm = LAT_TILES * LAT_TM

    @pl.when(step == 0)
    def _first_step():
        _build_pool_weight(wpool32_ref, wpool_ref)
        _build_bias_table(rpb_ref, bias_ref)

    blk = lax.rem(step, seq // tm)
    st = {}

    def pool_delta(g):
        lanes = slice(g * LANES, (g + 1) * LANES)
        pe = jnp.concatenate([pp_ref[:, lanes], p_ref[:, lanes], pn_ref[:, lanes]], axis=0)
        pos = blk * tm - POOL_HALO + lax.broadcasted_iota(jnp.int32, (tm + 2 * POOL_HALO, 1), 0)
        return _pool_delta(jnp.where((pos >= 0) & (pos < seq), pe, 0.0), g, blk * tm, seq)

    block_gates = []

    def gate_fillers(rows, gates):
        if not block_gates:
            fillers = _gate_fillers(block_gates, lambda: h_ref[...], lambda c0, c1: wg_ref[:, c0:c1])
            return [lambda f=f: (f(), gates.append(block_gates[-1][rows])) for f in fillers]
        gates.extend(g[rows] for g in block_gates)
        return [lambda: None] * N_HEADS

    pool_fillers = _pool_fillers(st, pool_delta, wpool_ref, pscale_ref, wbrp_ref)

    def attention(t, fillers):
        return _lat_attention(q_ref[t * LAT_TM:(t + 1) * LAT_TM, :], blk * LAT_TILES + t, fillers,
                              k_ref, v_ref, ck_ref, cv_ref, bias_ref, n_rows=n_rows)

    merge = functools.partial(_merge_fillers, st, x_ref=x_ref, o_ref=o_ref, gate2=mod_ref[0, 5:6, :],
                              wbra_ref=wbra_ref, wout_ref=wout_ref)
    _attend_tiles(LAT_TILES, LAT_TM, attention, gate_fillers, pool_fillers, merge)


def _lat_mixer(x_all, tok0, n_batch, seq, mod3, h, q, p, k3, v3, ck3, cv3, rpb_pad, w_in, w_pool, pscale,
               wbrp, wbra, wout):
    tm = LAT_TILES * LAT_TM
    n_rows = seq // GRID_W
    nb = seq // tm
    tile0 = tok0 // tm
    halo_per_tile = tm // POOL_HALO
    n_halo = p.shape[0] // POOL_HALO
    row = lambda s: (s, 0)
    seq_map = lambda s: (s // nb, 0, 0)
    return pl.pallas_call(
        functools.partial(_lat_mixer_kernel, seq=seq, n_rows=n_rows),
        grid=(n_batch * nb,),
        in_specs=[pl.BlockSpec((tm, D_MODEL), lambda s: (tile0 + s, 0)),
                  pl.BlockSpec((tm, D_MODEL), row),
                  pl.BlockSpec((tm, D_ATT), row),
                  pl.BlockSpec((tm, D_POOL), row),
                  pl.BlockSpec((POOL_HALO, D_POOL), lambda s: (jnp.maximum(s * halo_per_tile - 1, 0), 0)),
                  pl.BlockSpec((POOL_HALO, D_POOL),
                               lambda s: (jnp.minimum((s + 1) * halo_per_tile, n_halo - 1), 0)),
                  pl.BlockSpec((1, N_MOD, D_MODEL), lambda s: (1 + s // nb, 0, 0)),
                  pl.BlockSpec((1, seq, D_ATT), seq_map), pl.BlockSpec((1, seq, D_ATT), seq_map),
                  pl.BlockSpec((1,) + ck3.shape[1:], seq_map), pl.BlockSpec((1,) + cv3.shape[1:], seq_map),
                  _const_spec(rpb_pad.shape),
                  _resident_spec(w_in, (2 * D_MODEL, O_G // (2 * D_MODEL))),
                  _const_spec(w_pool.shape), _const_spec((1, D_POOL)),
                  _resident_spec(wbrp), _resident_spec(wbra), _resident_spec(wout)],
        out_specs=pl.BlockSpec((tm, D_MODEL), row),
        out_shape=jax.ShapeDtypeStruct((n_batch * seq, D_MODEL), F32),
        scratch_shapes=[pltpu.VMEM((N_HEADS, 2 * NA_KH - 2, GRID_W, LANES), F32),
                        pltpu.VMEM((D_POOL, D_POOL), BF16)],
        compiler_params=_params(),
        name="lat_mixer",
    )(x_all, h, q, p, p, p, mod3, k3, v3, ck3, cv3, rpb_pad, w_in, w_pool, pscale, wbrp, wbra, wout)


def kernel(x_prompt, x_sample, cache_k, cache_v, c, c_ctx, w_ada, b_ada, g_ff1, w_ff1_in, w_ff1_out,
           g_mix, w_in, q_gain, k_gain, w_pool, pool_scale, rpb, w_br_pool, w_br_att, w_out, g_ff2,
           w_ff2_in, w_ff2_out):
    n_ctx, seq_ctx, _ = x_prompt.shape
    n_lat, seq_lat, _ = x_sample.shape
    depth = w_ada.shape[0]
    assert depth == 1 and n_lat + 1 <= 8
    l = 0

    xp = x_prompt.reshape(n_ctx * seq_ctx, D_MODEL)
    xs = x_sample.reshape(n_lat * seq_lat, D_MODEL)
    t_ctx, t_lat = xp.shape[0], xs.shape[0]
    cond8 = jnp.concatenate([c_ctx[None], c, jnp.zeros((7 - n_lat, D_MODEL), F32)], axis=0)
    mod3 = _adaln(cond8, w_ada[l], b_ada[l]).reshape(8, N_MOD, D_MODEL)

    pscale = pool_scale[l].reshape(1, D_POOL)
    qg2 = jnp.tile(q_gain[l], 2).reshape(1, LANES)
    kg2 = jnp.tile(k_gain[l], 2).reshape(1, LANES)
    g1, gm, g2 = g_ff1[l].reshape(1, -1), g_mix[l].reshape(1, -1), g_ff2[l].reshape(1, -1)
    rpb_pad = jnp.pad(rpb[l], ((0, 0), (0, 0), (0, LANES - rpb.shape[-1])))
    ck3 = cache_k[:, l].reshape(n_lat, -1, D_ATT).astype(BF16)
    cv3 = cache_v[:, l].reshape(n_lat, -1, D_ATT).astype(BF16)

    x1, w2gu, w2dn, win, wbrp, wbra, wout = _ffn(
        xp, xs, mod3, seq_lat, 0, g1, w_ff1_in[l], w_ff1_out[l], "ffn1", convert=True,
        preconvert=(w_ff2_in[l], w_ff2_out[l], w_in[l], w_br_pool[l], w_br_att[l], w_out[l]))
    x2p, k_p, v_p = _ctx_mixer(x1, t_ctx, mod3, gm, win, qg2, kg2, w_pool[l], pscale, wbrp, wbra, wout, seq_ctx)
    h_s, p_s, q_s, k_s, v_s = _lat_proj(x1, t_ctx, t_lat, seq_lat, mod3, gm, win, qg2, kg2)
    x2s = _lat_mixer(x1, t_ctx, n_lat, seq_lat, mod3, h_s, q_s, p_s,
                     k_s.reshape(n_lat, seq_lat, D_ATT), v_s.reshape(n_lat, seq_lat, D_ATT), ck3, cv3,
                     rpb_pad, win, w_pool[l], pscale, wbrp, wbra, wout)
    yp, ys = _ffn(x2p, x2s, mod3, seq_lat, 6, g2, w2gu, w2dn, "ffn2", split_out=True)

    new_k = k_p.reshape(n_ctx, 1, seq_ctx, N_HEADS, HEAD_DIM)
    new_v = v_p.reshape(n_ctx, 1, seq_ctx, N_HEADS, HEAD_DIM)
    return (yp.reshape(x_prompt.shape), ys.reshape(x_sample.shape), new_k, new_v)
```

```python
import functools
import math

import jax
import jax.numpy as jnp
from jax import lax
from jax.experimental import pallas as pl
from jax.experimental.pallas import tpu as pltpu

F32 = jnp.float32
BF16 = jnp.bfloat16

D_MODEL = 1024
N_HEADS = 8
HEAD_DIM = 64
D_ATT = N_HEADS * HEAD_DIM
D_POOL = 512
POOL_WINDOWS = (2, 4, 8, 16)
POOL_HALO = 8
D_FF = 2816
N_MOD = 9
GRID_W = 64
NA_KH = 8
NA_KW = 16
EPS = 1e-6
NEG = -1e30

LANES = 128
BF16_ROWS = 16
N_PAIRS = D_ATT // LANES
MXU_COLS = 256
FF_CHUNKS = ((0, 768), (768, 768), (1536, 768), (2304, 512))
ADALN_TN = 2304
FFN_TM = 512
CTX_TB = 2
PROJ_TM = 1024
LAT_ROWS = 4
LAT_TM = LAT_ROWS * GRID_W
LAT_TILES = 2
LAT_KEY_ROWS = 12
SCORE_LOOKAHEAD = 2
N_CONV = 8
PRE_CHUNKS = 16
VMEM_LIMIT = 56 * 1024 * 1024

Q_SCALE = 1.0 / math.sqrt(HEAD_DIM)

O_Q = D_POOL
O_K = D_POOL + D_ATT
O_V = D_POOL + 2 * D_ATT
O_G = D_POOL + 3 * D_ATT


def _dot(a, b):
    return jnp.dot(a, b, preferred_element_type=F32)


def _dot_nt(a, b):
    return lax.dot_general(a, b, (((1,), (1,)), ((), ())), preferred_element_type=F32)


def _rms_mod(x, g, shift, scale):
    ms = jnp.mean(x * x, axis=-1, keepdims=True)
    return x * lax.rsqrt(ms + EPS) * (g * (1.0 + scale)) + shift


def _low_half():
    return lax.broadcasted_iota(jnp.int32, (1, LANES), 1) < HEAD_DIM


def _head_norm(t, gain2):
    lo = _low_half()
    sq = t * t
    s0 = jnp.sum(jnp.where(lo, sq, 0.0), axis=-1, keepdims=True)
    s1 = jnp.sum(jnp.where(lo, 0.0, sq), axis=-1, keepdims=True)
    r0 = lax.rsqrt(s0 * (1.0 / HEAD_DIM) + EPS)
    r1 = lax.rsqrt(s1 * (1.0 / HEAD_DIM) + EPS)
    return t * jnp.where(lo, r0, r1) * gain2


def _own_lanes(head, mine, other):
    lo = _low_half()
    return jnp.where(lo, mine, other) if head % 2 == 0 else jnp.where(lo, other, mine)


def _pipelined_heads(scores, weights, weighted_values, fillers):
    lo = _low_half()
    outs = []
    pending_scores = [scores(head) for head in range(SCORE_LOOKAHEAD)]
    pending_weights = [weights(0, *pending_scores.pop(0))]
    for head in range(N_HEADS):
        if head + SCORE_LOOKAHEAD < N_HEADS:
            pending_scores.append(scores(head + SCORE_LOOKAHEAD))
        fillers[head]()
        if head + 1 < N_HEADS:
            pending_weights.append(weights(head + 1, *pending_scores.pop(0)))
        outs.append(weighted_values(head, pending_weights.pop(0)))
    att_pairs = []
    for pr in range(N_PAIRS):
        even, odd = outs[2 * pr], outs[2 * pr + 1]
        num = jnp.where(lo, even, odd)
        den = pltpu.roll(jnp.where(lo, odd, even), HEAD_DIM, 1)
        att_pairs.append(num / den)
    return jnp.concatenate(att_pairs, axis=1)


def _pool_delta(p_ext, g, pos0, seq_len):
    w = POOL_WINDOWS[g]
    te = p_ext.shape[0]
    tm = te - 2 * POOL_HALO
    t = (pos0 + lax.broadcasted_iota(jnp.int32, (tm, 1), 0)).astype(F32)
    s = p_ext
    step = 1
    while step < w:
        s = s + pltpu.roll(s, step, 0)
        step *= 2
    ahead = w // 2 - 1
    if ahead:
        s = pltpu.roll(s, te - ahead, 0)
    win = s[POOL_HALO:POOL_HALO + tm]
    cnt = jnp.minimum(t + float(w // 2), float(seq_len)) - jnp.maximum(t - float(w // 2), 0.0)
    return win / cnt - p_ext[POOL_HALO:POOL_HALO + tm]


def _const_spec(shape):
    nd = len(shape)
    return pl.BlockSpec(shape, lambda *_: (0,) * nd)


def _resident_spec(w, col_block=None):
    width, cidx = col_block if col_block else (w.shape[1], 0)
    return pl.BlockSpec((w.shape[0], width), lambda *_: (0, cidx), pipeline_mode=pl.Buffered(1))


def _params():
    return pltpu.CompilerParams(dimension_semantics=("arbitrary",), vmem_limit_bytes=VMEM_LIMIT)


def _build_pool_weight(wpool32_ref, wpool_ref):
    wpool_ref[...] = jnp.zeros(wpool_ref.shape, BF16)
    for g in range(len(POOL_WINDOWS)):
        blk = slice(g * LANES, (g + 1) * LANES)
        wpool_ref[blk, blk] = wpool32_ref[g].astype(BF16)


def _adaln_kernel(cond_ref, wlo_ref, whi_ref, b_ref, o_ref):
    c = cond_ref[...]
    s = (c * jax.nn.sigmoid(c)).astype(BF16)
    half = wlo_ref.shape[1]
    o_ref[:, :half] = _dot(s, wlo_ref[...].astype(BF16)) + b_ref[:, :half]
    o_ref[:, half:] = _dot(s, whi_ref[...].astype(BF16)) + b_ref[:, half:]


def _adaln(cond8, w_ada, b_ada):
    n = w_ada.shape[1]
    tn = ADALN_TN
    return pl.pallas_call(
        _adaln_kernel,
        grid=(n // tn,),
        in_specs=[pl.BlockSpec((8, D_MODEL), lambda i: (0, 0)),
                  pl.BlockSpec((D_MODEL, tn // 2), lambda i: (0, 2 * i)),
                  pl.BlockSpec((D_MODEL, tn // 2), lambda i: (0, 2 * i + 1)),
                  pl.BlockSpec((1, tn), lambda i: (0, i))],
        out_specs=pl.BlockSpec((8, tn), lambda i: (0, i)),
        out_shape=jax.ShapeDtypeStruct((8, n), F32),
        compiler_params=_params(),
        name="adaln",
    )(cond8, w_ada, w_ada, b_ada.reshape(1, n))


def _ffn_kernel(*refs, mod_off, n_a, split_out, n_conv, n_pre):
    refs = list(refs)
    xa_ref, xb_ref, mod_ref, g_ref, wgu_in, wdn_in = refs[:6]
    pre_src = refs[6:6 + n_pre]
    n_out = 2 if split_out else 1
    out_refs = refs[6 + n_pre:6 + n_pre + n_out]
    pre_dst = refs[6 + n_pre + n_out:6 + 2 * n_pre + n_out]
    wgu_ref, wdn_ref = refs[6 + 2 * n_pre + n_out:] if n_conv else (wgu_in, wdn_in)
    step = pl.program_id(0)

    if n_conv:
        @pl.when(step < n_conv)
        def _convert():
            for src, dst in ((wgu_in, wgu_ref), (wdn_in, wdn_ref)):
                rows = src.shape[0]
                dst[pl.ds(pl.multiple_of(step * rows, rows), rows), :] = src[...].astype(BF16)

    tile = step - n_conv

    def half_step(x_ref, o_ref):
        shift = mod_ref[0, mod_off:mod_off + 1, :]
        scale = mod_ref[0, mod_off + 1:mod_off + 2, :]
        gate = mod_ref[0, mod_off + 2:mod_off + 3, :]
        x = x_ref[...]
        h = _rms_mod(x, g_ref[...], shift, scale).astype(BF16)
        acc = None
        for c0, cw in FF_CHUNKS:
            a = _dot(h, wgu_ref[:, c0:c0 + cw])
            u = _dot(h, wgu_ref[:, D_FF + c0:D_FF + c0 + cw])
            t = (a * jax.nn.sigmoid(a) * u).astype(BF16)
            part = _dot(t, wdn_ref[c0:c0 + cw, :])
            acc = part if acc is None else acc + part
        o_ref[...] = x + (0.5 * gate) * acc

    @pl.when((tile >= 0) & (tile < n_a))
    def _tokens_a():
        half_step(xa_ref, out_refs[0])

    @pl.when(tile >= n_a)
    def _tokens_b():
        half_step(xb_ref, out_refs[-1])

    if n_pre:
        @pl.when((tile >= 0) & (tile < PRE_CHUNKS))
        def _preconvert():
            for src, dst in zip(pre_src, pre_dst):
                dst[...] = src[...].astype(BF16)


def _ffn(xa, xb, mod3, seq_b, mod_off, g, wgu, wdn, name, split_out=False, convert=False, preconvert=()):
    tm = FFN_TM
    n_a, n_b = xa.shape[0] // tm, xb.shape[0] // tm
    n_conv = N_CONV if convert else 0
    assert n_a + n_b >= PRE_CHUNKS
    tile = lambda s: jnp.maximum(s - n_conv, 0)
    a_map = lambda s: (jnp.minimum(tile(s), n_a - 1), 0)
    b_map = lambda s: (jnp.maximum(tile(s) - n_a, 0), 0)
    mod_map = lambda s: (jnp.where(tile(s) < n_a, 0, 1 + (jnp.maximum(tile(s) - n_a, 0) * tm) // seq_b), 0, 0)

    def chunk_spec(w, n_chunks, index):
        rows = w.shape[0] // n_chunks
        assert rows * n_chunks == w.shape[0] and rows % BF16_ROWS == 0
        return pl.BlockSpec((rows, w.shape[1]), lambda s: (jnp.minimum(index(s), n_chunks - 1), 0))

    if convert:
        w_specs = [chunk_spec(wgu, N_CONV, lambda s: s), chunk_spec(wdn, N_CONV, lambda s: s)]
        scratch = [pltpu.VMEM(wgu.shape, BF16), pltpu.VMEM(wdn.shape, BF16)]
    else:
        w_specs = [_resident_spec(wgu), _resident_spec(wdn)]
        scratch = []
    pre_specs = lambda: [chunk_spec(w, PRE_CHUNKS, tile) for w in preconvert]
    if split_out:
        out_specs = [pl.BlockSpec((tm, D_MODEL), a_map), pl.BlockSpec((tm, D_MODEL), b_map)]
        out_shape = [jax.ShapeDtypeStruct(xa.shape, F32), jax.ShapeDtypeStruct(xb.shape, F32)]
    else:
        out_specs = [pl.BlockSpec((tm, D_MODEL), lambda s: (tile(s), 0))]
        out_shape = [jax.ShapeDtypeStruct((xa.shape[0] + xb.shape[0], D_MODEL), F32)]
    return pl.pallas_call(
        functools.partial(_ffn_kernel, mod_off=mod_off, n_a=n_a, split_out=split_out, n_conv=n_conv,
                          n_pre=len(preconvert)),
        grid=(n_conv + n_a + n_b,),
        in_specs=[pl.BlockSpec((tm, D_MODEL), a_map), pl.BlockSpec((tm, D_MODEL), b_map),
                  pl.BlockSpec((1, N_MOD, D_MODEL), mod_map), _const_spec((1, D_MODEL))] + w_specs + pre_specs(),
        out_specs=out_specs + pre_specs(),
        out_shape=out_shape + [jax.ShapeDtypeStruct(w.shape, BF16) for w in preconvert],
        scratch_shapes=scratch,
        compiler_params=_params(),
        name=name,
    )(xa, xb, mod3, g, wgu, wdn, *preconvert)


def _heads_to_sublanes(t):
    heads = jnp.stack([t[:, h * HEAD_DIM:(h + 1) * HEAD_DIM] for h in range(N_HEADS)], axis=0)
    return jnp.swapaxes(heads, 0, 1)


def _gate_fillers(gates, h_rows, gate_cols):
    assert 2 * D_MODEL // MXU_COLS == N_HEADS

    def gate_block():
        c0 = len(gates) * MXU_COLS
        gates.append(jax.nn.sigmoid(_dot(h_rows(), gate_cols(c0, c0 + MXU_COLS))))

    return [gate_block] * N_HEADS


def _pool_fillers(st, delta_fn, wpool_ref, pscale_ref, wbrp_ref):
    def pool_delta():
        deltas = [delta_fn(g).astype(BF16) for g in range(len(POOL_WINDOWS))]
        st["d"] = jnp.concatenate(deltas, axis=1)

    def pool_weight():
        st["y"] = (_dot(st.pop("d"), wpool_ref[...]) * pscale_ref[...]).astype(BF16)

    def pool_out():
        st["a"] = _dot(st.pop("y"), wbrp_ref[...])

    return [pool_delta, pool_weight, pool_out]


def _merge_fillers(st, gates, rows, att, x_ref, o_ref, gate2, wbra_ref, wout_ref):
    def attention_branch():
        st["b"] = _dot(att.astype(BF16), wbra_ref[...])

    def gate():
        g = jnp.concatenate(gates, axis=1)
        merged = g[:, :D_MODEL] * st["a"][rows] + g[:, D_MODEL:] * st.pop("b")
        st["merged"] = merged.astype(BF16)

    def out():
        o_ref[rows, :] = x_ref[rows, :] + gate2 * _dot(st.pop("merged"), wout_ref[...])

    return [attention_branch, gate, out]


def _attend_tiles(n_tiles, tile_rows, attention, gate_fillers, pool_fillers, merge_fillers):
    idle = lambda: None
    extra = pool_fillers + [idle] * (N_HEADS - len(pool_fillers))
    for t in range(n_tiles):
        rows = slice(t * tile_rows, (t + 1) * tile_rows)
        gates = []
        fillers = [lambda g=g, e=e: (g(), e()) for g, e in zip(gate_fillers(rows, gates), extra)]
        pieces = merge_fillers(gates, rows, attention(t, fillers))
        extra = [idle] + pieces + [idle] * (N_HEADS - 1 - len(pieces))
    for piece in pieces:
        piece()


def _ctx_mixer_kernel(x_ref, mod_ref, g_ref, win_ref, qg_ref, kg_ref, wpool32_ref, pscale_ref,
                      wbrp_ref, wbra_ref, wout_ref, o_ref, ko_ref, vo_ref, wpool_ref, *, seq):
    @pl.when(pl.program_id(0) == 0)
    def _first_step():
        _build_pool_weight(wpool32_ref, wpool_ref)

    m_rows = x_ref.shape[0]
    h = _rms_mod(x_ref[...], g_ref[...], mod_ref[0, 3:4, :], mod_ref[0, 4:5, :]).astype(BF16)
    p = _dot(h, win_ref[:, 0:O_Q])
    q = _dot(h, win_ref[:, O_Q:O_K])
    k = _dot(h, win_ref[:, O_K:O_V])
    v = _dot(h, win_ref[:, O_V:O_G])
    vo_ref[...] = _heads_to_sublanes(v)
    scale = Q_SCALE
    zero = jnp.zeros((), BF16)
    one = jnp.ones((), BF16)
    st = {}

    def pool_delta(g):
        lanes = slice(g * LANES, (g + 1) * LANES)
        zeros_halo = jnp.zeros((POOL_HALO, LANES), F32)
        return jnp.concatenate(
            [_pool_delta(jnp.concatenate([zeros_halo, p[e * seq:(e + 1) * seq, lanes], zeros_halo], axis=0),
                         g, 0, seq) for e in range(m_rows // seq)], axis=0)

    def gate_fillers(rows, gates):
        return _gate_fillers(gates, lambda: h[rows], lambda c0, c1: win_ref[:, O_G + c0:O_G + c1])

    def attention(e, fillers):
        rows = slice(e * seq, (e + 1) * seq)
        qn, kn, vb = [], [], []
        for pr in range(N_PAIRS):
            lanes = slice(pr * LANES, (pr + 1) * LANES)
            kn.append(_head_norm(k[rows, lanes], kg_ref[...]))
            qn.append((_head_norm(q[rows, lanes], qg_ref[...]) * scale).astype(BF16))
            vb.append(v[rows, lanes].astype(BF16))
        ko_ref[rows] = _heads_to_sublanes(jnp.concatenate(kn, axis=1))
        kb = [t.astype(BF16) for t in kn]

        def scores(head):
            return (_dot_nt(_own_lanes(head, qn[head // 2], zero), kb[head // 2]),)

        def weights(head, s):
            return jnp.exp(s - s.max(axis=-1, keepdims=True)).astype(BF16)

        def weighted_values(head, e_):
            return _dot(e_, _own_lanes(head, vb[head // 2], one))

        return _pipelined_heads(scores, weights, weighted_values, fillers)

    merge = functools.partial(_merge_fillers, st, x_ref=x_ref, o_ref=o_ref, gate2=mod_ref[0, 5:6, :],
                              wbra_ref=wbra_ref, wout_ref=wout_ref)
    _attend_tiles(m_rows // seq, seq, attention, gate_fillers,
                  _pool_fillers(st, pool_delta, wpool_ref, pscale_ref, wbrp_ref), merge)


def _ctx_mixer(x_all, n_tok, mod3, g, w_in, qg2, kg2, w_pool, pscale, wbrp, wbra, wout, seq):
    tm = CTX_TB * seq
    row = lambda s: (s, 0)
    heads = lambda s: (s, 0, 0)
    return pl.pallas_call(
        functools.partial(_ctx_mixer_kernel, seq=seq),
        grid=(n_tok // tm,),
        in_specs=[pl.BlockSpec((tm, D_MODEL), row),
                  _const_spec((1, N_MOD, D_MODEL)), _const_spec((1, D_MODEL)),
                  _resident_spec(w_in), _const_spec((1, LANES)), _const_spec((1, LANES)),
                  _const_spec(w_pool.shape), _const_spec((1, D_POOL)),
                  _resident_spec(wbrp), _resident_spec(wbra), _resident_spec(wout)],
        out_specs=[pl.BlockSpec((tm, D_MODEL), row),
                   pl.BlockSpec((tm, N_HEADS, HEAD_DIM), heads),
                   pl.BlockSpec((tm, N_HEADS, HEAD_DIM), heads)],
        out_shape=[jax.ShapeDtypeStruct((n_tok, D_MODEL), F32),
                   jax.ShapeDtypeStruct((n_tok, N_HEADS, HEAD_DIM), F32),
                   jax.ShapeDtypeStruct((n_tok, N_HEADS, HEAD_DIM), F32)],
        scratch_shapes=[pltpu.VMEM((D_POOL, D_POOL), BF16)],
        compiler_params=_params(),
        name="ctx_mixer",
    )(x_all, mod3, g, w_in, qg2, kg2, w_pool, pscale, wbrp, wbra, wout)


def _lat_proj_kernel(x_ref, mod_ref, g_ref, win_ref, qg_ref, kg_ref, ck_ref, cv_ref,
                     h_ref, p_ref, q_ref, k_ref, v_ref, cko_ref, cvo_ref, *, steps_per_seq):
    @pl.when(lax.rem(pl.program_id(0), steps_per_seq) == 0)
    def _cache_layout():
        for src, dst in ((ck_ref, cko_ref), (cv_ref, cvo_ref)):
            heads = jnp.swapaxes(src[0, 0], 0, 1)
            dst[0] = jnp.concatenate([heads[h] for h in range(N_HEADS)], axis=1).astype(BF16)

    h = _rms_mod(x_ref[...], g_ref[...], mod_ref[0, 3:4, :], mod_ref[0, 4:5, :]).astype(BF16)
    h_ref[...] = h
    p_ref[...] = _dot(h, win_ref[:, 0:O_Q])
    q = _dot(h, win_ref[:, O_Q:O_K])
    k = _dot(h, win_ref[:, O_K:O_V])
    v_ref[...] = _dot(h, win_ref[:, O_V:O_G]).astype(BF16)
    scale = Q_SCALE
    for pr in range(N_PAIRS):
        lanes = slice(pr * LANES, (pr + 1) * LANES)
        q_ref[:, lanes] = (_head_norm(q[:, lanes], qg_ref[...]) * scale).astype(BF16)
        k_ref[:, lanes] = _head_norm(k[:, lanes], kg_ref[...]).astype(BF16)


def _lat_proj(x_all, tok0, n_tok, seq, mod3, g, w_in, qg2, kg2, cache_k, cache_v, layer):
    tm = PROJ_TM
    assert seq % tm == 0
    tile0 = tok0 // tm
    row = lambda s: (s, 0)
    n_cache = cache_k.shape[2]
    cache_in = pl.BlockSpec((1, 1) + cache_k.shape[2:], lambda s: ((s * tm) // seq, layer, 0, 0, 0))
    cache_out = pl.BlockSpec((1, n_cache, D_ATT), lambda s: ((s * tm) // seq, 0, 0))
    cache_shape = jax.ShapeDtypeStruct((n_tok // seq, n_cache, D_ATT), BF16)
    shapes = [(D_MODEL, BF16), (D_POOL, F32), (D_ATT, BF16), (D_ATT, BF16), (D_ATT, BF16)]
    return pl.pallas_call(
        functools.partial(_lat_proj_kernel, steps_per_seq=seq // tm),
        grid=(n_tok // tm,),
        in_specs=[pl.BlockSpec((tm, D_MODEL), lambda s: (tile0 + s, 0)),
                  pl.BlockSpec((1, N_MOD, D_MODEL), lambda s: (1 + (s * tm) // seq, 0, 0)),
                  _const_spec((1, D_MODEL)), _resident_spec(w_in, (O_G, 0)),
                  _const_spec((1, LANES)), _const_spec((1, LANES)), cache_in, cache_in],
        out_specs=[pl.BlockSpec((tm, width), row) for width, _ in shapes] + [cache_out, cache_out],
        out_shape=[jax.ShapeDtypeStruct((n_tok, width), dtype) for width, dtype in shapes] + [cache_shape] * 2,
        compiler_params=_params(),
        name="lat_proj",
    )(x_all, mod3, g, w_in, qg2, kg2, cache_k, cache_v)


def _build_bias_table(rpb_ref, bias_ref):
    lane = lax.broadcasted_iota(jnp.int32, (GRID_W, LANES), 1)
    qc = lax.broadcasted_iota(jnp.int32, (GRID_W, LANES), 0)
    kc = lane & (GRID_W - 1)
    c0 = jnp.clip(qc - NA_KW // 2, 0, GRID_W - NA_KW)
    col_ok = (kc >= c0) & (kc < c0 + NA_KW)
    first_half = lane < GRID_W
    shift_a = LANES - (NA_KW - 1)
    shift_b = (GRID_W - (NA_KW - 1)) % LANES
    for h in range(N_HEADS):
        for i in range(2 * NA_KH - 2):
            va = jnp.broadcast_to(rpb_ref[h, i:i + 1, :], (GRID_W, LANES))
            vb = jnp.broadcast_to(rpb_ref[h, i + 1:i + 2, :], (GRID_W, LANES))
            ra = pltpu.roll(va, shift_a, 1, stride=1, stride_axis=0)
            rb = pltpu.roll(vb, shift_b, 1, stride=1, stride_axis=0)
            bias_ref[h, i] = jnp.where(col_ok, jnp.where(first_half, ra, rb), NEG)


def _lat_attention(qn, j, fillers, k_ref, v_ref, ck_ref, cv_ref, bias_ref, *, n_rows):
    n_keys = LAT_KEY_ROWS * GRID_W
    r0q = j * LAT_ROWS
    u0 = jnp.clip(r0q - NA_KH // 2, 0, n_rows - LAT_KEY_ROWS)
    key0 = pl.multiple_of(u0 * GRID_W, GRID_W)
    log2_w = GRID_W.bit_length() - 1
    zero = jnp.zeros((), BF16)
    one = jnp.ones((), BF16)

    lane_row = lax.broadcasted_iota(jnp.int32, (1, LANES), 1) & (HEAD_DIM - 1)
    q_tile_row = jnp.right_shift(lax.broadcasted_iota(jnp.int32, (LAT_TM, 1), 0), log2_w)
    q_hot = jnp.where(lane_row == q_tile_row, 1.0, 0.0).astype(BF16)
    k_row = u0 + jnp.right_shift(lax.broadcasted_iota(jnp.int32, (n_keys, 1), 0), log2_w)
    first = jnp.clip(r0q + lane_row - NA_KH // 2, 0, n_rows - NA_KH)
    outside = (lane_row < LAT_ROWS) & ((k_row < first) | (k_row >= first + NA_KH))
    k_mask = jnp.where(outside, NEG, 0.0).astype(BF16)

    def scores(head):
        lanes = slice((head // 2) * LANES, (head // 2 + 1) * LANES)
        qp = qn[:, lanes]
        s_loc = _dot_nt(_own_lanes(head, qp, q_hot),
                        _own_lanes(head, k_ref[0, pl.ds(key0, n_keys), lanes], k_mask))
        return s_loc, _dot_nt(_own_lanes(head, qp, zero), ck_ref[0, :, lanes])

    def weights(head, s_loc, s_ctx):
        bias_rows = []
        for jr in range(LAT_ROWS):
            blocks = []
            for pi in range(LAT_KEY_ROWS // 2):
                idx = jnp.clip(u0 - r0q + 2 * pi - jr + NA_KH - 1, 0, 2 * NA_KH - 3)
                blocks.append(bias_ref[head, idx])
            bias_rows.append(jnp.concatenate(blocks, axis=1))
        s_loc = s_loc + jnp.concatenate(bias_rows, axis=0)
        m = jnp.maximum(s_loc.max(axis=-1, keepdims=True), s_ctx.max(axis=-1, keepdims=True))
        return jnp.concatenate([jnp.exp(s_loc - m).astype(BF16), jnp.exp(s_ctx - m).astype(BF16)], axis=1)

    def weighted_values(head, e):
        lanes = slice((head // 2) * LANES, (head // 2 + 1) * LANES)
        values = jnp.concatenate([_own_lanes(head, v_ref[0, pl.ds(key0, n_keys), lanes], one),
                                  _own_lanes(head, cv_ref[0, :, lanes], one)], axis=0)
        return _dot(e, values)

    return _pipelined_heads(scores, weights, weighted_values, fillers)


def _lat_mixer_kernel(x_ref, h_ref, q_ref, p_ref, pp_ref, pn_ref, mod_ref, k_ref, v_ref, ck_ref, cv_ref,
                      rpb_ref, wg_ref, wpool32_ref, pscale_ref, wbrp_ref, wbra_ref, wout_ref,
                      o_ref, bias_ref, wpool_ref, *, seq, n_rows):
    step = pl.program_id(0)
    tm = LAT_TILES * LAT_TM

    @pl.when(step == 0)
    def _first_step():
        _build_pool_weight(wpool32_ref, wpool_ref)
        _build_bias_table(rpb_ref, bias_ref)

    blk = lax.rem(step, seq // tm)
    st = {}

    def pool_delta(g):
        lanes = slice(g * LANES, (g + 1) * LANES)
        pe = jnp.concatenate([pp_ref[:, lanes], p_ref[:, lanes], pn_ref[:, lanes]], axis=0)
        pos = blk * tm - POOL_HALO + lax.broadcasted_iota(jnp.int32, (tm + 2 * POOL_HALO, 1), 0)
        return _pool_delta(jnp.where((pos >= 0) & (pos < seq), pe, 0.0), g, blk * tm, seq)

    block_gates = []

    def gate_fillers(rows, gates):
        if not block_gates:
            fillers = _gate_fillers(block_gates, lambda: h_ref[...], lambda c0, c1: wg_ref[:, c0:c1])
            return [lambda f=f: (f(), gates.append(block_gates[-1][rows])) for f in fillers]
        gates.extend(g[rows] for g in block_gates)
        return [lambda: None] * N_HEADS

    pool_fillers = _pool_fillers(st, pool_delta, wpool_ref, pscale_ref, wbrp_ref)

    def attention(t, fillers):
        return _lat_attention(q_ref[t * LAT_TM:(t + 1) * LAT_TM, :], blk * LAT_TILES + t, fillers,
                              k_ref, v_ref, ck_ref, cv_ref, bias_ref, n_rows=n_rows)

    merge = functools.partial(_merge_fillers, st, x_ref=x_ref, o_ref=o_ref, gate2=mod_ref[0, 5:6, :],
                              wbra_ref=wbra_ref, wout_ref=wout_ref)
    _attend_tiles(LAT_TILES, LAT_TM, attention, gate_fillers, pool_fillers, merge)


def _lat_mixer(x_all, tok0, n_batch, seq, mod3, h, q, p, k3, v3, ck3, cv3, rpb_pad, w_in, w_pool, pscale,
               wbrp, wbra, wout):
    tm = LAT_TILES * LAT_TM
    n_rows = seq // GRID_W
    nb = seq // tm
    tile0 = tok0 // tm
    halo_per_tile = tm // POOL_HALO
    n_halo = p.shape[0] // POOL_HALO
    row = lambda s: (s, 0)
    seq_map = lambda s: (s // nb, 0, 0)
    return pl.pallas_call(
        functools.partial(_lat_mixer_kernel, seq=seq, n_rows=n_rows),
        grid=(n_batch * nb,),
        in_specs=[pl.BlockSpec((tm, D_MODEL), lambda s: (tile0 + s, 0)),
                  pl.BlockSpec((tm, D_MODEL), row),
                  pl.BlockSpec((tm, D_ATT), row),
                  pl.BlockSpec((tm, D_POOL), row),
                  pl.BlockSpec((POOL_HALO, D_POOL), lambda s: (jnp.maximum(s * halo_per_tile - 1, 0), 0)),
                  pl.BlockSpec((POOL_HALO, D_POOL),
                               lambda s: (jnp.minimum((s + 1) * halo_per_tile, n_halo - 1), 0)),
                  pl.BlockSpec((1, N_MOD, D_MODEL), lambda s: (1 + s // nb, 0, 0)),
                  pl.BlockSpec((1, seq, D_ATT), seq_map), pl.BlockSpec((1, seq, D_ATT), seq_map),
                  pl.BlockSpec((1,) + ck3.shape[1:], seq_map), pl.BlockSpec((1,) + cv3.shape[1:], seq_map),
                  _const_spec(rpb_pad.shape),
                  _resident_spec(w_in, (2 * D_MODEL, O_G // (2 * D_MODEL))),
                  _const_spec(w_pool.shape), _const_spec((1, D_POOL)),
                  _resident_spec(wbrp), _resident_spec(wbra), _resident_spec(wout)],
        out_specs=pl.BlockSpec((tm, D_MODEL), row),
        out_shape=jax.ShapeDtypeStruct((n_batch * seq, D_MODEL), F32),
        scratch_shapes=[pltpu.VMEM((N_HEADS, 2 * NA_KH - 2, GRID_W, LANES), F32),
                        pltpu.VMEM((D_POOL, D_POOL), BF16)],
        compiler_params=_params(),
        name="lat_mixer",
    )(x_all, h, q, p, p, p, mod3, k3, v3, ck3, cv3, rpb_pad, w_in, w_pool, pscale, wbrp, wbra, wout)


def kernel(x_prompt, x_sample, cache_k, cache_v, c, c_ctx, w_ada, b_ada, g_ff1, w_ff1_in, w_ff1_out,
           g_mix, w_in, q_gain, k_gain, w_pool, pool_scale, rpb, w_br_pool, w_br_att, w_out, g_ff2,
           w_ff2_in, w_ff2_out):
    n_ctx, seq_ctx, _ = x_prompt.shape
    n_lat, seq_lat, _ = x_sample.shape
    depth = w_ada.shape[0]
    assert depth == 1 and n_lat + 1 <= 8
    l = 0

    xp = x_prompt.reshape(n_ctx * seq_ctx, D_MODEL)
    xs = x_sample.reshape(n_lat * seq_lat, D_MODEL)
    t_ctx, t_lat = xp.shape[0], xs.shape[0]
    cond8 = jnp.concatenate([c_ctx[None], c, jnp.zeros((7 - n_lat, D_MODEL), F32)], axis=0)
    mod3 = _adaln(cond8, w_ada[l], b_ada[l]).reshape(8, N_MOD, D_MODEL)

    pscale = pool_scale[l].reshape(1, D_POOL)
    qg2 = jnp.tile(q_gain[l], 2).reshape(1, LANES)
    kg2 = jnp.tile(k_gain[l], 2).reshape(1, LANES)
    g1, gm, g2 = g_ff1[l].reshape(1, -1), g_mix[l].reshape(1, -1), g_ff2[l].reshape(1, -1)
    rpb_pad = jnp.pad(rpb[l], ((0, 0), (0, 0), (0, LANES - rpb.shape[-1])))

    x1, w2gu, w2dn, win, wbrp, wbra, wout = _ffn(
        xp, xs, mod3, seq_lat, 0, g1, w_ff1_in[l], w_ff1_out[l], "ffn1", convert=True,
        preconvert=(w_ff2_in[l], w_ff2_out[l], w_in[l], w_br_pool[l], w_br_att[l], w_out[l]))
    x2p, k_p, v_p = _ctx_mixer(x1, t_ctx, mod3, gm, win, qg2, kg2, w_pool[l], pscale, wbrp, wbra, wout, seq_ctx)
    h_s, p_s, q_s, k_s, v_s, ck3, cv3 = _lat_proj(x1, t_ctx, t_lat, seq_lat, mod3, gm, win, qg2, kg2,
                                                  cache_k, cache_v, l)
    x2s = _lat_mixer(x1, t_ctx, n_lat, seq_lat, mod3, h_s, q_s, p_s,
                     k_s.reshape(n_lat, seq_lat, D_ATT), v_s.reshape(n_lat, seq_lat, D_ATT), ck3, cv3,
                     rpb_pad, win, w_pool[l], pscale, wbrp, wbra, wout)
    yp, ys = _ffn(x2p, x2s, mod3, seq_lat, 6, g2, w2gu, w2dn, "ffn2", split_out=True)

    new_k = k_p.reshape(n_ctx, 1, seq_ctx, N_HEADS, HEAD_DIM)
    new_v = v_p.reshape(n_ctx, 1, seq_ctx, N_HEADS, HEAD_DIM)
    return (yp.reshape(x_prompt.shape), ys.reshape(x_sample.shape), new_k, new_v)
```

```python
import functools
import math

import jax
import jax.numpy as jnp
from jax import lax
from jax.experimental import pallas as pl
from jax.experimental.pallas import tpu as pltpu

F32 = jnp.float32
BF16 = jnp.bfloat16

D_MODEL = 1024
N_HEADS = 8
HEAD_DIM = 64
D_ATT = N_HEADS * HEAD_DIM
D_POOL = 512
POOL_WINDOWS = (2, 4, 8, 16)
POOL_HALO = 8
D_FF = 2816
N_MOD = 9
GRID_W = 64
NA_KH = 8
NA_KW = 16
EPS = 1e-6
NEG = -1e30

LANES = 128
BF16_ROWS = 16
N_PAIRS = D_ATT // LANES
MXU_COLS = 256
FF_CHUNKS = ((0, 768), (768, 768), (1536, 768), (2304, 512))
ADALN_TN = 2304
FFN_TM = 512
FFN_PRE = 256
CTX_TB = 2
PROJ_TM = 1024
LAT_ROWS = 4
LAT_TM = LAT_ROWS * GRID_W
LAT_TILES = 2
LAT_KEY_ROWS = 12
SCORE_LOOKAHEAD = 2
N_CONV = 16
PRE_CHUNKS = (16, 32)
VMEM_LIMIT = 56 * 1024 * 1024

Q_SCALE = 1.0 / math.sqrt(HEAD_DIM)

O_Q = D_POOL
O_K = D_POOL + D_ATT
O_V = D_POOL + 2 * D_ATT
O_G = D_POOL + 3 * D_ATT


def _dot(a, b):
    return jnp.dot(a, b, preferred_element_type=F32)


def _dot_nt(a, b):
    return lax.dot_general(a, b, (((1,), (1,)), ((), ())), preferred_element_type=F32)


def _rms_mod(x, g, shift, scale):
    ms = jnp.mean(x * x, axis=-1, keepdims=True)
    return x * lax.rsqrt(ms + EPS) * (g * (1.0 + scale)) + shift


def _low_half():
    return lax.broadcasted_iota(jnp.int32, (1, LANES), 1) < HEAD_DIM


def _head_norm(t, gain2):
    lo = _low_half()
    sq = t * t
    s0 = jnp.sum(jnp.where(lo, sq, 0.0), axis=-1, keepdims=True)
    s1 = jnp.sum(jnp.where(lo, 0.0, sq), axis=-1, keepdims=True)
    r0 = lax.rsqrt(s0 * (1.0 / HEAD_DIM) + EPS)
    r1 = lax.rsqrt(s1 * (1.0 / HEAD_DIM) + EPS)
    return t * jnp.where(lo, r0, r1) * gain2


def _own_lanes(head, mine, other):
    lo = _low_half()
    return jnp.where(lo, mine, other) if head % 2 == 0 else jnp.where(lo, other, mine)


def _pipelined_heads(scores, weights, weighted_values, fillers):
    lo = _low_half()
    outs = []
    pending_scores = [scores(head) for head in range(SCORE_LOOKAHEAD)]
    pending_weights = [weights(0, *pending_scores.pop(0))]
    for head in range(N_HEADS):
        if head + SCORE_LOOKAHEAD < N_HEADS:
            pending_scores.append(scores(head + SCORE_LOOKAHEAD))
        fillers[head]()
        if head + 1 < N_HEADS:
            pending_weights.append(weights(head + 1, *pending_scores.pop(0)))
        outs.append(weighted_values(head, pending_weights.pop(0)))
    att_pairs = []
    for pr in range(N_PAIRS):
        even, odd = outs[2 * pr], outs[2 * pr + 1]
        num = jnp.where(lo, even, odd)
        den = pltpu.roll(jnp.where(lo, odd, even), HEAD_DIM, 1)
        att_pairs.append(num / den)
    return jnp.concatenate(att_pairs, axis=1)


def _pool_delta(p_ext, g, pos0, seq_len):
    w = POOL_WINDOWS[g]
    te = p_ext.shape[0]
    tm = te - 2 * POOL_HALO
    t = (pos0 + lax.broadcasted_iota(jnp.int32, (tm, 1), 0)).astype(F32)
    s = p_ext
    step = 1
    while step < w:
        s = s + pltpu.roll(s, step, 0)
        step *= 2
    ahead = w // 2 - 1
    if ahead:
        s = pltpu.roll(s, te - ahead, 0)
    win = s[POOL_HALO:POOL_HALO + tm]
    cnt = jnp.minimum(t + float(w // 2), float(seq_len)) - jnp.maximum(t - float(w // 2), 0.0)
    return win / cnt - p_ext[POOL_HALO:POOL_HALO + tm]


def _const_spec(shape):
    nd = len(shape)
    return pl.BlockSpec(shape, lambda *_: (0,) * nd)


def _resident_spec(w, col_block=None):
    width, cidx = col_block if col_block else (w.shape[1], 0)
    return pl.BlockSpec((w.shape[0], width), lambda *_: (0, cidx), pipeline_mode=pl.Buffered(1))


def _params():
    return pltpu.CompilerParams(dimension_semantics=("arbitrary",), vmem_limit_bytes=VMEM_LIMIT)


def _build_pool_weight(wpool32_ref, wpool_ref):
    wpool_ref[...] = jnp.zeros(wpool_ref.shape, BF16)
    for g in range(len(POOL_WINDOWS)):
        blk = slice(g * LANES, (g + 1) * LANES)
        wpool_ref[blk, blk] = wpool32_ref[g].astype(BF16)


def _adaln_kernel(cond_ref, wlo_ref, whi_ref, b_ref, o_ref):
    c = cond_ref[...]
    s = (c * jax.nn.sigmoid(c)).astype(BF16)
    half = wlo_ref.shape[1]
    o_ref[:, :half] = _dot(s, wlo_ref[...].astype(BF16)) + b_ref[:, :half]
    o_ref[:, half:] = _dot(s, whi_ref[...].astype(BF16)) + b_ref[:, half:]


def _adaln(cond8, w_ada, b_ada):
    n = w_ada.shape[1]
    tn = ADALN_TN
    return pl.pallas_call(
        _adaln_kernel,
        grid=(n // tn,),
        in_specs=[pl.BlockSpec((8, D_MODEL), lambda i: (0, 0)),
                  pl.BlockSpec((D_MODEL, tn // 2), lambda i: (0, 2 * i)),
                  pl.BlockSpec((D_MODEL, tn // 2), lambda i: (0, 2 * i + 1)),
                  pl.BlockSpec((1, tn), lambda i: (0, i))],
        out_specs=pl.BlockSpec((8, tn), lambda i: (0, i)),
        out_shape=jax.ShapeDtypeStruct((8, n), F32),
        compiler_params=_params(),
        name="adaln",
    )(cond8, w_ada, w_ada, b_ada.reshape(1, n))


def _ffn_kernel(*refs, mod_off, n_a, split_out, n_conv, pre_chunks):
    refs = list(refs)
    xa_ref, xb_ref, xan_ref, xbn_ref, mod_ref, modn_ref, g_ref, wgu_in, wdn_in = refs[:9]
    n_pre = len(pre_chunks)
    pre_src = refs[9:9 + n_pre]
    n_out = 2 if split_out else 1
    out_refs = refs[9 + n_pre:9 + n_pre + n_out]
    pre_dst = refs[9 + n_pre + n_out:9 + 2 * n_pre + n_out]
    scratch = refs[9 + 2 * n_pre + n_out:]
    wgu_ref, wdn_ref = scratch[:2] if n_conv else (wgu_in, wdn_in)
    h_ref, a_ref, u_ref = scratch[-3:]
    step = pl.program_id(0)

    if n_conv:
        @pl.when(step < n_conv)
        def _convert():
            for src, dst in ((wgu_in, wgu_ref), (wdn_in, wdn_ref)):
                rows = src.shape[0]
                dst[pl.ds(pl.multiple_of(step * rows, rows), rows), :] = src[...].astype(BF16)

    tile = step - n_conv

    def prepare(x_ref, m_ref):
        shift = m_ref[0, mod_off:mod_off + 1, :]
        scale = m_ref[0, mod_off + 1:mod_off + 2, :]
        h = _rms_mod(x_ref[...], g_ref[...], shift, scale).astype(BF16)
        h_ref[...] = h
        a_ref[...] = _dot(h, wgu_ref[:, 0:FFN_PRE])
        u_ref[...] = _dot(h, wgu_ref[:, D_FF:D_FF + FFN_PRE])

    def half_step(x_ref, xn_ref, o_ref, first):
        @pl.when(first)
        def _first_tile():
            prepare(x_ref, mod_ref)

        gate = mod_ref[0, mod_off + 2:mod_off + 3, :]
        acc = None
        for ci, (c0, cw) in enumerate(FF_CHUNKS):
            if ci == 0:
                a = jnp.concatenate([a_ref[...], _dot(h_ref[...], wgu_ref[:, FFN_PRE:cw])], axis=1)
                u = jnp.concatenate([u_ref[...], _dot(h_ref[...], wgu_ref[:, D_FF + FFN_PRE:D_FF + cw])], axis=1)
            else:
                a = _dot(h_ref[...], wgu_ref[:, c0:c0 + cw])
                u = _dot(h_ref[...], wgu_ref[:, D_FF + c0:D_FF + c0 + cw])
            if ci == len(FF_CHUNKS) - 1:
                prepare(xn_ref, modn_ref)
            t = (a * jax.nn.sigmoid(a) * u).astype(BF16)
            part = _dot(t, wdn_ref[c0:c0 + cw, :])
            acc = part if acc is None else acc + part
        o_ref[...] = x_ref[...] + (0.5 * gate) * acc

    @pl.when((tile >= 0) & (tile < n_a))
    def _tokens_a():
        half_step(xa_ref, xan_ref, out_refs[0], tile == 0)

    @pl.when(tile >= n_a)
    def _tokens_b():
        half_step(xb_ref, xbn_ref, out_refs[-1], tile == n_a)

    for n_chunks in sorted(set(pre_chunks)):
        @pl.when((tile >= 0) & (tile < n_chunks))
        def _preconvert():
            for src, dst, n in zip(pre_src, pre_dst, pre_chunks):
                if n == n_chunks:
                    dst[...] = src[...].astype(BF16)


def _ffn(xa, xb, mod3, seq_b, mod_off, g, wgu, wdn, name, split_out=False, convert=False, preconvert=()):
    tm = FFN_TM
    n_a, n_b = xa.shape[0] // tm, xb.shape[0] // tm
    n_conv = N_CONV if convert else 0
    pre_chunks = tuple(max(n for n in PRE_CHUNKS if n <= n_a + n_b and w.shape[0] % (n * BF16_ROWS) == 0)
                       for w in preconvert)
    tile =lambda s: jnp.maximum(s - n_conv, 0)
    a_map = lambda s: (jnp.minimum(tile(s), n_a - 1), 0)
    b_map = lambda s: (jnp.maximum(tile(s) - n_a, 0), 0)
    mod_map = lambda s: (jnp.where(tile(s) < n_a, 0, 1 + (jnp.maximum(tile(s) - n_a, 0) * tm) // seq_b), 0, 0)
    following = lambda index: (lambda s: index(jnp.minimum(s + 1, n_conv + n_a + n_b - 1)))

    def chunk_spec(w, n_chunks, index):
        rows = w.shape[0] // n_chunks
        assert rows * n_chunks == w.shape[0] and rows % BF16_ROWS == 0
        return pl.BlockSpec((rows, w.shape[1]), lambda s: (jnp.minimum(index(s), n_chunks - 1), 0))

    if convert:
        w_specs = [chunk_spec(wgu, N_CONV, lambda s: s), chunk_spec(wdn, N_CONV, lambda s: s)]
        scratch = [pltpu.VMEM(wgu.shape, BF16), pltpu.VMEM(wdn.shape, BF16)]
    else:
        w_specs = [_resident_spec(wgu), _resident_spec(wdn)]
        scratch = []
    scratch += [pltpu.VMEM((tm, D_MODEL), BF16), pltpu.VMEM((tm, FFN_PRE), F32), pltpu.VMEM((tm, FFN_PRE), F32)]
    pre_specs = lambda: [chunk_spec(w, n, tile) for w, n in zip(preconvert, pre_chunks)]
    if split_out:
        out_specs = [pl.BlockSpec((tm, D_MODEL), a_map), pl.BlockSpec((tm, D_MODEL), b_map)]
        out_shape = [jax.ShapeDtypeStruct(xa.shape, F32), jax.ShapeDtypeStruct(xb.shape, F32)]
    else:
        out_specs = [pl.BlockSpec((tm, D_MODEL), lambda s: (tile(s), 0))]
        out_shape = [jax.ShapeDtypeStruct((xa.shape[0] + xb.shape[0], D_MODEL), F32)]
    return pl.pallas_call(
        functools.partial(_ffn_kernel, mod_off=mod_off, n_a=n_a, split_out=split_out, n_conv=n_conv,
                          pre_chunks=pre_chunks),
        grid=(n_conv + n_a + n_b,),
        in_specs=[pl.BlockSpec((tm, D_MODEL), a_map), pl.BlockSpec((tm, D_MODEL), b_map),
                  pl.BlockSpec((tm, D_MODEL), following(a_map)), pl.BlockSpec((tm, D_MODEL), following(b_map)),
                  pl.BlockSpec((1, N_MOD, D_MODEL), mod_map), pl.BlockSpec((1, N_MOD, D_MODEL), following(mod_map)),
                  _const_spec((1, D_MODEL))] + w_specs + pre_specs(),
        out_specs=out_specs + pre_specs(),
        out_shape=out_shape + [jax.ShapeDtypeStruct(w.shape, BF16) for w in preconvert],
        scratch_shapes=scratch,
        compiler_params=_params(),
        name=name,
    )(xa, xb, xa, xb, mod3, mod3, g, wgu, wdn, *preconvert)


def _heads_to_sublanes(t):
    heads = jnp.stack([t[:, h * HEAD_DIM:(h + 1) * HEAD_DIM] for h in range(N_HEADS)], axis=0)
    return jnp.swapaxes(heads, 0, 1)


def _gate_fillers(gates, h_rows, gate_cols):
    assert 2 * D_MODEL // MXU_COLS == N_HEADS

    def gate_block():
        c0 = len(gates) * MXU_COLS
        gates.append(jax.nn.sigmoid(_dot(h_rows(), gate_cols(c0, c0 + MXU_COLS))))

    return [gate_block] * N_HEADS


def _pool_fillers(st, delta_fn, wpool_ref, pscale_ref, wbrp_ref):
    def pool_delta():
        deltas = [delta_fn(g).astype(BF16) for g in range(len(POOL_WINDOWS))]
        st["d"] = jnp.concatenate(deltas, axis=1)

    def pool_weight():
        st["y"] = (_dot(st.pop("d"), wpool_ref[...]) * pscale_ref[...]).astype(BF16)

    def pool_out():
        st["a"] = _dot(st.pop("y"), wbrp_ref[...])

    return [pool_delta, pool_weight, pool_out]


def _merge_fillers(st, gates, rows, att, x_ref, o_ref, gate2, wbra_ref, wout_ref):
    def attention_branch():
        st["b"] = _dot(att.astype(BF16), wbra_ref[...])

    def gate():
        g = jnp.concatenate(gates, axis=1)
        merged = g[:, :D_MODEL] * st["a"][rows] + g[:, D_MODEL:] * st.pop("b")
        st["merged"] = merged.astype(BF16)

    def out():
        o_ref[rows, :] = x_ref[rows, :] + gate2 * _dot(st.pop("merged"), wout_ref[...])

    return [attention_branch, gate, out]


def _attend_tiles(n_tiles, tile_rows, attention, gate_fillers, pool_fillers, merge_fillers):
    idle = lambda: None
    extra = pool_fillers + [idle] * (N_HEADS - len(pool_fillers))
    for t in range(n_tiles):
        rows = slice(t * tile_rows, (t + 1) * tile_rows)
        gates = []
        fillers = [lambda g=g, e=e: (g(), e()) for g, e in zip(gate_fillers(rows, gates), extra)]
        pieces = merge_fillers(gates, rows, attention(t, fillers))
        extra = [idle] + pieces + [idle] * (N_HEADS - 1 - len(pieces))
    for piece in pieces:
        piece()


def _ctx_mixer_kernel(x_ref, mod_ref, g_ref, win_ref, qg_ref, kg_ref, wpool32_ref, pscale_ref,
                      wbrp_ref, wbra_ref, wout_ref, o_ref, ko_ref, vo_ref, wpool_ref, *, seq):
    @pl.when(pl.program_id(0) == 0)
    def _first_step():
        _build_pool_weight(wpool32_ref, wpool_ref)

    m_rows = x_ref.shape[0]
    h = _rms_mod(x_ref[...], g_ref[...], mod_ref[0, 3:4, :], mod_ref[0, 4:5, :]).astype(BF16)
    p = _dot(h, win_ref[:, 0:O_Q])
    q = _dot(h, win_ref[:, O_Q:O_K])
    k = _dot(h, win_ref[:, O_K:O_V])
    v = _dot(h, win_ref[:, O_V:O_G])
    vo_ref[...] = _heads_to_sublanes(v)
    scale = Q_SCALE
    zero = jnp.zeros((), BF16)
    one = jnp.ones((), BF16)
    st = {}

    def pool_delta(g):
        lanes = slice(g * LANES, (g + 1) * LANES)
        zeros_halo = jnp.zeros((POOL_HALO, LANES), F32)
        return jnp.concatenate(
            [_pool_delta(jnp.concatenate([zeros_halo, p[e * seq:(e + 1) * seq, lanes], zeros_halo], axis=0),
                         g, 0, seq) for e in range(m_rows // seq)], axis=0)

    def gate_fillers(rows, gates):
        return _gate_fillers(gates, lambda: h[rows], lambda c0, c1: win_ref[:, O_G + c0:O_G + c1])

    def attention(e, fillers):
        rows = slice(e * seq, (e + 1) * seq)
        qn, kn, vb = [], [], []
        for pr in range(N_PAIRS):
            lanes = slice(pr * LANES, (pr + 1) * LANES)
            kn.append(_head_norm(k[rows, lanes], kg_ref[...]))
            qn.append((_head_norm(q[rows, lanes], qg_ref[...]) * scale).astype(BF16))
            vb.append(v[rows, lanes].astype(BF16))
        ko_ref[rows] = _heads_to_sublanes(jnp.concatenate(kn, axis=1))
        kb = [t.astype(BF16) for t in kn]

        def scores(head):
            return (_dot_nt(_own_lanes(head, qn[head // 2], zero), kb[head // 2]),)

        def weights(head, s):
            return jnp.exp(s - s.max(axis=-1, keepdims=True)).astype(BF16)

        def weighted_values(head, e_):
            return _dot(e_, _own_lanes(head, vb[head // 2], one))

        return _pipelined_heads(scores, weights, weighted_values, fillers)

    merge = functools.partial(_merge_fillers, st, x_ref=x_ref, o_ref=o_ref, gate2=mod_ref[0, 5:6, :],
                              wbra_ref=wbra_ref, wout_ref=wout_ref)
    _attend_tiles(m_rows // seq, seq, attention, gate_fillers,
                  _pool_fillers(st, pool_delta, wpool_ref, pscale_ref, wbrp_ref), merge)


def _ctx_mixer(x_all, n_tok, mod3, g, w_in, qg2, kg2, w_pool, pscale, wbrp, wbra, wout, seq):
    tm = CTX_TB * seq
    row = lambda s: (s, 0)
    heads = lambda s: (s, 0, 0)
    return pl.pallas_call(
        functools.partial(_ctx_mixer_kernel, seq=seq),
        grid=(n_tok // tm,),
        in_specs=[pl.BlockSpec((tm, D_MODEL), row),
                  _const_spec((1, N_MOD, D_MODEL)), _const_spec((1, D_MODEL)),
                  _resident_spec(w_in), _const_spec((1, LANES)), _const_spec((1, LANES)),
                  _const_spec(w_pool.shape), _const_spec((1, D_POOL)),
                  _resident_spec(wbrp), _resident_spec(wbra), _resident_spec(wout)],
        out_specs=[pl.BlockSpec((tm, D_MODEL), row),
                   pl.BlockSpec((tm, N_HEADS, HEAD_DIM), heads),
                   pl.BlockSpec((tm, N_HEADS, HEAD_DIM), heads)],
        out_shape=[jax.ShapeDtypeStruct((n_tok, D_MODEL), F32),
                   jax.ShapeDtypeStruct((n_tok, N_HEADS, HEAD_DIM), F32),
                   jax.ShapeDtypeStruct((n_tok, N_HEADS, HEAD_DIM), F32)],
        scratch_shapes=[pltpu.VMEM((D_POOL, D_POOL), BF16)],
        compiler_params=_params(),
        name="ctx_mixer",
    )(x_all, mod3, g, w_in, qg2, kg2, w_pool, pscale, wbrp, wbra, wout)


def _lat_proj_kernel(x_ref, mod_ref, g_ref, win_ref, qg_ref, kg_ref, h_ref, p_ref, q_ref, k_ref, v_ref):
    h = _rms_mod(x_ref[...], g_ref[...], mod_ref[0, 3:4, :], mod_ref[0, 4:5, :]).astype(BF16)
    h_ref[...] = h
    p_ref[...] = _dot(h, win_ref[:, 0:O_Q])
    q = _dot(h, win_ref[:, O_Q:O_K])
    k = _dot(h, win_ref[:, O_K:O_V])
    v_ref[...] = _dot(h, win_ref[:, O_V:O_G]).astype(BF16)
    scale = Q_SCALE
    for pr in range(N_PAIRS):
        lanes = slice(pr * LANES, (pr + 1) * LANES)
        q_ref[:, lanes] = (_head_norm(q[:, lanes], qg_ref[...]) * scale).astype(BF16)
        k_ref[:, lanes] = _head_norm(k[:, lanes], kg_ref[...]).astype(BF16)


def _lat_proj(x_all, tok0, n_tok, seq, mod3, g, w_in, qg2, kg2):
    tm = PROJ_TM
    tile0 = tok0 // tm
    row = lambda s: (s, 0)
    shapes = [(D_MODEL, BF16), (D_POOL, F32), (D_ATT, BF16), (D_ATT, BF16), (D_ATT, BF16)]
    return pl.pallas_call(
        _lat_proj_kernel,
        grid=(n_tok // tm,),
        in_specs=[pl.BlockSpec((tm, D_MODEL), lambda s: (tile0 + s, 0)),
                  pl.BlockSpec((1, N_MOD, D_MODEL), lambda s: (1 + (s * tm) // seq, 0, 0)),
                  _const_spec((1, D_MODEL)), _resident_spec(w_in, (O_G, 0)),
                  _const_spec((1, LANES)), _const_spec((1, LANES))],
        out_specs=[pl.BlockSpec((tm, width), row) for width, _ in shapes],
        out_shape=[jax.ShapeDtypeStruct((n_tok, width), dtype) for width, dtype in shapes],
        compiler_params=_params(),
        name="lat_proj",
    )(x_all, mod3, g, w_in, qg2, kg2)


def _build_bias_table(rpb_ref, bias_ref):
    lane = lax.broadcasted_iota(jnp.int32, (GRID_W, LANES), 1)
    qc = lax.broadcasted_iota(jnp.int32, (GRID_W, LANES), 0)
    kc = lane & (GRID_W - 1)
    c0 = jnp.clip(qc - NA_KW // 2, 0, GRID_W - NA_KW)
    col_ok = (kc >= c0) & (kc < c0 + NA_KW)
    first_half = lane < GRID_W
    shift_a = LANES - (NA_KW - 1)
    shift_b = (GRID_W - (NA_KW - 1)) % LANES
    for h in range(N_HEADS):
        for i in range(2 * NA_KH - 2):
            va = jnp.broadcast_to(rpb_ref[h, i:i + 1, :], (GRID_W, LANES))
            vb = jnp.broadcast_to(rpb_ref[h, i + 1:i + 2, :], (GRID_W, LANES))
            ra = pltpu.roll(va, shift_a, 1, stride=1, stride_axis=0)
            rb = pltpu.roll(vb, shift_b, 1, stride=1, stride_axis=0)
            bias_ref[h, i] = jnp.where(col_ok, jnp.where(first_half, ra, rb), NEG)


def _lat_attention(qn, j, fillers, k_ref, v_ref, ck_ref, cv_ref, bias_ref, *, n_rows):
    n_keys = LAT_KEY_ROWS * GRID_W
    r0q = j * LAT_ROWS
    u0 = jnp.clip(r0q - NA_KH // 2, 0, n_rows - LAT_KEY_ROWS)
    key0 = pl.multiple_of(u0 * GRID_W, GRID_W)
    log2_w = GRID_W.bit_length() - 1
    zero = jnp.zeros((), BF16)
    one = jnp.ones((), BF16)

    lane_row = lax.broadcasted_iota(jnp.int32, (1, LANES), 1) & (HEAD_DIM - 1)
    q_tile_row = jnp.right_shift(lax.broadcasted_iota(jnp.int32, (LAT_TM, 1), 0), log2_w)
    q_hot = jnp.where(lane_row == q_tile_row, 1.0, 0.0).astype(BF16)
    k_row = u0 + jnp.right_shift(lax.broadcasted_iota(jnp.int32, (n_keys, 1), 0), log2_w)
    first = jnp.clip(r0q + lane_row - NA_KH // 2, 0, n_rows - NA_KH)
    outside = (lane_row < LAT_ROWS) & ((k_row < first) | (k_row >= first + NA_KH))
    k_mask = jnp.where(outside, NEG, 0.0).astype(BF16)

    def scores(head):
        lanes = slice((head // 2) * LANES, (head // 2 + 1) * LANES)
        qp = qn[:, lanes]
        s_loc = _dot_nt(_own_lanes(head, qp, q_hot),
                        _own_lanes(head, k_ref[0, pl.ds(key0, n_keys), lanes], k_mask))
        return s_loc, _dot_nt(_own_lanes(head, qp, zero), ck_ref[0, :, lanes])

    def weights(head, s_loc, s_ctx):
        bias_rows = []
        for jr in range(LAT_ROWS):
            blocks = []
            for pi in range(LAT_KEY_ROWS // 2):
                idx = jnp.clip(u0 - r0q + 2 * pi - jr + NA_KH - 1, 0, 2 * NA_KH - 3)
                blocks.append(bias_ref[head, idx])
            bias_rows.append(jnp.concatenate(blocks, axis=1))
        s_loc = s_loc + jnp.concatenate(bias_rows, axis=0)
        m = jnp.maximum(s_loc.max(axis=-1, keepdims=True), s_ctx.max(axis=-1, keepdims=True))
        return jnp.concatenate([jnp.exp(s_loc - m).astype(BF16), jnp.exp(s_ctx - m).astype(BF16)], axis=1)

    def weighted_values(head, e):
        lanes = slice((head // 2) * LANES, (head // 2 + 1) * LANES)
        values = jnp.concatenate([_own_lanes(head, v_ref[0, pl.ds(key0, n_keys), lanes], one),
                                  _own_lanes(head, cv_ref[0, :, lanes], one)], axis=0)
        return _dot(e, values)

    return _pipelined_heads(scores, weights, weighted_values, fillers)


def _lat_mixer_kernel(x_ref, h_ref, q_ref, p_ref, pp_ref, pn_ref, mod_ref, k_ref, v_ref, ck_ref, cv_ref,
                      rpb_ref, wg_ref, wpool32_ref, pscale_ref, wbrp_ref, wbra_ref, wout_ref,
                      o_ref, bias_ref, wpool_ref, *, seq, n_rows):
    step = pl.program_id(0)
    tm = LAT_TILES * LAT_TM

    @pl.when(step == 0)
    def _first_step():
        _build_pool_weight(wpool32_ref, wpool_ref)
        _build_bias_table(rpb_ref, bias_ref)

    blk = lax.rem(step, seq // tm)
    st = {}

    def pool_delta(g):
        lanes = slice(g * LANES, (g + 1) * LANES)
        pe = jnp.concatenate([pp_ref[:, lanes], p_ref[:, lanes], pn_ref[:, lanes]], axis=0)
        pos = blk * tm - POOL_HALO + lax.broadcasted_iota(jnp.int32, (tm + 2 * POOL_HALO, 1), 0)
        return _pool_delta(jnp.where((pos >= 0) & (pos < seq), pe, 0.0), g, blk * tm, seq)

    block_gates = []

    def gate_fillers(rows, gates):
        if not block_gates:
            fillers = _gate_fillers(block_gates, lambda: h_ref[...], lambda c0, c1: wg_ref[:, c0:c1])
            return [lambda f=f: (f(), gates.append(block_gates[-1][rows])) for f in fillers]
        gates.extend(g[rows] for g in block_gates)
        return [lambda: None] * N_HEADS

    pool_fillers = _pool_fillers(st, pool_delta, wpool_ref, pscale_ref, wbrp_ref)

    def attention(t, fillers):
        return _lat_attention(q_ref[t * LAT_TM:(t + 1) * LAT_TM, :], blk * LAT_TILES + t, fillers,
                              k_ref, v_ref, ck_ref, cv_ref, bias_ref, n_rows=n_rows)

    merge = functools.partial(_merge_fillers, st, x_ref=x_ref, o_ref=o_ref, gate2=mod_ref[0, 5:6, :],
                              wbra_ref=wbra_ref, wout_ref=wout_ref)
    _attend_tiles(LAT_TILES, LAT_TM, attention, gate_fillers, pool_fillers, merge)


def _lat_mixer(x_all, tok0, n_batch, seq, mod3, h, q, p, k3, v3, ck3, cv3, rpb_pad, w_in, w_pool, pscale,
               wbrp, wbra, wout):
    tm = LAT_TILES * LAT_TM
    n_rows = seq // GRID_W
    nb = seq // tm
    tile0 = tok0 // tm
    halo_per_tile = tm // POOL_HALO
    n_halo = p.shape[0] // POOL_HALO
    row = lambda s: (s, 0)
    seq_map = lambda s: (s // nb, 0, 0)
    return pl.pallas_call(
        functools.partial(_lat_mixer_kernel, seq=seq, n_rows=n_rows),
        grid=(n_batch * nb,),
        in_specs=[pl.BlockSpec((tm, D_MODEL), lambda s: (tile0 + s, 0)),
                  pl.BlockSpec((tm, D_MODEL), row),
                  pl.BlockSpec((tm, D_ATT), row),
                  pl.BlockSpec((tm, D_POOL), row),
                  pl.BlockSpec((POOL_HALO, D_POOL), lambda s: (jnp.maximum(s * halo_per_tile - 1, 0), 0)),
                  pl.BlockSpec((POOL_HALO, D_POOL),
                               lambda s: (jnp.minimum((s + 1) * halo_per_tile, n_halo - 1), 0)),
                  pl.BlockSpec((1, N_MOD, D_MODEL), lambda s: (1 + s // nb, 0, 0)),
                  pl.BlockSpec((1, seq, D_ATT), seq_map), pl.BlockSpec((1, seq, D_ATT), seq_map),
                  pl.BlockSpec((1,) + ck3.shape[1:], seq_map), pl.BlockSpec((1,) + cv3.shape[1:], seq_map),
                  _const_spec(rpb_pad.shape),
                  _resident_spec(w_in, (2 * D_MODEL, O_G // (2 * D_MODEL))),
                  _const_spec(w_pool.shape), _const_spec((1, D_POOL)),
                  _resident_spec(wbrp), _resident_spec(wbra), _resident_spec(wout)],
        out_specs=pl.BlockSpec((tm, D_MODEL), row),
        out_shape=jax.ShapeDtypeStruct((n_batch * seq, D_MODEL), F32),
        scratch_shapes=[pltpu.VMEM((N_HEADS, 2 * NA_KH - 2, GRID_W, LANES), F32),
                        pltpu.VMEM((D_POOL, D_POOL), BF16)],
        compiler_params=_params(),
        name="lat_mixer",
    )(x_all, h, q, p, p, p, mod3, k3, v3, ck3, cv3, rpb_pad, w_in, w_pool, pscale, wbrp, wbra, wout)


def kernel(x_prompt, x_sample, cache_k, cache_v, c, c_ctx, w_ada, b_ada, g_ff1, w_ff1_in, w_ff1_out,
           g_mix, w_in, q_gain, k_gain, w_pool, pool_scale, rpb, w_br_pool, w_br_att, w_out, g_ff2,
           w_ff2_in, w_ff2_out):
    n_ctx, seq_ctx, _ = x_prompt.shape
    n_lat, seq_lat, _ = x_sample.shape
    depth = w_ada.shape[0]
    assert depth == 1 and n_lat + 1 <= 8
    l = 0

    xp = x_prompt.reshape(n_ctx * seq_ctx, D_MODEL)
    xs = x_sample.reshape(n_lat * seq_lat, D_MODEL)
    t_ctx, t_lat = xp.shape[0], xs.shape[0]
    cond8 = jnp.concatenate([c_ctx[None], c, jnp.zeros((7 - n_lat, D_MODEL), F32)], axis=0)
    mod3 = _adaln(cond8, w_ada[l], b_ada[l]).reshape(8, N_MOD, D_MODEL)

    pscale = pool_scale[l].reshape(1, D_POOL)
    qg2 = jnp.tile(q_gain[l], 2).reshape(1, LANES)
    kg2 = jnp.tile(k_gain[l], 2).reshape(1, LANES)
    g1, gm, g2 = g_ff1[l].reshape(1, -1), g_mix[l].reshape(1, -1), g_ff2[l].reshape(1, -1)
    rpb_pad = jnp.pad(rpb[l], ((0, 0), (0, 0), (0, LANES - rpb.shape[-1])))
    ck3 = cache_k[:, l].reshape(n_lat, -1, D_ATT).astype(BF16)
    cv3 = cache_v[:, l].reshape(n_lat, -1, D_ATT).astype(BF16)

    x1, w2gu, w2dn, win, wbrp, wbra, wout = _ffn(
        xp, xs, mod3, seq_lat, 0, g1, w_ff1_in[l], w_ff1_out[l], "ffn1", convert=True,
        preconvert=(w_ff2_in[l], w_ff2_out[l], w_in[l], w_br_pool[l], w_br_att[l], w_out[l]))
    x2p, k_p, v_p = _ctx_mixer(x1, t_ctx, mod3, gm, win, qg2, kg2, w_pool[l], pscale, wbrp, wbra, wout, seq_ctx)
    h_s, p_s, q_s, k_s, v_s = _lat_proj(x1, t_ctx, t_lat, seq_lat, mod3, gm, win, qg2, kg2)
    x2s = _lat_mixer(x1, t_ctx, n_lat, seq_lat, mod3, h_s, q_s, p_s,
                     k_s.reshape(n_lat, seq_lat, D_ATT), v_s.reshape(n_lat, seq_lat, D_ATT), ck3, cv3,
                     rpb_pad, win, w_pool[l], pscale, wbrp, wbra, wout)
    yp, ys = _ffn(x2p, x2s, mod3, seq_lat, 6, g2, w2gu, w2dn, "ffn2", split_out=True)

    new_k = k_p.reshape(n_ctx, 1, seq_ctx, N_HEADS, HEAD_DIM)
    new_v = v_p.reshape(n_ctx, 1, seq_ctx, N_HEADS, HEAD_DIM)
    return (yp.reshape(x_prompt.shape), ys.reshape(x_sample.shape), new_k, new_v)
```

```python
import functools
import math

import jax
import jax.numpy as jnp
from jax import lax
from jax.experimental import pallas as pl
from jax.experimental.pallas import tpu as pltpu

F32 = jnp.float32
BF16 = jnp.bfloat16

D_MODEL = 1024
N_HEADS = 8
HEAD_DIM = 64
D_ATT = N_HEADS * HEAD_DIM
D_POOL = 512
POOL_WINDOWS = (2, 4, 8, 16)
POOL_HALO = 8
D_FF = 2816
N_MOD = 9
GRID_W = 64
NA_KH = 8
NA_KW = 16
EPS = 1e-6
NEG = -1e30

LANES = 128
BF16_ROWS = 16
N_PAIRS = D_ATT // LANES
MXU_COLS = 256
FF_CHUNKS = ((0, 768), (768, 768), (1536, 768), (2304, 512))
ADALN_TN = 2304
FFN_TM = 512
CTX_TB = 2
PROJ_TM = 1024
LAT_ROWS = 4
LAT_TM = LAT_ROWS * GRID_W
LAT_TILES = 2
LAT_KEY_ROWS = 12
SCORE_LOOKAHEAD = 2
N_CONV = 8
PRE_CHUNKS = 16
VMEM_LIMIT = 56 * 1024 * 1024

Q_SCALE = 1.0 / math.sqrt(HEAD_DIM)

O_Q = D_POOL
O_K = D_POOL + D_ATT
O_V = D_POOL + 2 * D_ATT
O_G = D_POOL + 3 * D_ATT


def _dot(a, b):
    return jnp.dot(a, b, preferred_element_type=F32)


def _dot_nt(a, b):
    return lax.dot_general(a, b, (((1,), (1,)), ((), ())), preferred_element_type=F32)


def _rms_mod(x, g, shift, scale):
    ms = jnp.mean(x * x, axis=-1, keepdims=True)
    return x * lax.rsqrt(ms + EPS) * (g * (1.0 + scale)) + shift


def _low_half():
    return lax.broadcasted_iota(jnp.int32, (1, LANES), 1) < HEAD_DIM


def _head_norm(t, gain2):
    lo = _low_half()
    sq = t * t
    s0 = jnp.sum(jnp.where(lo, sq, 0.0), axis=-1, keepdims=True)
    s1 = jnp.sum(jnp.where(lo, 0.0, sq), axis=-1, keepdims=True)
    r0 = lax.rsqrt(s0 * (1.0 / HEAD_DIM) + EPS)
    r1 = lax.rsqrt(s1 * (1.0 / HEAD_DIM) + EPS)
    return t * jnp.where(lo, r0, r1) * gain2


def _own_lanes(head, mine, other):
    lo = _low_half()
    return jnp.where(lo, mine, other) if head % 2 == 0 else jnp.where(lo, other, mine)


def _pipelined_heads(scores, weights, weighted_values, fillers):
    lo = _low_half()
    outs = []
    pending_scores = [scores(head) for head in range(SCORE_LOOKAHEAD)]
    pending_weights = [weights(0, *pending_scores.pop(0))]
    for head in range(N_HEADS):
        if head + SCORE_LOOKAHEAD < N_HEADS:
            pending_scores.append(scores(head + SCORE_LOOKAHEAD))
        fillers[head]()
        if head + 1 < N_HEADS:
            pending_weights.append(weights(head + 1, *pending_scores.pop(0)))
        outs.append(weighted_values(head, pending_weights.pop(0)))
    att_pairs = []
    for pr in range(N_PAIRS):
        even, odd = outs[2 * pr], outs[2 * pr + 1]
        num = jnp.where(lo, even, odd)
        den = pltpu.roll(jnp.where(lo, odd, even), HEAD_DIM, 1)
        att_pairs.append(num / den)
    return jnp.concatenate(att_pairs, axis=1)


def _pool_delta(p_ext, g, pos0, seq_len):
    w = POOL_WINDOWS[g]
    te = p_ext.shape[0]
    tm = te - 2 * POOL_HALO
    t = (pos0 + lax.broadcasted_iota(jnp.int32, (tm, 1), 0)).astype(F32)
    s = p_ext
    step = 1
    while step < w:
        s = s + pltpu.roll(s, step, 0)
        step *= 2
    ahead = w // 2 - 1
    if ahead:
        s = pltpu.roll(s, te - ahead, 0)
    win = s[POOL_HALO:POOL_HALO + tm]
    cnt = jnp.minimum(t + float(w // 2), float(seq_len)) - jnp.maximum(t - float(w // 2), 0.0)
    return win / cnt - p_ext[POOL_HALO:POOL_HALO + tm]


def _const_spec(shape):
    nd = len(shape)
    return pl.BlockSpec(shape, lambda *_: (0,) * nd)


def _resident_spec(w, col_block=None):
    width, cidx = col_block if col_block else (w.shape[1], 0)
    return pl.BlockSpec((w.shape[0], width), lambda *_: (0, cidx), pipeline_mode=pl.Buffered(1))


def _params():
    return pltpu.CompilerParams(dimension_semantics=("arbitrary",), vmem_limit_bytes=VMEM_LIMIT)


def _build_pool_weight(wpool32_ref, wpool_ref):
    wpool_ref[...] = jnp.zeros(wpool_ref.shape, BF16)
    for g in range(len(POOL_WINDOWS)):
        blk = slice(g * LANES, (g + 1) * LANES)
        wpool_ref[blk, blk] = wpool32_ref[g].astype(BF16)


def _adaln_kernel(cond_ref, wlo_ref, whi_ref, b_ref, o_ref):
    c = cond_ref[...]
    s = (c * jax.nn.sigmoid(c)).astype(BF16)
    half = wlo_ref.shape[1]
    o_ref[:, :half] = _dot(s, wlo_ref[...].astype(BF16)) + b_ref[:, :half]
    o_ref[:, half:] = _dot(s, whi_ref[...].astype(BF16)) + b_ref[:, half:]


def _adaln(cond8, w_ada, b_ada):
    n = w_ada.shape[1]
    tn = ADALN_TN
    return pl.pallas_call(
        _adaln_kernel,
        grid=(n // tn,),
        in_specs=[pl.BlockSpec((8, D_MODEL), lambda i: (0, 0)),
                  pl.BlockSpec((D_MODEL, tn // 2), lambda i: (0, 2 * i)),
                  pl.BlockSpec((D_MODEL, tn // 2), lambda i: (0, 2 * i + 1)),
                  pl.BlockSpec((1, tn), lambda i: (0, i))],
        out_specs=pl.BlockSpec((8, tn), lambda i: (0, i)),
        out_shape=jax.ShapeDtypeStruct((8, n), F32),
        compiler_params=_params(),
        name="adaln",
    )(cond8, w_ada, w_ada, b_ada.reshape(1, n))


def _ffn_kernel(*refs, mod_off, n_a, split_out, n_conv, n_pre):
    refs = list(refs)
    xa_ref, xb_ref, mod_ref, g_ref, wgu_in, wdn_in = refs[:6]
    pre_src = refs[6:6 + n_pre]
    n_out = 2 if split_out else 1
    out_refs = refs[6 + n_pre:6 + n_pre + n_out]
    pre_dst = refs[6 + n_pre + n_out:6 + 2 * n_pre + n_out]
    wgu_ref, wdn_ref = refs[6 + 2 * n_pre + n_out:] if n_conv else (wgu_in, wdn_in)
    step = pl.program_id(0)

    if n_conv:
        @pl.when(step < n_conv)
        def _convert():
            for src, dst in ((wgu_in, wgu_ref), (wdn_in, wdn_ref)):
                rows = src.shape[0]
                dst[pl.ds(pl.multiple_of(step * rows, rows), rows), :] = src[...].astype(BF16)

    tile = step - n_conv

    def half_step(x_ref, o_ref):
        shift = mod_ref[0, mod_off:mod_off + 1, :]
        scale = mod_ref[0, mod_off + 1:mod_off + 2, :]
        gate = mod_ref[0, mod_off + 2:mod_off + 3, :]
        x = x_ref[...]
        h = _rms_mod(x, g_ref[...], shift, scale).astype(BF16)
        acc = None
        for c0, cw in FF_CHUNKS:
            a = _dot(h, wgu_ref[:, c0:c0 + cw])
            u = _dot(h, wgu_ref[:, D_FF + c0:D_FF + c0 + cw])
            t = (a * jax.nn.sigmoid(a) * u).astype(BF16)
            part = _dot(t, wdn_ref[c0:c0 + cw, :])
            acc = part if acc is None else acc + part
        o_ref[...] = x + (0.5 * gate) * acc

    @pl.when((tile >= 0) & (tile < n_a))
    def _tokens_a():
        half_step(xa_ref, out_refs[0])

    @pl.when(tile >= n_a)
    def _tokens_b():
        half_step(xb_ref, out_refs[-1])

    if n_pre:
        @pl.when((tile >= 0) & (tile < PRE_CHUNKS))
        def _preconvert():
            for src, dst in zip(pre_src, pre_dst):
                dst[...] = src[...].astype(BF16)


def _ffn(xa, xb, mod3, seq_b, mod_off, g, wgu, wdn, name, split_out=False, convert=False, preconvert=()):
    tm = FFN_TM
    n_a, n_b = xa.shape[0] // tm, xb.shape[0] // tm
    n_conv = N_CONV if convert else 0
    assert n_a + n_b >= PRE_CHUNKS
    tile = lambda s: jnp.maximum(s - n_conv, 0)
    a_map = lambda s: (jnp.minimum(tile(s), n_a - 1), 0)
    b_map = lambda s: (jnp.maximum(tile(s) - n_a, 0), 0)
    mod_map = lambda s: (jnp.where(tile(s) < n_a, 0, 1 + (jnp.maximum(tile(s) - n_a, 0) * tm) // seq_b), 0, 0)

    def chunk_spec(w, n_chunks, index):
        rows = w.shape[0] // n_chunks
        assert rows * n_chunks == w.shape[0] and rows % BF16_ROWS == 0
        return pl.BlockSpec((rows, w.shape[1]), lambda s: (jnp.minimum(index(s), n_chunks - 1), 0))

    if convert:
        w_specs = [chunk_spec(wgu, N_CONV, lambda s: s), chunk_spec(wdn, N_CONV, lambda s: s)]
        scratch = [pltpu.VMEM(wgu.shape, BF16), pltpu.VMEM(wdn.shape, BF16)]
    else:
        w_specs = [_resident_spec(wgu), _resident_spec(wdn)]
        scratch = []
    pre_specs = lambda: [chunk_spec(w, PRE_CHUNKS, tile) for w in preconvert]
    if split_out:
        out_specs = [pl.BlockSpec((tm, D_MODEL), a_map), pl.BlockSpec((tm, D_MODEL), b_map)]
        out_shape = [jax.ShapeDtypeStruct(xa.shape, F32), jax.ShapeDtypeStruct(xb.shape, F32)]
    else:
        out_specs = [pl.BlockSpec((tm, D_MODEL), lambda s: (tile(s), 0))]
        out_shape = [jax.ShapeDtypeStruct((xa.shape[0] + xb.shape[0], D_MODEL), F32)]
    return pl.pallas_call(
        functools.partial(_ffn_kernel, mod_off=mod_off, n_a=n_a, split_out=split_out, n_conv=n_conv,
                          n_pre=len(preconvert)),
        grid=(n_conv + n_a + n_b,),
        in_specs=[pl.BlockSpec((tm, D_MODEL), a_map), pl.BlockSpec((tm, D_MODEL), b_map),
                  pl.BlockSpec((1, N_MOD, D_MODEL), mod_map), _const_spec((1, D_MODEL))] + w_specs + pre_specs(),
        out_specs=out_specs + pre_specs(),
        out_shape=out_shape + [jax.ShapeDtypeStruct(w.shape, BF16) for w in preconvert],
        scratch_shapes=scratch,
        compiler_params=_params(),
        name=name,
    )(xa, xb, mod3, g, wgu, wdn, *preconvert)


def _heads_to_sublanes(t):
    heads = jnp.stack([t[:, h * HEAD_DIM:(h + 1) * HEAD_DIM] for h in range(N_HEADS)], axis=0)
    return jnp.swapaxes(heads, 0, 1)


def _gate_fillers(gates, h_rows, gate_cols):
    assert 2 * D_MODEL // MXU_COLS == N_HEADS

    def gate_block():
        c0 = len(gates) * MXU_COLS
        gates.append(jax.nn.sigmoid(_dot(h_rows(), gate_cols(c0, c0 + MXU_COLS))))

    return [gate_block] * N_HEADS


def _pool_fillers(st, delta_fn, wpool_ref, pscale_ref, wbrp_ref):
    def pool_delta():
        deltas = [delta_fn(g).astype(BF16) for g in range(len(POOL_WINDOWS))]
        st["d"] = jnp.concatenate(deltas, axis=1)

    def pool_weight():
        st["y"] = (_dot(st.pop("d"), wpool_ref[...]) * pscale_ref[...]).astype(BF16)

    def pool_out():
        st["a"] = _dot(st.pop("y"), wbrp_ref[...])

    return [pool_delta, pool_weight, pool_out]


def _merge_fillers(st, gates, rows, att, x_ref, o_ref, gate2, wbra_ref, wout_ref):
    def attention_branch():
        st["b"] = _dot(att.astype(BF16), wbra_ref[...])

    def gate():
        g = jnp.concatenate(gates, axis=1)
        merged = g[:, :D_MODEL] * st["a"][rows] + g[:, D_MODEL:] * st.pop("b")
        st["merged"] = merged.astype(BF16)

    def out():
        o_ref[rows, :] = x_ref[rows, :] + gate2 * _dot(st.pop("merged"), wout_ref[...])

    return [attention_branch, gate, out]


def _attend_tiles(n_tiles, tile_rows, attention, gate_fillers, pool_fillers, merge_fillers):
    idle = lambda: None
    extra = pool_fillers + [idle] * (N_HEADS - len(pool_fillers))
    for t in range(n_tiles):
        rows = slice(t * tile_rows, (t + 1) * tile_rows)
        gates = []
        fillers = [lambda g=g, e=e: (g(), e()) for g, e in zip(gate_fillers(rows, gates), extra)]
        pieces = merge_fillers(gates, rows, attention(t, fillers))
        extra = [idle] + pieces + [idle] * (N_HEADS - 1 - len(pieces))
    for piece in pieces:
        piece()


def _ctx_mixer_kernel(x_ref, mod_ref, g_ref, win_ref, qg_ref, kg_ref, wpool32_ref, pscale_ref,
                      wbrp_ref, wbra_ref, wout_ref, o_ref, ko_ref, vo_ref, wpool_ref, *, seq):
    @pl.when(pl.program_id(0) == 0)
    def _first_step():
        _build_pool_weight(wpool32_ref, wpool_ref)

    m_rows = x_ref.shape[0]
    h = _rms_mod(x_ref[...], g_ref[...], mod_ref[0, 3:4, :], mod_ref[0, 4:5, :]).astype(BF16)
    p = _dot(h, win_ref[:, 0:O_Q])
    q = _dot(h, win_ref[:, O_Q:O_K])
    k = _dot(h, win_ref[:, O_K:O_V])
    v = _dot(h, win_ref[:, O_V:O_G])
    for e in range(m_rows // seq):
        vo_ref[e] = v[e * seq:(e + 1) * seq].T
    scale = Q_SCALE
    zero = jnp.zeros((), BF16)
    one = jnp.ones((), BF16)
    st = {}

    def pool_delta(g):
        lanes = slice(g * LANES, (g + 1) * LANES)
        zeros_halo = jnp.zeros((POOL_HALO, LANES), F32)
        return jnp.concatenate(
            [_pool_delta(jnp.concatenate([zeros_halo, p[e * seq:(e + 1) * seq, lanes], zeros_halo], axis=0),
                         g, 0, seq) for e in range(m_rows // seq)], axis=0)

    def gate_fillers(rows, gates):
        return _gate_fillers(gates, lambda: h[rows], lambda c0, c1: win_ref[:, O_G + c0:O_G + c1])

    def attention(e, fillers):
        rows = slice(e * seq, (e + 1) * seq)
        qn, kn, vb = [], [], []
        for pr in range(N_PAIRS):
            lanes = slice(pr * LANES, (pr + 1) * LANES)
            kn.append(_head_norm(k[rows, lanes], kg_ref[...]))
            qn.append((_head_norm(q[rows, lanes], qg_ref[...]) * scale).astype(BF16))
            vb.append(v[rows, lanes].astype(BF16))
        ko_ref[e] = jnp.concatenate(kn, axis=1).T
        kb = [t.astype(BF16) for t in kn]

        def scores(head):
            return (_dot_nt(_own_lanes(head, qn[head // 2], zero), kb[head // 2]),)

        def weights(head, s):
            return jnp.exp(s - s.max(axis=-1, keepdims=True)).astype(BF16)

        def weighted_values(head, e_):
            return _dot(e_, _own_lanes(head, vb[head // 2], one))

        return _pipelined_heads(scores, weights, weighted_values, fillers)

    merge = functools.partial(_merge_fillers, st, x_ref=x_ref, o_ref=o_ref, gate2=mod_ref[0, 5:6, :],
                              wbra_ref=wbra_ref, wout_ref=wout_ref)
    _attend_tiles(m_rows // seq, seq, attention, gate_fillers,
                  _pool_fillers(st, pool_delta, wpool_ref, pscale_ref, wbrp_ref), merge)


def _ctx_mixer(x_all, n_tok, mod3, g, w_in, qg2, kg2, w_pool, pscale, wbrp, wbra, wout, seq):
    tm = CTX_TB * seq
    row = lambda s: (s, 0)
    heads = lambda s: (s, 0, 0)
    return pl.pallas_call(
        functools.partial(_ctx_mixer_kernel, seq=seq),
        grid=(n_tok // tm,),
        in_specs=[pl.BlockSpec((tm, D_MODEL), row),
                  _const_spec((1, N_MOD, D_MODEL)), _const_spec((1, D_MODEL)),
                  _resident_spec(w_in), _const_spec((1, LANES)), _const_spec((1, LANES)),
                  _const_spec(w_pool.shape), _const_spec((1, D_POOL)),
                  _resident_spec(wbrp), _resident_spec(wbra), _resident_spec(wout)],
        out_specs=[pl.BlockSpec((tm, D_MODEL), row),
                   pl.BlockSpec((CTX_TB, D_ATT, seq), heads),
                   pl.BlockSpec((CTX_TB, D_ATT, seq), heads)],
        out_shape=[jax.ShapeDtypeStruct((n_tok, D_MODEL), F32),
                   jax.ShapeDtypeStruct((n_tok // seq, D_ATT, seq), F32),
                   jax.ShapeDtypeStruct((n_tok // seq, D_ATT, seq), F32)],
        scratch_shapes=[pltpu.VMEM((D_POOL, D_POOL), BF16)],
        compiler_params=_params(),
        name="ctx_mixer",
    )(x_all, mod3, g, w_in, qg2, kg2, w_pool, pscale, wbrp, wbra, wout)


def _lat_proj_kernel(x_ref, mod_ref, g_ref, win_ref, qg_ref, kg_ref, h_ref, p_ref, q_ref, k_ref, v_ref):
    h = _rms_mod(x_ref[...], g_ref[...], mod_ref[0, 3:4, :], mod_ref[0, 4:5, :]).astype(BF16)
    h_ref[...] = h
    p_ref[...] = _dot(h, win_ref[:, 0:O_Q])
    q = _dot(h, win_ref[:, O_Q:O_K])
    k = _dot(h, win_ref[:, O_K:O_V])
    v_ref[...] = _dot(h, win_ref[:, O_V:O_G]).astype(BF16)
    scale = Q_SCALE
    for pr in range(N_PAIRS):
        lanes = slice(pr * LANES, (pr + 1) * LANES)
        q_ref[:, lanes] = (_head_norm(q[:, lanes], qg_ref[...]) * scale).astype(BF16)
        k_ref[:, lanes] = _head_norm(k[:, lanes], kg_ref[...]).astype(BF16)


def _lat_proj(x_all, tok0, n_tok, seq, mod3, g, w_in, qg2, kg2):
    tm = PROJ_TM
    tile0 = tok0 // tm
    row = lambda s: (s, 0)
    shapes = [(D_MODEL, BF16), (D_POOL, F32), (D_ATT, BF16), (D_ATT, BF16), (D_ATT, BF16)]
    return pl.pallas_call(
        _lat_proj_kernel,
        grid=(n_tok // tm,),
        in_specs=[pl.BlockSpec((tm, D_MODEL), lambda s: (tile0 + s, 0)),
                  pl.BlockSpec((1, N_MOD, D_MODEL), lambda s: (1 + (s * tm) // seq, 0, 0)),
                  _const_spec((1, D_MODEL)), _resident_spec(w_in, (O_G, 0)),
                  _const_spec((1, LANES)), _const_spec((1, LANES))],
        out_specs=[pl.BlockSpec((tm, width), row) for width, _ in shapes],
        out_shape=[jax.ShapeDtypeStruct((n_tok, width), dtype) for width, dtype in shapes],
        compiler_params=_params(),
        name="lat_proj",
    )(x_all, mod3, g, w_in, qg2, kg2)


def _build_bias_table(rpb_ref, bias_ref):
    lane = lax.broadcasted_iota(jnp.int32, (GRID_W, LANES), 1)
    qc = lax.broadcasted_iota(jnp.int32, (GRID_W, LANES), 0)
    kc = lane & (GRID_W - 1)
    c0 = jnp.clip(qc - NA_KW // 2, 0, GRID_W - NA_KW)
    col_ok = (kc >= c0) & (kc < c0 + NA_KW)
    first_half = lane < GRID_W
    shift_a = LANES - (NA_KW - 1)
    shift_b = (GRID_W - (NA_KW - 1)) % LANES
    for h in range(N_HEADS):
        for i in range(2 * NA_KH - 2):
            va = jnp.broadcast_to(rpb_ref[h, i:i + 1, :], (GRID_W, LANES))
            vb = jnp.broadcast_to(rpb_ref[h, i + 1:i + 2, :], (GRID_W, LANES))
            ra = pltpu.roll(va, shift_a, 1, stride=1, stride_axis=0)
            rb = pltpu.roll(vb, shift_b, 1, stride=1, stride_axis=0)
            bias_ref[h, i] = jnp.where(col_ok, jnp.where(first_half, ra, rb), NEG)


def _lat_attention(qn, j, fillers, k_ref, v_ref, ck_ref, cv_ref, bias_ref, *, n_rows):
    n_keys = LAT_KEY_ROWS * GRID_W
    r0q = j * LAT_ROWS
    u0 = jnp.clip(r0q - NA_KH // 2, 0, n_rows - LAT_KEY_ROWS)
    key0 = pl.multiple_of(u0 * GRID_W, GRID_W)
    log2_w = GRID_W.bit_length() - 1
    zero = jnp.zeros((), BF16)
    one = jnp.ones((), BF16)

    lane_row = lax.broadcasted_iota(jnp.int32, (1, LANES), 1) & (HEAD_DIM - 1)
    q_tile_row = jnp.right_shift(lax.broadcasted_iota(jnp.int32, (LAT_TM, 1), 0), log2_w)
    q_hot = jnp.where(lane_row == q_tile_row, 1.0, 0.0).astype(BF16)
    k_row = u0 + jnp.right_shift(lax.broadcasted_iota(jnp.int32, (n_keys, 1), 0), log2_w)
    first = jnp.clip(r0q + lane_row - NA_KH // 2, 0, n_rows - NA_KH)
    outside = (lane_row < LAT_ROWS) & ((k_row < first) | (k_row >= first + NA_KH))
    k_mask = jnp.where(outside, NEG, 0.0).astype(BF16)

    def scores(head):
        lanes = slice((head // 2) * LANES, (head // 2 + 1) * LANES)
        qp = qn[:, lanes]
        s_loc = _dot_nt(_own_lanes(head, qp, q_hot),
                        _own_lanes(head, k_ref[0, pl.ds(key0, n_keys), lanes], k_mask))
        return s_loc, _dot_nt(_own_lanes(head, qp, zero), ck_ref[0, :, lanes])

    def weights(head, s_loc, s_ctx):
        bias_rows = []
        for jr in range(LAT_ROWS):
            blocks = []
            for pi in range(LAT_KEY_ROWS // 2):
                idx = jnp.clip(u0 - r0q + 2 * pi - jr + NA_KH - 1, 0, 2 * NA_KH - 3)
                blocks.append(bias_ref[head, idx])
            bias_rows.append(jnp.concatenate(blocks, axis=1))
        s_loc = s_loc + jnp.concatenate(bias_rows, axis=0)
        m = jnp.maximum(s_loc.max(axis=-1, keepdims=True), s_ctx.max(axis=-1, keepdims=True))
        return jnp.concatenate([jnp.exp(s_loc - m).astype(BF16), jnp.exp(s_ctx - m).astype(BF16)], axis=1)

    def weighted_values(head, e):
        lanes = slice((head // 2) * LANES, (head // 2 + 1) * LANES)
        values = jnp.concatenate([_own_lanes(head, v_ref[0, pl.ds(key0, n_keys), lanes], one),
                                  _own_lanes(head, cv_ref[0, :, lanes], one)], axis=0)
        return _dot(e, values)

    return _pipelined_heads(scores, weights, weighted_values, fillers)


def _lat_mixer_kernel(x_ref, h_ref, q_ref, p_ref, pp_ref, pn_ref, mod_ref, k_ref, v_ref, ck_ref, cv_ref,
                      rpb_ref, wg_ref, wpool32_ref, pscale_ref, wbrp_ref, wbra_ref, wout_ref,
                      o_ref, bias_ref, wpool_ref, *, seq, n_rows):
    step = pl.program_id(0)
    tm = LAT_TILES * LAT_TM

    @pl.when(step == 0)
    def _first_step():
        _build_pool_weight(wpool32_ref, wpool_ref)
        _build_bias_table(rpb_ref, bias_ref)

    blk = lax.rem(step, seq // tm)
    st = {}

    def pool_delta(g):
        lanes = slice(g * LANES, (g + 1) * LANES)
        pe = jnp.concatenate([pp_ref[:, lanes], p_ref[:, lanes], pn_ref[:, lanes]], axis=0)
        pos = blk * tm - POOL_HALO + lax.broadcasted_iota(jnp.int32, (tm + 2 * POOL_HALO, 1), 0)
        return _pool_delta(jnp.where((pos >= 0) & (pos < seq), pe, 0.0), g, blk * tm, seq)

    block_gates = []

    def gate_fillers(rows, gates):
        if not block_gates:
            fillers = _gate_fillers(block_gates, lambda: h_ref[...], lambda c0, c1: wg_ref[:, c0:c1])
            return [lambda f=f: (f(), gates.append(block_gates[-1][rows])) for f in fillers]
        gates.extend(g[rows] for g in block_gates)
        return [lambda: None] * N_HEADS

    pool_fillers = _pool_fillers(st, pool_delta, wpool_ref, pscale_ref, wbrp_ref)

    def attention(t, fillers):
        return _lat_attention(q_ref[t * LAT_TM:(t + 1) * LAT_TM, :], blk * LAT_TILES + t, fillers,
                              k_ref, v_ref, ck_ref, cv_ref, bias_ref, n_rows=n_rows)

    merge = functools.partial(_merge_fillers, st, x_ref=x_ref, o_ref=o_ref, gate2=mod_ref[0, 5:6, :],
                              wbra_ref=wbra_ref, wout_ref=wout_ref)
    _attend_tiles(LAT_TILES, LAT_TM, attention, gate_fillers, pool_fillers, merge)


def _lat_mixer(x_all, tok0, n_batch, seq, mod3, h, q, p, k3, v3, ck3, cv3, rpb_pad, w_in, w_pool, pscale,
               wbrp, wbra, wout):
    tm = LAT_TILES * LAT_TM
    n_rows = seq // GRID_W
    nb = seq // tm
    tile0 = tok0 // tm
    halo_per_tile = tm // POOL_HALO
    n_halo = p.shape[0] // POOL_HALO
    row = lambda s: (s, 0)
    seq_map = lambda s: (s // nb, 0, 0)
    return pl.pallas_call(
        functools.partial(_lat_mixer_kernel, seq=seq, n_rows=n_rows),
        grid=(n_batch * nb,),
        in_specs=[pl.BlockSpec((tm, D_MODEL), lambda s: (tile0 + s, 0)),
                  pl.BlockSpec((tm, D_MODEL), row),
                  pl.BlockSpec((tm, D_ATT), row),
                  pl.BlockSpec((tm, D_POOL), row),
                  pl.BlockSpec((POOL_HALO, D_POOL), lambda s: (jnp.maximum(s * halo_per_tile - 1, 0), 0)),
                  pl.BlockSpec((POOL_HALO, D_POOL),
                               lambda s: (jnp.minimum((s + 1) * halo_per_tile, n_halo - 1), 0)),
                  pl.BlockSpec((1, N_MOD, D_MODEL), lambda s: (1 + s // nb, 0, 0)),
                  pl.BlockSpec((1, seq, D_ATT), seq_map), pl.BlockSpec((1, seq, D_ATT), seq_map),
                  pl.BlockSpec((1,) + ck3.shape[1:], seq_map), pl.BlockSpec((1,) + cv3.shape[1:], seq_map),
                  _const_spec(rpb_pad.shape),
                  _resident_spec(w_in, (2 * D_MODEL, O_G // (2 * D_MODEL))),
                  _const_spec(w_pool.shape), _const_spec((1, D_POOL)),
                  _resident_spec(wbrp), _resident_spec(wbra), _resident_spec(wout)],
        out_specs=pl.BlockSpec((tm, D_MODEL), row),
        out_shape=jax.ShapeDtypeStruct((n_batch * seq, D_MODEL), F32),
        scratch_shapes=[pltpu.VMEM((N_HEADS, 2 * NA_KH - 2, GRID_W, LANES), F32),
                        pltpu.VMEM((D_POOL, D_POOL), BF16)],
        compiler_params=_params(),
        name="lat_mixer",
    )(x_all, h, q, p, p, p, mod3, k3, v3, ck3, cv3, rpb_pad, w_in, w_pool, pscale, wbrp, wbra, wout)


def kernel(x_prompt, x_sample, cache_k, cache_v, c, c_ctx, w_ada, b_ada, g_ff1, w_ff1_in, w_ff1_out,
           g_mix, w_in, q_gain, k_gain, w_pool, pool_scale, rpb, w_br_pool, w_br_att, w_out, g_ff2,
           w_ff2_in, w_ff2_out):
    n_ctx, seq_ctx, _ = x_prompt.shape
    n_lat, seq_lat, _ = x_sample.shape
    depth = w_ada.shape[0]
    assert depth == 1 and n_lat + 1 <= 8
    l = 0

    xp = x_prompt.reshape(n_ctx * seq_ctx, D_MODEL)
    xs = x_sample.reshape(n_lat * seq_lat, D_MODEL)
    t_ctx, t_lat = xp.shape[0], xs.shape[0]
    cond8 = jnp.concatenate([c_ctx[None], c, jnp.zeros((7 - n_lat, D_MODEL), F32)], axis=0)
    mod3 = _adaln(cond8, w_ada[l], b_ada[l]).reshape(8, N_MOD, D_MODEL)

    pscale = pool_scale[l].reshape(1, D_POOL)
    qg2 = jnp.tile(q_gain[l], 2).reshape(1, LANES)
    kg2 = jnp.tile(k_gain[l], 2).reshape(1, LANES)
    g1, gm, g2 = g_ff1[l].reshape(1, -1), g_mix[l].reshape(1, -1), g_ff2[l].reshape(1, -1)
    rpb_pad = jnp.pad(rpb[l], ((0, 0), (0, 0), (0, LANES - rpb.shape[-1])))
    ck3 = cache_k[:, l].reshape(n_lat, -1, D_ATT).astype(BF16)
    cv3 = cache_v[:, l].reshape(n_lat, -1, D_ATT).astype(BF16)

    x1, w2gu, w2dn, win, wbrp, wbra, wout = _ffn(
        xp, xs, mod3, seq_lat, 0, g1, w_ff1_in[l], w_ff1_out[l], "ffn1", convert=True,
        preconvert=(w_ff2_in[l], w_ff2_out[l], w_in[l], w_br_pool[l], w_br_att[l], w_out[l]))
    x2p, k_p, v_p = _ctx_mixer(x1, t_ctx, mod3, gm, win, qg2, kg2, w_pool[l], pscale, wbrp, wbra, wout, seq_ctx)
    h_s, p_s, q_s, k_s, v_s = _lat_proj(x1, t_ctx, t_lat, seq_lat, mod3, gm, win, qg2, kg2)
    x2s = _lat_mixer(x1, t_ctx, n_lat, seq_lat, mod3, h_s, q_s, p_s,
                     k_s.reshape(n_lat, seq_lat, D_ATT), v_s.reshape(n_lat, seq_lat, D_ATT), ck3, cv3,
                     rpb_pad, win, w_pool[l], pscale, wbrp, wbra, wout)
    yp, ys = _ffn(x2p, x2s, mod3, seq_lat, 6, g2, w2gu, w2dn, "ffn2", split_out=True)

    new_k = jnp.transpose(k_p.reshape(n_ctx, 1, N_HEADS, HEAD_DIM, seq_ctx), (0, 1, 4, 2, 3))
    new_v = jnp.transpose(v_p.reshape(n_ctx, 1, N_HEADS, HEAD_DIM, seq_ctx), (0, 1, 4, 2, 3))
    return (yp.reshape(x_prompt.shape), ys.reshape(x_sample.shape), new_k, new_v)
```

```python
import functools
import math

import jax
import jax.numpy as jnp
from jax import lax
from jax.experimental import pallas as pl
from jax.experimental.pallas import tpu as pltpu

F32 = jnp.float32
BF16 = jnp.bfloat16

D_MODEL = 1024
N_HEADS = 8
HEAD_DIM = 64
D_ATT = N_HEADS * HEAD_DIM
D_POOL = 512
POOL_WINDOWS = (2, 4, 8, 16)
POOL_HALO = 8
D_FF = 2816
N_MOD = 9
GRID_W = 64
NA_KH = 8
NA_KW = 16
EPS = 1e-6
NEG = -1e30

LANES = 128
BF16_ROWS = 16
N_PAIRS = D_ATT // LANES
MXU_COLS = 256
FF_CHUNKS = ((0, 768), (768, 768), (1536, 768), (2304, 512))
ADALN_TN = 2304
FFN_TM = 512
CTX_TB = 2
PROJ_TM = 1024
LAT_ROWS = 4
LAT_TM = LAT_ROWS * GRID_W
LAT_TILES = 2
LAT_KEY_ROWS = 12
SCORE_LOOKAHEAD = 2
N_CONV = 8
PRE_CHUNKS = 16
VMEM_LIMIT = 56 * 1024 * 1024

Q_SCALE = 1.0 / math.sqrt(HEAD_DIM)

O_Q = D_POOL
O_K = D_POOL + D_ATT
O_V = D_POOL + 2 * D_ATT
O_G = D_POOL + 3 * D_ATT


def _dot(a, b):
    return jnp.dot(a, b, preferred_element_type=F32)


def _dot_nt(a, b):
    return lax.dot_general(a, b, (((1,), (1,)), ((), ())), preferred_element_type=F32)


def _rms_mod(x, g, shift, scale):
    ms = jnp.mean(x * x, axis=-1, keepdims=True)
    return x * lax.rsqrt(ms + EPS) * (g * (1.0 + scale)) + shift


def _low_half():
    return lax.broadcasted_iota(jnp.int32, (1, LANES), 1) < HEAD_DIM


def _head_norm(t, gain2):
    lo = _low_half()
    sq = t * t
    s0 = jnp.sum(jnp.where(lo, sq, 0.0), axis=-1, keepdims=True)
    s1 = jnp.sum(jnp.where(lo, 0.0, sq), axis=-1, keepdims=True)
    r0 = lax.rsqrt(s0 * (1.0 / HEAD_DIM) + EPS)
    r1 = lax.rsqrt(s1 * (1.0 / HEAD_DIM) + EPS)
    return t * jnp.where(lo, r0, r1) * gain2


def _own_lanes(head, mine, other):
    lo = _low_half()
    return jnp.where(lo, mine, other) if head % 2 == 0 else jnp.where(lo, other, mine)


def _pipelined_heads(scores, weights, weighted_values, fillers):
    lo = _low_half()
    outs = []
    pending_scores = [scores(head) for head in range(SCORE_LOOKAHEAD)]
    pending_weights = [weights(0, *pending_scores.pop(0))]
    for head in range(N_HEADS):
        if head + SCORE_LOOKAHEAD < N_HEADS:
            pending_scores.append(scores(head + SCORE_LOOKAHEAD))
        fillers[head]()
        if head + 1 < N_HEADS:
            pending_weights.append(weights(head + 1, *pending_scores.pop(0)))
        outs.append(weighted_values(head, pending_weights.pop(0)))
    att_pairs = []
    for pr in range(N_PAIRS):
        even, odd = outs[2 * pr], outs[2 * pr + 1]
        num = jnp.where(lo, even, odd)
        den = pltpu.roll(jnp.where(lo, odd, even), HEAD_DIM, 1)
        att_pairs.append(num / den)
    return jnp.concatenate(att_pairs, axis=1)


def _pool_delta(p_ext, g, pos0, seq_len):
    w = POOL_WINDOWS[g]
    te = p_ext.shape[0]
    tm = te - 2 * POOL_HALO
    t = (pos0 + lax.broadcasted_iota(jnp.int32, (tm, 1), 0)).astype(F32)
    s = p_ext
    step = 1
    while step < w:
        s = s + pltpu.roll(s, step, 0)
        step *= 2
    ahead = w // 2 - 1
    if ahead:
        s = pltpu.roll(s, te - ahead, 0)
    win = s[POOL_HALO:POOL_HALO + tm]
    cnt = jnp.minimum(t + float(w // 2), float(seq_len)) - jnp.maximum(t - float(w // 2), 0.0)
    return win / cnt - p_ext[POOL_HALO:POOL_HALO + tm]


def _const_spec(shape):
    nd = len(shape)
    return pl.BlockSpec(shape, lambda *_: (0,) * nd)


def _resident_spec(w, col_block=None):
    width, cidx = col_block if col_block else (w.shape[1], 0)
    return pl.BlockSpec((w.shape[0], width), lambda *_: (0, cidx), pipeline_mode=pl.Buffered(1))


def _params():
    return pltpu.CompilerParams(dimension_semantics=("arbitrary",), vmem_limit_bytes=VMEM_LIMIT)


def _build_pool_weight(wpool32_ref, wpool_ref):
    wpool_ref[...] = jnp.zeros(wpool_ref.shape, BF16)
    for g in range(len(POOL_WINDOWS)):
        blk = slice(g * LANES, (g + 1) * LANES)
        wpool_ref[blk, blk] = wpool32_ref[g].astype(BF16)


def _adaln_kernel(cond_ref, wlo_ref, whi_ref, b_ref, o_ref):
    c = cond_ref[...]
    s = (c * jax.nn.sigmoid(c)).astype(BF16)
    half = wlo_ref.shape[1]
    o_ref[:, :half] = _dot(s, wlo_ref[...].astype(BF16)) + b_ref[:, :half]
    o_ref[:, half:] = _dot(s, whi_ref[...].astype(BF16)) + b_ref[:, half:]


def _adaln(cond8, w_ada, b_ada):
    n = w_ada.shape[1]
    tn = ADALN_TN
    return pl.pallas_call(
        _adaln_kernel,
        grid=(n // tn,),
        in_specs=[pl.BlockSpec((8, D_MODEL), lambda i: (0, 0)),
                  pl.BlockSpec((D_MODEL, tn // 2), lambda i: (0, 2 * i)),
                  pl.BlockSpec((D_MODEL, tn // 2), lambda i: (0, 2 * i + 1)),
                  pl.BlockSpec((1, tn), lambda i: (0, i))],
        out_specs=pl.BlockSpec((8, tn), lambda i: (0, i)),
        out_shape=jax.ShapeDtypeStruct((8, n), F32),
        compiler_params=_params(),
        name="adaln",
    )(cond8, w_ada, w_ada, b_ada.reshape(1, n))


def _ffn_kernel(*refs, mod_off, n_a, split_out, n_conv, n_pre):
    refs = list(refs)
    xa_ref, xb_ref, mod_ref, g_ref, wgu_in, wdn_in = refs[:6]
    pre_src = refs[6:6 + n_pre]
    n_out = 2 if split_out else 1
    out_refs = refs[6 + n_pre:6 + n_pre + n_out]
    pre_dst = refs[6 + n_pre + n_out:6 + 2 * n_pre + n_out]
    wgu_ref, wdn_ref = refs[6 + 2 * n_pre + n_out:] if n_conv else (wgu_in, wdn_in)
    step = pl.program_id(0)

    if n_conv:
        @pl.when(step < n_conv)
        def _convert():
            for src, dst in ((wgu_in, wgu_ref), (wdn_in, wdn_ref)):
                rows = src.shape[0]
                dst[pl.ds(pl.multiple_of(step * rows, rows), rows), :] = src[...].astype(BF16)

    tile = step - n_conv

    def half_step(x_ref, o_ref):
        shift = mod_ref[0, mod_off:mod_off + 1, :]
        scale = mod_ref[0, mod_off + 1:mod_off + 2, :]
        gate = mod_ref[0, mod_off + 2:mod_off + 3, :]
        x = x_ref[...]
        h = _rms_mod(x, g_ref[...], shift, scale).astype(BF16)
        acc = None
        for c0, cw in FF_CHUNKS:
            a = _dot(h, wgu_ref[:, c0:c0 + cw])
            u = _dot(h, wgu_ref[:, D_FF + c0:D_FF + c0 + cw])
            t = (a * jax.nn.sigmoid(a) * u).astype(BF16)
            part = _dot(t, wdn_ref[c0:c0 + cw, :])
            acc = part if acc is None else acc + part
        o_ref[...] = x + (0.5 * gate) * acc

    @pl.when((tile >= 0) & (tile < n_a))
    def _tokens_a():
        half_step(xa_ref, out_refs[0])

    @pl.when(tile >= n_a)
    def _tokens_b():
        half_step(xb_ref, out_refs[-1])

    if n_pre:
        @pl.when((tile >= 0) & (tile < PRE_CHUNKS))
        def _preconvert():
            for src, dst in zip(pre_src, pre_dst):
                dst[...] = src[...].astype(BF16)


def _ffn(xa, xb, mod3, seq_b, mod_off, g, wgu, wdn, name, split_out=False, convert=False, preconvert=()):
    tm = FFN_TM
    n_a, n_b = xa.shape[0] // tm, xb.shape[0] // tm
    n_conv = N_CONV if convert else 0
    assert n_a + n_b >= PRE_CHUNKS
    tile = lambda s: jnp.maximum(s - n_conv, 0)
    a_map = lambda s: (jnp.minimum(tile(s), n_a - 1), 0)
    b_map = lambda s: (jnp.maximum(tile(s) - n_a, 0), 0)
    mod_map = lambda s: (jnp.where(tile(s) < n_a, 0, 1 + (jnp.maximum(tile(s) - n_a, 0) * tm) // seq_b), 0, 0)

    def chunk_spec(w, n_chunks, index):
        rows = w.shape[0] // n_chunks
        assert rows * n_chunks == w.shape[0] and rows % BF16_ROWS == 0
        return pl.BlockSpec((rows, w.shape[1]), lambda s: (jnp.minimum(index(s), n_chunks - 1), 0))

    if convert:
        w_specs = [chunk_spec(wgu, N_CONV, lambda s: s), chunk_spec(wdn, N_CONV, lambda s: s)]
        scratch = [pltpu.VMEM(wgu.shape, BF16), pltpu.VMEM(wdn.shape, BF16)]
    else:
        w_specs = [_resident_spec(wgu), _resident_spec(wdn)]
        scratch = []
    pre_specs = lambda: [chunk_spec(w, PRE_CHUNKS, tile) for w in preconvert]
    if split_out:
        out_specs = [pl.BlockSpec((tm, D_MODEL), a_map), pl.BlockSpec((tm, D_MODEL), b_map)]
        out_shape = [jax.ShapeDtypeStruct(xa.shape, F32), jax.ShapeDtypeStruct(xb.shape, F32)]
    else:
        out_specs = [pl.BlockSpec((tm, D_MODEL), lambda s: (tile(s), 0))]
        out_shape = [jax.ShapeDtypeStruct((xa.shape[0] + xb.shape[0], D_MODEL), F32)]
    return pl.pallas_call(
        functools.partial(_ffn_kernel, mod_off=mod_off, n_a=n_a, split_out=split_out, n_conv=n_conv,
                          n_pre=len(preconvert)),
        grid=(n_conv + n_a + n_b,),
        in_specs=[pl.BlockSpec((tm, D_MODEL), a_map), pl.BlockSpec((tm, D_MODEL), b_map),
                  pl.BlockSpec((1, N_MOD, D_MODEL), mod_map), _const_spec((1, D_MODEL))] + w_specs + pre_specs(),
        out_specs=out_specs + pre_specs(),
        out_shape=out_shape + [jax.ShapeDtypeStruct(w.shape, BF16) for w in preconvert],
        scratch_shapes=scratch,
        compiler_params=_params(),
        name=name,
    )(xa, xb, mod3, g, wgu, wdn, *preconvert)


def _gate_fillers(gates, h_rows, gate_cols):
    assert 2 * D_MODEL // MXU_COLS == N_HEADS

    def gate_block():
        c0 = len(gates) * MXU_COLS
        gates.append(jax.nn.sigmoid(_dot(h_rows(), gate_cols(c0, c0 + MXU_COLS))))

    return [gate_block] * N_HEADS


def _pool_fillers(st, delta_fn, wpool_ref, pscale_ref, wbrp_ref):
    def pool_delta():
        deltas = [delta_fn(g).astype(BF16) for g in range(len(POOL_WINDOWS))]
        st["d"] = jnp.concatenate(deltas, axis=1)

    def pool_weight():
        st["y"] = (_dot(st.pop("d"), wpool_ref[...]) * pscale_ref[...]).astype(BF16)

    def pool_out():
        st["a"] = _dot(st.pop("y"), wbrp_ref[...])

    return [pool_delta, pool_weight, pool_out]


def _merge_fillers(st, gates, rows, att, x_ref, o_ref, gate2, wbra_ref, wout_ref):
    def attention_branch():
        st["b"] = _dot(att.astype(BF16), wbra_ref[...])

    def gate():
        g = jnp.concatenate(gates, axis=1)
        merged = g[:, :D_MODEL] * st["a"][rows] + g[:, D_MODEL:] * st.pop("b")
        st["merged"] = merged.astype(BF16)

    def out():
        o_ref[rows, :] = x_ref[rows, :] + gate2 * _dot(st.pop("merged"), wout_ref[...])

    return [attention_branch, gate, out]


def _attend_tiles(n_tiles, tile_rows, attention, gate_fillers, pool_fillers, merge_fillers):
    idle = lambda: None
    extra = pool_fillers + [idle] * (N_HEADS - len(pool_fillers))
    for t in range(n_tiles):
        rows = slice(t * tile_rows, (t + 1) * tile_rows)
        gates = []
        fillers = [lambda g=g, e=e: (g(), e()) for g, e in zip(gate_fillers(rows, gates), extra)]
        pieces = merge_fillers(gates, rows, attention(t, fillers))
        extra = [idle] + pieces + [idle] * (N_HEADS - 1 - len(pieces))
    for piece in pieces:
        piece()


def _ctx_mixer_kernel(x_ref, mod_ref, g_ref, win_ref, qg_ref, kg_ref, wpool32_ref, pscale_ref,
                      wbrp_ref, wbra_ref, wout_ref, o_ref, ko_ref, vo_ref, wpool_ref, *, seq):
    @pl.when(pl.program_id(0) == 0)
    def _first_step():
        _build_pool_weight(wpool32_ref, wpool_ref)

    m_rows = x_ref.shape[0]
    h = _rms_mod(x_ref[...], g_ref[...], mod_ref[0, 3:4, :], mod_ref[0, 4:5, :]).astype(BF16)
    p = _dot(h, win_ref[:, 0:O_Q])
    q = _dot(h, win_ref[:, O_Q:O_K])
    k = _dot(h, win_ref[:, O_K:O_V])
    v = _dot(h, win_ref[:, O_V:O_G])
    for e in range(m_rows // seq):
        vo_ref[e] = v[e * seq:(e + 1) * seq].T
    scale = Q_SCALE
    zero = jnp.zeros((), BF16)
    one = jnp.ones((), BF16)
    st = {}

    def pool_delta(g):
        lanes = slice(g * LANES, (g + 1) * LANES)
        zeros_halo = jnp.zeros((POOL_HALO, LANES), F32)
        return jnp.concatenate(
            [_pool_delta(jnp.concatenate([zeros_halo, p[e * seq:(e + 1) * seq, lanes], zeros_halo], axis=0),
                         g, 0, seq) for e in range(m_rows // seq)], axis=0)

    def gate_fillers(rows, gates):
        return _gate_fillers(gates, lambda: h[rows], lambda c0, c1: win_ref[:, O_G + c0:O_G + c1])

    def attention(e, fillers):
        rows = slice(e * seq, (e + 1) * seq)
        qn, kn, vb = [], [], []
        for pr in range(N_PAIRS):
            lanes = slice(pr * LANES, (pr + 1) * LANES)
            kn.append(_head_norm(k[rows, lanes], kg_ref[...]))
            qn.append((_head_norm(q[rows, lanes], qg_ref[...]) * scale).astype(BF16))
            vb.append(v[rows, lanes].astype(BF16))
        ko_ref[e] = jnp.concatenate(kn, axis=1).T
        kb = [t.astype(BF16) for t in kn]

        def scores(head):
            return (_dot_nt(_own_lanes(head, qn[head // 2], zero), kb[head // 2]),)

        def weights(head, s):
            return jnp.exp(s - s.max(axis=-1, keepdims=True)).astype(BF16)

        def weighted_values(head, e_):
            return _dot(e_, _own_lanes(head, vb[head // 2], one))

        return _pipelined_heads(scores, weights, weighted_values, fillers)

    merge = functools.partial(_merge_fillers, st, x_ref=x_ref, o_ref=o_ref, gate2=mod_ref[0, 5:6, :],
                              wbra_ref=wbra_ref, wout_ref=wout_ref)
    _attend_tiles(m_rows // seq, seq, attention, gate_fillers,
                  _pool_fillers(st, pool_delta, wpool_ref, pscale_ref, wbrp_ref), merge)


def _ctx_mixer(x_all, n_tok, mod3, g, w_in, qg2, kg2, w_pool, pscale, wbrp, wbra, wout, seq):
    tm = CTX_TB * seq
    row = lambda s: (s, 0)
    heads = lambda s: (s, 0, 0)
    return pl.pallas_call(
        functools.partial(_ctx_mixer_kernel, seq=seq),
        grid=(n_tok // tm,),
        in_specs=[pl.BlockSpec((tm, D_MODEL), row),
                  _const_spec((1, N_MOD, D_MODEL)), _const_spec((1, D_MODEL)),
                  _resident_spec(w_in), _const_spec((1, LANES)), _const_spec((1, LANES)),
                  _const_spec(w_pool.shape), _const_spec((1, D_POOL)),
                  _resident_spec(wbrp), _resident_spec(wbra), _resident_spec(wout)],
        out_specs=[pl.BlockSpec((tm, D_MODEL), row),
                   pl.BlockSpec((CTX_TB, D_ATT, seq), heads),
                   pl.BlockSpec((CTX_TB, D_ATT, seq), heads)],
        out_shape=[jax.ShapeDtypeStruct((n_tok, D_MODEL), F32),
                   jax.ShapeDtypeStruct((n_tok // seq, D_ATT, seq), F32),
                   jax.ShapeDtypeStruct((n_tok // seq, D_ATT, seq), F32)],
        scratch_shapes=[pltpu.VMEM((D_POOL, D_POOL), BF16)],
        compiler_params=_params(),
        name="ctx_mixer",
    )(x_all, mod3, g, w_in, qg2, kg2, w_pool, pscale, wbrp, wbra, wout)


def _lat_proj_kernel(x_ref, mod_ref, g_ref, win_ref, qg_ref, kg_ref, h_ref, p_ref, q_ref, k_ref, v_ref):
    h = _rms_mod(x_ref[...], g_ref[...], mod_ref[0, 3:4, :], mod_ref[0, 4:5, :]).astype(BF16)
    h_ref[...] = h
    p_ref[...] = _dot(h, win_ref[:, 0:O_Q])
    q = _dot(h, win_ref[:, O_Q:O_K])
    k = _dot(h, win_ref[:, O_K:O_V])
    v_ref[...] = _dot(h, win_ref[:, O_V:O_G]).astype(BF16)
    scale = Q_SCALE
    for pr in range(N_PAIRS):
        lanes = slice(pr * LANES, (pr + 1) * LANES)
        q_ref[:, lanes] = (_head_norm(q[:, lanes], qg_ref[...]) * scale).astype(BF16)
        k_ref[:, lanes] = _head_norm(k[:, lanes], kg_ref[...]).astype(BF16)


def _lat_proj(x_all, tok0, n_tok, seq, mod3, g, w_in, qg2, kg2):
    tm = PROJ_TM
    tile0 = tok0 // tm
    row = lambda s: (s, 0)
    shapes = [(D_MODEL, BF16), (D_POOL, F32), (D_ATT, BF16), (D_ATT, BF16), (D_ATT, BF16)]
    return pl.pallas_call(
        _lat_proj_kernel,
        grid=(n_tok // tm,),
        in_specs=[pl.BlockSpec((tm, D_MODEL), lambda s: (tile0 + s, 0)),
                  pl.BlockSpec((1, N_MOD, D_MODEL), lambda s: (1 + (s * tm) // seq, 0, 0)),
                  _const_spec((1, D_MODEL)), _resident_spec(w_in, (O_G, 0)),
                  _const_spec((1, LANES)), _const_spec((1, LANES))],
        out_specs=[pl.BlockSpec((tm, width), row) for width, _ in shapes],
        out_shape=[jax.ShapeDtypeStruct((n_tok, width), dtype) for width, dtype in shapes],
        compiler_params=_params(),
        name="lat_proj",
    )(x_all, mod3, g, w_in, qg2, kg2)


def _build_bias_table(rpb_ref, bias_ref):
    lane = lax.broadcasted_iota(jnp.int32, (GRID_W, LANES), 1)
    qc = lax.broadcasted_iota(jnp.int32, (GRID_W, LANES), 0)
    kc = lane & (GRID_W - 1)
    c0 = jnp.clip(qc - NA_KW // 2, 0, GRID_W - NA_KW)
    col_ok = (kc >= c0) & (kc < c0 + NA_KW)
    first_half = lane < GRID_W
    shift_a = LANES - (NA_KW - 1)
    shift_b = (GRID_W - (NA_KW - 1)) % LANES
    for h in range(N_HEADS):
        for i in range(2 * NA_KH - 2):
            va = jnp.broadcast_to(rpb_ref[h, i:i + 1, :], (GRID_W, LANES))
            vb = jnp.broadcast_to(rpb_ref[h, i + 1:i + 2, :], (GRID_W, LANES))
            ra = pltpu.roll(va, shift_a, 1, stride=1, stride_axis=0)
            rb = pltpu.roll(vb, shift_b, 1, stride=1, stride_axis=0)
            bias_ref[h, i] = jnp.where(col_ok, jnp.where(first_half, ra, rb), NEG)


def _lat_attention(qn, j, fillers, k_ref, v_ref, ck_ref, cv_ref, bias_ref, *, n_rows):
    n_keys = LAT_KEY_ROWS * GRID_W
    r0q = j * LAT_ROWS
    u0 = jnp.clip(r0q - NA_KH // 2, 0, n_rows - LAT_KEY_ROWS)
    key0 = pl.multiple_of(u0 * GRID_W, GRID_W)
    log2_w = GRID_W.bit_length() - 1
    zero = jnp.zeros((), BF16)
    one = jnp.ones((), BF16)

    lane_row = lax.broadcasted_iota(jnp.int32, (1, LANES), 1) & (HEAD_DIM - 1)
    q_tile_row = jnp.right_shift(lax.broadcasted_iota(jnp.int32, (LAT_TM, 1), 0), log2_w)
    q_hot = jnp.where(lane_row == q_tile_row, 1.0, 0.0).astype(BF16)
    k_row = u0 + jnp.right_shift(lax.broadcasted_iota(jnp.int32, (n_keys, 1), 0), log2_w)
    first = jnp.clip(r0q + lane_row - NA_KH // 2, 0, n_rows - NA_KH)
    outside = (lane_row < LAT_ROWS) & ((k_row < first) | (k_row >= first + NA_KH))
    k_mask = jnp.where(outside, NEG, 0.0).astype(BF16)

    def scores(head):
        lanes = slice((head // 2) * LANES, (head // 2 + 1) * LANES)
        qp = qn[:, lanes]
        s_loc = _dot_nt(_own_lanes(head, qp, q_hot),
                        _own_lanes(head, k_ref[0, pl.ds(key0, n_keys), lanes], k_mask))
        return s_loc, _dot(_own_lanes(head, qp, zero), ck_ref[lanes, :])

    def weights(head, s_loc, s_ctx):
        bias_rows = []
        for jr in range(LAT_ROWS):
            blocks = []
            for pi in range(LAT_KEY_ROWS // 2):
                idx = jnp.clip(u0 - r0q + 2 * pi - jr + NA_KH - 1, 0, 2 * NA_KH - 3)
                blocks.append(bias_ref[head, idx])
            bias_rows.append(jnp.concatenate(blocks, axis=1))
        s_loc = s_loc + jnp.concatenate(bias_rows, axis=0)
        m = jnp.maximum(s_loc.max(axis=-1, keepdims=True), s_ctx.max(axis=-1, keepdims=True))
        return jnp.concatenate([jnp.exp(s_loc - m).astype(BF16), jnp.exp(s_ctx - m).astype(BF16)], axis=1)

    def weighted_values(head, e):
        lanes = slice((head // 2) * LANES, (head // 2 + 1) * LANES)
        values = _own_lanes(head, v_ref[0, pl.ds(key0, n_keys), lanes], one)
        return _dot(e[:, :n_keys], values) + _dot_nt(e[:, n_keys:], cv_ref[head])

    return _pipelined_heads(scores, weights, weighted_values, fillers)


def _lat_mixer_kernel(x_ref, h_ref, q_ref, p_ref, pp_ref, pn_ref, mod_ref, k_ref, v_ref, ckin_ref, cvin_ref,
                      rpb_ref, wg_ref, wpool32_ref, pscale_ref, wbrp_ref, wbra_ref, wout_ref,
                      o_ref, bias_ref, wpool_ref, ck_ref, cv_ref, *, seq, n_rows):
    step = pl.program_id(0)
    tm = LAT_TILES * LAT_TM

    @pl.when(step == 0)
    def _first_step():
        _build_pool_weight(wpool32_ref, wpool_ref)
        _build_bias_table(rpb_ref, bias_ref)

    blk = lax.rem(step, seq // tm)
    st = {}

    @pl.when(blk == 0)
    def _sequence_cache():
        ck_ref[...] = ckin_ref[0].astype(BF16)
        low = lax.broadcasted_iota(jnp.int32, (LANES, 1), 0) < HEAD_DIM
        for head in range(N_HEADS):
            rows = slice((head // 2) * LANES, (head // 2 + 1) * LANES)
            cv_ref[head] = jnp.where(low if head % 2 == 0 else ~low, cvin_ref[0, rows, :], 1.0).astype(BF16)

    def pool_delta(g):
        lanes = slice(g * LANES, (g + 1) * LANES)
        pe = jnp.concatenate([pp_ref[:, lanes], p_ref[:, lanes], pn_ref[:, lanes]], axis=0)
        pos = blk * tm - POOL_HALO + lax.broadcasted_iota(jnp.int32, (tm + 2 * POOL_HALO, 1), 0)
        return _pool_delta(jnp.where((pos >= 0) & (pos < seq), pe, 0.0), g, blk * tm, seq)

    block_gates = []

    def gate_fillers(rows, gates):
        if not block_gates:
            fillers = _gate_fillers(block_gates, lambda: h_ref[...], lambda c0, c1: wg_ref[:, c0:c1])
            return [lambda f=f: (f(), gates.append(block_gates[-1][rows])) for f in fillers]
        gates.extend(g[rows] for g in block_gates)
        return [lambda: None] * N_HEADS

    pool_fillers = _pool_fillers(st, pool_delta, wpool_ref, pscale_ref, wbrp_ref)

    def attention(t, fillers):
        return _lat_attention(q_ref[t * LAT_TM:(t + 1) * LAT_TM, :], blk * LAT_TILES + t, fillers,
                              k_ref, v_ref, ck_ref, cv_ref, bias_ref, n_rows=n_rows)

    merge = functools.partial(_merge_fillers, st, x_ref=x_ref, o_ref=o_ref, gate2=mod_ref[0, 5:6, :],
                              wbra_ref=wbra_ref, wout_ref=wout_ref)
    _attend_tiles(LAT_TILES, LAT_TM, attention, gate_fillers, pool_fillers, merge)


def _lat_mixer(x_all, tok0, n_batch, seq, mod3, h, q, p, k3, v3, ck3, cv3, rpb_pad, w_in, w_pool, pscale,
               wbrp, wbra, wout):
    tm = LAT_TILES * LAT_TM
    n_rows = seq // GRID_W
    nb = seq // tm
    tile0 = tok0 // tm
    halo_per_tile = tm // POOL_HALO
    n_halo = p.shape[0] // POOL_HALO
    row = lambda s: (s, 0)
    seq_map = lambda s: (s // nb, 0, 0)
    return pl.pallas_call(
        functools.partial(_lat_mixer_kernel, seq=seq, n_rows=n_rows),
        grid=(n_batch * nb,),
        in_specs=[pl.BlockSpec((tm, D_MODEL), lambda s: (tile0 + s, 0)),
                  pl.BlockSpec((tm, D_MODEL), row),
                  pl.BlockSpec((tm, D_ATT), row),
                  pl.BlockSpec((tm, D_POOL), row),
                  pl.BlockSpec((POOL_HALO, D_POOL), lambda s: (jnp.maximum(s * halo_per_tile - 1, 0), 0)),
                  pl.BlockSpec((POOL_HALO, D_POOL),
                               lambda s: (jnp.minimum((s + 1) * halo_per_tile, n_halo - 1), 0)),
                  pl.BlockSpec((1, N_MOD, D_MODEL), lambda s: (1 + s // nb, 0, 0)),
                  pl.BlockSpec((1, seq, D_ATT), seq_map), pl.BlockSpec((1, seq, D_ATT), seq_map),
                  pl.BlockSpec((1,) + ck3.shape[1:], seq_map), pl.BlockSpec((1,) + cv3.shape[1:], seq_map),
                  _const_spec(rpb_pad.shape),
                  _resident_spec(w_in, (2 * D_MODEL, O_G // (2 * D_MODEL))),
                  _const_spec(w_pool.shape), _const_spec((1, D_POOL)),
                  _resident_spec(wbrp), _resident_spec(wbra), _resident_spec(wout)],
        out_specs=pl.BlockSpec((tm, D_MODEL), row),
        out_shape=jax.ShapeDtypeStruct((n_batch * seq, D_MODEL), F32),
        scratch_shapes=[pltpu.VMEM((N_HEADS, 2 * NA_KH - 2, GRID_W, LANES), F32),
                        pltpu.VMEM((D_POOL, D_POOL), BF16),
                        pltpu.VMEM(ck3.shape[1:], BF16),
                        pltpu.VMEM((N_HEADS, LANES, cv3.shape[2]), BF16)],
        compiler_params=_params(),
        name="lat_mixer",
    )(x_all, h, q, p, p, p, mod3, k3, v3, ck3, cv3, rpb_pad, w_in, w_pool, pscale, wbrp, wbra, wout)


def kernel(x_prompt, x_sample, cache_k, cache_v, c, c_ctx, w_ada, b_ada, g_ff1, w_ff1_in, w_ff1_out,
           g_mix, w_in, q_gain, k_gain, w_pool, pool_scale, rpb, w_br_pool, w_br_att, w_out, g_ff2,
           w_ff2_in, w_ff2_out):
    n_ctx, seq_ctx, _ = x_prompt.shape
    n_lat, seq_lat, _ = x_sample.shape
    depth = w_ada.shape[0]
    assert depth == 1 and n_lat + 1 <= 8
    l = 0

    xp = x_prompt.reshape(n_ctx * seq_ctx, D_MODEL)
    xs = x_sample.reshape(n_lat * seq_lat, D_MODEL)
    t_ctx, t_lat = xp.shape[0], xs.shape[0]
    cond8 = jnp.concatenate([c_ctx[None], c, jnp.zeros((7 - n_lat, D_MODEL), F32)], axis=0)
    mod3 = _adaln(cond8, w_ada[l], b_ada[l]).reshape(8, N_MOD, D_MODEL)

    pscale = pool_scale[l].reshape(1, D_POOL)
    qg2 = jnp.tile(q_gain[l], 2).reshape(1, LANES)
    kg2 = jnp.tile(k_gain[l], 2).reshape(1, LANES)
    g1, gm, g2 = g_ff1[l].reshape(1, -1), g_mix[l].reshape(1, -1), g_ff2[l].reshape(1, -1)
    rpb_pad = jnp.pad(rpb[l], ((0, 0), (0, 0), (0, LANES - rpb.shape[-1])))
    ck3 = jnp.transpose(cache_k[:, l], (0, 2, 3, 1)).reshape(n_lat, D_ATT, -1)
    cv3 = jnp.transpose(cache_v[:, l], (0, 2, 3, 1)).reshape(n_lat, D_ATT, -1)

    x1, w2gu, w2dn, win, wbrp, wbra, wout = _ffn(
        xp, xs, mod3, seq_lat, 0, g1, w_ff1_in[l], w_ff1_out[l], "ffn1", convert=True,
        preconvert=(w_ff2_in[l], w_ff2_out[l], w_in[l], w_br_pool[l], w_br_att[l], w_out[l]))
    x2p, k_p, v_p = _ctx_mixer(x1, t_ctx, mod3, gm, win, qg2, kg2, w_pool[l], pscale, wbrp, wbra, wout, seq_ctx)
    h_s, p_s, q_s, k_s, v_s = _lat_proj(x1, t_ctx, t_lat, seq_lat, mod3, gm, win, qg2, kg2)
    x2s = _lat_mixer(x1, t_ctx, n_lat, seq_lat, mod3, h_s, q_s, p_s,
                     k_s.reshape(n_lat, seq_lat, D_ATT), v_s.reshape(n_lat, seq_lat, D_ATT), ck3, cv3,
                     rpb_pad, win, w_pool[l], pscale, wbrp, wbra, wout)
    yp, ys = _ffn(x2p, x2s, mod3, seq_lat, 6, g2, w2gu, w2dn, "ffn2", split_out=True)

    new_k = jnp.transpose(k_p.reshape(n_ctx, 1, N_HEADS, HEAD_DIM, seq_ctx), (0, 1, 4, 2, 3))
    new_v = jnp.transpose(v_p.reshape(n_ctx, 1, N_HEADS, HEAD_DIM, seq_ctx), (0, 1, 4, 2, 3))
    return (yp.reshape(x_prompt.shape), ys.reshape(x_sample.shape), new_k, new_v)
```

```python
import functools
import math

import jax
import jax.numpy as jnp
from jax import lax
from jax.experimental import pallas as pl
from jax.experimental.pallas import tpu as pltpu

F32 = jnp.float32
BF16 = jnp.bfloat16

D_MODEL = 1024
N_HEADS = 8
HEAD_DIM = 64
D_ATT = N_HEADS * HEAD_DIM
D_POOL = 512
POOL_WINDOWS = (2, 4, 8, 16)
POOL_HALO = 8
D_FF = 2816
N_MOD = 9
GRID_W = 64
NA_KH = 8
NA_KW = 16
EPS = 1e-6
NEG = -1e30

LANES = 128
BF16_ROWS = 16
N_PAIRS = D_ATT // LANES
MXU_COLS = 256
FF_CHUNKS = ((0, 768), (768, 768), (1536, 768), (2304, 512))
ADALN_TN = 2304
FFN_TM = 512
CTX_TB = 2
PROJ_TM = 1024
LAT_ROWS = 4
LAT_TM = LAT_ROWS * GRID_W
LAT_TILES = 2
LAT_KEY_ROWS = 12
SCORE_LOOKAHEAD = 2
N_CONV = 8
PRE_CHUNKS = 16
VMEM_LIMIT = 56 * 1024 * 1024

Q_SCALE = 1.0 / math.sqrt(HEAD_DIM)

O_Q = D_POOL
O_K = D_POOL + D_ATT
O_V = D_POOL + 2 * D_ATT
O_G = D_POOL + 3 * D_ATT


def _dot(a, b):
    return jnp.dot(a, b, preferred_element_type=F32)


def _dot_nt(a, b):
    return lax.dot_general(a, b, (((1,), (1,)), ((), ())), preferred_element_type=F32)


def _rms_mod(x, g, shift, scale):
    ms = jnp.mean(x * x, axis=-1, keepdims=True)
    return x * lax.rsqrt(ms + EPS) * (g * (1.0 + scale)) + shift


def _low_half():
    return lax.broadcasted_iota(jnp.int32, (1, LANES), 1) < HEAD_DIM


def _head_norm(t, gain2):
    lo = _low_half()
    sq = t * t
    s0 = jnp.sum(jnp.where(lo, sq, 0.0), axis=-1, keepdims=True)
    s1 = jnp.sum(jnp.where(lo, 0.0, sq), axis=-1, keepdims=True)
    r0 = lax.rsqrt(s0 * (1.0 / HEAD_DIM) + EPS)
    r1 = lax.rsqrt(s1 * (1.0 / HEAD_DIM) + EPS)
    return t * jnp.where(lo, r0, r1) * gain2


def _own_lanes(head, mine, other):
    lo = _low_half()
    return jnp.where(lo, mine, other) if head % 2 == 0 else jnp.where(lo, other, mine)


def _pipelined_heads(scores, weights, weighted_values, fillers):
    lo = _low_half()
    outs = []
    pending_scores = [scores(head) for head in range(SCORE_LOOKAHEAD)]
    pending_weights = [weights(0, *pending_scores.pop(0))]
    for head in range(N_HEADS):
        if head + SCORE_LOOKAHEAD < N_HEADS:
            pending_scores.append(scores(head + SCORE_LOOKAHEAD))
        fillers[head]()
        if head + 1 < N_HEADS:
            pending_weights.append(weights(head + 1, *pending_scores.pop(0)))
        outs.append(weighted_values(head, pending_weights.pop(0)))
    att_pairs = []
    for pr in range(N_PAIRS):
        even, odd = outs[2 * pr], outs[2 * pr + 1]
        num = jnp.where(lo, even, odd)
        den = pltpu.roll(jnp.where(lo, odd, even), HEAD_DIM, 1)
        att_pairs.append(num / den)
    return jnp.concatenate(att_pairs, axis=1)


def _pool_delta(p_ext, g, pos0, seq_len):
    w = POOL_WINDOWS[g]
    te = p_ext.shape[0]
    tm = te - 2 * POOL_HALO
    t = (pos0 + lax.broadcasted_iota(jnp.int32, (tm, 1), 0)).astype(F32)
    s = p_ext
    step = 1
    while step < w:
        s = s + pltpu.roll(s, step, 0)
        step *= 2
    ahead = w // 2 - 1
    if ahead:
        s = pltpu.roll(s, te - ahead, 0)
    win = s[POOL_HALO:POOL_HALO + tm]
    cnt = jnp.minimum(t + float(w // 2), float(seq_len)) - jnp.maximum(t - float(w // 2), 0.0)
    return win / cnt - p_ext[POOL_HALO:POOL_HALO + tm]


def _const_spec(shape):
    nd = len(shape)
    return pl.BlockSpec(shape, lambda *_: (0,) * nd)


def _resident_spec(w, col_block=None):
    width, cidx = col_block if col_block else (w.shape[1], 0)
    return pl.BlockSpec((w.shape[0], width), lambda *_: (0, cidx), pipeline_mode=pl.Buffered(1))


def _params():
    return pltpu.CompilerParams(dimension_semantics=("arbitrary",), vmem_limit_bytes=VMEM_LIMIT)


def _build_pool_weight(wpool32_ref, wpool_ref):
    wpool_ref[...] = jnp.zeros(wpool_ref.shape, BF16)
    for g in range(len(POOL_WINDOWS)):
        blk = slice(g * LANES, (g + 1) * LANES)
        wpool_ref[blk, blk] = wpool32_ref[g].astype(BF16)


def _adaln_kernel(cctx_ref, c_ref, wlo_ref, whi_ref, b_ref, o_ref, cond_ref):
    n_lat = c_ref.shape[0]
    cond_ref[...] = jnp.zeros(cond_ref.shape, F32)
    cond_ref[0:1, :] = cctx_ref[...]
    cond_ref[1:1 + n_lat, :] = c_ref[...]
    c = cond_ref[...]
    s = (c * jax.nn.sigmoid(c)).astype(BF16)
    half = wlo_ref.shape[1]
    o_ref[:, :half] = _dot(s, wlo_ref[...].astype(BF16)) + b_ref[:, :half]
    o_ref[:, half:] = _dot(s, whi_ref[...].astype(BF16)) + b_ref[:, half:]


def _adaln(c_ctx, c, w_ada, b_ada):
    n = w_ada.shape[1]
    tn = ADALN_TN
    return pl.pallas_call(
        _adaln_kernel,
        grid=(n // tn,),
        in_specs=[_const_spec((1, D_MODEL)), _const_spec(c.shape),
                  pl.BlockSpec((D_MODEL, tn // 2), lambda i: (0, 2 * i)),
                  pl.BlockSpec((D_MODEL, tn // 2), lambda i: (0, 2 * i + 1)),
                  pl.BlockSpec((1, tn), lambda i: (0, i))],
        out_specs=pl.BlockSpec((8, tn), lambda i: (0, i)),
        out_shape=jax.ShapeDtypeStruct((8, n), F32),
        scratch_shapes=[pltpu.VMEM((8, D_MODEL), F32)],
        compiler_params=_params(),
        name="adaln",
    )(c_ctx.reshape(1, D_MODEL), c, w_ada, w_ada, b_ada.reshape(1, n))


def _ffn_kernel(*refs, mod_off, n_a, split_out, n_conv, n_pre):
    refs = list(refs)
    xa_ref, xb_ref, mod_ref, g_ref, wgu_in, wdn_in = refs[:6]
    pre_src = refs[6:6 + n_pre]
    n_out = 2 if split_out else 1
    out_refs = refs[6 + n_pre:6 + n_pre + n_out]
    pre_dst = refs[6 + n_pre + n_out:6 + 2 * n_pre + n_out]
    wgu_ref, wdn_ref = refs[6 + 2 * n_pre + n_out:] if n_conv else (wgu_in, wdn_in)
    step = pl.program_id(0)

    if n_conv:
        @pl.when(step < n_conv)
        def _convert():
            for src, dst in ((wgu_in, wgu_ref), (wdn_in, wdn_ref)):
                rows = src.shape[0]
                dst[pl.ds(pl.multiple_of(step * rows, rows), rows), :] = src[...].astype(BF16)

    tile = step - n_conv

    def half_step(x_ref, o_ref):
        shift = mod_ref[0, mod_off:mod_off + 1, :]
        scale = mod_ref[0, mod_off + 1:mod_off + 2, :]
        gate = mod_ref[0, mod_off + 2:mod_off + 3, :]
        x = x_ref[...]
        h = _rms_mod(x, g_ref[...], shift, scale).astype(BF16)
        acc = None
        for c0, cw in FF_CHUNKS:
            a = _dot(h, wgu_ref[:, c0:c0 + cw])
            u = _dot(h, wgu_ref[:, D_FF + c0:D_FF + c0 + cw])
            t = (a * jax.nn.sigmoid(a) * u).astype(BF16)
            part = _dot(t, wdn_ref[c0:c0 + cw, :])
            acc = part if acc is None else acc + part
        o_ref[...] = x + (0.5 * gate) * acc

    @pl.when((tile >= 0) & (tile < n_a))
    def _tokens_a():
        half_step(xa_ref, out_refs[0])

    @pl.when(tile >= n_a)
    def _tokens_b():
        half_step(xb_ref, out_refs[-1])

    if n_pre:
        @pl.when((tile >= 0) & (tile < PRE_CHUNKS))
        def _preconvert():
            for src, dst in zip(pre_src, pre_dst):
                dst[...] = src[...].astype(BF16)


def _ffn(xa, xb, mod3, seq_b, mod_off, g, wgu, wdn, name, split_out=False, convert=False, preconvert=()):
    tm = FFN_TM
    n_a, n_b = xa.shape[0] // tm, xb.shape[0] // tm
    n_conv = N_CONV if convert else 0
    assert n_a + n_b >= PRE_CHUNKS
    tile = lambda s: jnp.maximum(s - n_conv, 0)
    a_map = lambda s: (jnp.minimum(tile(s), n_a - 1), 0)
    b_map = lambda s: (jnp.maximum(tile(s) - n_a, 0), 0)
    mod_map = lambda s: (jnp.where(tile(s) < n_a, 0, 1 + (jnp.maximum(tile(s) - n_a, 0) * tm) // seq_b), 0, 0)

    def chunk_spec(w, n_chunks, index):
        rows = w.shape[0] // n_chunks
        assert rows * n_chunks == w.shape[0] and rows % BF16_ROWS == 0
        return pl.BlockSpec((rows, w.shape[1]), lambda s: (jnp.minimum(index(s), n_chunks - 1), 0))

    if convert:
        w_specs = [chunk_spec(wgu, N_CONV, lambda s: s), chunk_spec(wdn, N_CONV, lambda s: s)]
        scratch = [pltpu.VMEM(wgu.shape, BF16), pltpu.VMEM(wdn.shape, BF16)]
    else:
        w_specs = [_resident_spec(wgu), _resident_spec(wdn)]
        scratch = []
    pre_specs = lambda: [chunk_spec(w, PRE_CHUNKS, tile) for w in preconvert]
    if split_out:
        out_specs = [pl.BlockSpec((tm, D_MODEL), a_map), pl.BlockSpec((tm, D_MODEL), b_map)]
        out_shape = [jax.ShapeDtypeStruct(xa.shape, F32), jax.ShapeDtypeStruct(xb.shape, F32)]
    else:
        out_specs = [pl.BlockSpec((tm, D_MODEL), lambda s: (tile(s), 0))]
        out_shape = [jax.ShapeDtypeStruct((xa.shape[0] + xb.shape[0], D_MODEL), F32)]
    return pl.pallas_call(
        functools.partial(_ffn_kernel, mod_off=mod_off, n_a=n_a, split_out=split_out, n_conv=n_conv,
                          n_pre=len(preconvert)),
        grid=(n_conv + n_a + n_b,),
        in_specs=[pl.BlockSpec((tm, D_MODEL), a_map), pl.BlockSpec((tm, D_MODEL), b_map),
                  pl.BlockSpec((1, N_MOD, D_MODEL), mod_map), _const_spec((1, D_MODEL))] + w_specs + pre_specs(),
        out_specs=out_specs + pre_specs(),
        out_shape=out_shape + [jax.ShapeDtypeStruct(w.shape, BF16) for w in preconvert],
        scratch_shapes=scratch,
        compiler_params=_params(),
        name=name,
    )(xa, xb, mod3, g, wgu, wdn, *preconvert)


def _gate_fillers(gates, h_rows, gate_cols):
    assert 2 * D_MODEL // MXU_COLS == N_HEADS

    def gate_block():
        c0 = len(gates) * MXU_COLS
        gates.append(jax.nn.sigmoid(_dot(h_rows(), gate_cols(c0, c0 + MXU_COLS))))

    return [gate_block] * N_HEADS


def _pool_fillers(st, delta_fn, wpool_ref, pscale_ref, wbrp_ref):
    def pool_delta():
        deltas = [delta_fn(g).astype(BF16) for g in range(len(POOL_WINDOWS))]
        st["d"] = jnp.concatenate(deltas, axis=1)

    def pool_weight():
        st["y"] = (_dot(st.pop("d"), wpool_ref[...]) * pscale_ref[...]).astype(BF16)

    def pool_out():
        st["a"] = _dot(st.pop("y"), wbrp_ref[...])

    return [pool_delta, pool_weight, pool_out]


def _merge_fillers(st, gates, rows, att, x_ref, o_ref, gate2, wbra_ref, wout_ref):
    def attention_branch():
        st["b"] = _dot(att.astype(BF16), wbra_ref[...])

    def gate():
        g = jnp.concatenate(gates, axis=1)
        merged = g[:, :D_MODEL] * st["a"][rows] + g[:, D_MODEL:] * st.pop("b")
        st["merged"] = merged.astype(BF16)

    def out():
        o_ref[rows, :] = x_ref[rows, :] + gate2 * _dot(st.pop("merged"), wout_ref[...])

    return [attention_branch, gate, out]


def _attend_tiles(n_tiles, tile_rows, attention, gate_fillers, pool_fillers, merge_fillers):
    idle = lambda: None
    extra = pool_fillers + [idle] * (N_HEADS - len(pool_fillers))
    for t in range(n_tiles):
        rows = slice(t * tile_rows, (t + 1) * tile_rows)
        gates = []
        fillers = [lambda g=g, e=e: (g(), e()) for g, e in zip(gate_fillers(rows, gates), extra)]
        pieces = merge_fillers(gates, rows, attention(t, fillers))
        extra = [idle] + pieces + [idle] * (N_HEADS - 1 - len(pieces))
    for piece in pieces:
        piece()


def _ctx_mixer_kernel(x_ref, mod_ref, g_ref, win_ref, qk_ref, wpool32_ref, pscale_ref,
                      wbrp_ref, wbra_ref, wout_ref, o_ref, ko_ref, vo_ref, wpool_ref, *, seq):
    @pl.when(pl.program_id(0) == 0)
    def _first_step():
        _build_pool_weight(wpool32_ref, wpool_ref)

    m_rows = x_ref.shape[0]
    h = _rms_mod(x_ref[...], g_ref[...], mod_ref[0, 3:4, :], mod_ref[0, 4:5, :]).astype(BF16)
    p = _dot(h, win_ref[:, 0:O_Q])
    q = _dot(h, win_ref[:, O_Q:O_K])
    k = _dot(h, win_ref[:, O_K:O_V])
    v = _dot(h, win_ref[:, O_V:O_G])
    for e in range(m_rows // seq):
        vo_ref[e] = v[e * seq:(e + 1) * seq].T
    scale = Q_SCALE
    zero = jnp.zeros((), BF16)
    one = jnp.ones((), BF16)
    st = {}

    def pool_delta(g):
        lanes = slice(g * LANES, (g + 1) * LANES)
        zeros_halo = jnp.zeros((POOL_HALO, LANES), F32)
        return jnp.concatenate(
            [_pool_delta(jnp.concatenate([zeros_halo, p[e * seq:(e + 1) * seq, lanes], zeros_halo], axis=0),
                         g, 0, seq) for e in range(m_rows // seq)], axis=0)

    def gate_fillers(rows, gates):
        return _gate_fillers(gates, lambda: h[rows], lambda c0, c1: win_ref[:, O_G + c0:O_G + c1])

    def attention(e, fillers):
        rows = slice(e * seq, (e + 1) * seq)
        qn, kn, vb = [], [], []
        for pr in range(N_PAIRS):
            lanes = slice(pr * LANES, (pr + 1) * LANES)
            kn.append(_head_norm(k[rows, lanes], qk_ref[1:2, :]))
            qn.append((_head_norm(q[rows, lanes], qk_ref[0:1, :]) * scale).astype(BF16))
            vb.append(v[rows, lanes].astype(BF16))
        ko_ref[e] = jnp.concatenate(kn, axis=1).T
        kb = [t.astype(BF16) for t in kn]

        def scores(head):
            return (_dot_nt(_own_lanes(head, qn[head // 2], zero), kb[head // 2]),)

        def weights(head, s):
            return jnp.exp(s - s.max(axis=-1, keepdims=True)).astype(BF16)

        def weighted_values(head, e_):
            return _dot(e_, _own_lanes(head, vb[head // 2], one))

        return _pipelined_heads(scores, weights, weighted_values, fillers)

    merge = functools.partial(_merge_fillers, st, x_ref=x_ref, o_ref=o_ref, gate2=mod_ref[0, 5:6, :],
                              wbra_ref=wbra_ref, wout_ref=wout_ref)
    _attend_tiles(m_rows // seq, seq, attention, gate_fillers,
                  _pool_fillers(st, pool_delta, wpool_ref, pscale_ref, wbrp_ref), merge)


def _ctx_mixer(x_all, n_tok, mod3, g, w_in, qk2, w_pool, pscale, wbrp, wbra, wout, seq):
    tm = CTX_TB * seq
    row = lambda s: (s, 0)
    heads = lambda s: (s, 0, 0)
    return pl.pallas_call(
        functools.partial(_ctx_mixer_kernel, seq=seq),
        grid=(n_tok // tm,),
        in_specs=[pl.BlockSpec((tm, D_MODEL), row),
                  _const_spec((1, N_MOD, D_MODEL)), _const_spec((1, D_MODEL)),
                  _resident_spec(w_in), _const_spec(qk2.shape),
                  _const_spec(w_pool.shape), _const_spec((1, D_POOL)),
                  _resident_spec(wbrp), _resident_spec(wbra), _resident_spec(wout)],
        out_specs=[pl.BlockSpec((tm, D_MODEL), row),
                   pl.BlockSpec((CTX_TB, D_ATT, seq), heads),
                   pl.BlockSpec((CTX_TB, D_ATT, seq), heads)],
        out_shape=[jax.ShapeDtypeStruct((n_tok, D_MODEL), F32),
                   jax.ShapeDtypeStruct((n_tok // seq, D_ATT, seq), F32),
                   jax.ShapeDtypeStruct((n_tok // seq, D_ATT, seq), F32)],
        scratch_shapes=[pltpu.VMEM((D_POOL, D_POOL), BF16)],
        compiler_params=_params(),
        name="ctx_mixer",
    )(x_all, mod3, g, w_in, qk2, w_pool, pscale, wbrp, wbra, wout)


def _lat_proj_kernel(x_ref, mod_ref, g_ref, win_ref, qk_ref, h_ref, p_ref, q_ref, k_ref, v_ref):
    h = _rms_mod(x_ref[...], g_ref[...], mod_ref[0, 3:4, :], mod_ref[0, 4:5, :]).astype(BF16)
    h_ref[...] = h
    p_ref[...] = _dot(h, win_ref[:, 0:O_Q])
    q = _dot(h, win_ref[:, O_Q:O_K])
    k = _dot(h, win_ref[:, O_K:O_V])
    v_ref[...] = _dot(h, win_ref[:, O_V:O_G]).astype(BF16)
    scale = Q_SCALE
    for pr in range(N_PAIRS):
        lanes = slice(pr * LANES, (pr + 1) * LANES)
        q_ref[:, lanes] = (_head_norm(q[:, lanes], qk_ref[0:1, :]) * scale).astype(BF16)
        k_ref[:, lanes] = _head_norm(k[:, lanes], qk_ref[1:2, :]).astype(BF16)


def _lat_proj(x_all, tok0, n_tok, seq, mod3, g, w_in, qk2):
    tm = PROJ_TM
    tile0 = tok0 // tm
    row = lambda s: (s, 0)
    shapes = [(D_MODEL, BF16), (D_POOL, F32), (D_ATT, BF16), (D_ATT, BF16), (D_ATT, BF16)]
    return pl.pallas_call(
        _lat_proj_kernel,
        grid=(n_tok // tm,),
        in_specs=[pl.BlockSpec((tm, D_MODEL), lambda s: (tile0 + s, 0)),
                  pl.BlockSpec((1, N_MOD, D_MODEL), lambda s: (1 + (s * tm) // seq, 0, 0)),
                  _const_spec((1, D_MODEL)), _resident_spec(w_in, (O_G, 0)),
                  _const_spec(qk2.shape)],
        out_specs=[pl.BlockSpec((tm, width), row) for width, _ in shapes],
        out_shape=[jax.ShapeDtypeStruct((n_tok, width), dtype) for width, dtype in shapes],
        compiler_params=_params(),
        name="lat_proj",
    )(x_all, mod3, g, w_in, qk2)


def _build_bias_table(rpb_ref, bias_ref):
    lane = lax.broadcasted_iota(jnp.int32, (GRID_W, LANES), 1)
    qc = lax.broadcasted_iota(jnp.int32, (GRID_W, LANES), 0)
    kc = lane & (GRID_W - 1)
    c0 = jnp.clip(qc - NA_KW // 2, 0, GRID_W - NA_KW)
    col_ok = (kc >= c0) & (kc < c0 + NA_KW)
    first_half = lane < GRID_W
    shift_a = LANES - (NA_KW - 1)
    shift_b = (GRID_W - (NA_KW - 1)) % LANES
    for h in range(N_HEADS):
        for i in range(2 * NA_KH - 2):
            va = jnp.broadcast_to(rpb_ref[h, i:i + 1, :], (GRID_W, LANES))
            vb = jnp.broadcast_to(rpb_ref[h, i + 1:i + 2, :], (GRID_W, LANES))
            ra = pltpu.roll(va, shift_a, 1, stride=1, stride_axis=0)
            rb = pltpu.roll(vb, shift_b, 1, stride=1, stride_axis=0)
            bias_ref[h, i] = jnp.where(col_ok, jnp.where(first_half, ra, rb), NEG)


def _lat_attention(qn, j, fillers, k_ref, v_ref, ck_ref, cv_ref, bias_ref, *, n_rows):
    n_keys = LAT_KEY_ROWS * GRID_W
    r0q = j * LAT_ROWS
    u0 = jnp.clip(r0q - NA_KH // 2, 0, n_rows - LAT_KEY_ROWS)
    key0 = pl.multiple_of(u0 * GRID_W, GRID_W)
    log2_w = GRID_W.bit_length() - 1
    zero = jnp.zeros((), BF16)
    one = jnp.ones((), BF16)

    lane_row = lax.broadcasted_iota(jnp.int32, (1, LANES), 1) & (HEAD_DIM - 1)
    q_tile_row = jnp.right_shift(lax.broadcasted_iota(jnp.int32, (LAT_TM, 1), 0), log2_w)
    q_hot = jnp.where(lane_row == q_tile_row, 1.0, 0.0).astype(BF16)
    k_row = u0 + jnp.right_shift(lax.broadcasted_iota(jnp.int32, (n_keys, 1), 0), log2_w)
    first = jnp.clip(r0q + lane_row - NA_KH // 2, 0, n_rows - NA_KH)
    outside = (lane_row < LAT_ROWS) & ((k_row < first) | (k_row >= first + NA_KH))
    k_mask = jnp.where(outside, NEG, 0.0).astype(BF16)

    def scores(head):
        lanes = slice((head // 2) * LANES, (head // 2 + 1) * LANES)
        qp = qn[:, lanes]
        s_loc = _dot_nt(_own_lanes(head, qp, q_hot),
                        _own_lanes(head, k_ref[0, pl.ds(key0, n_keys), lanes], k_mask))
        return s_loc, _dot(_own_lanes(head, qp, zero), ck_ref[lanes, :])

    def weights(head, s_loc, s_ctx):
        bias_rows = []
        for jr in range(LAT_ROWS):
            blocks = []
            for pi in range(LAT_KEY_ROWS // 2):
                idx = jnp.clip(u0 - r0q + 2 * pi - jr + NA_KH - 1, 0, 2 * NA_KH - 3)
                blocks.append(bias_ref[head, idx])
            bias_rows.append(jnp.concatenate(blocks, axis=1))
        s_loc = s_loc + jnp.concatenate(bias_rows, axis=0)
        m = jnp.maximum(s_loc.max(axis=-1, keepdims=True), s_ctx.max(axis=-1, keepdims=True))
        return jnp.concatenate([jnp.exp(s_loc - m).astype(BF16), jnp.exp(s_ctx - m).astype(BF16)], axis=1)

    def weighted_values(head, e):
        lanes = slice((head // 2) * LANES, (head // 2 + 1) * LANES)
        values = _own_lanes(head, v_ref[0, pl.ds(key0, n_keys), lanes], one)
        return _dot(e[:, :n_keys], values) + _dot_nt(e[:, n_keys:], cv_ref[head])

    return _pipelined_heads(scores, weights, weighted_values, fillers)


def _lat_mixer_kernel(x_ref, h_ref, q_ref, p_ref, pp_ref, pn_ref, mod_ref, k_ref, v_ref, ckin_ref, cvin_ref,
                      rpb_ref, wg_ref, wpool32_ref, pscale_ref, wbrp_ref, wbra_ref, wout_ref,
                      o_ref, bias_ref, wpool_ref, ck_ref, cv_ref, *, seq, n_rows):
    step = pl.program_id(0)
    tm = LAT_TILES * LAT_TM

    @pl.when(step == 0)
    def _first_step():
        _build_pool_weight(wpool32_ref, wpool_ref)
        _build_bias_table(rpb_ref, bias_ref)

    blk = lax.rem(step, seq // tm)
    st = {}

    @pl.when(blk == 0)
    def _sequence_cache():
        ck_ref[...] = ckin_ref[0].astype(BF16)
        low = lax.broadcasted_iota(jnp.int32, (LANES, 1), 0) < HEAD_DIM
        for head in range(N_HEADS):
            rows = slice((head // 2) * LANES, (head // 2 + 1) * LANES)
            cv_ref[head] = jnp.where(low if head % 2 == 0 else ~low, cvin_ref[0, rows, :], 1.0).astype(BF16)

    def pool_delta(g):
        lanes = slice(g * LANES, (g + 1) * LANES)
        pe = jnp.concatenate([pp_ref[:, lanes], p_ref[:, lanes], pn_ref[:, lanes]], axis=0)
        pos = blk * tm - POOL_HALO + lax.broadcasted_iota(jnp.int32, (tm + 2 * POOL_HALO, 1), 0)
        return _pool_delta(jnp.where((pos >= 0) & (pos < seq), pe, 0.0), g, blk * tm, seq)

    block_gates = []

    def gate_fillers(rows, gates):
        if not block_gates:
            fillers = _gate_fillers(block_gates, lambda: h_ref[...], lambda c0, c1: wg_ref[:, c0:c1])
            return [lambda f=f: (f(), gates.append(block_gates[-1][rows])) for f in fillers]
        gates.extend(g[rows] for g in block_gates)
        return [lambda: None] * N_HEADS

    pool_fillers = _pool_fillers(st, pool_delta, wpool_ref, pscale_ref, wbrp_ref)

    def attention(t, fillers):
        return _lat_attention(q_ref[t * LAT_TM:(t + 1) * LAT_TM, :], blk * LAT_TILES + t, fillers,
                              k_ref, v_ref, ck_ref, cv_ref, bias_ref, n_rows=n_rows)

    merge = functools.partial(_merge_fillers, st, x_ref=x_ref, o_ref=o_ref, gate2=mod_ref[0, 5:6, :],
                              wbra_ref=wbra_ref, wout_ref=wout_ref)
    _attend_tiles(LAT_TILES, LAT_TM, attention, gate_fillers, pool_fillers, merge)


def _lat_mixer(x_all, tok0, n_batch, seq, mod3, h, q, p, k3, v3, ck3, cv3, rpb_pad, w_in, w_pool, pscale,
               wbrp, wbra, wout):
    tm = LAT_TILES * LAT_TM
    n_rows = seq // GRID_W
    nb = seq // tm
    tile0 = tok0 // tm
    halo_per_tile = tm // POOL_HALO
    n_halo = p.shape[0] // POOL_HALO
    row = lambda s: (s, 0)
    seq_map = lambda s: (s // nb, 0, 0)
    return pl.pallas_call(
        functools.partial(_lat_mixer_kernel, seq=seq, n_rows=n_rows),
        grid=(n_batch * nb,),
        in_specs=[pl.BlockSpec((tm, D_MODEL), lambda s: (tile0 + s, 0)),
                  pl.BlockSpec((tm, D_MODEL), row),
                  pl.BlockSpec((tm, D_ATT), row),
                  pl.BlockSpec((tm, D_POOL), row),
                  pl.BlockSpec((POOL_HALO, D_POOL), lambda s: (jnp.maximum(s * halo_per_tile - 1, 0), 0)),
                  pl.BlockSpec((POOL_HALO, D_POOL),
                               lambda s: (jnp.minimum((s + 1) * halo_per_tile, n_halo - 1), 0)),
                  pl.BlockSpec((1, N_MOD, D_MODEL), lambda s: (1 + s // nb, 0, 0)),
                  pl.BlockSpec((1, seq, D_ATT), seq_map), pl.BlockSpec((1, seq, D_ATT), seq_map),
                  pl.BlockSpec((1,) + ck3.shape[1:], seq_map), pl.BlockSpec((1,) + cv3.shape[1:], seq_map),
                  _const_spec(rpb_pad.shape),
                  _resident_spec(w_in, (2 * D_MODEL, O_G // (2 * D_MODEL))),
                  _const_spec(w_pool.shape), _const_spec((1, D_POOL)),
                  _resident_spec(wbrp), _resident_spec(wbra), _resident_spec(wout)],
        out_specs=pl.BlockSpec((tm, D_MODEL), row),
        out_shape=jax.ShapeDtypeStruct((n_batch * seq, D_MODEL), F32),
        scratch_shapes=[pltpu.VMEM((N_HEADS, 2 * NA_KH - 2, GRID_W, LANES), F32),
                        pltpu.VMEM((D_POOL, D_POOL), BF16),
                        pltpu.VMEM(ck3.shape[1:], BF16),
                        pltpu.VMEM((N_HEADS, LANES, cv3.shape[2]), BF16)],
        compiler_params=_params(),
        name="lat_mixer",
    )(x_all, h, q, p, p, p, mod3, k3, v3, ck3, cv3, rpb_pad, w_in, w_pool, pscale, wbrp, wbra, wout)


def kernel(x_prompt, x_sample, cache_k, cache_v, c, c_ctx, w_ada, b_ada, g_ff1, w_ff1_in, w_ff1_out,
           g_mix, w_in, q_gain, k_gain, w_pool, pool_scale, rpb, w_br_pool, w_br_att, w_out, g_ff2,
           w_ff2_in, w_ff2_out):
    n_ctx, seq_ctx, _ = x_prompt.shape
    n_lat, seq_lat, _ = x_sample.shape
    depth = w_ada.shape[0]
    assert depth == 1 and n_lat + 1 <= 8
    l = 0

    xp = x_prompt.reshape(n_ctx * seq_ctx, D_MODEL)
    xs = x_sample.reshape(n_lat * seq_lat, D_MODEL)
    t_ctx, t_lat = xp.shape[0], xs.shape[0]
    mod3 = _adaln(c_ctx, c, w_ada[l], b_ada[l]).reshape(8, N_MOD, D_MODEL)

    pscale = pool_scale[l].reshape(1, D_POOL)
    qk2 = jnp.tile(jnp.stack([q_gain[l], k_gain[l]]), (1, LANES // HEAD_DIM))
    g1, gm, g2 = g_ff1[l].reshape(1, -1), g_mix[l].reshape(1, -1), g_ff2[l].reshape(1, -1)
    rpb_pad = jnp.pad(rpb[l], ((0, 0), (0, 0), (0, LANES - rpb.shape[-1])))
    ck3 = jnp.transpose(cache_k[:, l], (0, 2, 3, 1)).reshape(n_lat, D_ATT, -1)
    cv3 = jnp.transpose(cache_v[:, l], (0, 2, 3, 1)).reshape(n_lat, D_ATT, -1)

    x1, w2gu, w2dn, win, wbrp, wbra, wout = _ffn(
        xp, xs, mod3, seq_lat, 0, g1, w_ff1_in[l], w_ff1_out[l], "ffn1", convert=True,
        preconvert=(w_ff2_in[l], w_ff2_out[l], w_in[l], w_br_pool[l], w_br_att[l], w_out[l]))
    x2p, k_p, v_p = _ctx_mixer(x1, t_ctx, mod3, gm, win, qk2, w_pool[l], pscale, wbrp, wbra, wout, seq_ctx)
    h_s, p_s, q_s, k_s, v_s = _lat_proj(x1, t_ctx, t_lat, seq_lat, mod3, gm, win, qk2)
    x2s = _lat_mixer(x1, t_ctx, n_lat, seq_lat, mod3, h_s, q_s, p_s,
                     k_s.reshape(n_lat, seq_lat, D_ATT), v_s.reshape(n_lat, seq_lat, D_ATT), ck3, cv3,
                     rpb_pad, win, w_pool[l], pscale, wbrp, wbra, wout)
    yp, ys = _ffn(x2p, x2s, mod3, seq_lat, 6, g2, w2gu, w2dn, "ffn2", split_out=True)

    new_k = jnp.transpose(k_p.reshape(n_ctx, 1, N_HEADS, HEAD_DIM, seq_ctx), (0, 1, 4, 2, 3))
    new_v = jnp.transpose(v_p.reshape(n_ctx, 1, N_HEADS, HEAD_DIM, seq_ctx), (0, 1, 4, 2, 3))
    return (yp.reshape(x_prompt.shape), ys.reshape(x_sample.shape), new_k, new_v)
```

```python
import functools
import math

import jax
import jax.numpy as jnp
from jax import lax
from jax.experimental import pallas as pl
from jax.experimental.pallas import tpu as pltpu

F32 = jnp.float32
BF16 = jnp.bfloat16

D_MODEL = 1024
N_HEADS = 8
HEAD_DIM = 64
D_ATT = N_HEADS * HEAD_DIM
D_POOL = 512
POOL_WINDOWS = (2, 4, 8, 16)
POOL_HALO = 8
D_FF = 2816
N_MOD = 9
GRID_W = 64
NA_KH = 8
NA_KW = 16
EPS = 1e-6
NEG = -1e30

LANES = 128
BF16_ROWS = 16
N_PAIRS = D_ATT // LANES
MXU_COLS = 256
FF_CHUNKS = ((0, 768), (768, 768), (1536, 768), (2304, 512))
ADALN_TN = 2304
FFN_TM = 512
CTX_TB = 2
PROJ_TM = 1024
LAT_ROWS = 4
LAT_TM = LAT_ROWS * GRID_W
LAT_TILES = 2
LAT_KEY_ROWS = 12
SCORE_LOOKAHEAD = 2
N_CONV = 8
PRE_CHUNKS = 16
VMEM_LIMIT = 56 * 1024 * 1024

Q_SCALE = 1.0 / math.sqrt(HEAD_DIM)

O_Q = D_POOL
O_K = D_POOL + D_ATT
O_V = D_POOL + 2 * D_ATT
O_G = D_POOL + 3 * D_ATT


def _dot(a, b):
    return jnp.dot(a, b, preferred_element_type=F32)


def _dot_nt(a, b):
    return lax.dot_general(a, b, (((1,), (1,)), ((), ())), preferred_element_type=F32)


def _rms_mod(x, g, shift, scale):
    ms = jnp.mean(x * x, axis=-1, keepdims=True)
    return x * lax.rsqrt(ms + EPS) * (g * (1.0 + scale)) + shift


def _mod_row(mod_ref, row, r):
    return mod_ref[pl.ds(row, 1), r * D_MODEL:(r + 1) * D_MODEL]


def _low_half():
    return lax.broadcasted_iota(jnp.int32, (1, LANES), 1) < HEAD_DIM


def _head_norm(t, gain2):
    lo = _low_half()
    sq = t * t
    s0 = jnp.sum(jnp.where(lo, sq, 0.0), axis=-1, keepdims=True)
    s1 = jnp.sum(jnp.where(lo, 0.0, sq), axis=-1, keepdims=True)
    r0 = lax.rsqrt(s0 * (1.0 / HEAD_DIM) + EPS)
    r1 = lax.rsqrt(s1 * (1.0 / HEAD_DIM) + EPS)
    return t * jnp.where(lo, r0, r1) * gain2


def _own_lanes(head, mine, other):
    lo = _low_half()
    return jnp.where(lo, mine, other) if head % 2 == 0 else jnp.where(lo, other, mine)


def _pipelined_heads(scores, weights, weighted_values, fillers):
    lo = _low_half()
    outs = []
    pending_scores = [scores(head) for head in range(SCORE_LOOKAHEAD)]
    pending_weights = [weights(0, *pending_scores.pop(0))]
    for head in range(N_HEADS):
        if head + SCORE_LOOKAHEAD < N_HEADS:
            pending_scores.append(scores(head + SCORE_LOOKAHEAD))
        fillers[head]()
        if head + 1 < N_HEADS:
            pending_weights.append(weights(head + 1, *pending_scores.pop(0)))
        outs.append(weighted_values(head, pending_weights.pop(0)))
    att_pairs = []
    for pr in range(N_PAIRS):
        even, odd = outs[2 * pr], outs[2 * pr + 1]
        num = jnp.where(lo, even, odd)
        den = pltpu.roll(jnp.where(lo, odd, even), HEAD_DIM, 1)
        att_pairs.append(num / den)
    return jnp.concatenate(att_pairs, axis=1)


def _pool_delta(p_ext, g, pos0, seq_len):
    w = POOL_WINDOWS[g]
    te = p_ext.shape[0]
    tm = te - 2 * POOL_HALO
    t = (pos0 + lax.broadcasted_iota(jnp.int32, (tm, 1), 0)).astype(F32)
    s = p_ext
    step = 1
    while step < w:
        s = s + pltpu.roll(s, step, 0)
        step *= 2
    ahead = w // 2 - 1
    if ahead:
        s = pltpu.roll(s, te - ahead, 0)
    win = s[POOL_HALO:POOL_HALO + tm]
    cnt = jnp.minimum(t + float(w // 2), float(seq_len)) - jnp.maximum(t - float(w // 2), 0.0)
    return win / cnt - p_ext[POOL_HALO:POOL_HALO + tm]


def _const_spec(shape):
    nd = len(shape)
    return pl.BlockSpec(shape, lambda *_: (0,) * nd)


def _resident_spec(w, col_block=None):
    width, cidx = col_block if col_block else (w.shape[1], 0)
    return pl.BlockSpec((w.shape[0], width), lambda *_: (0, cidx), pipeline_mode=pl.Buffered(1))


def _params():
    return pltpu.CompilerParams(dimension_semantics=("arbitrary",), vmem_limit_bytes=VMEM_LIMIT)


def _build_pool_weight(wpool32_ref, wpool_ref):
    wpool_ref[...] = jnp.zeros(wpool_ref.shape, BF16)
    for g in range(len(POOL_WINDOWS)):
        blk = slice(g * LANES, (g + 1) * LANES)
        wpool_ref[blk, blk] = wpool32_ref[g].astype(BF16)


def _adaln_kernel(cctx_ref, c_ref, wlo_ref, whi_ref, b_ref, o_ref, cond_ref):
    n_lat = c_ref.shape[0]
    cond_ref[...] = jnp.zeros(cond_ref.shape, F32)
    cond_ref[0:1, :] = cctx_ref[...]
    cond_ref[1:1 + n_lat, :] = c_ref[...]
    c = cond_ref[...]
    s = (c * jax.nn.sigmoid(c)).astype(BF16)
    half = wlo_ref.shape[1]
    o_ref[:, :half] = _dot(s, wlo_ref[...].astype(BF16)) + b_ref[:, :half]
    o_ref[:, half:] = _dot(s, whi_ref[...].astype(BF16)) + b_ref[:, half:]


def _adaln(c_ctx, c, w_ada, b_ada):
    n = w_ada.shape[1]
    tn = ADALN_TN
    return pl.pallas_call(
        _adaln_kernel,
        grid=(n // tn,),
        in_specs=[_const_spec((1, D_MODEL)), _const_spec(c.shape),
                  pl.BlockSpec((D_MODEL, tn // 2), lambda i: (0, 2 * i)),
                  pl.BlockSpec((D_MODEL, tn // 2), lambda i: (0, 2 * i + 1)),
                  pl.BlockSpec((1, tn), lambda i: (0, i))],
        out_specs=pl.BlockSpec((8, tn), lambda i: (0, i)),
        out_shape=jax.ShapeDtypeStruct((8, n), F32),
        scratch_shapes=[pltpu.VMEM((8, D_MODEL), F32)],
        compiler_params=_params(),
        name="adaln",
    )(c_ctx.reshape(1, D_MODEL), c, w_ada, w_ada, b_ada.reshape(1, n))


def _ffn_kernel(*refs, mod_off, n_a, split_out, n_conv, n_pre, seq_b):
    refs = list(refs)
    xa_ref, xb_ref, mod_ref, g_ref, wgu_in, wdn_in = refs[:6]
    pre_src = refs[6:6 + n_pre]
    n_out = 2 if split_out else 1
    out_refs = refs[6 + n_pre:6 + n_pre + n_out]
    pre_dst = refs[6 + n_pre + n_out:6 + 2 * n_pre + n_out]
    wgu_ref, wdn_ref = refs[6 + 2 * n_pre + n_out:] if n_conv else (wgu_in, wdn_in)
    step = pl.program_id(0)

    if n_conv:
        @pl.when(step < n_conv)
        def _convert():
            for src, dst in ((wgu_in, wgu_ref), (wdn_in, wdn_ref)):
                rows = src.shape[0]
                dst[pl.ds(pl.multiple_of(step * rows, rows), rows), :] = src[...].astype(BF16)

    tile = step - n_conv

    def half_step(x_ref, o_ref):
        tm = x_ref.shape[0]
        row = jnp.where(tile < n_a, 0, 1 + (jnp.maximum(tile - n_a, 0) * tm) // seq_b)
        shift = _mod_row(mod_ref, row, mod_off)
        scale = _mod_row(mod_ref, row, mod_off + 1)
        gate = _mod_row(mod_ref, row, mod_off + 2)
        x = x_ref[...]
        h = _rms_mod(x, g_ref[...], shift, scale).astype(BF16)
        acc = None
        for c0, cw in FF_CHUNKS:
            a = _dot(h, wgu_ref[:, c0:c0 + cw])
            u = _dot(h, wgu_ref[:, D_FF + c0:D_FF + c0 + cw])
            t = (a * jax.nn.sigmoid(a) * u).astype(BF16)
            part = _dot(t, wdn_ref[c0:c0 + cw, :])
            acc = part if acc is None else acc + part
        o_ref[...] = x + (0.5 * gate) * acc

    @pl.when((tile >= 0) & (tile < n_a))
    def _tokens_a():
        half_step(xa_ref, out_refs[0])

    @pl.when(tile >= n_a)
    def _tokens_b():
        half_step(xb_ref, out_refs[-1])

    if n_pre:
        @pl.when((tile >= 0) & (tile < PRE_CHUNKS))
        def _preconvert():
            for src, dst in zip(pre_src, pre_dst):
                dst[...] = src[...].astype(BF16)


def _ffn(xa, xb, mod2, seq_b, mod_off, g, wgu, wdn, name, split_out=False, convert=False, preconvert=()):
    tm = FFN_TM
    n_a, n_b = xa.shape[0] // tm, xb.shape[0] // tm
    n_conv = N_CONV if convert else 0
    assert n_a + n_b >= PRE_CHUNKS
    tile = lambda s: jnp.maximum(s - n_conv, 0)
    a_map = lambda s: (jnp.minimum(tile(s), n_a - 1), 0)
    b_map = lambda s: (jnp.maximum(tile(s) - n_a, 0), 0)

    def chunk_spec(w, n_chunks, index):
        rows = w.shape[0] // n_chunks
        assert rows * n_chunks == w.shape[0] and rows % BF16_ROWS == 0
        return pl.BlockSpec((rows, w.shape[1]), lambda s: (jnp.minimum(index(s), n_chunks - 1), 0))

    if convert:
        w_specs = [chunk_spec(wgu, N_CONV, lambda s: s), chunk_spec(wdn, N_CONV, lambda s: s)]
        scratch = [pltpu.VMEM(wgu.shape, BF16), pltpu.VMEM(wdn.shape, BF16)]
    else:
        w_specs = [_resident_spec(wgu), _resident_spec(wdn)]
        scratch = []
    pre_specs = lambda: [chunk_spec(w, PRE_CHUNKS, tile) for w in preconvert]
    if split_out:
        out_specs = [pl.BlockSpec((tm, D_MODEL), a_map), pl.BlockSpec((tm, D_MODEL), b_map)]
        out_shape = [jax.ShapeDtypeStruct(xa.shape, F32), jax.ShapeDtypeStruct(xb.shape, F32)]
    else:
        out_specs = [pl.BlockSpec((tm, D_MODEL), lambda s: (tile(s), 0))]
        out_shape = [jax.ShapeDtypeStruct((xa.shape[0] + xb.shape[0], D_MODEL), F32)]
    return pl.pallas_call(
        functools.partial(_ffn_kernel, mod_off=mod_off, n_a=n_a, split_out=split_out, n_conv=n_conv,
                          n_pre=len(preconvert), seq_b=seq_b),
        grid=(n_conv + n_a + n_b,),
        in_specs=[pl.BlockSpec((tm, D_MODEL), a_map), pl.BlockSpec((tm, D_MODEL), b_map),
                  _const_spec(mod2.shape), _const_spec((1, D_MODEL))] + w_specs + pre_specs(),
        out_specs=out_specs + pre_specs(),
        out_shape=out_shape + [jax.ShapeDtypeStruct(w.shape, BF16) for w in preconvert],
        scratch_shapes=scratch,
        compiler_params=_params(),
        name=name,
    )(xa, xb, mod2, g, wgu, wdn, *preconvert)


def _gate_fillers(gates, h_rows, gate_cols):
    assert 2 * D_MODEL // MXU_COLS == N_HEADS

    def gate_block():
        c0 = len(gates) * MXU_COLS
        gates.append(jax.nn.sigmoid(_dot(h_rows(), gate_cols(c0, c0 + MXU_COLS))))

    return [gate_block] * N_HEADS


def _pool_fillers(st, delta_fn, wpool_ref, pscale_ref, wbrp_ref):
    def pool_delta():
        deltas = [delta_fn(g).astype(BF16) for g in range(len(POOL_WINDOWS))]
        st["d"] = jnp.concatenate(deltas, axis=1)

    def pool_weight():
        st["y"] = (_dot(st.pop("d"), wpool_ref[...]) * pscale_ref[...]).astype(BF16)

    def pool_out():
        st["a"] = _dot(st.pop("y"), wbrp_ref[...])

    return [pool_delta, pool_weight, pool_out]


def _merge_fillers(st, gates, rows, att, x_ref, o_ref, gate2, wbra_ref, wout_ref):
    def attention_branch():
        st["b"] = _dot(att.astype(BF16), wbra_ref[...])

    def gate():
        g = jnp.concatenate(gates, axis=1)
        merged = g[:, :D_MODEL] * st["a"][rows] + g[:, D_MODEL:] * st.pop("b")
        st["merged"] = merged.astype(BF16)

    def out():
        o_ref[rows, :] = x_ref[rows, :] + gate2 * _dot(st.pop("merged"), wout_ref[...])

    return [attention_branch, gate, out]


def _attend_tiles(n_tiles, tile_rows, attention, gate_fillers, pool_fillers, merge_fillers):
    idle = lambda: None
    extra = pool_fillers + [idle] * (N_HEADS - len(pool_fillers))
    for t in range(n_tiles):
        rows = slice(t * tile_rows, (t + 1) * tile_rows)
        gates = []
        fillers = [lambda g=g, e=e: (g(), e()) for g, e in zip(gate_fillers(rows, gates), extra)]
        pieces = merge_fillers(gates, rows, attention(t, fillers))
        extra = [idle] + pieces + [idle] * (N_HEADS - 1 - len(pieces))
    for piece in pieces:
        piece()


def _ctx_mixer_kernel(x_ref, mod_ref, g_ref, win_ref, qk_ref, wpool32_ref, pscale_ref,
                      wbrp_ref, wbra_ref, wout_ref, o_ref, ko_ref, vo_ref, wpool_ref, *, seq):
    @pl.when(pl.program_id(0) == 0)
    def _first_step():
        _build_pool_weight(wpool32_ref, wpool_ref)

    m_rows = x_ref.shape[0]
    h = _rms_mod(x_ref[...], g_ref[...], _mod_row(mod_ref, 0, 3), _mod_row(mod_ref, 0, 4)).astype(BF16)
    p = _dot(h, win_ref[:, 0:O_Q])
    q = _dot(h, win_ref[:, O_Q:O_K])
    k = _dot(h, win_ref[:, O_K:O_V])
    v = _dot(h, win_ref[:, O_V:O_G])
    for e in range(m_rows // seq):
        vo_ref[e] = v[e * seq:(e + 1) * seq].T
    scale = Q_SCALE
    zero = jnp.zeros((), BF16)
    one = jnp.ones((), BF16)
    st = {}

    def pool_delta(g):
        lanes = slice(g * LANES, (g + 1) * LANES)
        zeros_halo = jnp.zeros((POOL_HALO, LANES), F32)
        return jnp.concatenate(
            [_pool_delta(jnp.concatenate([zeros_halo, p[e * seq:(e + 1) * seq, lanes], zeros_halo], axis=0),
                         g, 0, seq) for e in range(m_rows // seq)], axis=0)

    def gate_fillers(rows, gates):
        return _gate_fillers(gates, lambda: h[rows], lambda c0, c1: win_ref[:, O_G + c0:O_G + c1])

    def attention(e, fillers):
        rows = slice(e * seq, (e + 1) * seq)
        qn, kn, vb = [], [], []
        for pr in range(N_PAIRS):
            lanes = slice(pr * LANES, (pr + 1) * LANES)
            kn.append(_head_norm(k[rows, lanes], qk_ref[1:2, :]))
            qn.append((_head_norm(q[rows, lanes], qk_ref[0:1, :]) * scale).astype(BF16))
            vb.append(v[rows, lanes].astype(BF16))
        ko_ref[e] = jnp.concatenate(kn, axis=1).T
        kb = [t.astype(BF16) for t in kn]

        def scores(head):
            return (_dot_nt(_own_lanes(head, qn[head // 2], zero), kb[head // 2]),)

        def weights(head, s):
            return jnp.exp(s - s.max(axis=-1, keepdims=True)).astype(BF16)

        def weighted_values(head, e_):
            return _dot(e_, _own_lanes(head, vb[head // 2], one))

        return _pipelined_heads(scores, weights, weighted_values, fillers)

    merge = functools.partial(_merge_fillers, st, x_ref=x_ref, o_ref=o_ref, gate2=_mod_row(mod_ref, 0, 5),
                              wbra_ref=wbra_ref, wout_ref=wout_ref)
    _attend_tiles(m_rows // seq, seq, attention, gate_fillers,
                  _pool_fillers(st, pool_delta, wpool_ref, pscale_ref, wbrp_ref), merge)


def _ctx_mixer(x_all, n_tok, mod2, g, w_in, qk2, w_pool, pscale, wbrp, wbra, wout, seq):
    tm = CTX_TB * seq
    row = lambda s: (s, 0)
    heads = lambda s: (s, 0, 0)
    return pl.pallas_call(
        functools.partial(_ctx_mixer_kernel, seq=seq),
        grid=(n_tok // tm,),
        in_specs=[pl.BlockSpec((tm, D_MODEL), row),
                  _const_spec(mod2.shape), _const_spec((1, D_MODEL)),
                  _resident_spec(w_in), _const_spec(qk2.shape),
                  _const_spec(w_pool.shape), _const_spec((1, D_POOL)),
                  _resident_spec(wbrp), _resident_spec(wbra), _resident_spec(wout)],
        out_specs=[pl.BlockSpec((tm, D_MODEL), row),
                   pl.BlockSpec((CTX_TB, D_ATT, seq), heads),
                   pl.BlockSpec((CTX_TB, D_ATT, seq), heads)],
        out_shape=[jax.ShapeDtypeStruct((n_tok, D_MODEL), F32),
                   jax.ShapeDtypeStruct((n_tok // seq, D_ATT, seq), F32),
                   jax.ShapeDtypeStruct((n_tok // seq, D_ATT, seq), F32)],
        scratch_shapes=[pltpu.VMEM((D_POOL, D_POOL), BF16)],
        compiler_params=_params(),
        name="ctx_mixer",
    )(x_all, mod2, g, w_in, qk2, w_pool, pscale, wbrp, wbra, wout)


def _lat_proj_kernel(x_ref, mod_ref, g_ref, win_ref, qk_ref, h_ref, p_ref, q_ref, k_ref, v_ref, *, seq):
    row = 1 + (pl.program_id(0) * x_ref.shape[0]) // seq
    h = _rms_mod(x_ref[...], g_ref[...], _mod_row(mod_ref, row, 3), _mod_row(mod_ref, row, 4)).astype(BF16)
    h_ref[...] = h
    p_ref[...] = _dot(h, win_ref[:, 0:O_Q])
    q = _dot(h, win_ref[:, O_Q:O_K])
    k = _dot(h, win_ref[:, O_K:O_V])
    v_ref[...] = _dot(h, win_ref[:, O_V:O_G]).astype(BF16)
    scale = Q_SCALE
    for pr in range(N_PAIRS):
        lanes = slice(pr * LANES, (pr + 1) * LANES)
        q_ref[:, lanes] = (_head_norm(q[:, lanes], qk_ref[0:1, :]) * scale).astype(BF16)
        k_ref[:, lanes] = _head_norm(k[:, lanes], qk_ref[1:2, :]).astype(BF16)


def _lat_proj(x_all, tok0, n_tok, seq, mod2, g, w_in, qk2):
    tm = PROJ_TM
    tile0 = tok0 // tm
    row = lambda s: (s, 0)
    shapes = [(D_MODEL, BF16), (D_POOL, F32), (D_ATT, BF16), (D_ATT, BF16), (D_ATT, BF16)]
    return pl.pallas_call(
        functools.partial(_lat_proj_kernel, seq=seq),
        grid=(n_tok // tm,),
        in_specs=[pl.BlockSpec((tm, D_MODEL), lambda s: (tile0 + s, 0)),
                  _const_spec(mod2.shape),
                  _const_spec((1, D_MODEL)), _resident_spec(w_in, (O_G, 0)),
                  _const_spec(qk2.shape)],
        out_specs=[pl.BlockSpec((tm, width), row) for width, _ in shapes],
        out_shape=[jax.ShapeDtypeStruct((n_tok, width), dtype) for width, dtype in shapes],
        compiler_params=_params(),
        name="lat_proj",
    )(x_all, mod2, g, w_in, qk2)


def _build_bias_table(rpb_ref, bias_ref):
    lane = lax.broadcasted_iota(jnp.int32, (GRID_W, LANES), 1)
    qc = lax.broadcasted_iota(jnp.int32, (GRID_W, LANES), 0)
    kc = lane & (GRID_W - 1)
    c0 = jnp.clip(qc - NA_KW // 2, 0, GRID_W - NA_KW)
    col_ok = (kc >= c0) & (kc < c0 + NA_KW)
    first_half = lane < GRID_W
    shift_a = LANES - (NA_KW - 1)
    shift_b = (GRID_W - (NA_KW - 1)) % LANES
    for h in range(N_HEADS):
        for i in range(2 * NA_KH - 2):
            va = jnp.broadcast_to(rpb_ref[h, i:i + 1, :], (GRID_W, LANES))
            vb = jnp.broadcast_to(rpb_ref[h, i + 1:i + 2, :], (GRID_W, LANES))
            ra = pltpu.roll(va, shift_a, 1, stride=1, stride_axis=0)
            rb = pltpu.roll(vb, shift_b, 1, stride=1, stride_axis=0)
            bias_ref[h, i] = jnp.where(col_ok, jnp.where(first_half, ra, rb), NEG)


def _lat_attention(qn, j, fillers, k_ref, v_ref, ck_ref, cv_ref, bias_ref, *, n_rows):
    n_keys = LAT_KEY_ROWS * GRID_W
    r0q = j * LAT_ROWS
    u0 = jnp.clip(r0q - NA_KH // 2, 0, n_rows - LAT_KEY_ROWS)
    key0 = pl.multiple_of(u0 * GRID_W, GRID_W)
    log2_w = GRID_W.bit_length() - 1
    zero = jnp.zeros((), BF16)
    one = jnp.ones((), BF16)

    lane_row = lax.broadcasted_iota(jnp.int32, (1, LANES), 1) & (HEAD_DIM - 1)
    q_tile_row = jnp.right_shift(lax.broadcasted_iota(jnp.int32, (LAT_TM, 1), 0), log2_w)
    q_hot = jnp.where(lane_row == q_tile_row, 1.0, 0.0).astype(BF16)
    k_row = u0 + jnp.right_shift(lax.broadcasted_iota(jnp.int32, (n_keys, 1), 0), log2_w)
    first = jnp.clip(r0q + lane_row - NA_KH // 2, 0, n_rows - NA_KH)
    outside = (lane_row < LAT_ROWS) & ((k_row < first) | (k_row >= first + NA_KH))
    k_mask = jnp.where(outside, NEG, 0.0).astype(BF16)

    def scores(head):
        lanes = slice((head // 2) * LANES, (head // 2 + 1) * LANES)
        qp = qn[:, lanes]
        s_loc = _dot_nt(_own_lanes(head, qp, q_hot),
                        _own_lanes(head, k_ref[0, pl.ds(key0, n_keys), lanes], k_mask))
        return s_loc, _dot(_own_lanes(head, qp, zero), ck_ref[lanes, :])

    def weights(head, s_loc, s_ctx):
        bias_rows = []
        for jr in range(LAT_ROWS):
            blocks = []
            for pi in range(LAT_KEY_ROWS // 2):
                idx = jnp.clip(u0 - r0q + 2 * pi - jr + NA_KH - 1, 0, 2 * NA_KH - 3)
                blocks.append(bias_ref[head, idx])
            bias_rows.append(jnp.concatenate(blocks, axis=1))
        s_loc = s_loc + jnp.concatenate(bias_rows, axis=0)
        m = jnp.maximum(s_loc.max(axis=-1, keepdims=True), s_ctx.max(axis=-1, keepdims=True))
        return jnp.concatenate([jnp.exp(s_loc - m).astype(BF16), jnp.exp(s_ctx - m).astype(BF16)], axis=1)

    def weighted_values(head, e):
        lanes = slice((head // 2) * LANES, (head // 2 + 1) * LANES)
        values = _own_lanes(head, v_ref[0, pl.ds(key0, n_keys), lanes], one)
        return _dot(e[:, :n_keys], values) + _dot_nt(e[:, n_keys:], cv_ref[head])

    return _pipelined_heads(scores, weights, weighted_values, fillers)


def _lat_mixer_kernel(x_ref, h_ref, q_ref, p_ref, pp_ref, pn_ref, mod_ref, k_ref, v_ref, ckin_ref, cvin_ref,
                      rpb_ref, wg_ref, wpool32_ref, pscale_ref, wbrp_ref, wbra_ref, wout_ref,
                      o_ref, bias_ref, wpool_ref, ck_ref, cv_ref, *, seq, n_rows):
    step = pl.program_id(0)
    tm = LAT_TILES * LAT_TM

    @pl.when(step == 0)
    def _first_step():
        _build_pool_weight(wpool32_ref, wpool_ref)
        _build_bias_table(rpb_ref, bias_ref)

    blk = lax.rem(step, seq // tm)
    st = {}

    @pl.when(blk == 0)
    def _sequence_cache():
        ck_ref[...] = ckin_ref[0].astype(BF16)
        low = lax.broadcasted_iota(jnp.int32, (LANES, 1), 0) < HEAD_DIM
        for head in range(N_HEADS):
            rows = slice((head // 2) * LANES, (head // 2 + 1) * LANES)
            cv_ref[head] = jnp.where(low if head % 2 == 0 else ~low, cvin_ref[0, rows, :], 1.0).astype(BF16)

    def pool_delta(g):
        lanes = slice(g * LANES, (g + 1) * LANES)
        pe = jnp.concatenate([pp_ref[:, lanes], p_ref[:, lanes], pn_ref[:, lanes]], axis=0)
        pos = blk * tm - POOL_HALO + lax.broadcasted_iota(jnp.int32, (tm + 2 * POOL_HALO, 1), 0)
        return _pool_delta(jnp.where((pos >= 0) & (pos < seq), pe, 0.0), g, blk * tm, seq)

    block_gates = []

    def gate_fillers(rows, gates):
        if not block_gates:
            fillers = _gate_fillers(block_gates, lambda: h_ref[...], lambda c0, c1: wg_ref[:, c0:c1])
            return [lambda f=f: (f(), gates.append(block_gates[-1][rows])) for f in fillers]
        gates.extend(g[rows] for g in block_gates)
        return [lambda: None] * N_HEADS

    pool_fillers = _pool_fillers(st, pool_delta, wpool_ref, pscale_ref, wbrp_ref)

    def attention(t, fillers):
        return _lat_attention(q_ref[t * LAT_TM:(t + 1) * LAT_TM, :], blk * LAT_TILES + t, fillers,
                              k_ref, v_ref, ck_ref, cv_ref, bias_ref, n_rows=n_rows)

    merge = functools.partial(_merge_fillers, st, x_ref=x_ref, o_ref=o_ref,
                              gate2=_mod_row(mod_ref, 1 + step // (seq // tm), 5),
                              wbra_ref=wbra_ref, wout_ref=wout_ref)
    _attend_tiles(LAT_TILES, LAT_TM, attention, gate_fillers, pool_fillers, merge)


def _lat_mixer(x_all, tok0, n_batch, seq, mod2, h, q, p, k3, v3, ck3, cv3, rpb_pad, w_in, w_pool, pscale,
               wbrp, wbra, wout):
    tm = LAT_TILES * LAT_TM
    n_rows = seq // GRID_W
    nb = seq // tm
    tile0 = tok0 // tm
    halo_per_tile = tm // POOL_HALO
    n_halo = p.shape[0] // POOL_HALO
    row = lambda s: (s, 0)
    seq_map = lambda s: (s // nb, 0, 0)
    return pl.pallas_call(
        functools.partial(_lat_mixer_kernel, seq=seq, n_rows=n_rows),
        grid=(n_batch * nb,),
        in_specs=[pl.BlockSpec((tm, D_MODEL), lambda s: (tile0 + s, 0)),
                  pl.BlockSpec((tm, D_MODEL), row),
                  pl.BlockSpec((tm, D_ATT), row),
                  pl.BlockSpec((tm, D_POOL), row),
                  pl.BlockSpec((POOL_HALO, D_POOL), lambda s: (jnp.maximum(s * halo_per_tile - 1, 0), 0)),
                  pl.BlockSpec((POOL_HALO, D_POOL),
                               lambda s: (jnp.minimum((s + 1) * halo_per_tile, n_halo - 1), 0)),
                  _const_spec(mod2.shape),
                  pl.BlockSpec((1, seq, D_ATT), seq_map), pl.BlockSpec((1, seq, D_ATT), seq_map),
                  pl.BlockSpec((1,) + ck3.shape[1:], seq_map), pl.BlockSpec((1,) + cv3.shape[1:], seq_map),
                  _const_spec(rpb_pad.shape),
                  _resident_spec(w_in, (2 * D_MODEL, O_G // (2 * D_MODEL))),
                  _const_spec(w_pool.shape), _const_spec((1, D_POOL)),
                  _resident_spec(wbrp), _resident_spec(wbra), _resident_spec(wout)],
        out_specs=pl.BlockSpec((tm, D_MODEL), row),
        out_shape=jax.ShapeDtypeStruct((n_batch * seq, D_MODEL), F32),
        scratch_shapes=[pltpu.VMEM((N_HEADS, 2 * NA_KH - 2, GRID_W, LANES), F32),
                        pltpu.VMEM((D_POOL, D_POOL), BF16),
                        pltpu.VMEM(ck3.shape[1:], BF16),
                        pltpu.VMEM((N_HEADS, LANES, cv3.shape[2]), BF16)],
        compiler_params=_params(),
        name="lat_mixer",
    )(x_all, h, q, p, p, p, mod2, k3, v3, ck3, cv3, rpb_pad, w_in, w_pool, pscale, wbrp, wbra, wout)


def kernel(x_prompt, x_sample, cache_k, cache_v, c, c_ctx, w_ada, b_ada, g_ff1, w_ff1_in, w_ff1_out,
           g_mix, w_in, q_gain, k_gain, w_pool, pool_scale, rpb, w_br_pool, w_br_att, w_out, g_ff2,
           w_ff2_in, w_ff2_out):
    n_ctx, seq_ctx, _ = x_prompt.shape
    n_lat, seq_lat, _ = x_sample.shape
    depth = w_ada.shape[0]
    assert depth == 1 and n_lat + 1 <= 8
    l = 0

    xp = x_prompt.reshape(n_ctx * seq_ctx, D_MODEL)
    xs = x_sample.reshape(n_lat * seq_lat, D_MODEL)
    t_ctx, t_lat = xp.shape[0], xs.shape[0]
    mod2 = _adaln(c_ctx, c, w_ada[l], b_ada[l])

    pscale = pool_scale[l].reshape(1, D_POOL)
    qk2 = jnp.tile(jnp.stack([q_gain[l], k_gain[l]]), (1, LANES // HEAD_DIM))
    g1, gm, g2 = g_ff1[l].reshape(1, -1), g_mix[l].reshape(1, -1), g_ff2[l].reshape(1, -1)
    rpb_pad = jnp.pad(rpb[l], ((0, 0), (0, 0), (0, LANES - rpb.shape[-1])))
    ck3 = jnp.transpose(cache_k[:, l], (0, 2, 3, 1)).reshape(n_lat, D_ATT, -1)
    cv3 = jnp.transpose(cache_v[:, l], (0, 2, 3, 1)).reshape(n_lat, D_ATT, -1)

    x1, w2gu, w2dn, win, wbrp, wbra, wout = _ffn(
        xp, xs, mod2, seq_lat, 0, g1, w_ff1_in[l], w_ff1_out[l], "ffn1", convert=True,
        preconvert=(w_ff2_in[l], w_ff2_out[l], w_in[l], w_br_pool[l], w_br_att[l], w_out[l]))
    x2p, k_p, v_p = _ctx_mixer(x1, t_ctx, mod2, gm, win, qk2, w_pool[l], pscale, wbrp, wbra, wout, seq_ctx)
    h_s, p_s, q_s, k_s, v_s = _lat_proj(x1, t_ctx, t_lat, seq_lat, mod2, gm, win, qk2)
    x2s = _lat_mixer(x1, t_ctx, n_lat, seq_lat, mod2, h_s, q_s, p_s,
                     k_s.reshape(n_lat, seq_lat, D_ATT), v_s.reshape(n_lat, seq_lat, D_ATT), ck3, cv3,
                     rpb_pad, win, w_pool[l], pscale, wbrp, wbra, wout)
    yp, ys = _ffn(x2p, x2s, mod2, seq_lat, 6, g2, w2gu, w2dn, "ffn2", split_out=True)

    new_k = jnp.transpose(k_p.reshape(n_ctx, 1, N_HEADS, HEAD_DIM, seq_ctx), (0, 1, 4, 2, 3))
    new_v = jnp.transpose(v_p.reshape(n_ctx, 1, N_HEADS, HEAD_DIM, seq_ctx), (0, 1, 4, 2, 3))
    return (yp.reshape(x_prompt.shape), ys.reshape(x_sample.shape), new_k, new_v)
```

```python
import functools
import math

import jax
import jax.numpy as jnp
from jax import lax
from jax.experimental import pallas as pl
from jax.experimental.pallas import tpu as pltpu

F32 = jnp.float32
BF16 = jnp.bfloat16

D_MODEL = 1024
N_HEADS = 8
HEAD_DIM = 64
D_ATT = N_HEADS * HEAD_DIM
D_POOL = 512
POOL_WINDOWS = (2, 4, 8, 16)
POOL_HALO = 8
D_FF = 2816
N_MOD = 9
GRID_W = 64
NA_KH = 8
NA_KW = 16
EPS = 1e-6
NEG = -1e30

LANES = 128
BF16_ROWS = 16
N_PAIRS = D_ATT // LANES
MXU_COLS = 256
FF_CHUNKS = ((0, 768), (768, 768), (1536, 768), (2304, 512))
ADALN_TN = 2304
FFN_TM = 512
CTX_TB = 4
PROJ_TM = 1024
LAT_ROWS = 4
LAT_TM = LAT_ROWS * GRID_W
LAT_TILES = 2
LAT_KEY_ROWS = 12
SCORE_LOOKAHEAD = 2
N_CONV = 8
PRE_CHUNKS = 16
VMEM_LIMIT = 56 * 1024 * 1024

Q_SCALE = 1.0 / math.sqrt(HEAD_DIM)

O_Q = D_POOL
O_K = D_POOL + D_ATT
O_V = D_POOL + 2 * D_ATT
O_G = D_POOL + 3 * D_ATT


def _dot(a, b):
    return jnp.dot(a, b, preferred_element_type=F32)


def _dot_nt(a, b):
    return lax.dot_general(a, b, (((1,), (1,)), ((), ())), preferred_element_type=F32)


def _rms_mod(x, g, shift, scale):
    ms = jnp.mean(x * x, axis=-1, keepdims=True)
    return x * lax.rsqrt(ms + EPS) * (g * (1.0 + scale)) + shift


def _mod_row(mod_ref, row, r):
    return mod_ref[pl.ds(row, 1), r * D_MODEL:(r + 1) * D_MODEL]


def _low_half():
    return lax.broadcasted_iota(jnp.int32, (1, LANES), 1) < HEAD_DIM


def _head_norm(t, gain2):
    lo = _low_half()
    sq = t * t
    s0 = jnp.sum(jnp.where(lo, sq, 0.0), axis=-1, keepdims=True)
    s1 = jnp.sum(jnp.where(lo, 0.0, sq), axis=-1, keepdims=True)
    r0 = lax.rsqrt(s0 * (1.0 / HEAD_DIM) + EPS)
    r1 = lax.rsqrt(s1 * (1.0 / HEAD_DIM) + EPS)
    return t * jnp.where(lo, r0, r1) * gain2


def _own_lanes(head, mine, other):
    lo = _low_half()
    return jnp.where(lo, mine, other) if head % 2 == 0 else jnp.where(lo, other, mine)


def _pipelined_heads(scores, weights, weighted_values, fillers):
    lo = _low_half()
    outs = []
    pending_scores = [scores(head) for head in range(SCORE_LOOKAHEAD)]
    pending_weights = [weights(0, *pending_scores.pop(0))]
    for head in range(N_HEADS):
        if head + SCORE_LOOKAHEAD < N_HEADS:
            pending_scores.append(scores(head + SCORE_LOOKAHEAD))
        fillers[head]()
        if head + 1 < N_HEADS:
            pending_weights.append(weights(head + 1, *pending_scores.pop(0)))
        outs.append(weighted_values(head, pending_weights.pop(0)))
    att_pairs = []
    for pr in range(N_PAIRS):
        even, odd = outs[2 * pr], outs[2 * pr + 1]
        num = jnp.where(lo, even, odd)
        den = pltpu.roll(jnp.where(lo, odd, even), HEAD_DIM, 1)
        att_pairs.append(num / den)
    return jnp.concatenate(att_pairs, axis=1)


def _pool_delta(p_ext, g, pos0, seq_len):
    w = POOL_WINDOWS[g]
    te = p_ext.shape[0]
    tm = te - 2 * POOL_HALO
    t = (pos0 + lax.broadcasted_iota(jnp.int32, (tm, 1), 0)).astype(F32)
    s = p_ext
    step = 1
    while step < w:
        s = s + pltpu.roll(s, step, 0)
        step *= 2
    ahead = w // 2 - 1
    if ahead:
        s = pltpu.roll(s, te - ahead, 0)
    win = s[POOL_HALO:POOL_HALO + tm]
    cnt = jnp.minimum(t + float(w // 2), float(seq_len)) - jnp.maximum(t - float(w // 2), 0.0)
    return win / cnt - p_ext[POOL_HALO:POOL_HALO + tm]


def _const_spec(shape):
    nd = len(shape)
    return pl.BlockSpec(shape, lambda *_: (0,) * nd)


def _resident_spec(w, col_block=None):
    width, cidx = col_block if col_block else (w.shape[1], 0)
    return pl.BlockSpec((w.shape[0], width), lambda *_: (0, cidx), pipeline_mode=pl.Buffered(1))


def _params():
    return pltpu.CompilerParams(dimension_semantics=("arbitrary",), vmem_limit_bytes=VMEM_LIMIT)


def _build_pool_weight(wpool32_ref, wpool_ref):
    wpool_ref[...] = jnp.zeros(wpool_ref.shape, BF16)
    for g in range(len(POOL_WINDOWS)):
        blk = slice(g * LANES, (g + 1) * LANES)
        wpool_ref[blk, blk] = wpool32_ref[g].astype(BF16)


def _adaln_kernel(wlo_ref, whi_ref, b_ref, cctx_ref, c_ref, o_ref, cond_ref):
    @pl.when(pl.program_id(0) == 0)
    def _first_step():
        n_lat = c_ref.shape[0]
        cond_ref[...] = jnp.zeros(cond_ref.shape, F32)
        cond_ref[0:1, :] = cctx_ref[...]
        cond_ref[1:1 + n_lat, :] = c_ref[...]

    c = cond_ref[...]
    s = (c * jax.nn.sigmoid(c)).astype(BF16)
    half = wlo_ref.shape[1]
    o_ref[:, :half] = _dot(s, wlo_ref[...].astype(BF16)) + b_ref[:, :half]
    o_ref[:, half:] = _dot(s, whi_ref[...].astype(BF16)) + b_ref[:, half:]


def _adaln(c_ctx, c, w_ada, b_ada):
    n = w_ada.shape[1]
    tn = ADALN_TN
    return pl.pallas_call(
        _adaln_kernel,
        grid=(n // tn,),
        in_specs=[pl.BlockSpec((D_MODEL, tn // 2), lambda i: (0, 2 * i)),
                  pl.BlockSpec((D_MODEL, tn // 2), lambda i: (0, 2 * i + 1)),
                  pl.BlockSpec((1, tn), lambda i: (0, i)),
                  _const_spec((1, D_MODEL)), _const_spec(c.shape)],
        out_specs=pl.BlockSpec((8, tn), lambda i: (0, i)),
        out_shape=jax.ShapeDtypeStruct((8, n), F32),
        scratch_shapes=[pltpu.VMEM((8, D_MODEL), F32)],
        compiler_params=_params(),
        name="adaln",
    )(w_ada, w_ada, b_ada.reshape(1, n), c_ctx.reshape(1, D_MODEL), c)


def _ffn_kernel(*refs, mod_off, n_a, split_out, n_conv, n_pre, seq_b):
    refs = list(refs)
    wgu_in, wdn_in, xa_ref, xb_ref, mod_ref, g_ref = refs[:6]
    pre_src = refs[6:6 + n_pre]
    n_out = 2 if split_out else 1
    out_refs = refs[6 + n_pre:6 + n_pre + n_out]
    pre_dst = refs[6 + n_pre + n_out:6 + 2 * n_pre + n_out]
    wgu_ref, wdn_ref = refs[6 + 2 * n_pre + n_out:] if n_conv else (wgu_in, wdn_in)
    step = pl.program_id(0)

    if n_conv:
        @pl.when(step < n_conv)
        def _convert():
            for src, dst in ((wgu_in, wgu_ref), (wdn_in, wdn_ref)):
                rows = src.shape[0]
                dst[pl.ds(pl.multiple_of(step * rows, rows), rows), :] = src[...].astype(BF16)

    tile = step - n_conv

    def half_step(x_ref, o_ref):
        tm = x_ref.shape[0]
        row = jnp.where(tile < n_a, 0, 1 + (jnp.maximum(tile - n_a, 0) * tm) // seq_b)
        shift = _mod_row(mod_ref, row, mod_off)
        scale = _mod_row(mod_ref, row, mod_off + 1)
        gate = _mod_row(mod_ref, row, mod_off + 2)
        x = x_ref[...]
        h = _rms_mod(x, g_ref[...], shift, scale).astype(BF16)
        acc = None
        for c0, cw in FF_CHUNKS:
            a = _dot(h, wgu_ref[:, c0:c0 + cw])
            u = _dot(h, wgu_ref[:, D_FF + c0:D_FF + c0 + cw])
            t = (a * jax.nn.sigmoid(a) * u).astype(BF16)
            part = _dot(t, wdn_ref[c0:c0 + cw, :])
            acc = part if acc is None else acc + part
        o_ref[...] = x + (0.5 * gate) * acc

    @pl.when((tile >= 0) & (tile < n_a))
    def _tokens_a():
        half_step(xa_ref, out_refs[0])

    @pl.when(tile >= n_a)
    def _tokens_b():
        half_step(xb_ref, out_refs[-1])

    if n_pre:
        @pl.when((tile >= 0) & (tile < PRE_CHUNKS))
        def _preconvert():
            for src, dst in zip(pre_src, pre_dst):
                dst[...] = src[...].astype(BF16)


def _ffn(xa, xb, mod2, seq_b, mod_off, g, wgu, wdn, name, split_out=False, convert=False, preconvert=()):
    tm = FFN_TM
    n_a, n_b = xa.shape[0] // tm, xb.shape[0] // tm
    n_conv = N_CONV if convert else 0
    assert n_a + n_b >= PRE_CHUNKS
    tile = lambda s: jnp.maximum(s - n_conv, 0)
    a_map = lambda s: (jnp.minimum(tile(s), n_a - 1), 0)
    b_map = lambda s: (jnp.maximum(tile(s) - n_a, 0), 0)

    def chunk_spec(w, n_chunks, index):
        rows = w.shape[0] // n_chunks
        assert rows * n_chunks == w.shape[0] and rows % BF16_ROWS == 0
        return pl.BlockSpec((rows, w.shape[1]), lambda s: (jnp.minimum(index(s), n_chunks - 1), 0))

    if convert:
        w_specs = [chunk_spec(wgu, N_CONV, lambda s: s), chunk_spec(wdn, N_CONV, lambda s: s)]
        scratch = [pltpu.VMEM(wgu.shape, BF16), pltpu.VMEM(wdn.shape, BF16)]
    else:
        w_specs = [_resident_spec(wgu), _resident_spec(wdn)]
        scratch = []
    pre_specs = lambda: [chunk_spec(w, PRE_CHUNKS, tile) for w in preconvert]
    if split_out:
        out_specs = [pl.BlockSpec((tm, D_MODEL), a_map), pl.BlockSpec((tm, D_MODEL), b_map)]
        out_shape = [jax.ShapeDtypeStruct(xa.shape, F32), jax.ShapeDtypeStruct(xb.shape, F32)]
    else:
        out_specs = [pl.BlockSpec((tm, D_MODEL), lambda s: (tile(s), 0))]
        out_shape = [jax.ShapeDtypeStruct((xa.shape[0] + xb.shape[0], D_MODEL), F32)]
    return pl.pallas_call(
        functools.partial(_ffn_kernel, mod_off=mod_off, n_a=n_a, split_out=split_out, n_conv=n_conv,
                          n_pre=len(preconvert), seq_b=seq_b),
        grid=(n_conv + n_a + n_b,),
        in_specs=w_specs + [pl.BlockSpec((tm, D_MODEL), a_map), pl.BlockSpec((tm, D_MODEL), b_map),
                            _const_spec(mod2.shape), _const_spec((1, D_MODEL))] + pre_specs(),
        out_specs=out_specs + pre_specs(),
        out_shape=out_shape + [jax.ShapeDtypeStruct(w.shape, BF16) for w in preconvert],
        scratch_shapes=scratch,
        compiler_params=_params(),
        name=name,
    )(wgu, wdn, xa, xb, mod2, g, *preconvert)


def _gate_fillers(gates, h_rows, gate_cols):
    assert 2 * D_MODEL // MXU_COLS == N_HEADS

    def gate_block():
        c0 = len(gates) * MXU_COLS
        gates.append(jax.nn.sigmoid(_dot(h_rows(), gate_cols(c0, c0 + MXU_COLS))))

    return [gate_block] * N_HEADS


def _pool_fillers(st, delta_fn, wpool_ref, pscale_ref, wbrp_ref):
    def pool_delta():
        deltas = [delta_fn(g).astype(BF16) for g in range(len(POOL_WINDOWS))]
        st["d"] = jnp.concatenate(deltas, axis=1)

    def pool_weight():
        st["y"] = (_dot(st.pop("d"), wpool_ref[...]) * pscale_ref[...]).astype(BF16)

    def pool_out():
        st["a"] = _dot(st.pop("y"), wbrp_ref[...])

    return [pool_delta, pool_weight, pool_out]


def _merge_fillers(st, gates, rows, att, x_ref, o_ref, gate2, wbra_ref, wout_ref):
    def attention_branch():
        st["b"] = _dot(att.astype(BF16), wbra_ref[...])

    def gate():
        g = jnp.concatenate(gates, axis=1)
        merged = g[:, :D_MODEL] * st["a"][rows] + g[:, D_MODEL:] * st.pop("b")
        st["merged"] = merged.astype(BF16)

    def out():
        o_ref[rows, :] = x_ref[rows, :] + gate2 * _dot(st.pop("merged"), wout_ref[...])

    return [attention_branch, gate, out]


def _attend_tiles(n_tiles, tile_rows, attention, gate_fillers, pool_fillers, merge_fillers):
    idle = lambda: None
    extra = pool_fillers + [idle] * (N_HEADS - len(pool_fillers))
    for t in range(n_tiles):
        rows = slice(t * tile_rows, (t + 1) * tile_rows)
        gates = []
        fillers = [lambda g=g, e=e: (g(), e()) for g, e in zip(gate_fillers(rows, gates), extra)]
        pieces = merge_fillers(gates, rows, attention(t, fillers))
        extra = [idle] + pieces + [idle] * (N_HEADS - 1 - len(pieces))
    for piece in pieces:
        piece()


def _ctx_mixer_kernel(x_ref, mod_ref, g_ref, win_ref, qk_ref, wpool32_ref, pscale_ref,
                      wbrp_ref, wbra_ref, wout_ref, o_ref, ko_ref, vo_ref, wpool_ref, *, seq):
    @pl.when(pl.program_id(0) == 0)
    def _first_step():
        _build_pool_weight(wpool32_ref, wpool_ref)

    m_rows = x_ref.shape[0]
    h = _rms_mod(x_ref[...], g_ref[...], _mod_row(mod_ref, 0, 3), _mod_row(mod_ref, 0, 4)).astype(BF16)
    p = _dot(h, win_ref[:, 0:O_Q])
    q = _dot(h, win_ref[:, O_Q:O_K])
    k = _dot(h, win_ref[:, O_K:O_V])
    v = _dot(h, win_ref[:, O_V:O_G])
    for e in range(m_rows // seq):
        vo_ref[e] = v[e * seq:(e + 1) * seq].T
    scale = Q_SCALE
    zero = jnp.zeros((), BF16)
    one = jnp.ones((), BF16)
    st = {}

    def pool_delta(g):
        lanes = slice(g * LANES, (g + 1) * LANES)
        zeros_halo = jnp.zeros((POOL_HALO, LANES), F32)
        return jnp.concatenate(
            [_pool_delta(jnp.concatenate([zeros_halo, p[e * seq:(e + 1) * seq, lanes], zeros_halo], axis=0),
                         g, 0, seq) for e in range(m_rows // seq)], axis=0)

    def gate_fillers(rows, gates):
        return _gate_fillers(gates, lambda: h[rows], lambda c0, c1: win_ref[:, O_G + c0:O_G + c1])

    def attention(e, fillers):
        rows = slice(e * seq, (e + 1) * seq)
        qn, kn, vb = [], [], []
        for pr in range(N_PAIRS):
            lanes = slice(pr * LANES, (pr + 1) * LANES)
            kn.append(_head_norm(k[rows, lanes], qk_ref[1:2, :]))
            qn.append((_head_norm(q[rows, lanes], qk_ref[0:1, :]) * scale).astype(BF16))
            vb.append(v[rows, lanes].astype(BF16))
        ko_ref[e] = jnp.concatenate(kn, axis=1).T
        kb = [t.astype(BF16) for t in kn]

        def scores(head):
            return (_dot_nt(_own_lanes(head, qn[head // 2], zero), kb[head // 2]),)

        def weights(head, s):
            return jnp.exp(s - s.max(axis=-1, keepdims=True)).astype(BF16)

        def weighted_values(head, e_):
            return _dot(e_, _own_lanes(head, vb[head // 2], one))

        return _pipelined_heads(scores, weights, weighted_values, fillers)

    merge = functools.partial(_merge_fillers, st, x_ref=x_ref, o_ref=o_ref, gate2=_mod_row(mod_ref, 0, 5),
                              wbra_ref=wbra_ref, wout_ref=wout_ref)
    _attend_tiles(m_rows // seq, seq, attention, gate_fillers,
                  _pool_fillers(st, pool_delta, wpool_ref, pscale_ref, wbrp_ref), merge)


def _ctx_mixer(x_all, n_tok, mod2, g, w_in, qk2, w_pool, pscale, wbrp, wbra, wout, seq):
    tm = CTX_TB * seq
    row = lambda s: (s, 0)
    heads = lambda s: (s, 0, 0)
    return pl.pallas_call(
        functools.partial(_ctx_mixer_kernel, seq=seq),
        grid=(n_tok // tm,),
        in_specs=[pl.BlockSpec((tm, D_MODEL), row),
                  _const_spec(mod2.shape), _const_spec((1, D_MODEL)),
                  _resident_spec(w_in), _const_spec(qk2.shape),
                  _const_spec(w_pool.shape), _const_spec((1, D_POOL)),
                  _resident_spec(wbrp), _resident_spec(wbra), _resident_spec(wout)],
        out_specs=[pl.BlockSpec((tm, D_MODEL), row),
                   pl.BlockSpec((CTX_TB, D_ATT, seq), heads),
                   pl.BlockSpec((CTX_TB, D_ATT, seq), heads)],
        out_shape=[jax.ShapeDtypeStruct((n_tok, D_MODEL), F32),
                   jax.ShapeDtypeStruct((n_tok // seq, D_ATT, seq), F32),
                   jax.ShapeDtypeStruct((n_tok // seq, D_ATT, seq), F32)],
        scratch_shapes=[pltpu.VMEM((D_POOL, D_POOL), BF16)],
        compiler_params=_params(),
        name="ctx_mixer",
    )(x_all, mod2, g, w_in, qk2, w_pool, pscale, wbrp, wbra, wout)


def _lat_proj_kernel(x_ref, mod_ref, g_ref, win_ref, qk_ref, h_ref, p_ref, q_ref, k_ref, v_ref, *, seq):
    row = 1 + (pl.program_id(0) * x_ref.shape[0]) // seq
    h = _rms_mod(x_ref[...], g_ref[...], _mod_row(mod_ref, row, 3), _mod_row(mod_ref, row, 4)).astype(BF16)
    h_ref[...] = h
    p_ref[...] = _dot(h, win_ref[:, 0:O_Q])
    q = _dot(h, win_ref[:, O_Q:O_K])
    k = _dot(h, win_ref[:, O_K:O_V])
    v_ref[...] = _dot(h, win_ref[:, O_V:O_G]).astype(BF16)
    scale = Q_SCALE
    for pr in range(N_PAIRS):
        lanes = slice(pr * LANES, (pr + 1) * LANES)
        q_ref[:, lanes] = (_head_norm(q[:, lanes], qk_ref[0:1, :]) * scale).astype(BF16)
        k_ref[:, lanes] = _head_norm(k[:, lanes], qk_ref[1:2, :]).astype(BF16)


def _lat_proj(x_all, tok0, n_tok, seq, mod2, g, w_in, qk2):
    tm = PROJ_TM
    tile0 = tok0 // tm
    row = lambda s: (s, 0)
    shapes = [(D_MODEL, BF16), (D_POOL, F32), (D_ATT, BF16), (D_ATT, BF16), (D_ATT, BF16)]
    return pl.pallas_call(
        functools.partial(_lat_proj_kernel, seq=seq),
        grid=(n_tok // tm,),
        in_specs=[pl.BlockSpec((tm, D_MODEL), lambda s: (tile0 + s, 0)),
                  _const_spec(mod2.shape),
                  _const_spec((1, D_MODEL)), _resident_spec(w_in, (O_G, 0)),
                  _const_spec(qk2.shape)],
        out_specs=[pl.BlockSpec((tm, width), row) for width, _ in shapes],
        out_shape=[jax.ShapeDtypeStruct((n_tok, width), dtype) for width, dtype in shapes],
        compiler_params=_params(),
        name="lat_proj",
    )(x_all, mod2, g, w_in, qk2)


def _build_bias_table(rpb_ref, bias_ref):
    lane = lax.broadcasted_iota(jnp.int32, (GRID_W, LANES), 1)
    qc = lax.broadcasted_iota(jnp.int32, (GRID_W, LANES), 0)
    kc = lane & (GRID_W - 1)
    c0 = jnp.clip(qc - NA_KW // 2, 0, GRID_W - NA_KW)
    col_ok = (kc >= c0) & (kc < c0 + NA_KW)
    first_half = lane < GRID_W
    shift_a = LANES - (NA_KW - 1)
    shift_b = (GRID_W - (NA_KW - 1)) % LANES
    for h in range(N_HEADS):
        for i in range(2 * NA_KH - 2):
            va = jnp.broadcast_to(rpb_ref[h, i:i + 1, :], (GRID_W, LANES))
            vb = jnp.broadcast_to(rpb_ref[h, i + 1:i + 2, :], (GRID_W, LANES))
            ra = pltpu.roll(va, shift_a, 1, stride=1, stride_axis=0)
            rb = pltpu.roll(vb, shift_b, 1, stride=1, stride_axis=0)
            bias_ref[h, i] = jnp.where(col_ok, jnp.where(first_half, ra, rb), NEG)


def _lat_attention(qn, j, fillers, k_ref, v_ref, ck_ref, cv_ref, bias_ref, *, n_rows):
    n_keys = LAT_KEY_ROWS * GRID_W
    r0q = j * LAT_ROWS
    u0 = jnp.clip(r0q - NA_KH // 2, 0, n_rows - LAT_KEY_ROWS)
    key0 = pl.multiple_of(u0 * GRID_W, GRID_W)
    log2_w = GRID_W.bit_length() - 1
    zero = jnp.zeros((), BF16)
    one = jnp.ones((), BF16)

    lane_row = lax.broadcasted_iota(jnp.int32, (1, LANES), 1) & (HEAD_DIM - 1)
    q_tile_row = jnp.right_shift(lax.broadcasted_iota(jnp.int32, (LAT_TM, 1), 0), log2_w)
    q_hot = jnp.where(lane_row == q_tile_row, 1.0, 0.0).astype(BF16)
    k_row = u0 + jnp.right_shift(lax.broadcasted_iota(jnp.int32, (n_keys, 1), 0), log2_w)
    first = jnp.clip(r0q + lane_row - NA_KH // 2, 0, n_rows - NA_KH)
    outside = (lane_row < LAT_ROWS) & ((k_row < first) | (k_row >= first + NA_KH))
    k_mask = jnp.where(outside, NEG, 0.0).astype(BF16)

    def scores(head):
        lanes = slice((head // 2) * LANES, (head // 2 + 1) * LANES)
        qp = qn[:, lanes]
        s_loc = _dot_nt(_own_lanes(head, qp, q_hot),
                        _own_lanes(head, k_ref[0, pl.ds(key0, n_keys), lanes], k_mask))
        return s_loc, _dot(_own_lanes(head, qp, zero), ck_ref[lanes, :])

    def weights(head, s_loc, s_ctx):
        bias_rows = []
        for jr in range(LAT_ROWS):
            blocks = []
            for pi in range(LAT_KEY_ROWS // 2):
                idx = jnp.clip(u0 - r0q + 2 * pi - jr + NA_KH - 1, 0, 2 * NA_KH - 3)
                blocks.append(bias_ref[head, idx])
            bias_rows.append(jnp.concatenate(blocks, axis=1))
        s_loc = s_loc + jnp.concatenate(bias_rows, axis=0)
        m = jnp.maximum(s_loc.max(axis=-1, keepdims=True), s_ctx.max(axis=-1, keepdims=True))
        return jnp.concatenate([jnp.exp(s_loc - m).astype(BF16), jnp.exp(s_ctx - m).astype(BF16)], axis=1)

    def weighted_values(head, e):
        lanes = slice((head // 2) * LANES, (head // 2 + 1) * LANES)
        values = _own_lanes(head, v_ref[0, pl.ds(key0, n_keys), lanes], one)
        return _dot(e[:, :n_keys], values) + _dot_nt(e[:, n_keys:], cv_ref[head])

    return _pipelined_heads(scores, weights, weighted_values, fillers)


def _lat_mixer_kernel(x_ref, h_ref, q_ref, p_ref, pp_ref, pn_ref, mod_ref, k_ref, v_ref, ckin_ref, cvin_ref,
                      rpb_ref, wg_ref, wpool32_ref, pscale_ref, wbrp_ref, wbra_ref, wout_ref,
                      o_ref, bias_ref, wpool_ref, ck_ref, cv_ref, *, seq, n_rows):
    step = pl.program_id(0)
    tm = LAT_TILES * LAT_TM

    @pl.when(step == 0)
    def _first_step():
        _build_pool_weight(wpool32_ref, wpool_ref)
        _build_bias_table(rpb_ref, bias_ref)

    blk = lax.rem(step, seq // tm)
    st = {}

    @pl.when(blk == 0)
    def _sequence_cache():
        ck_ref[...] = ckin_ref[0].astype(BF16)
        low = lax.broadcasted_iota(jnp.int32, (LANES, 1), 0) < HEAD_DIM
        for head in range(N_HEADS):
            rows = slice((head // 2) * LANES, (head // 2 + 1) * LANES)
            cv_ref[head] = jnp.where(low if head % 2 == 0 else ~low, cvin_ref[0, rows, :], 1.0).astype(BF16)

    def pool_delta(g):
        lanes = slice(g * LANES, (g + 1) * LANES)
        pe = jnp.concatenate([pp_ref[:, lanes], p_ref[:, lanes], pn_ref[:, lanes]], axis=0)
        pos = blk * tm - POOL_HALO + lax.broadcasted_iota(jnp.int32, (tm + 2 * POOL_HALO, 1), 0)
        return _pool_delta(jnp.where((pos >= 0) & (pos < seq), pe, 0.0), g, blk * tm, seq)

    block_gates = []

    def gate_fillers(rows, gates):
        if not block_gates:
            fillers = _gate_fillers(block_gates, lambda: h_ref[...], lambda c0, c1: wg_ref[:, c0:c1])
            return [lambda f=f: (f(), gates.append(block_gates[-1][rows])) for f in fillers]
        gates.extend(g[rows] for g in block_gates)
        return [lambda: None] * N_HEADS

    pool_fillers = _pool_fillers(st, pool_delta, wpool_ref, pscale_ref, wbrp_ref)

    def attention(t, fillers):
        return _lat_attention(q_ref[t * LAT_TM:(t + 1) * LAT_TM, :], blk * LAT_TILES + t, fillers,
                              k_ref, v_ref, ck_ref, cv_ref, bias_ref, n_rows=n_rows)

    merge = functools.partial(_merge_fillers, st, x_ref=x_ref, o_ref=o_ref,
                              gate2=_mod_row(mod_ref, 1 + step // (seq // tm), 5),
                              wbra_ref=wbra_ref, wout_ref=wout_ref)
    _attend_tiles(LAT_TILES, LAT_TM, attention, gate_fillers, pool_fillers, merge)


def _lat_mixer(x_all, tok0, n_batch, seq, mod2, h, q, p, k3, v3, ck3, cv3, rpb_pad, w_in, w_pool, pscale,
               wbrp, wbra, wout):
    tm = LAT_TILES * LAT_TM
    n_rows = seq // GRID_W
    nb = seq // tm
    tile0 = tok0 // tm
    halo_per_tile = tm // POOL_HALO
    n_halo = p.shape[0] // POOL_HALO
    row = lambda s: (s, 0)
    seq_map = lambda s: (s // nb, 0, 0)
    return pl.pallas_call(
        functools.partial(_lat_mixer_kernel, seq=seq, n_rows=n_rows),
        grid=(n_batch * nb,),
        in_specs=[pl.BlockSpec((tm, D_MODEL), lambda s: (tile0 + s, 0)),
                  pl.BlockSpec((tm, D_MODEL), row),
                  pl.BlockSpec((tm, D_ATT), row),
                  pl.BlockSpec((tm, D_POOL), row),
                  pl.BlockSpec((POOL_HALO, D_POOL), lambda s: (jnp.maximum(s * halo_per_tile - 1, 0), 0)),
                  pl.BlockSpec((POOL_HALO, D_POOL),
                               lambda s: (jnp.minimum((s + 1) * halo_per_tile, n_halo - 1), 0)),
                  _const_spec(mod2.shape),
                  pl.BlockSpec((1, seq, D_ATT), seq_map), pl.BlockSpec((1, seq, D_ATT), seq_map),
                  pl.BlockSpec((1,) + ck3.shape[1:], seq_map), pl.BlockSpec((1,) + cv3.shape[1:], seq_map),
                  _const_spec(rpb_pad.shape),
                  _resident_spec(w_in, (2 * D_MODEL, O_G // (2 * D_MODEL))),
                  _const_spec(w_pool.shape), _const_spec((1, D_POOL)),
                  _resident_spec(wbrp), _resident_spec(wbra), _resident_spec(wout)],
        out_specs=pl.BlockSpec((tm, D_MODEL), row),
        out_shape=jax.ShapeDtypeStruct((n_batch * seq, D_MODEL), F32),
        scratch_shapes=[pltpu.VMEM((N_HEADS, 2 * NA_KH - 2, GRID_W, LANES), F32),
                        pltpu.VMEM((D_POOL, D_POOL), BF16),
                        pltpu.VMEM(ck3.shape[1:], BF16),
                        pltpu.VMEM((N_HEADS, LANES, cv3.shape[2]), BF16)],
        compiler_params=_params(),
        name="lat_mixer",
    )(x_all, h, q, p, p, p, mod2, k3, v3, ck3, cv3, rpb_pad, w_in, w_pool, pscale, wbrp, wbra, wout)


def kernel(x_prompt, x_sample, cache_k, cache_v, c, c_ctx, w_ada, b_ada, g_ff1, w_ff1_in, w_ff1_out,
           g_mix, w_in, q_gain, k_gain, w_pool, pool_scale, rpb, w_br_pool, w_br_att, w_out, g_ff2,
           w_ff2_in, w_ff2_out):
    n_ctx, seq_ctx, _ = x_prompt.shape
    n_lat, seq_lat, _ = x_sample.shape
    depth = w_ada.shape[0]
    assert depth == 1 and n_lat + 1 <= 8
    l = 0

    xp = x_prompt.reshape(n_ctx * seq_ctx, D_MODEL)
    xs = x_sample.reshape(n_lat * seq_lat, D_MODEL)
    t_ctx, t_lat = xp.shape[0], xs.shape[0]
    mod2 = _adaln(c_ctx, c, w_ada[l], b_ada[l])

    pscale = pool_scale[l].reshape(1, D_POOL)
    qk2 = jnp.tile(jnp.stack([q_gain[l], k_gain[l]]), (1, LANES // HEAD_DIM))
    g1, gm, g2 = g_ff1[l].reshape(1, -1), g_mix[l].reshape(1, -1), g_ff2[l].reshape(1, -1)
    rpb_pad = jnp.pad(rpb[l], ((0, 0), (0, 0), (0, LANES - rpb.shape[-1])))
    ck3 = jnp.transpose(cache_k[:, l], (0, 2, 3, 1)).reshape(n_lat, D_ATT, -1)
    cv3 = jnp.transpose(cache_v[:, l], (0, 2, 3, 1)).reshape(n_lat, D_ATT, -1)

    x1, w2gu, w2dn, win, wbrp, wbra, wout = _ffn(
        xp, xs, mod2, seq_lat, 0, g1, w_ff1_in[l], w_ff1_out[l], "ffn1", convert=True,
        preconvert=(w_ff2_in[l], w_ff2_out[l], w_in[l], w_br_pool[l], w_br_att[l], w_out[l]))
    x2p, k_p, v_p = _ctx_mixer(x1, t_ctx, mod2, gm, win, qk2, w_pool[l], pscale, wbrp, wbra, wout, seq_ctx)
    h_s, p_s, q_s, k_s, v_s = _lat_proj(x1, t_ctx, t_lat, seq_lat, mod2, gm, win, qk2)
    x2s = _lat_mixer(x1, t_ctx, n_lat, seq_lat, mod2, h_s, q_s, p_s,
                     k_s.reshape(n_lat, seq_lat, D_ATT), v_s.reshape(n_lat, seq_lat, D_ATT), ck3, cv3,
                     rpb_pad, win, w_pool[l], pscale, wbrp, wbra, wout)
    yp, ys = _ffn(x2p, x2s, mod2, seq_lat, 6, g2, w2gu, w2dn, "ffn2", split_out=True)

    new_k = jnp.transpose(k_p.reshape(n_ctx, 1, N_HEADS, HEAD_DIM, seq_ctx), (0, 1, 4, 2, 3))
    new_v = jnp.transpose(v_p.reshape(n_ctx, 1, N_HEADS, HEAD_DIM, seq_ctx), (0, 1, 4, 2, 3))
    return (yp.reshape(x_prompt.shape), ys.reshape(x_sample.shape), new_k, new_v)
```

```python
import functools
import math

import jax
import jax.numpy as jnp
from jax import lax
from jax.experimental import pallas as pl
from jax.experimental.pallas import tpu as pltpu

F32 = jnp.float32
BF16 = jnp.bfloat16

D_MODEL = 1024
N_HEADS = 8
HEAD_DIM = 64
D_ATT = N_HEADS * HEAD_DIM
D_POOL = 512
POOL_WINDOWS = (2, 4, 8, 16)
POOL_HALO = 8
D_FF = 2816
N_MOD = 9
GRID_W = 64
NA_KH = 8
NA_KW = 16
EPS = 1e-6
NEG = -1e30

LANES = 128
BF16_ROWS = 16
N_PAIRS = D_ATT // LANES
MXU_COLS = 256
FF_CHUNKS = ((0, 768), (768, 768), (1536, 768), (2304, 512))
ADALN_TN = 2304
FFN_TM = 512
CTX_TB = 2
PROJ_TM = 1024
LAT_ROWS = 4
LAT_TM = LAT_ROWS * GRID_W
LAT_TILES = 2
LAT_KEY_ROWS = 12
SCORE_LOOKAHEAD = 2
N_CONV = 8
PRE_CHUNKS = 16
VMEM_LIMIT = 56 * 1024 * 1024

Q_SCALE = 1.0 / math.sqrt(HEAD_DIM)

O_Q = D_POOL
O_K = D_POOL + D_ATT
O_V = D_POOL + 2 * D_ATT
O_G = D_POOL + 3 * D_ATT


def _dot(a, b):
    return jnp.dot(a, b, preferred_element_type=F32)


def _dot_nt(a, b):
    return lax.dot_general(a, b, (((1,), (1,)), ((), ())), preferred_element_type=F32)


def _rms_mod(x, g, shift, scale):
    ms = jnp.mean(x * x, axis=-1, keepdims=True)
    return x * lax.rsqrt(ms + EPS) * (g * (1.0 + scale)) + shift


def _mod_row(mod_ref, row, r):
    return mod_ref[pl.ds(row, 1), r * D_MODEL:(r + 1) * D_MODEL]


def _low_half():
    return lax.broadcasted_iota(jnp.int32, (1, LANES), 1) < HEAD_DIM


def _head_norm(t, gain2):
    lo = _low_half()
    sq = t * t
    s0 = jnp.sum(jnp.where(lo, sq, 0.0), axis=-1, keepdims=True)
    s1 = jnp.sum(jnp.where(lo, 0.0, sq), axis=-1, keepdims=True)
    r0 = lax.rsqrt(s0 * (1.0 / HEAD_DIM) + EPS)
    r1 = lax.rsqrt(s1 * (1.0 / HEAD_DIM) + EPS)
    return t * jnp.where(lo, r0, r1) * gain2


def _own_lanes(head, mine, other):
    lo = _low_half()
    return jnp.where(lo, mine, other) if head % 2 == 0 else jnp.where(lo, other, mine)


def _pipelined_heads(scores, weights, weighted_values, fillers):
    lo = _low_half()
    outs = []
    pending_scores = [scores(head) for head in range(SCORE_LOOKAHEAD)]
    pending_weights = [weights(0, *pending_scores.pop(0))]
    for head in range(N_HEADS):
        if head + SCORE_LOOKAHEAD < N_HEADS:
            pending_scores.append(scores(head + SCORE_LOOKAHEAD))
        fillers[head]()
        if head + 1 < N_HEADS:
            pending_weights.append(weights(head + 1, *pending_scores.pop(0)))
        outs.append(weighted_values(head, pending_weights.pop(0)))
    att_pairs = []
    for pr in range(N_PAIRS):
        even, odd = outs[2 * pr], outs[2 * pr + 1]
        num = jnp.where(lo, even, odd)
        den = pltpu.roll(jnp.where(lo, odd, even), HEAD_DIM, 1)
        att_pairs.append(num / den)
    return jnp.concatenate(att_pairs, axis=1)


def _pool_delta(p_ext, g, pos0, seq_len):
    w = POOL_WINDOWS[g]
    te = p_ext.shape[0]
    tm = te - 2 * POOL_HALO
    t = (pos0 + lax.broadcasted_iota(jnp.int32, (tm, 1), 0)).astype(F32)
    s = p_ext
    step = 1
    while step < w:
        s = s + pltpu.roll(s, step, 0)
        step *= 2
    ahead = w // 2 - 1
    if ahead:
        s = pltpu.roll(s, te - ahead, 0)
    win = s[POOL_HALO:POOL_HALO + tm]
    cnt = jnp.minimum(t + float(w // 2), float(seq_len)) - jnp.maximum(t - float(w // 2), 0.0)
    return win / cnt - p_ext[POOL_HALO:POOL_HALO + tm]


def _const_spec(shape):
    nd = len(shape)
    return pl.BlockSpec(shape, lambda *_: (0,) * nd)


def _resident_spec(w, col_block=None):
    width, cidx = col_block if col_block else (w.shape[1], 0)
    return pl.BlockSpec((w.shape[0], width), lambda *_: (0, cidx), pipeline_mode=pl.Buffered(1))


def _params():
    return pltpu.CompilerParams(dimension_semantics=("arbitrary",), vmem_limit_bytes=VMEM_LIMIT)


def _build_pool_weight(wpool32_ref, wpool_ref):
    wpool_ref[...] = jnp.zeros(wpool_ref.shape, BF16)
    for g in range(len(POOL_WINDOWS)):
        blk = slice(g * LANES, (g + 1) * LANES)
        wpool_ref[blk, blk] = wpool32_ref[g].astype(BF16)


def _adaln_kernel(wlo_ref, whi_ref, b_ref, cctx_ref, c_ref, o_ref, cond_ref):
    @pl.when(pl.program_id(0) == 0)
    def _first_step():
        n_lat = c_ref.shape[0]
        cond_ref[...] = jnp.zeros(cond_ref.shape, F32)
        cond_ref[0:1, :] = cctx_ref[...]
        cond_ref[1:1 + n_lat, :] = c_ref[...]

    c = cond_ref[...]
    s = (c * jax.nn.sigmoid(c)).astype(BF16)
    half = wlo_ref.shape[1]
    o_ref[:, :half] = _dot(s, wlo_ref[...].astype(BF16)) + b_ref[:, :half]
    o_ref[:, half:] = _dot(s, whi_ref[...].astype(BF16)) + b_ref[:, half:]


def _adaln(c_ctx, c, w_ada, b_ada):
    n = w_ada.shape[1]
    tn = ADALN_TN
    return pl.pallas_call(
        _adaln_kernel,
        grid=(n // tn,),
        in_specs=[pl.BlockSpec((D_MODEL, tn // 2), lambda i: (0, 2 * i)),
                  pl.BlockSpec((D_MODEL, tn // 2), lambda i: (0, 2 * i + 1)),
                  pl.BlockSpec((1, tn), lambda i: (0, i)),
                  _const_spec((1, D_MODEL)), _const_spec(c.shape)],
        out_specs=pl.BlockSpec((8, tn), lambda i: (0, i)),
        out_shape=jax.ShapeDtypeStruct((8, n), F32),
        scratch_shapes=[pltpu.VMEM((8, D_MODEL), F32)],
        compiler_params=_params(),
        name="adaln",
    )(w_ada, w_ada, b_ada.reshape(1, n), c_ctx.reshape(1, D_MODEL), c)


def _ffn_kernel(*refs, mod_off, n_a, split_out, n_conv, n_pre, seq_b):
    refs = list(refs)
    wgu_in, wdn_in, xa_ref, xb_ref, mod_ref, g_ref = refs[:6]
    pre_src = refs[6:6 + n_pre]
    n_out = 2 if split_out else 1
    out_refs = refs[6 + n_pre:6 + n_pre + n_out]
    pre_dst = refs[6 + n_pre + n_out:6 + 2 * n_pre + n_out]
    scratch = refs[6 + 2 * n_pre + n_out:]
    wgu_ref, wdn_ref = scratch[:2]
    sem = None if n_conv else scratch[2]
    step = pl.program_id(0)

    def chunk_copies(ci):
        c0, cw = FF_CHUNKS[ci]
        return [pltpu.make_async_copy(wgu_in.at[:, pl.ds(c0, cw)], wgu_ref.at[:, pl.ds(c0, cw)], sem.at[ci, 0]),
                pltpu.make_async_copy(wgu_in.at[:, pl.ds(D_FF + c0, cw)], wgu_ref.at[:, pl.ds(D_FF + c0, cw)],
                                      sem.at[ci, 1]),
                pltpu.make_async_copy(wdn_in.at[pl.ds(c0, cw), :], wdn_ref.at[pl.ds(c0, cw), :], sem.at[ci, 2])]

    if n_conv:
        @pl.when(step < n_conv)
        def _convert():
            for src, dst in ((wgu_in, wgu_ref), (wdn_in, wdn_ref)):
                rows = src.shape[0]
                dst[pl.ds(pl.multiple_of(step * rows, rows), rows), :] = src[...].astype(BF16)

    tile = step - n_conv

    def half_step(x_ref, o_ref, fetch_weights=False):
        if fetch_weights:
            for ci in range(len(FF_CHUNKS)):
                for copy in chunk_copies(ci):
                    copy.start()
        tm = x_ref.shape[0]
        row = jnp.where(tile < n_a, 0, 1 + (jnp.maximum(tile - n_a, 0) * tm) // seq_b)
        shift = _mod_row(mod_ref, row, mod_off)
        scale = _mod_row(mod_ref, row, mod_off + 1)
        gate = _mod_row(mod_ref, row, mod_off + 2)
        x = x_ref[...]
        h = _rms_mod(x, g_ref[...], shift, scale).astype(BF16)
        acc = None
        for ci, (c0, cw) in enumerate(FF_CHUNKS):
            if fetch_weights:
                for copy in chunk_copies(ci):
                    copy.wait()
            a = _dot(h, wgu_ref[:, c0:c0 + cw])
            u = _dot(h, wgu_ref[:, D_FF + c0:D_FF + c0 + cw])
            t = (a * jax.nn.sigmoid(a) * u).astype(BF16)
            part = _dot(t, wdn_ref[c0:c0 + cw, :])
            acc = part if acc is None else acc + part
        o_ref[...] = x + (0.5 * gate) * acc

    if not n_conv:
        @pl.when(tile == 0)
        def _first_tile():
            half_step(xa_ref, out_refs[0], fetch_weights=True)

    @pl.when((tile >= (0 if n_conv else 1)) & (tile < n_a))
    def _tokens_a():
        half_step(xa_ref, out_refs[0])

    @pl.when(tile >= n_a)
    def _tokens_b():
        half_step(xb_ref, out_refs[-1])

    if n_pre:
        @pl.when((tile >= 0) & (tile < PRE_CHUNKS))
        def _preconvert():
            for src, dst in zip(pre_src, pre_dst):
                dst[...] = src[...].astype(BF16)


def _ffn(xa, xb, mod2, seq_b, mod_off, g, wgu, wdn, name, split_out=False, convert=False, preconvert=()):
    tm = FFN_TM
    n_a, n_b = xa.shape[0] // tm, xb.shape[0] // tm
    n_conv = N_CONV if convert else 0
    assert n_a + n_b >= PRE_CHUNKS
    tile = lambda s: jnp.maximum(s - n_conv, 0)
    a_map = lambda s: (jnp.minimum(tile(s), n_a - 1), 0)
    b_map = lambda s: (jnp.maximum(tile(s) - n_a, 0), 0)

    def chunk_spec(w, n_chunks, index):
        rows = w.shape[0] // n_chunks
        assert rows * n_chunks == w.shape[0] and rows % BF16_ROWS == 0
        return pl.BlockSpec((rows, w.shape[1]), lambda s: (jnp.minimum(index(s), n_chunks - 1), 0))

    if convert:
        w_specs = [chunk_spec(wgu, N_CONV, lambda s: s), chunk_spec(wdn, N_CONV, lambda s: s)]
        scratch = [pltpu.VMEM(wgu.shape, BF16), pltpu.VMEM(wdn.shape, BF16)]
    else:
        w_specs = [pl.BlockSpec(memory_space=pl.ANY), pl.BlockSpec(memory_space=pl.ANY)]
        scratch = [pltpu.VMEM(wgu.shape, BF16), pltpu.VMEM(wdn.shape, BF16),
                   pltpu.SemaphoreType.DMA((len(FF_CHUNKS), 3))]
    pre_specs = lambda: [chunk_spec(w, PRE_CHUNKS, tile) for w in preconvert]
    if split_out:
        out_specs = [pl.BlockSpec((tm, D_MODEL), a_map), pl.BlockSpec((tm, D_MODEL), b_map)]
        out_shape = [jax.ShapeDtypeStruct(xa.shape, F32), jax.ShapeDtypeStruct(xb.shape, F32)]
    else:
        out_specs = [pl.BlockSpec((tm, D_MODEL), lambda s: (tile(s), 0))]
        out_shape = [jax.ShapeDtypeStruct((xa.shape[0] + xb.shape[0], D_MODEL), F32)]
    return pl.pallas_call(
        functools.partial(_ffn_kernel, mod_off=mod_off, n_a=n_a, split_out=split_out, n_conv=n_conv,
                          n_pre=len(preconvert), seq_b=seq_b),
        grid=(n_conv + n_a + n_b,),
        in_specs=w_specs + [pl.BlockSpec((tm, D_MODEL), a_map), pl.BlockSpec((tm, D_MODEL), b_map),
                            _const_spec(mod2.shape), _const_spec((1, D_MODEL))] + pre_specs(),
        out_specs=out_specs + pre_specs(),
        out_shape=out_shape + [jax.ShapeDtypeStruct(w.shape, BF16) for w in preconvert],
        scratch_shapes=scratch,
        compiler_params=_params(),
        name=name,
    )(wgu, wdn, xa, xb, mod2, g, *preconvert)


def _gate_fillers(gates, h_rows, gate_cols):
    assert 2 * D_MODEL // MXU_COLS == N_HEADS

    def gate_block():
        c0 = len(gates) * MXU_COLS
        gates.append(jax.nn.sigmoid(_dot(h_rows(), gate_cols(c0, c0 + MXU_COLS))))

    return [gate_block] * N_HEADS


def _pool_fillers(st, delta_fn, wpool_ref, pscale_ref, wbrp_ref):
    def pool_delta():
        deltas = [delta_fn(g).astype(BF16) for g in range(len(POOL_WINDOWS))]
        st["d"] = jnp.concatenate(deltas, axis=1)

    def pool_weight():
        st["y"] = (_dot(st.pop("d"), wpool_ref[...]) * pscale_ref[...]).astype(BF16)

    def pool_out():
        st["a"] = _dot(st.pop("y"), wbrp_ref[...])

    return [pool_delta, pool_weight, pool_out]


def _merge_fillers(st, gates, rows, att, x_ref, o_ref, gate2, wbra_ref, wout_ref):
    def attention_branch():
        st["b"] = _dot(att.astype(BF16), wbra_ref[...])

    def gate():
        g = jnp.concatenate(gates, axis=1)
        merged = g[:, :D_MODEL] * st["a"][rows] + g[:, D_MODEL:] * st.pop("b")
        st["merged"] = merged.astype(BF16)

    def out():
        o_ref[rows, :] = x_ref[rows, :] + gate2 * _dot(st.pop("merged"), wout_ref[...])

    return [attention_branch, gate, out]


def _attend_tiles(n_tiles, tile_rows, attention, gate_fillers, pool_fillers, merge_fillers):
    idle = lambda: None
    extra = pool_fillers + [idle] * (N_HEADS - len(pool_fillers))
    for t in range(n_tiles):
        rows = slice(t * tile_rows, (t + 1) * tile_rows)
        gates = []
        fillers = [lambda g=g, e=e: (g(), e()) for g, e in zip(gate_fillers(rows, gates), extra)]
        pieces = merge_fillers(gates, rows, attention(t, fillers))
        extra = [idle] + pieces + [idle] * (N_HEADS - 1 - len(pieces))
    for piece in pieces:
        piece()


def _ctx_mixer_kernel(x_ref, mod_ref, g_ref, win_ref, qk_ref, wpool32_ref, pscale_ref,
                      wbrp_ref, wbra_ref, wout_ref, o_ref, ko_ref, vo_ref, wpool_ref, *, seq):
    @pl.when(pl.program_id(0) == 0)
    def _first_step():
        _build_pool_weight(wpool32_ref, wpool_ref)

    m_rows = x_ref.shape[0]
    h = _rms_mod(x_ref[...], g_ref[...], _mod_row(mod_ref, 0, 3), _mod_row(mod_ref, 0, 4)).astype(BF16)
    p = _dot(h, win_ref[:, 0:O_Q])
    q = _dot(h, win_ref[:, O_Q:O_K])
    k = _dot(h, win_ref[:, O_K:O_V])
    v = _dot(h, win_ref[:, O_V:O_G])
    for e in range(m_rows // seq):
        vo_ref[e] = v[e * seq:(e + 1) * seq].T
    scale = Q_SCALE
    zero = jnp.zeros((), BF16)
    one = jnp.ones((), BF16)
    st = {}

    def pool_delta(g):
        lanes = slice(g * LANES, (g + 1) * LANES)
        zeros_halo = jnp.zeros((POOL_HALO, LANES), F32)
        return jnp.concatenate(
            [_pool_delta(jnp.concatenate([zeros_halo, p[e * seq:(e + 1) * seq, lanes], zeros_halo], axis=0),
                         g, 0, seq) for e in range(m_rows // seq)], axis=0)

    def gate_fillers(rows, gates):
        return _gate_fillers(gates, lambda: h[rows], lambda c0, c1: win_ref[:, O_G + c0:O_G + c1])

    def attention(e, fillers):
        rows = slice(e * seq, (e + 1) * seq)
        qn, kn, vb = [], [], []
        for pr in range(N_PAIRS):
            lanes = slice(pr * LANES, (pr + 1) * LANES)
            kn.append(_head_norm(k[rows, lanes], qk_ref[1:2, :]))
            qn.append((_head_norm(q[rows, lanes], qk_ref[0:1, :]) * scale).astype(BF16))
            vb.append(v[rows, lanes].astype(BF16))
        ko_ref[e] = jnp.concatenate(kn, axis=1).T
        kb = [t.astype(BF16) for t in kn]

        def scores(head):
            return (_dot_nt(_own_lanes(head, qn[head // 2], zero), kb[head // 2]),)

        def weights(head, s):
            return jnp.exp(s - s.max(axis=-1, keepdims=True)).astype(BF16)

        def weighted_values(head, e_):
            return _dot(e_, _own_lanes(head, vb[head // 2], one))

        return _pipelined_heads(scores, weights, weighted_values, fillers)

    merge = functools.partial(_merge_fillers, st, x_ref=x_ref, o_ref=o_ref, gate2=_mod_row(mod_ref, 0, 5),
                              wbra_ref=wbra_ref, wout_ref=wout_ref)
    _attend_tiles(m_rows // seq, seq, attention, gate_fillers,
                  _pool_fillers(st, pool_delta, wpool_ref, pscale_ref, wbrp_ref), merge)


def _ctx_mixer(x_all, n_tok, mod2, g, w_in, qk2, w_pool, pscale, wbrp, wbra, wout, seq):
    tm = CTX_TB * seq
    row = lambda s: (s, 0)
    heads = lambda s: (s, 0, 0)
    return pl.pallas_call(
        functools.partial(_ctx_mixer_kernel, seq=seq),
        grid=(n_tok // tm,),
        in_specs=[pl.BlockSpec((tm, D_MODEL), row),
                  _const_spec(mod2.shape), _const_spec((1, D_MODEL)),
                  _resident_spec(w_in), _const_spec(qk2.shape),
                  _const_spec(w_pool.shape), _const_spec((1, D_POOL)),
                  _resident_spec(wbrp), _resident_spec(wbra), _resident_spec(wout)],
        out_specs=[pl.BlockSpec((tm, D_MODEL), row),
                   pl.BlockSpec((CTX_TB, D_ATT, seq), heads),
                   pl.BlockSpec((CTX_TB, D_ATT, seq), heads)],
        out_shape=[jax.ShapeDtypeStruct((n_tok, D_MODEL), F32),
                   jax.ShapeDtypeStruct((n_tok // seq, D_ATT, seq), F32),
                   jax.ShapeDtypeStruct((n_tok // seq, D_ATT, seq), F32)],
        scratch_shapes=[pltpu.VMEM((D_POOL, D_POOL), BF16)],
        compiler_params=_params(),
        name="ctx_mixer",
    )(x_all, mod2, g, w_in, qk2, w_pool, pscale, wbrp, wbra, wout)


def _lat_proj_kernel(x_ref, mod_ref, g_ref, win_ref, qk_ref, h_ref, p_ref, q_ref, k_ref, v_ref, *, seq):
    row = 1 + (pl.program_id(0) * x_ref.shape[0]) // seq
    h = _rms_mod(x_ref[...], g_ref[...], _mod_row(mod_ref, row, 3), _mod_row(mod_ref, row, 4)).astype(BF16)
    h_ref[...] = h
    p_ref[...] = _dot(h, win_ref[:, 0:O_Q])
    q = _dot(h, win_ref[:, O_Q:O_K])
    k = _dot(h, win_ref[:, O_K:O_V])
    v_ref[...] = _dot(h, win_ref[:, O_V:O_G]).astype(BF16)
    scale = Q_SCALE
    for pr in range(N_PAIRS):
        lanes = slice(pr * LANES, (pr + 1) * LANES)
        q_ref[:, lanes] = (_head_norm(q[:, lanes], qk_ref[0:1, :]) * scale).astype(BF16)
        k_ref[:, lanes] = _head_norm(k[:, lanes], qk_ref[1:2, :]).astype(BF16)


def _lat_proj(x_all, tok0, n_tok, seq, mod2, g, w_in, qk2):
    tm = PROJ_TM
    tile0 = tok0 // tm
    row = lambda s: (s, 0)
    shapes = [(D_MODEL, BF16), (D_POOL, F32), (D_ATT, BF16), (D_ATT, BF16), (D_ATT, BF16)]
    return pl.pallas_call(
        functools.partial(_lat_proj_kernel, seq=seq),
        grid=(n_tok // tm,),
        in_specs=[pl.BlockSpec((tm, D_MODEL), lambda s: (tile0 + s, 0)),
                  _const_spec(mod2.shape),
                  _const_spec((1, D_MODEL)), _resident_spec(w_in, (O_G, 0)),
                  _const_spec(qk2.shape)],
        out_specs=[pl.BlockSpec((tm, width), row) for width, _ in shapes],
        out_shape=[jax.ShapeDtypeStruct((n_tok, width), dtype) for width, dtype in shapes],
        compiler_params=_params(),
        name="lat_proj",
    )(x_all, mod2, g, w_in, qk2)


def _build_bias_table(rpb_ref, bias_ref):
    lane = lax.broadcasted_iota(jnp.int32, (GRID_W, LANES), 1)
    qc = lax.broadcasted_iota(jnp.int32, (GRID_W, LANES), 0)
    kc = lane & (GRID_W - 1)
    c0 = jnp.clip(qc - NA_KW // 2, 0, GRID_W - NA_KW)
    col_ok = (kc >= c0) & (kc < c0 + NA_KW)
    first_half = lane < GRID_W
    shift_a = LANES - (NA_KW - 1)
    shift_b = (GRID_W - (NA_KW - 1)) % LANES
    for h in range(N_HEADS):
        for i in range(2 * NA_KH - 2):
            va = jnp.broadcast_to(rpb_ref[h, i:i + 1, :], (GRID_W, LANES))
            vb = jnp.broadcast_to(rpb_ref[h, i + 1:i + 2, :], (GRID_W, LANES))
            ra = pltpu.roll(va, shift_a, 1, stride=1, stride_axis=0)
            rb = pltpu.roll(vb, shift_b, 1, stride=1, stride_axis=0)
            bias_ref[h, i] = jnp.where(col_ok, jnp.where(first_half, ra, rb), NEG)


def _lat_attention(qn, j, fillers, k_ref, v_ref, ck_ref, cv_ref, bias_ref, *, n_rows):
    n_keys = LAT_KEY_ROWS * GRID_W
    r0q = j * LAT_ROWS
    u0 = jnp.clip(r0q - NA_KH // 2, 0, n_rows - LAT_KEY_ROWS)
    key0 = pl.multiple_of(u0 * GRID_W, GRID_W)
    log2_w = GRID_W.bit_length() - 1
    zero = jnp.zeros((), BF16)
    one = jnp.ones((), BF16)

    lane_row = lax.broadcasted_iota(jnp.int32, (1, LANES), 1) & (HEAD_DIM - 1)
    q_tile_row = jnp.right_shift(lax.broadcasted_iota(jnp.int32, (LAT_TM, 1), 0), log2_w)
    q_hot = jnp.where(lane_row == q_tile_row, 1.0, 0.0).astype(BF16)
    k_row = u0 + jnp.right_shift(lax.broadcasted_iota(jnp.int32, (n_keys, 1), 0), log2_w)
    first = jnp.clip(r0q + lane_row - NA_KH // 2, 0, n_rows - NA_KH)
    outside = (lane_row < LAT_ROWS) & ((k_row < first) | (k_row >= first + NA_KH))
    k_mask = jnp.where(outside, NEG, 0.0).astype(BF16)

    def scores(head):
        lanes = slice((head // 2) * LANES, (head // 2 + 1) * LANES)
        qp = qn[:, lanes]
        s_loc = _dot_nt(_own_lanes(head, qp, q_hot),
                        _own_lanes(head, k_ref[0, pl.ds(key0, n_keys), lanes], k_mask))
        return s_loc, _dot(_own_lanes(head, qp, zero), ck_ref[lanes, :])

    def weights(head, s_loc, s_ctx):
        bias_rows = []
        for jr in range(LAT_ROWS):
            blocks = []
            for pi in range(LAT_KEY_ROWS // 2):
                idx = jnp.clip(u0 - r0q + 2 * pi - jr + NA_KH - 1, 0, 2 * NA_KH - 3)
                blocks.append(bias_ref[head, idx])
            bias_rows.append(jnp.concatenate(blocks, axis=1))
        s_loc = s_loc + jnp.concatenate(bias_rows, axis=0)
        m = jnp.maximum(s_loc.max(axis=-1, keepdims=True), s_ctx.max(axis=-1, keepdims=True))
        return jnp.concatenate([jnp.exp(s_loc - m).astype(BF16), jnp.exp(s_ctx - m).astype(BF16)], axis=1)

    def weighted_values(head, e):
        lanes = slice((head // 2) * LANES, (head // 2 + 1) * LANES)
        values = _own_lanes(head, v_ref[0, pl.ds(key0, n_keys), lanes], one)
        return _dot(e[:, :n_keys], values) + _dot_nt(e[:, n_keys:], cv_ref[head])

    return _pipelined_heads(scores, weights, weighted_values, fillers)


def _lat_mixer_kernel(x_ref, h_ref, q_ref, p_ref, pp_ref, pn_ref, mod_ref, k_ref, v_ref, ckin_ref, cvin_ref,
                      rpb_ref, wg_ref, wpool32_ref, pscale_ref, wbrp_ref, wbra_ref, wout_ref,
                      o_ref, bias_ref, wpool_ref, ck_ref, cv_ref, *, seq, n_rows):
    step = pl.program_id(0)
    tm = LAT_TILES * LAT_TM

    @pl.when(step == 0)
    def _first_step():
        _build_pool_weight(wpool32_ref, wpool_ref)
        _build_bias_table(rpb_ref, bias_ref)

    blk = lax.rem(step, seq // tm)
    st = {}

    @pl.when(blk == 0)
    def _sequence_cache():
        ck_ref[...] = ckin_ref[0].astype(BF16)
        low = lax.broadcasted_iota(jnp.int32, (LANES, 1), 0) < HEAD_DIM
        for head in range(N_HEADS):
            rows = slice((head // 2) * LANES, (head // 2 + 1) * LANES)
            cv_ref[head] = jnp.where(low if head % 2 == 0 else ~low, cvin_ref[0, rows, :], 1.0).astype(BF16)

    def pool_delta(g):
        lanes = slice(g * LANES, (g + 1) * LANES)
        pe = jnp.concatenate([pp_ref[:, lanes], p_ref[:, lanes], pn_ref[:, lanes]], axis=0)
        pos = blk * tm - POOL_HALO + lax.broadcasted_iota(jnp.int32, (tm + 2 * POOL_HALO, 1), 0)
        return _pool_delta(jnp.where((pos >= 0) & (pos < seq), pe, 0.0), g, blk * tm, seq)

    block_gates = []

    def gate_fillers(rows, gates):
        if not block_gates:
            fillers = _gate_fillers(block_gates, lambda: h_ref[...], lambda c0, c1: wg_ref[:, c0:c1])
            return [lambda f=f: (f(), gates.append(block_gates[-1][rows])) for f in fillers]
        gates.extend(g[rows] for g in block_gates)
        return [lambda: None] * N_HEADS

    pool_fillers = _pool_fillers(st, pool_delta, wpool_ref, pscale_ref, wbrp_ref)

    def attention(t, fillers):
        return _lat_attention(q_ref[t * LAT_TM:(t + 1) * LAT_TM, :], blk * LAT_TILES + t, fillers,
                              k_ref, v_ref, ck_ref, cv_ref, bias_ref, n_rows=n_rows)

    merge = functools.partial(_merge_fillers, st, x_ref=x_ref, o_ref=o_ref,
                              gate2=_mod_row(mod_ref, 1 + step // (seq // tm), 5),
                              wbra_ref=wbra_ref, wout_ref=wout_ref)
    _attend_tiles(LAT_TILES, LAT_TM, attention, gate_fillers, pool_fillers, merge)


def _lat_mixer(x_all, tok0, n_batch, seq, mod2, h, q, p, k3, v3, ck3, cv3, rpb_pad, w_in, w_pool, pscale,
               wbrp, wbra, wout):
    tm = LAT_TILES * LAT_TM
    n_rows = seq // GRID_W
    nb = seq // tm
    tile0 = tok0 // tm
    halo_per_tile = tm // POOL_HALO
    n_halo = p.shape[0] // POOL_HALO
    row = lambda s: (s, 0)
    seq_map = lambda s: (s // nb, 0, 0)
    return pl.pallas_call(
        functools.partial(_lat_mixer_kernel, seq=seq, n_rows=n_rows),
        grid=(n_batch * nb,),
        in_specs=[pl.BlockSpec((tm, D_MODEL), lambda s: (tile0 + s, 0)),
                  pl.BlockSpec((tm, D_MODEL), row),
                  pl.BlockSpec((tm, D_ATT), row),
                  pl.BlockSpec((tm, D_POOL), row),
                  pl.BlockSpec((POOL_HALO, D_POOL), lambda s: (jnp.maximum(s * halo_per_tile - 1, 0), 0)),
                  pl.BlockSpec((POOL_HALO, D_POOL),
                               lambda s: (jnp.minimum((s + 1) * halo_per_tile, n_halo - 1), 0)),
                  _const_spec(mod2.shape),
                  pl.BlockSpec((1, seq, D_ATT), seq_map), pl.BlockSpec((1, seq, D_ATT), seq_map),
                  pl.BlockSpec((1,) + ck3.shape[1:], seq_map), pl.BlockSpec((1,) + cv3.shape[1:], seq_map),
                  _const_spec(rpb_pad.shape),
                  _resident_spec(w_in, (2 * D_MODEL, O_G // (2 * D_MODEL))),
                  _const_spec(w_pool.shape), _const_spec((1, D_POOL)),
                  _resident_spec(wbrp), _resident_spec(wbra), _resident_spec(wout)],
        out_specs=pl.BlockSpec((tm, D_MODEL), row),
        out_shape=jax.ShapeDtypeStruct((n_batch * seq, D_MODEL), F32),
        scratch_shapes=[pltpu.VMEM((N_HEADS, 2 * NA_KH - 2, GRID_W, LANES), F32),
                        pltpu.VMEM((D_POOL, D_POOL), BF16),
                        pltpu.VMEM(ck3.shape[1:], BF16),
                        pltpu.VMEM((N_HEADS, LANES, cv3.shape[2]), BF16)],
        compiler_params=_params(),
        name="lat_mixer",
    )(x_all, h, q, p, p, p, mod2, k3, v3, ck3, cv3, rpb_pad, w_in, w_pool, pscale, wbrp, wbra, wout)


def kernel(x_prompt, x_sample, cache_k, cache_v, c, c_ctx, w_ada, b_ada, g_ff1, w_ff1_in, w_ff1_out,
           g_mix, w_in, q_gain, k_gain, w_pool, pool_scale, rpb, w_br_pool, w_br_att, w_out, g_ff2,
           w_ff2_in, w_ff2_out):
    n_ctx, seq_ctx, _ = x_prompt.shape
    n_lat, seq_lat, _ = x_sample.shape
    depth = w_ada.shape[0]
    assert depth == 1 and n_lat + 1 <= 8
    l = 0

    xp = x_prompt.reshape(n_ctx * seq_ctx, D_MODEL)
    xs = x_sample.reshape(n_lat * seq_lat, D_MODEL)
    t_ctx, t_lat = xp.shape[0], xs.shape[0]
    mod2 = _adaln(c_ctx, c, w_ada[l], b_ada[l])

    pscale = pool_scale[l].reshape(1, D_POOL)
    qk2 = jnp.tile(jnp.stack([q_gain[l], k_gain[l]]), (1, LANES // HEAD_DIM))
    g1, gm, g2 = g_ff1[l].reshape(1, -1), g_mix[l].reshape(1, -1), g_ff2[l].reshape(1, -1)
    rpb_pad = jnp.pad(rpb[l], ((0, 0), (0, 0), (0, LANES - rpb.shape[-1])))
    ck3 = jnp.transpose(cache_k[:, l], (0, 2, 3, 1)).reshape(n_lat, D_ATT, -1)
    cv3 = jnp.transpose(cache_v[:, l], (0, 2, 3, 1)).reshape(n_lat, D_ATT, -1)

    x1, w2gu, w2dn, win, wbrp, wbra, wout = _ffn(
        xp, xs, mod2, seq_lat, 0, g1, w_ff1_in[l], w_ff1_out[l], "ffn1", convert=True,
        preconvert=(w_ff2_in[l], w_ff2_out[l], w_in[l], w_br_pool[l], w_br_att[l], w_out[l]))
    x2p, k_p, v_p = _ctx_mixer(x1, t_ctx, mod2, gm, win, qk2, w_pool[l], pscale, wbrp, wbra, wout, seq_ctx)
    h_s, p_s, q_s, k_s, v_s = _lat_proj(x1, t_ctx, t_lat, seq_lat, mod2, gm, win, qk2)
    x2s = _lat_mixer(x1, t_ctx, n_lat, seq_lat, mod2, h_s, q_s, p_s,
                     k_s.reshape(n_lat, seq_lat, D_ATT), v_s.reshape(n_lat, seq_lat, D_ATT), ck3, cv3,
                     rpb_pad, win, w_pool[l], pscale, wbrp, wbra, wout)
    yp, ys = _ffn(x2p, x2s, mod2, seq_lat, 6, g2, w2gu, w2dn, "ffn2", split_out=True)

    new_k = jnp.transpose(k_p.reshape(n_ctx, 1, N_HEADS, HEAD_DIM, seq_ctx), (0, 1, 4, 2, 3))
    new_v = jnp.transpose(v_p.reshape(n_ctx, 1, N_HEADS, HEAD_DIM, seq_ctx), (0, 1, 4, 2, 3))
    return (yp.reshape(x_prompt.shape), ys.reshape(x_sample.shape), new_k, new_v)
```
